```python
import math
import functools
import jax
import jax.numpy as jnp
from jax import lax
import numpy as np

D_MODEL = 1024
BATCH = 8
SEQ = 2048
DEPTH = 2

GRID_W = 64
CTX_LEN = 256
EPS = 1e-6
HEAD_DIM = 64
GROUP_WIDTH = D_MODEL // 2
MIX_WIDTH = 2 * GROUP_WIDTH

HY_CH = GROUP_WIDTH
HY_BANDS = 16
HY_EMB = 2 * HY_BANDS + 1
HY_FILT_HID = 64
HY_FAST_DECAY = 0.3
HY_SLOW_DECAY = 1.5
HY_DECAY_TARGET = 1e-2
SHORT_CONV = 3

SSM_INNER = GROUP_WIDTH
SSM_HEADDIM = 64
SSM_HEADS = SSM_INNER // SSM_HEADDIM
SSM_GROUPS = 2
SSM_STATE = 128
SSM_CHUNK = 128
SSM_CONV_CH = SSM_INNER + 2 * SSM_GROUPS * SSM_STATE
EVEN_IN = 3 * HY_CH + SSM_INNER + SSM_CONV_CH + 2 * SSM_HEADS

MLA_HEADS = GROUP_WIDTH // HEAD_DIM
MLA_NOPE = 64
MLA_ROPE = 32
MLA_V = 64
MLA_Q_LORA = D_MODEL // 4
MLA_KV_LORA = D_MODEL // 8

GQA_HEADS = GROUP_WIDTH // HEAD_DIM
GQA_KV = GQA_HEADS // 4
WINDOW = 128
ODD_IN = MLA_Q_LORA + MLA_KV_LORA + MLA_ROPE + (GQA_HEADS + 2 * GQA_KV) * HEAD_DIM

ROPE_BASE = 10000.0
ATTN_BLOCK = 128

D_FF = ((8 * D_MODEL // 3 + 127) // 128) * 128
N_EXPERTS = 8
TOP_K = 2
D_FF_EXPERT = 7 * D_MODEL // 2

N_EVEN = (DEPTH + 1) // 2
N_ODD = DEPTH // 2

kernel_name = 'hybrid_flow_prefix_block'


def rmsnorm(x, g):
    xf = x.astype(jnp.float32)
    y = xf * lax.rsqrt(jnp.mean(xf * xf, axis=-1, keepdims=True) + EPS)
    return (y * g.astype(jnp.float32)).astype(x.dtype)


def ada_params(cond, w, b):
    m = jax.nn.silu(cond) @ w + b
    return [t[:, None, :] for t in jnp.split(m, 6, axis=-1)]


def modulate(h, shift, scale):
    return h * (1 + scale) + shift


def split_last(x, sizes):
    out, start = [], 0
    for s in sizes:
        out.append(x[..., start:start + s])
        start += s
    return out


def flip_seq(t):
    return jnp.flip(t, axis=1)


def dwconv_centred(x, w, b):
    k_w, ch = w.shape
    pad = k_w // 2
    y = lax.conv_general_dilated(x, w[:, None, :].astype(x.dtype), window_strides=(1,), padding=[(pad, pad)],
                                 dimension_numbers=('NWC', 'WIO', 'NWC'), feature_group_count=ch)
    return y + b


def hyena_filters(n, w1, b1, w2, b2, w3, freq):
    pos = jnp.arange(n, dtype=jnp.float32)
    t = pos / (n - 1)
    omega = 2.0 * math.pi * pos / n
    bands = jnp.linspace(1e-4, HY_BANDS - 1, HY_BANDS, dtype=jnp.float32)
    z = jnp.concatenate([t[:, None], jnp.cos(omega[:, None] * bands), -jnp.sin(omega[:, None] * bands)], axis=-1)
    f = freq.astype(jnp.float32)
    h = jnp.sin(f * (z @ w1.astype(jnp.float32) + b1.astype(jnp.float32)))
    h = jnp.sin(f * (h @ w2.astype(jnp.float32) + b2.astype(jnp.float32)))
    h = (h @ w3.astype(jnp.float32)).reshape(n, 2, HY_CH)
    deltas = jnp.abs(jnp.linspace(math.log(HY_DECAY_TARGET) / HY_SLOW_DECAY, math.log(HY_DECAY_TARGET) / HY_FAST_DECAY,
                                  HY_CH, dtype=jnp.float32))
    decay = jnp.exp(-t[:, None] * deltas[None, :])
    return h[:, 0] * decay, h[:, 1] * decay


def long_conv_bidir(u, h_fwd, h_bwd, bias):
    n, ch = h_fwd.shape
    filt = jnp.concatenate([h_fwd, jnp.zeros((1, ch), jnp.float32), h_bwd[1:][::-1]], axis=0)
    uf = jnp.fft.rfft(u, n=2 * n, axis=1)
    ff = jnp.fft.rfft(filt, n=2 * n, axis=0)
    y = jnp.fft.irfft(uf * ff[None], n=2 * n, axis=1)[:, :n]
    return y + u * bias


def hyena_mixer(proj, conv_w, conv_b, w1, b1, w2, b2, w3, freq, bias):
    n = proj.shape[1]
    x0, x1, v = jnp.split(dwconv_centred(proj, conv_w, conv_b).astype(jnp.float32), 3, axis=-1)
    h_fwd, h_bwd = hyena_filters(n, w1, b1, w2, b2, w3, freq)
    y = x0 * long_conv_bidir(x1 * v, h_fwd, h_bwd, bias.astype(jnp.float32))
    return y.astype(proj.dtype)


def segsum(a):
    t_len = a.shape[-1]
    cs = jnp.cumsum(a, axis=-1)
    seg = cs[..., :, None] - cs[..., None, :]
    mask = jnp.tril(jnp.ones((t_len, t_len), dtype=bool))
    return jnp.where(mask, seg, -jnp.inf)


def ssd_scan(x, dt, a_head, b_in, c_in, init_state):
    bsz, n, nh, hp = x.shape
    ns = b_in.shape[-1]
    nc = n // SSM_CHUNK
    xd = (x * dt[..., None]).reshape(bsz, nc, SSM_CHUNK, nh, hp)
    a = (dt * a_head).reshape(bsz, nc, SSM_CHUNK, nh).transpose(0, 3, 1, 2)
    bc = b_in.reshape(bsz, nc, SSM_CHUNK, nh, ns)
    cc = c_in.reshape(bsz, nc, SSM_CHUNK, nh, ns)
    a_cs = jnp.cumsum(a, axis=-1)
    lmat = jnp.exp(segsum(a))
    scores = jnp.einsum('bcqhn,bcshn->bhcqs', cc, bc) * lmat
    y_diag = jnp.einsum('bhcqs,bcshp->bcqhp', scores, xd)
    decay_states = jnp.exp(a_cs[..., -1:] - a_cs).transpose(0, 2, 3, 1)
    states = jnp.einsum('bcqhn,bcqhp->bchpn', bc * decay_states[..., None], xd)
    states = jnp.concatenate([init_state[:, None], states], axis=1)
    chunk_a = jnp.pad(a_cs[..., -1], ((0, 0), (0, 0), (1, 0)))
    decay_chunk = jnp.exp(segsum(chunk_a))
    new_states = jnp.einsum('bhzc,bchpn->bzhpn', decay_chunk, states)
    states_in, final_state = new_states[:, :-1], new_states[:, -1]
    out_decay = jnp.exp(a_cs).transpose(0, 2, 3, 1)
    y_off = jnp.einsum('bcqhn,bchpn->bcqhp', cc * out_decay[..., None], states_in)
    return (y_diag + y_off).reshape(bsz, n, nh, hp), final_state


def ssd_mixer(proj, init_f, init_b, conv_w, conv_b, dt_bias, a_log, d_skip, norm_g):
    bsz, n, _ = proj.shape
    f32 = jnp.float32
    z, xbc, dt_raw = split_last(proj, [SSM_INNER, SSM_CONV_CH, 2 * SSM_HEADS])
    xbc = jax.nn.silu(dwconv_centred(xbc, conv_w, conv_b)).astype(f32)
    xs, b_in, c_in = split_last(xbc, [SSM_INNER, SSM_GROUPS * SSM_STATE, SSM_GROUPS * SSM_STATE])
    xs = xs.reshape(bsz, n, SSM_HEADS, SSM_HEADDIM)
    rep = SSM_HEADS // SSM_GROUPS
    b_in = jnp.repeat(b_in.reshape(bsz, n, SSM_GROUPS, SSM_STATE), rep, axis=2)
    c_in = jnp.repeat(c_in.reshape(bsz, n, SSM_GROUPS, SSM_STATE), rep, axis=2)
    dt = jax.nn.softplus(dt_raw.astype(f32).reshape(bsz, n, 2, SSM_HEADS) + dt_bias.astype(f32))
    a = -jnp.exp(a_log.astype(f32))
    y_f, s_f = ssd_scan(xs, dt[:, :, 0], a[0], b_in, c_in, init_f)
    y_b, s_b = ssd_scan(flip_seq(xs), flip_seq(dt[:, :, 1]), a[1], flip_seq(b_in), flip_seq(c_in), init_b)
    y = y_f + flip_seq(y_b) + xs * d_skip.astype(f32)[:, None]
    y = y.reshape(bsz, n, SSM_INNER) * jax.nn.silu(z.astype(f32))
    yg = y.reshape(bsz, n, SSM_GROUPS, SSM_INNER // SSM_GROUPS)
    yg = yg * lax.rsqrt(jnp.mean(yg * yg, axis=-1, keepdims=True) + EPS)
    y = yg.reshape(bsz, n, SSM_INNER) * norm_g.astype(f32)
    return y.astype(proj.dtype), s_f, s_b


def even_mix(h, init_f, init_b, w_in, hy_conv_w, hy_conv_b, f_w1, f_b1, f_w2, f_b2, f_w3, hy_freq, hy_bias,
             ssm_conv_w, ssm_conv_b, dt_bias, a_log, d_skip, ssm_norm_g, w_out):
    proj = h @ w_in
    y_hy = hyena_mixer(proj[..., :3 * HY_CH], hy_conv_w, hy_conv_b, f_w1, f_b1, f_w2, f_b2, f_w3, hy_freq, hy_bias)
    y_ssm, s_f, s_b = ssd_mixer(proj[..., 3 * HY_CH:], init_f, init_b, ssm_conv_w, ssm_conv_b, dt_bias, a_log,
                                d_skip, ssm_norm_g)
    out = jnp.concatenate([y_hy, y_ssm], axis=-1) @ w_out
    return out, s_f, s_b


def axial_rope_tables(n_tokens, rot_dim):
    n_rows = n_tokens // GRID_W
    rows, cols = jnp.meshgrid(jnp.arange(n_rows), jnp.arange(GRID_W), indexing='ij')
    rows = rows.reshape(-1).astype(jnp.float32)
    cols = cols.reshape(-1).astype(jnp.float32)
    nf = rot_dim // 4
    inv = ROPE_BASE ** (-jnp.arange(nf, dtype=jnp.float32) / nf)
    ang = jnp.concatenate([rows[:, None] * inv, cols[:, None] * inv], axis=-1)
    return jnp.cos(ang), jnp.sin(ang)


def apply_rope(x, rope):
    cos, sin = rope
    half = x.shape[-1] // 2
    x1, x2 = x[..., :half], x[..., half:]
    c, s = cos[None, :, None, :], sin[None, :, None, :]
    return jnp.concatenate([x1 * c - x2 * s, x2 * c + x1 * s], axis=-1).astype(x.dtype)


def dense_attention_blocked(q, k, v, scale):
    bsz, n, nh, d = q.shape
    nb = n // ATTN_BLOCK
    qb = q.reshape(bsz, nb, ATTN_BLOCK, nh, d).transpose(1, 0, 2, 3, 4)

    def one_block(qi):
        s = jnp.einsum('bqhd,bkhd->bhqk', qi, k).astype(jnp.float32) * scale
        p = jax.nn.softmax(s, axis=-1).astype(v.dtype)
        return jnp.einsum('bhqk,bkhd->bqhd', p, v)

    o = lax.map(one_block, qb)
    return o.transpose(1, 0, 2, 3, 4).reshape(bsz, n, nh, v.shape[-1])


def window_gqa_sink(q, k, v, k_ctx, v_ctx, sink):
    bsz, n, gh, d = q.shape
    n_kv = k.shape[2]
    g = gh // n_kv
    w = WINDOW
    nb = n // w
    c_len = k_ctx.shape[1]
    scale = d ** -0.5
    qb = q.reshape(bsz, nb, w, n_kv, g, d)

    def band(t):
        tp = jnp.pad(t, ((0, 0), (w, w), (0, 0), (0, 0))).reshape(bsz, nb + 2, w, n_kv, d)
        return jnp.concatenate([tp[:, :-2], tp[:, 1:-1], tp[:, 2:]], axis=2)

    kb, vb = band(k), band(v)
    qpos = jnp.arange(nb)[:, None] * w + jnp.arange(w)[None, :]
    kpos = (jnp.arange(nb)[:, None] - 1) * w + jnp.arange(3 * w)[None, :]
    rel = kpos[:, None, :] - qpos[:, :, None]
    valid = (jnp.abs(rel) <= w) & (kpos[:, None, :] >= 0) & (kpos[:, None, :] < n)
    s_loc = jnp.einsum('bnqkgd,bnrkd->bnkgqr', qb, kb).astype(jnp.float32) * scale
    s_loc = jnp.where(valid[None, :, None, None], s_loc, -jnp.inf)
    s_ctx = jnp.einsum('bnqkgd,bckd->bnkgqc', qb, k_ctx).astype(jnp.float32) * scale
    s_sink = jnp.broadcast_to(sink.astype(jnp.float32).reshape(1, 1, n_kv, g, 1, 1), s_loc.shape[:-1] + (1,))
    p = jax.nn.softmax(jnp.concatenate([s_loc, s_ctx, s_sink], axis=-1), axis=-1).astype(v.dtype)
    o = (jnp.einsum('bnkgqr,bnrkd->bnqkgd', p[..., :3 * w], vb)
         + jnp.einsum('bnkgqc,bckd->bnqkgd', p[..., 3 * w:3 * w + c_len], v_ctx))
    return o.reshape(bsz, n, gh, d)


def ctx_gqa_sink(q, k, v, sink):
    bsz, c_len, gh, d = q.shape
    n_kv = k.shape[2]
    g = gh // n_kv
    qg = q.reshape(bsz, c_len, n_kv, g, d)
    s = jnp.einsum('bqkgd,bckd->bkgqc', qg, k).astype(jnp.float32) * d ** -0.5
    s_sink = jnp.broadcast_to(sink.astype(jnp.float32).reshape(1, n_kv, g, 1, 1), s.shape[:-1] + (1,))
    p = jax.nn.softmax(jnp.concatenate([s, s_sink], axis=-1), axis=-1)[..., :-1].astype(v.dtype)
    return jnp.einsum('bkgqc,bckd->bqkgd', p, v).reshape(bsz, c_len, gh, d)


def odd_project(h, w_in, q_norm_g, w_q_up, kv_norm_g, w_kv_up):
    bsz, n, _ = h.shape
    q_lat, kv_lat, k_rope, q_g, k_g, v_g = split_last(
        h @ w_in, [MLA_Q_LORA, MLA_KV_LORA, MLA_ROPE, GQA_HEADS * HEAD_DIM, GQA_KV * HEAD_DIM, GQA_KV * HEAD_DIM])
    q = (rmsnorm(q_lat, q_norm_g) @ w_q_up).reshape(bsz, n, MLA_HEADS, MLA_NOPE + MLA_ROPE)
    kv = (rmsnorm(kv_lat, kv_norm_g) @ w_kv_up).reshape(bsz, n, MLA_HEADS, MLA_NOPE + MLA_V)
    return (q[..., :MLA_NOPE], q[..., MLA_NOPE:], kv[..., :MLA_NOPE], k_rope[:, :, None, :], kv[..., MLA_NOPE:],
            q_g.reshape(bsz, n, GQA_HEADS, HEAD_DIM), k_g.reshape(bsz, n, GQA_KV, HEAD_DIM),
            v_g.reshape(bsz, n, GQA_KV, HEAD_DIM))


def mla_keys(k_nope, k_rope):
    return jnp.concatenate([k_nope, jnp.broadcast_to(k_rope, k_nope.shape[:-1] + (MLA_ROPE,))], axis=-1)


def odd_mix(a_lat, a_ctx, need_ctx, rope_mla, rope_gqa, w_in, q_norm_g, w_q_up, kv_norm_g, w_kv_up, sink, w_out):
    bsz, n, _ = a_lat.shape
    qn_l, qr_l, kn_l, kr_l, v_l, qg_l, kg_l, vg_l = odd_project(a_lat, w_in, q_norm_g, w_q_up, kv_norm_g, w_kv_up)
    qn_c, qr_c, kn_c, kr_c, v_c, qg_c, kg_c, vg_c = odd_project(a_ctx, w_in, q_norm_g, w_q_up, kv_norm_g, w_kv_up)
    mla_scale = (MLA_NOPE + MLA_ROPE) ** -0.5
    k_c = mla_keys(kn_c, kr_c)
    q_l = jnp.concatenate([qn_l, apply_rope(qr_l, rope_mla)], axis=-1)
    k_l = mla_keys(kn_l, apply_rope(kr_l, rope_mla))
    o_mla = dense_attention_blocked(q_l, jnp.concatenate([k_c, k_l], axis=1), jnp.concatenate([v_c, v_l], axis=1),
                                    mla_scale)
    o_win = window_gqa_sink(apply_rope(qg_l, rope_gqa), apply_rope(kg_l, rope_gqa), vg_l, kg_c, vg_c, sink)
    m_lat = jnp.concatenate([o_mla.reshape(bsz, n, -1), o_win.reshape(bsz, n, -1)], axis=-1) @ w_out
    if not need_ctx:
        return m_lat, None
    c_len = a_ctx.shape[1]
    o_mla_c = dense_attention_blocked(jnp.concatenate([qn_c, qr_c], axis=-1), k_c, v_c, mla_scale)
    o_win_c = ctx_gqa_sink(qg_c, kg_c, vg_c, sink)
    m_ctx = jnp.concatenate([o_mla_c.reshape(bsz, c_len, -1), o_win_c.reshape(bsz, c_len, -1)], axis=-1) @ w_out
    return m_lat, m_ctx


def swiglu(h, w_gate, w_up, w_down):
    return (jax.nn.silu(h @ w_gate) * (h @ w_up)) @ w_down


def moe_swiglu(h, router, w_gate, w_up, w_down):
    bsz, n, d = h.shape
    t = h.reshape(-1, d)
    logits = (t @ router).astype(jnp.float32)
    vals, idx = lax.top_k(logits, TOP_K)
    wts = jax.nn.softmax(vals, axis=-1)
    gates = jnp.sum(jax.nn.one_hot(idx, N_EXPERTS, dtype=jnp.float32) * wts[..., None], axis=1).astype(t.dtype)
    out = jnp.zeros_like(t)
    for e in range(N_EXPERTS):
        out = out + gates[:, e:e + 1] * swiglu(t, w_gate[e], w_up[e], w_down[e])
    return out.reshape(bsz, n, d)


def setup_inputs(seed: int = 0) -> dict:
    key = jax.random.key(seed)
    keys = iter(jax.random.split(key, 64))
    f32 = jnp.float32
    d = D_MODEL

    def nrm(shape, scale):
        return scale * jax.random.normal(next(keys), shape, f32)

    def gain(shape):
        return 1.0 + 0.05 * jax.random.normal(next(keys), shape, f32)

    dt0 = jnp.exp(jax.random.uniform(next(keys), (N_EVEN, 2, SSM_HEADS), f32, math.log(1e-3), math.log(1e-1)))
    a0 = jax.random.uniform(next(keys), (N_EVEN, 2, SSM_HEADS), f32, 1.0, 16.0)
    return {
        'x': nrm((BATCH, SEQ, d), 1.0),
        'c': nrm((BATCH, d), 1.0),
        'ctx': nrm((BATCH, CTX_LEN, d), 1.0),
        'c_ctx': nrm((d,), 1.0),
        'ada_w': nrm((DEPTH, d, 6 * d), d ** -0.5),
        'ada_b': nrm((DEPTH, 6 * d), 0.02),
        'mix_pre_g': gain((DEPTH, d)),
        'mix_post_g': gain((DEPTH, d)),
        'ffn_pre_g': gain((DEPTH, d)),
        'ffn_post_g': gain((DEPTH, d)),
        'ev_w_in': nrm((N_EVEN, d, EVEN_IN), d ** -0.5),
        'ev_hy_conv_w': nrm((N_EVEN, SHORT_CONV, 3 * HY_CH), SHORT_CONV ** -0.5),
        'ev_hy_conv_b': nrm((N_EVEN, 3 * HY_CH), 0.02),
        'ev_hy_filt_w1': nrm((N_EVEN, HY_EMB, HY_FILT_HID), HY_EMB ** -0.5),
        'ev_hy_filt_b1': nrm((N_EVEN, HY_FILT_HID), 0.1),
        'ev_hy_filt_w2': nrm((N_EVEN, HY_FILT_HID, HY_FILT_HID), HY_FILT_HID ** -0.5),
        'ev_hy_filt_b2': nrm((N_EVEN, HY_FILT_HID), 0.1),
        'ev_hy_filt_w3': nrm((N_EVEN, HY_FILT_HID, 2 * HY_CH), 0.1 * HY_FILT_HID ** -0.5),
        'ev_hy_freq': gain((N_EVEN, HY_FILT_HID)),
        'ev_hy_bias': nrm((N_EVEN, HY_CH), 0.5),
        'ev_ssm_conv_w': nrm((N_EVEN, SHORT_CONV, SSM_CONV_CH), SHORT_CONV ** -0.5),
        'ev_ssm_conv_b': nrm((N_EVEN, SSM_CONV_CH), 0.02),
        'ev_ssm_dt_bias': dt0 + jnp.log(-jnp.expm1(-dt0)),
        'ev_ssm_a_log': jnp.log(a0),
        'ev_ssm_d': gain((N_EVEN, SSM_HEADS)),
        'ev_ssm_norm_g': gain((N_EVEN, SSM_INNER)),
        'ev_w_out': nrm((N_EVEN, MIX_WIDTH, d), MIX_WIDTH ** -0.5),
        'ev_ffn_w_gate': nrm((N_EVEN, d, D_FF), d ** -0.5),
        'ev_ffn_w_up': nrm((N_EVEN, d, D_FF), d ** -0.5),
        'ev_ffn_w_down': nrm((N_EVEN, D_FF, d), D_FF ** -0.5),
        'od_w_in': nrm((N_ODD, d, ODD_IN), d ** -0.5),
        'od_mla_q_norm_g': gain((N_ODD, MLA_Q_LORA)),
        'od_mla_w_q_up': nrm((N_ODD, MLA_Q_LORA, MLA_HEADS * (MLA_NOPE + MLA_ROPE)), MLA_Q_LORA ** -0.5),
        'od_mla_kv_norm_g': gain((N_ODD, MLA_KV_LORA)),
        'od_mla_w_kv_up': nrm((N_ODD, MLA_KV_LORA, MLA_HEADS * (MLA_NOPE + MLA_V)), MLA_KV_LORA ** -0.5),
        'od_gqa_sink': nrm((N_ODD, GQA_HEADS), 0.5),
        'od_w_out': nrm((N_ODD, MIX_WIDTH, d), MIX_WIDTH ** -0.5),
        'od_router': nrm((N_ODD, d, N_EXPERTS), d ** -0.5),
        'od_moe_w_gate': nrm((N_ODD, N_EXPERTS, d, D_FF_EXPERT), d ** -0.5),
        'od_moe_w_up': nrm((N_ODD, N_EXPERTS, d, D_FF_EXPERT), d ** -0.5),
        'od_moe_w_down': nrm((N_ODD, N_EXPERTS, D_FF_EXPERT, d), D_FF_EXPERT ** -0.5),
    }


def reference(x, c, ctx, c_ctx, ada_w, ada_b, mix_pre_g, mix_post_g, ffn_pre_g, ffn_post_g,
              ev_w_in, ev_hy_conv_w, ev_hy_conv_b, ev_hy_filt_w1, ev_hy_filt_b1, ev_hy_filt_w2, ev_hy_filt_b2,
              ev_hy_filt_w3, ev_hy_freq, ev_hy_bias, ev_ssm_conv_w, ev_ssm_conv_b, ev_ssm_dt_bias, ev_ssm_a_log,
              ev_ssm_d, ev_ssm_norm_g, ev_w_out, ev_ffn_w_gate, ev_ffn_w_up, ev_ffn_w_down,
              od_w_in, od_mla_q_norm_g, od_mla_w_q_up, od_mla_kv_norm_g, od_mla_w_kv_up, od_gqa_sink, od_w_out,
              od_router, od_moe_w_gate, od_moe_w_up, od_moe_w_down):
    bsz, n, _ = x.shape
    rope_mla = axial_rope_tables(n, MLA_ROPE)
    rope_gqa = axial_rope_tables(n, HEAD_DIM)
    h_lat, h_ctx = x, ctx
    for i in range(DEPTH):
        j = i // 2
        last = i == DEPTH - 1
        sh1, sc1, g1, sh2, sc2, g2 = ada_params(c, ada_w[i], ada_b[i])
        csh1, csc1, cg1, csh2, csc2, cg2 = ada_params(c_ctx[None, :], ada_w[i], ada_b[i])
        a_lat = modulate(rmsnorm(h_lat, mix_pre_g[i]), sh1, sc1)
        a_ctx = modulate(rmsnorm(h_ctx, mix_pre_g[i]), csh1, csc1)
        if i % 2 == 0:
            ev = (ev_w_in[j], ev_hy_conv_w[j], ev_hy_conv_b[j], ev_hy_filt_w1[j], ev_hy_filt_b1[j], ev_hy_filt_w2[j],
                  ev_hy_filt_b2[j], ev_hy_filt_w3[j], ev_hy_freq[j], ev_hy_bias[j], ev_ssm_conv_w[j], ev_ssm_conv_b[j],
                  ev_ssm_dt_bias[j], ev_ssm_a_log[j], ev_ssm_d[j], ev_ssm_norm_g[j], ev_w_out[j])
            zero_state = jnp.zeros((bsz, SSM_HEADS, SSM_HEADDIM, SSM_STATE), jnp.float32)
            m_ctx, s_f, s_b = even_mix(a_ctx, zero_state, zero_state, *ev)
            m_lat, _, _ = even_mix(a_lat, s_f, s_b, *ev)
            ffn = functools.partial(swiglu, w_gate=ev_ffn_w_gate[j], w_up=ev_ffn_w_up[j], w_down=ev_ffn_w_down[j])
        else:
            m_lat, m_ctx = odd_mix(a_lat, a_ctx, not last, rope_mla, rope_gqa, od_w_in[j], od_mla_q_norm_g[j],
                                   od_mla_w_q_up[j], od_mla_kv_norm_g[j], od_mla_w_kv_up[j], od_gqa_sink[j],
                                   od_w_out[j])
            ffn = functools.partial(moe_swiglu, router=od_router[j], w_gate=od_moe_w_gate[j],
                                    w_up=od_moe_w_up[j], w_down=od_moe_w_down[j])
        h_lat = h_lat + g1 * rmsnorm(m_lat, mix_post_g[i])
        f_lat = ffn(modulate(rmsnorm(h_lat, ffn_pre_g[i]), sh2, sc2))
        h_lat = h_lat + g2 * rmsnorm(f_lat, ffn_post_g[i])
        if not last:
            h_ctx = h_ctx + cg1 * rmsnorm(m_ctx, mix_post_g[i])
            f_ctx = ffn(modulate(rmsnorm(h_ctx, ffn_pre_g[i]), csh2, csc2))
            h_ctx = h_ctx + cg2 * rmsnorm(f_ctx, ffn_post_g[i])
    return h_lat
```

```python
import functools
import math

import jax
import jax.numpy as jnp
from jax import lax
from jax.experimental import pallas as pl
from jax.experimental.pallas import tpu as pltpu

F32 = jnp.float32
BF16 = jnp.bfloat16
HIGHEST = lax.Precision.HIGHEST

D_MODEL = 1024
GRID_W = 64
EPS = 1e-6
HEAD_DIM = 64
GROUP_WIDTH = D_MODEL // 2

HY_CH = GROUP_WIDTH
HY_BANDS = 16
HY_EMB = 2 * HY_BANDS + 1
HY_FILT_HID = 64
HY_FAST_DECAY = 0.3
HY_SLOW_DECAY = 1.5
HY_DECAY_TARGET = 1e-2

SSM_INNER = GROUP_WIDTH
SSM_HEADDIM = 64
SSM_HEADS = SSM_INNER // SSM_HEADDIM
SSM_GROUPS = 2
SSM_STATE = 128
SSM_CHUNK = 128
SSM_CONV_CH = SSM_INNER + 2 * SSM_GROUPS * SSM_STATE

MLA_HEADS = GROUP_WIDTH // HEAD_DIM
MLA_NOPE = 64
MLA_ROPE = 32
MLA_V = 64
MLA_Q_LORA = D_MODEL // 4
MLA_KV_LORA = D_MODEL // 8

GQA_HEADS = GROUP_WIDTH // HEAD_DIM
GQA_KV = GQA_HEADS // 4
WINDOW = 128
ROPE_BASE = 10000.0

D_FF = ((8 * D_MODEL // 3 + 127) // 128) * 128
N_EXPERTS = 8
D_FF_EXPERT = 7 * D_MODEL // 2

LANES = 128
VMEM_LIMIT = 56 * 1024 * 1024
MOE_TILE = 1024
MOE_SUB = 256
MOE_FCHUNK = 512


def _cparams(*sem):
    return pltpu.CompilerParams(dimension_semantics=sem, vmem_limit_bytes=VMEM_LIMIT)


def _silu(x):
    return x / (1.0 + jnp.exp(-x))


def _rms(x, g):
    return x * lax.rsqrt(jnp.mean(x * x, axis=-1, keepdims=True) + EPS) * g


def _norm_mod(h, g, shift, scale):
    return _rms(h, g) * (1.0 + scale) + shift


def _dot(a, b):
    return jnp.dot(a, b, preferred_element_type=F32)


def _dot_nt(a, b):
    return lax.dot_general(a, b, (((1,), (1,)), ((), ())), preferred_element_type=F32)


def _ada_body(c_ref, w_ref, b_ref, o_ref):
    s = _silu(c_ref[...]).astype(BF16)
    o_ref[0] = _dot(s, w_ref[0].astype(BF16)) + b_ref[0]


def _ada(cc, ada_w, ada_b):
    depth, d, n6 = ada_w.shape
    rows = cc.shape[0]
    tn = 1536
    return pl.pallas_call(
        _ada_body,
        grid=(depth, n6 // tn),
        in_specs=[pl.BlockSpec((rows, d), lambda l, j: (0, 0)),
                  pl.BlockSpec((1, d, tn), lambda l, j: (l, 0, j)),
                  pl.BlockSpec((1, 1, tn), lambda l, j: (l, 0, j))],
        out_specs=pl.BlockSpec((1, rows, tn), lambda l, j: (l, 0, j)),
        out_shape=jax.ShapeDtypeStruct((depth, rows, n6), F32),
        compiler_params=_cparams("arbitrary", "arbitrary"),
        name="ada",
    )(cc, ada_w, ada_b.reshape(depth, 1, n6))


def _even_in_body(h_ref, mod_ref, g_ref, w_ref, wdt_ref, o_ref, dt_ref):
    a = _norm_mod(h_ref[...], g_ref[...], mod_ref[0:1, :], mod_ref[1:2, :]).astype(BF16)
    n_out = o_ref.shape[-1]
    for j in range(0, n_out, 512):
        o_ref[:, j:j + 512] = _dot(a, w_ref[:, j:j + 512]).astype(BF16)
    dt_ref[...] = _dot(a, wdt_ref[...])


def _even_in(h_all, mod, g_pre, w_main, w_dt, tm):
    G, n, d = h_all.shape
    n_main = w_main.shape[1]
    return pl.pallas_call(
        _even_in_body,
        grid=(G, n // tm),
        in_specs=[pl.BlockSpec((None, tm, d), lambda g, t: (g, t, 0)),
                  pl.BlockSpec((None, 6, d), lambda g, t: (g, 0, 0)),
                  pl.BlockSpec((1, d), lambda g, t: (0, 0)),
                  pl.BlockSpec((d, n_main), lambda g, t: (0, 0)),
                  pl.BlockSpec((d, LANES), lambda g, t: (0, 0))],
        out_specs=[pl.BlockSpec((None, tm, n_main), lambda g, t: (g, t, 0)),
                   pl.BlockSpec((None, tm, LANES), lambda g, t: (g, t, 0))],
        out_shape=[jax.ShapeDtypeStruct((G, n, n_main), BF16),
                   jax.ShapeDtypeStruct((G, n, LANES), F32)],
        compiler_params=_cparams("arbitrary", "arbitrary"),
        name="even_in",
    )(h_all, mod, g_pre, w_main, w_dt)


def _dwconv3(x, w, b):
    L = x.shape[0]
    row = lax.broadcasted_iota(jnp.int32, x.shape, 0)
    prev = jnp.where(row >= 1, pltpu.roll(x, 1, axis=0), 0.0)
    nxt = jnp.where(row < L - 1, pltpu.roll(x, L - 1, axis=0), 0.0)
    return prev * w[0:1, :] + x * w[1:2, :] + nxt * w[2:3, :] + b


def _hy_pre_body(p_ref, cw_ref, cb_ref, u_ref, x0_ref):
    c = HY_CH
    parts = []
    for k in range(3):
        x = p_ref[:, k * c:(k + 1) * c].astype(F32)
        parts.append(_dwconv3(x, cw_ref[:, k * c:(k + 1) * c], cb_ref[:, k * c:(k + 1) * c]))
    x0_ref[...] = parts[0].astype(BF16)
    u_ref[...] = (parts[1] * parts[2]).astype(BF16)


def _hy_pre(proj, nseq, L, index_map, cw, cb):
    c3 = 3 * HY_CH
    return pl.pallas_call(
        _hy_pre_body,
        grid=(nseq,),
        in_specs=[pl.BlockSpec((None, L, c3), index_map),
                  pl.BlockSpec((3, c3), lambda s: (0, 0)),
                  pl.BlockSpec((1, c3), lambda s: (0, 0))],
        out_specs=[pl.BlockSpec((None, L, HY_CH), lambda s: (s, 0, 0)),
                   pl.BlockSpec((None, L, HY_CH), lambda s: (s, 0, 0))],
        out_shape=[jax.ShapeDtypeStruct((nseq, L, HY_CH), BF16),
                   jax.ShapeDtypeStruct((nseq, L, HY_CH), BF16)],
        compiler_params=_cparams("arbitrary"),
        name="hy_pre",
    )(proj, cw, cb)


def _hy_filter_body(z_ref, t_ref, w1_ref, b1_ref, w2_ref, b2_ref, w3_ref, f_ref, dl_ref, o_ref):
    f = f_ref[...]
    h = jnp.sin(f * (jnp.dot(z_ref[...], w1_ref[...], precision=HIGHEST, preferred_element_type=F32) + b1_ref[...]))
    h = jnp.sin(f * (jnp.dot(h, w2_ref[...], precision=HIGHEST, preferred_element_type=F32) + b2_ref[...]))
    h = jnp.dot(h, w3_ref[...], precision=HIGHEST, preferred_element_type=F32)
    decay = jnp.exp(-t_ref[...] * dl_ref[...])
    hf = h[:, :HY_CH] * decay
    hb = h[:, HY_CH:] * decay
    row = lax.broadcasted_iota(jnp.int32, hb.shape, 0)
    hb = jnp.where(row == 0, 0.0, hb)
    o_ref[:, :HY_CH] = hf.astype(BF16)
    o_ref[:, HY_CH:] = hb.astype(BF16)


def _hy_filter(L, w1, b1, w2, b2, w3, freq):
    pos = jnp.arange(L, dtype=F32)
    t = pos / (L - 1)
    omega = 2.0 * math.pi * pos / L
    bands = jnp.linspace(1e-4, HY_BANDS - 1, HY_BANDS, dtype=F32)
    z = jnp.concatenate([t[:, None], jnp.cos(omega[:, None] * bands), -jnp.sin(omega[:, None] * bands)], axis=-1)
    z = jnp.pad(z, ((0, 0), (0, LANES - HY_EMB)))
    w1p = jnp.pad(w1, ((0, LANES - HY_EMB), (0, 0)))
    deltas = jnp.abs(jnp.linspace(math.log(HY_DECAY_TARGET) / HY_SLOW_DECAY,
                                  math.log(HY_DECAY_TARGET) / HY_FAST_DECAY, HY_CH, dtype=F32))
    return pl.pallas_call(
        _hy_filter_body,
        out_shape=jax.ShapeDtypeStruct((L, 2 * HY_CH), BF16),
        compiler_params=pltpu.CompilerParams(vmem_limit_bytes=VMEM_LIMIT),
        name="hy_filter",
    )(z, t[:, None], w1p, b1[None, :], w2, b2[None, :], w3, freq[None, :], deltas[None, :])


def _dft_tables(L):
    k = jnp.arange(L, dtype=jnp.int32)[:, None]
    t = jnp.arange(L, dtype=jnp.int32)[None, :]
    ang = ((k * t) % (2 * L)).astype(F32) * (math.pi / L)
    alt = jnp.where(t % 2 == 0, 1.0, -1.0).astype(F32)
    fre = jnp.cos(ang)
    fim = jnp.where(k == 0, alt, -jnp.sin(ang))
    f2 = jnp.stack([fre, fim]).astype(BF16)
    fimt = jnp.where(t == 0, jnp.where(k % 2 == 0, 1.0, -1.0).astype(F32), -jnp.sin(ang)).astype(BF16)
    return f2, fimt


def _filt_dft_body(f_ref, x_ref, o_ref):
    o_ref[...] = _dot(f_ref[...], x_ref[...])


def _filt_dft(f2, hfb):
    _, L, _ = f2.shape
    cols = hfb.shape[1]
    tr = min(L, 1024)
    tc = 512
    return pl.pallas_call(
        _filt_dft_body,
        grid=(2, L // tr, cols // tc),
        in_specs=[pl.BlockSpec((None, tr, L), lambda p, i, j: (p, i, 0)),
                  pl.BlockSpec((L, tc), lambda p, i, j: (0, j))],
        out_specs=pl.BlockSpec((None, tr, tc), lambda p, i, j: (p, i, j)),
        out_shape=jax.ShapeDtypeStruct((2, L, cols), F32),
        compiler_params=_cparams("arbitrary", "arbitrary", "arbitrary"),
        name="filt_dft",
    )(f2, hfb)


def _hy_conv_body(u_ref, x0_ref, f2_ref, fret_ref, fimt_ref, s_ref, bias_ref, o_ref, acc_ref, *, L, fc):
    k = pl.program_id(1)
    c = HY_CH

    @pl.when(k == 0)
    def _():
        acc_ref[...] = jnp.zeros_like(acc_ref)

    u = u_ref[...]
    ure = _dot(f2_ref[0], u)
    uim = _dot(f2_ref[1], u)
    srf = s_ref[0, :, :c]
    srb = s_ref[0, :, c:]
    sif = s_ref[1, :, :c]
    sib = s_ref[1, :, c:]
    is0 = (lax.broadcasted_iota(jnp.int32, (fc, c), 0) + k * fc) == 0
    wgt = jnp.where(is0, 1.0 / (2 * L), 1.0 / L)
    hre = srf + srb
    a_m = hre * wgt
    b_m = jnp.where(is0, 0.0, sif - sib) * wgt
    d_m = jnp.where(is0, sif + sib, hre) * wgt
    yre = (ure * a_m - uim * b_m).astype(BF16)
    yim = (ure * b_m + uim * d_m).astype(BF16)
    acc_ref[...] += _dot(fret_ref[...], yre) + _dot(fimt_ref[...], yim)

    @pl.when(k == pl.num_programs(1) - 1)
    def _():
        y = acc_ref[...] + u.astype(F32) * bias_ref[...]
        o_ref[...] = (x0_ref[...].astype(F32) * y).astype(BF16)


def _hy_conv(u, x0, f2, fimt, s, bias):
    nseq, L, c = u.shape
    fc = min(L, 512)
    return pl.pallas_call(
        functools.partial(_hy_conv_body, L=L, fc=fc),
        grid=(nseq, L // fc),
        in_specs=[pl.BlockSpec((None, L, c), lambda s_, k: (s_, 0, 0)),
                  pl.BlockSpec((None, L, c), lambda s_, k: (s_, 0, 0)),
                  pl.BlockSpec((2, fc, L), lambda s_, k: (0, k, 0)),
                  pl.BlockSpec((None, L, fc), lambda s_, k: (0, 0, k)),
                  pl.BlockSpec((L, fc), lambda s_, k: (0, k)),
                  pl.BlockSpec((2, fc, 2 * c), lambda s_, k: (0, k, 0)),
                  pl.BlockSpec((1, c), lambda s_, k: (0, 0))],
        out_specs=pl.BlockSpec((None, L, c), lambda s_, k: (s_, 0, 0)),
        out_shape=jax.ShapeDtypeStruct((nseq, L, c), BF16),
        scratch_shapes=[pltpu.VMEM((L, c), F32)],
        compiler_params=_cparams("arbitrary", "arbitrary"),
        name="hy_conv",
    )(u, x0, f2, f2, fimt, s, bias)


def _hyena(proj, batch, n, ctx_len, p):
    cw, cb, w1, b1, w2, b2, w3, freq, bias = p
    outs = []
    for L, imap in ((n, lambda s: (s, 0, 0)), (ctx_len, lambda s: (batch, s, 0))):
        u, x0 = _hy_pre(proj, batch, L, imap, cw, cb[None, :])
        hfb = _hy_filter(L, w1, b1, w2, b2, w3, freq)
        f2, fimt = _dft_tables(L)
        s = _filt_dft(f2, hfb)
        outs.append(_hy_conv(u, x0, f2, fimt, s, bias[None, :]))
    return outs


def _softplus(x):
    return jnp.maximum(x, 0.0) + jnp.log(1.0 + jnp.exp(-jnp.abs(x)))


def _chunk_cumsum(a, axis, reverse):
    q = SSM_CHUNK
    size = a.shape[axis]
    pos = lax.broadcasted_iota(jnp.int32, a.shape, axis) % q
    s = 1
    while s < q:
        if reverse:
            a = a + jnp.where(pos < q - s, pltpu.roll(a, size - s, axis=axis), 0.0)
        else:
            a = a + jnp.where(pos >= s, pltpu.roll(a, s, axis=axis), 0.0)
        s *= 2
    return a


def _ssd_prepare(L, xbc_ref, dtc_ref, dtr_ref, cw_ref, cb_ref, dtbc_ref, alc_ref, dtbr_ref, alr_ref,
                 x_s, c_s, bt_s, csc_s, csr_s, dtr_s):
    nc = L // SSM_CHUNK
    nh = SSM_HEADS
    inner = SSM_INNER
    gs = SSM_GROUPS * SSM_STATE
    xs = _silu(_dwconv3(xbc_ref[:, :inner].astype(F32), cw_ref[:, :inner], cb_ref[:, :inner]))
    x_s[0:L, :] = xs.astype(BF16)
    bm = _silu(_dwconv3(xbc_ref[:, inner:inner + gs].astype(F32), cw_ref[:, inner:inner + gs],
                        cb_ref[:, inner:inner + gs]))
    for c in range(nc):
        bt_s[c] = bm[c * SSM_CHUNK:(c + 1) * SSM_CHUNK, :].T.astype(BF16)
    cm = _silu(_dwconv3(xbc_ref[:, inner + gs:].astype(F32), cw_ref[:, inner + gs:], cb_ref[:, inner + gs:]))
    c_s[0:L, :] = cm.astype(BF16)
    a_col = _softplus(dtc_ref[...] + dtbc_ref[...]) * (-jnp.exp(alc_ref[...]))
    lane = lax.broadcasted_iota(jnp.int32, a_col.shape, 1)
    csc_s[0:L, :] = jnp.where(lane < nh, _chunk_cumsum(a_col, 0, False), _chunk_cumsum(a_col, 0, True))
    dt_row = _softplus(dtr_ref[...] + dtbr_ref[...])
    a_row = (dt_row * (-jnp.exp(alr_ref[...]))).reshape(2 * nh * nc, SSM_CHUNK)
    rown = lax.broadcasted_iota(jnp.int32, a_row.shape, 0)
    csr_s[0:2 * nh * nc, :] = jnp.where(rown < nh * nc, _chunk_cumsum(a_row, 1, False),
                                        _chunk_cumsum(a_row, 1, True))
    dtr_s[0:2 * nh * nc, :] = dt_row.reshape(2 * nh * nc, SSM_CHUNK)


def _ssd_chunk(c, d, nc, first, x_s, c_s, bt_s, csc_s, csr_s, dtr_s, st_s, y_s):
    q = SSM_CHUNK
    ns = SSM_STATE
    hp = SSM_HEADDIM
    hpg = SSM_HEADS // SSM_GROUPS
    gw = hpg * hp
    r0 = pl.multiple_of(c * q, q)
    ri = lax.broadcasted_iota(jnp.int32, (q, q), 0)
    ci = lax.broadcasted_iota(jnp.int32, (q, q), 1)
    keep = (ci <= ri) if d == 0 else (ci >= ri)
    lane_blk = lax.broadcasted_iota(jnp.int32, (q, gw), 1) // hp
    cc = c_s[pl.ds(r0, q), :]
    xc = x_s[pl.ds(r0, q), :]
    btc = bt_s[c]
    ys = []
    for g in range(SSM_GROUPS):
        cg = cc[:, g * ns:(g + 1) * ns]
        btg = btc[g * ns:(g + 1) * ns, :]
        xg = xc[:, g * gw:(g + 1) * gw]
        s_cb = _dot(cg, btg)
        btg32 = btg.astype(F32)
        m_rows, w_rows, e_cols, tots = [], [], [], []
        for hh in range(hpg):
            idx = d * SSM_HEADS + g * hpg + hh
            csc = csc_s[pl.ds(r0, q), idx:idx + 1]
            csr = csr_s[pl.ds(idx * nc + c, 1), :]
            dtr = dtr_s[pl.ds(idx * nc + c, 1), :]
            lmat = jnp.where(keep, jnp.exp(csc - csr), 0.0)
            m_rows.append((s_cb * lmat * dtr).astype(BF16))
            tot = csr[:, q - 1:q] if d == 0 else csr[:, 0:1]
            w_rows.append((btg32 * (jnp.exp(tot - csr) * dtr)).astype(BF16))
            e_cols.append(jnp.exp(csc))
            tots.append(jnp.exp(tot))
        yd = _dot(jnp.concatenate(m_rows, axis=0), xg)
        ds = _dot(jnp.concatenate(w_rows, axis=0), xg)
        st = st_s[d, g]
        yo = _dot(cg, st.astype(BF16))
        y_g = jnp.zeros((q, gw), F32)
        st_new = jnp.zeros((ns, gw), F32)
        for hh in range(hpg):
            sel = lane_blk == hh
            y_g = jnp.where(sel, yd[hh * q:(hh + 1) * q, :] + yo * e_cols[hh], y_g)
            st_new = jnp.where(sel, st * tots[hh] + ds[hh * ns:(hh + 1) * ns, :], st_new)
        st_s[d, g] = st_new
        ys.append(y_g)
    y = jnp.concatenate(ys, axis=1)
    if first:
        y_s[pl.ds(r0, q), :] = y
    else:
        y_s[pl.ds(r0, q), :] += y


def _ssd_body(xl_ref, xc_ref, dcl_ref, dcc_ref, drl_ref, drc_ref, cw_ref, cb_ref, dtbc_ref, alc_ref,
              dtbr_ref, alr_ref, dsk_ref, yl_ref, yc_ref,
              x_s, c_s, bt_s, csc_s, csr_s, dtr_s, st_s, y_s, *, n, ctx_len):
    st_s[...] = jnp.zeros_like(st_s)
    for L, xbc_ref, dtc_ref, dtr_ref, o_ref in ((ctx_len, xc_ref, dcc_ref, drc_ref, yc_ref),
                                                 (n, xl_ref, dcl_ref, drl_ref, yl_ref)):
        nc = L // SSM_CHUNK
        _ssd_prepare(L, xbc_ref, dtc_ref, dtr_ref, cw_ref, cb_ref, dtbc_ref, alc_ref, dtbr_ref, alr_ref,
                     x_s, c_s, bt_s, csc_s, csr_s, dtr_s)
        args = (x_s, c_s, bt_s, csc_s, csr_s, dtr_s, st_s, y_s)

        def fwd(c, carry, nc=nc, args=args):
            _ssd_chunk(c, 0, nc, True, *args)
            return carry

        def bwd(i, carry, nc=nc, args=args):
            _ssd_chunk(nc - 1 - i, 1, nc, False, *args)
            return carry

        lax.fori_loop(0, nc, fwd, 0)
        lax.fori_loop(0, nc, bwd, 0)
        o_ref[...] = (y_s[0:L, :] + x_s[0:L, :].astype(F32) * dsk_ref[...]).astype(BF16)


def _ssd(proj, dt, batch, n, ctx_len, p):
    conv_w, conv_b, dt_bias, a_log, d_skip = p
    nh = SSM_HEADS
    ncl, ncc = n // SSM_CHUNK, ctx_len // SSM_CHUNK
    cch = SSM_CONV_CH
    xbc_blk = (SSM_INNER * 3 + SSM_INNER) // cch
    dt16 = dt[:, :, :2 * nh]
    dtr_l = dt16[:batch].transpose(0, 2, 1).reshape(batch, 2 * nh, ncl, SSM_CHUNK)
    dtr_c = dt16[batch].reshape(batch, ctx_len, 2 * nh).transpose(0, 2, 1).reshape(batch, 2 * nh, ncc, SSM_CHUNK)
    pad = LANES - 2 * nh
    dtb_col = jnp.pad(dt_bias.reshape(1, 2 * nh), ((0, 0), (0, pad)))
    al_col = jnp.pad(a_log.reshape(1, 2 * nh), ((0, 0), (0, pad)))
    dtb_row = jnp.broadcast_to(dt_bias.reshape(2 * nh, 1, 1), (2 * nh, 1, SSM_CHUNK))
    al_row = jnp.broadcast_to(a_log.reshape(2 * nh, 1, 1), (2 * nh, 1, SSM_CHUNK))
    dsk = jnp.repeat(d_skip, SSM_HEADDIM)[None, :]
    full = lambda shape: pl.BlockSpec(shape, lambda b: (0,) * len(shape))
    return pl.pallas_call(
        functools.partial(_ssd_body, n=n, ctx_len=ctx_len),
        grid=(batch,),
        in_specs=[pl.BlockSpec((None, n, cch), lambda b: (b, 0, xbc_blk)),
                  pl.BlockSpec((None, ctx_len, cch), lambda b: (batch, b, xbc_blk)),
                  pl.BlockSpec((None, n, LANES), lambda b: (b, 0, 0)),
                  pl.BlockSpec((None, ctx_len, LANES), lambda b: (batch, b, 0)),
                  pl.BlockSpec((None, 2 * nh, ncl, SSM_CHUNK), lambda b: (b, 0, 0, 0)),
                  pl.BlockSpec((None, 2 * nh, ncc, SSM_CHUNK), lambda b: (b, 0, 0, 0)),
                  full((3, cch)), full((1, cch)), full((1, LANES)), full((1, LANES)),
                  full((2 * nh, 1, SSM_CHUNK)), full((2 * nh, 1, SSM_CHUNK)), full((1, SSM_INNER))],
        out_specs=[pl.BlockSpec((None, n, SSM_INNER), lambda b: (b, 0, 0)),
                   pl.BlockSpec((None, ctx_len, SSM_INNER), lambda b: (b, 0, 0))],
        out_shape=[jax.ShapeDtypeStruct((batch, n, SSM_INNER), BF16),
                   jax.ShapeDtypeStruct((batch, ctx_len, SSM_INNER), BF16)],
        scratch_shapes=[pltpu.VMEM((n, SSM_INNER), BF16),
                        pltpu.VMEM((n, SSM_GROUPS * SSM_STATE), BF16),
                        pltpu.VMEM((ncl, SSM_GROUPS * SSM_STATE, SSM_CHUNK), BF16),
                        pltpu.VMEM((n, LANES), F32),
                        pltpu.VMEM((2 * nh * ncl, SSM_CHUNK), F32),
                        pltpu.VMEM((2 * nh * ncl, SSM_CHUNK), F32),
                        pltpu.VMEM((2, SSM_GROUPS, SSM_STATE, SSM_INNER // SSM_GROUPS), F32),
                        pltpu.VMEM((n, SSM_INNER), F32)],
        compiler_params=_cparams("arbitrary"),
        name="ssd",
    )(proj, proj, dt, dt, dtr_l, dtr_c, conv_w, conv_b[None, :], dtb_col, al_col, dtb_row, al_row, dsk)


def _mix_out_even_body(yhl_ref, yhc_ref, ysl_ref, ysc_ref, z_ref, ng_ref, w_ref, h_ref, mod_ref, pg_ref, o_ref,
                       *, batch):
    is_ctx = pl.program_id(0) == batch
    y_hy = jnp.where(is_ctx, yhc_ref[...], yhl_ref[...])
    ys = jnp.where(is_ctx, ysc_ref[...], ysl_ref[...]).astype(F32) * _silu(z_ref[...].astype(F32))
    gw = SSM_INNER // SSM_GROUPS
    parts = []
    for g in range(SSM_GROUPS):
        yg = ys[:, g * gw:(g + 1) * gw]
        parts.append(yg * lax.rsqrt(jnp.mean(yg * yg, axis=-1, keepdims=True) + EPS))
    y_ssm = (jnp.concatenate(parts, axis=1) * ng_ref[...]).astype(BF16)
    m = _dot(y_hy, w_ref[0:HY_CH, :]) + _dot(y_ssm, w_ref[HY_CH:, :])
    o_ref[...] = h_ref[...] + mod_ref[2:3, :] * _rms(m, pg_ref[...])


def _mix_out_even(y_hy_l, y_hy_c, y_ss_l, y_ss_c, proj, norm_g, w_out, h_all, mod, post_g, tm):
    G, n, d = h_all.shape
    batch = G - 1
    c = HY_CH
    z_blk = (3 * HY_CH) // SSM_INNER
    lat = lambda g, t: (jnp.minimum(g, batch - 1), t, 0)
    ctx = lambda g, t: (0, t, 0)
    return pl.pallas_call(
        functools.partial(_mix_out_even_body, batch=batch),
        grid=(G, n // tm),
        in_specs=[pl.BlockSpec((None, tm, c), lat), pl.BlockSpec((None, tm, c), ctx),
                  pl.BlockSpec((None, tm, c), lat), pl.BlockSpec((None, tm, c), ctx),
                  pl.BlockSpec((None, tm, SSM_INNER), lambda g, t: (g, t, z_blk)),
                  pl.BlockSpec((1, SSM_INNER), lambda g, t: (0, 0)),
                  pl.BlockSpec((2 * c, d), lambda g, t: (0, 0)),
                  pl.BlockSpec((None, tm, d), lambda g, t: (g, t, 0)),
                  pl.BlockSpec((None, 6, d), lambda g, t: (g, 0, 0)),
                  pl.BlockSpec((1, d), lambda g, t: (0, 0))],
        out_specs=pl.BlockSpec((None, tm, d), lambda g, t: (g, t, 0)),
        out_shape=jax.ShapeDtypeStruct((G, n, d), F32),
        compiler_params=_cparams("arbitrary", "arbitrary"),
        name="mix_out_even",
    )(y_hy_l, y_hy_c.reshape(1, n, c), y_ss_l, y_ss_c.reshape(1, n, c), proj, norm_g, w_out, h_all, mod, post_g)


def _mix_out_odd_body(a_ref, b_ref, w_ref, h_ref, mod_ref, pg_ref, o_ref):
    half = a_ref.shape[-1]
    m = _dot(a_ref[...], w_ref[0:half, :]) + _dot(b_ref[...], w_ref[half:, :])
    o_ref[...] = h_ref[...] + mod_ref[2:3, :] * _rms(m, pg_ref[...])


def _mix_out_odd(o_a, o_b, w_out, h_all, mod, post_g, tm):
    batch, n, c = o_a.shape
    d = h_all.shape[-1]
    return pl.pallas_call(
        _mix_out_odd_body,
        grid=(batch, n // tm),
        in_specs=[pl.BlockSpec((None, tm, c), lambda g, t: (g, t, 0)),
                  pl.BlockSpec((None, tm, c), lambda g, t: (g, t, 0)),
                  pl.BlockSpec((2 * c, d), lambda g, t: (0, 0)),
                  pl.BlockSpec((None, tm, d), lambda g, t: (g, t, 0)),
                  pl.BlockSpec((None, 6, d), lambda g, t: (g, 0, 0)),
                  pl.BlockSpec((1, d), lambda g, t: (0, 0))],
        out_specs=pl.BlockSpec((None, tm, d), lambda g, t: (g, t, 0)),
        out_shape=jax.ShapeDtypeStruct((batch, n, d), F32),
        compiler_params=_cparams("arbitrary", "arbitrary"),
        name="mix_out_odd",
    )(o_a, o_b, w_out, h_all, mod, post_g)


def _ffn_body(h_ref, mod_ref, g_ref, wg_ref, wu_ref, wd_ref, pg_ref, o_ref, *, fchunk):
    h = h_ref[...]
    a = _norm_mod(h, g_ref[...], mod_ref[3:4, :], mod_ref[4:5, :]).astype(BF16)
    ff = wg_ref.shape[1]
    f = jnp.zeros(h.shape, F32)
    for j in range(0, ff, fchunk):
        gate = _dot(a, wg_ref[:, j:j + fchunk])
        up = _dot(a, wu_ref[:, j:j + fchunk])
        f = f + _dot((_silu(gate) * up).astype(BF16), wd_ref[j:j + fchunk, :])
    o_ref[...] = h + mod_ref[5:6, :] * _rms(f, pg_ref[...])


def _ffn(h_all, mod, g_pre, wg, wu, wd, post_g, tm):
    G, n, d = h_all.shape
    ff = wg.shape[1]
    return pl.pallas_call(
        functools.partial(_ffn_body, fchunk=ff // 2),
        grid=(G, n // tm),
        in_specs=[pl.BlockSpec((None, tm, d), lambda g, t: (g, t, 0)),
                  pl.BlockSpec((None, 6, d), lambda g, t: (g, 0, 0)),
                  pl.BlockSpec((1, d), lambda g, t: (0, 0)),
                  pl.BlockSpec((d, ff), lambda g, t: (0, 0)),
                  pl.BlockSpec((d, ff), lambda g, t: (0, 0)),
                  pl.BlockSpec((ff, d), lambda g, t: (0, 0)),
                  pl.BlockSpec((1, d), lambda g, t: (0, 0))],
        out_specs=pl.BlockSpec((None, tm, d), lambda g, t: (g, t, 0)),
        out_shape=jax.ShapeDtypeStruct((G, n, d), F32),
        compiler_params=_cparams("arbitrary", "arbitrary"),
        name="ffn",
    )(h_all, mod, g_pre, wg, wu, wd, post_g)


def _rope(x, cos, sin_a, sin_b, half):
    return x * cos + pltpu.roll(x, LANES - half, axis=1) * sin_a + pltpu.roll(x, half, axis=1) * sin_b


def _odd_in_body(h_ref, mod_ref, g_ref, w_ref, qg_ref, wq_ref, kg_ref, wk_ref, wv_ref, rp_ref,
                 qm_ref, km_ref, vm_ref, qw_ref, kw_ref, vw_ref):
    a = _norm_mod(h_ref[...], g_ref[...], mod_ref[0:1, :], mod_ref[1:2, :]).astype(BF16)
    ql, kl = MLA_Q_LORA, MLA_KV_LORA
    o_qg = ql + kl
    o_kg = o_qg + GQA_HEADS * HEAD_DIM
    o_vg = o_kg + GQA_KV * HEAD_DIM
    o_kr = o_vg + GQA_KV * HEAD_DIM
    cm, s1m, s2m = rp_ref[:, 0:LANES], rp_ref[:, LANES:2 * LANES], rp_ref[:, 2 * LANES:3 * LANES]
    cg, s1g, s2g = rp_ref[:, 3 * LANES:4 * LANES], rp_ref[:, 4 * LANES:5 * LANES], rp_ref[:, 5 * LANES:6 * LANES]
    mla_scale = (MLA_NOPE + MLA_ROPE) ** -0.5
    gqa_scale = HEAD_DIM ** -0.5
    q_lat = _rms(_dot(a, w_ref[:, 0:ql]), qg_ref[...]).astype(BF16)
    q = _dot(q_lat, wq_ref[...])
    kv_lat = _rms(_dot(a, w_ref[:, ql:o_qg]), kg_ref[...]).astype(BF16)
    k = _dot(kv_lat, wk_ref[...])
    vm_ref[...] = _dot(kv_lat, wv_ref[...]).astype(BF16)
    k_rope = _rope(_dot(a, w_ref[:, o_kr:o_kr + LANES]), cm, s1m, s2m, MLA_ROPE // 2)
    for hd in range(MLA_HEADS):
        sl = slice(hd * LANES, (hd + 1) * LANES)
        qm_ref[:, sl] = (_rope(q[:, sl], cm, s1m, s2m, MLA_ROPE // 2) * mla_scale).astype(BF16)
        km_ref[:, sl] = (k[:, sl] + k_rope).astype(BF16)
    qg = _dot(a, w_ref[:, o_qg:o_kg])
    for blk in range(GQA_HEADS * HEAD_DIM // LANES):
        sl = slice(blk * LANES, (blk + 1) * LANES)
        qw_ref[:, sl] = (_rope(qg[:, sl], cg, s1g, s2g, HEAD_DIM // 2) * gqa_scale).astype(BF16)
    kw_ref[...] = _rope(_dot(a, w_ref[:, o_kg:o_vg]), cg, s1g, s2g, HEAD_DIM // 2).astype(BF16)
    vw_ref[...] = _dot(a, w_ref[:, o_vg:o_kr]).astype(BF16)


def _rope_tables(n):
    rows = (jnp.arange(n) // GRID_W).astype(F32)
    cols = (jnp.arange(n) % GRID_W).astype(F32)

    def cs(rot_dim):
        nf = rot_dim // 4
        inv = ROPE_BASE ** (-jnp.arange(nf, dtype=F32) / nf)
        ang = jnp.concatenate([rows[:, None] * inv, cols[:, None] * inv], axis=-1)
        return jnp.cos(ang), jnp.sin(ang)

    one = jnp.ones((n, 1), F32)
    zero = jnp.zeros((n, 1), F32)
    c, s = cs(MLA_ROPE)
    hm = MLA_ROPE // 2
    pad_n, pad_t = MLA_NOPE, LANES - MLA_NOPE - MLA_ROPE
    cm = jnp.concatenate([jnp.tile(one, (1, pad_n)), c, c, jnp.tile(one, (1, pad_t))], axis=1)
    s1m = jnp.concatenate([jnp.tile(zero, (1, pad_n)), -s, jnp.tile(zero, (1, hm + pad_t))], axis=1)
    s2m = jnp.concatenate([jnp.tile(zero, (1, pad_n + hm)), s, jnp.tile(zero, (1, pad_t))], axis=1)
    c, s = cs(HEAD_DIM)
    z32 = jnp.zeros_like(s)
    cg = jnp.concatenate([c, c, c, c], axis=1)
    s1g = jnp.concatenate([-s, z32, -s, z32], axis=1)
    s2g = jnp.concatenate([z32, s, z32, s], axis=1)
    lat = jnp.concatenate([cm, s1m, s2m, cg, s1g, s2g], axis=1)
    ident = jnp.concatenate([jnp.ones((n, LANES), F32), jnp.zeros((n, 2 * LANES), F32)] * 2, axis=1)
    return jnp.stack([lat, ident])


def _odd_in(h_all, mod, g_pre, w_in, q_norm_g, w_q_up, kv_norm_g, w_kv_up, tm):
    G, n, d = h_all.shape
    batch = G - 1
    ql, kl, rp = MLA_Q_LORA, MLA_KV_LORA, MLA_ROPE
    gq, gk = GQA_HEADS * HEAD_DIM, GQA_KV * HEAD_DIM
    o = [0, ql, ql + kl, ql + kl + rp, ql + kl + rp + gq, ql + kl + rp + gq + gk]
    kr_cols = jnp.pad(w_in[:, o[2]:o[3]], ((0, 0), (MLA_NOPE, LANES - MLA_NOPE - rp)))
    w = jnp.concatenate([w_in[:, o[0]:o[2]], w_in[:, o[3]:], kr_cols], axis=1).astype(BF16)
    hq = MLA_NOPE + MLA_ROPE
    wq = jnp.pad(w_q_up.reshape(ql, MLA_HEADS, hq), ((0, 0), (0, 0), (0, LANES - hq)))
    wq = wq.reshape(ql, MLA_HEADS * LANES).astype(BF16)
    wkv = w_kv_up.reshape(kl, MLA_HEADS, MLA_NOPE + MLA_V)
    wk = jnp.pad(wkv[:, :, :MLA_NOPE], ((0, 0), (0, 0), (0, LANES - MLA_NOPE)))
    wk = wk.reshape(kl, MLA_HEADS * LANES).astype(BF16)
    wv = wkv[:, :, MLA_NOPE:].reshape(kl, MLA_HEADS * MLA_V).astype(BF16)
    ropes = _rope_tables(n)
    nw = w.shape[1]
    full = lambda shape: pl.BlockSpec(shape, lambda g, t: (0,) * len(shape))
    blk = lambda width: pl.BlockSpec((None, tm, width), lambda g, t: (g, t, 0))
    widths = [MLA_HEADS * LANES, MLA_HEADS * LANES, MLA_HEADS * MLA_V, gq, gk, gk]
    return pl.pallas_call(
        _odd_in_body,
        grid=(G, n // tm),
        in_specs=[blk(d), pl.BlockSpec((None, 6, d), lambda g, t: (g, 0, 0)), full((1, d)), full((d, nw)),
                  full((1, ql)), full((ql, MLA_HEADS * LANES)), full((1, kl)), full((kl, MLA_HEADS * LANES)),
                  full((kl, MLA_HEADS * MLA_V)),
                  pl.BlockSpec((None, tm, 6 * LANES), lambda g, t: (g // batch, t, 0))],
        out_specs=[blk(wd) for wd in widths],
        out_shape=[jax.ShapeDtypeStruct((G, n, wd), BF16) for wd in widths],
        compiler_params=_cparams("arbitrary", "arbitrary"),
        name="odd_in",
    )(h_all, mod, g_pre, w, q_norm_g[None, :], wq, kv_norm_g[None, :], wk, wv, ropes)


def _mla_body(q_ref, kl_ref, kc_ref, vl_ref, vc_ref, o_ref):
    lane = lax.broadcasted_iota(jnp.int32, o_ref.shape, 1)
    outs = []
    for hd in range(2):
        sl = slice(hd * LANES, (hd + 1) * LANES)
        q = q_ref[:, sl]
        s_l = _dot_nt(q, kl_ref[:, sl])
        s_c = _dot_nt(q, kc_ref[:, sl])
        m = jnp.maximum(jnp.max(s_l, axis=-1, keepdims=True), jnp.max(s_c, axis=-1, keepdims=True))
        p_l = jnp.exp(s_l - m)
        p_c = jnp.exp(s_c - m)
        den = jnp.sum(p_l, axis=-1, keepdims=True) + jnp.sum(p_c, axis=-1, keepdims=True)
        pv = _dot(p_l.astype(BF16), vl_ref[...]) + _dot(p_c.astype(BF16), vc_ref[...])
        outs.append(pv / den)
    o_ref[...] = jnp.where(lane < MLA_V, outs[0], outs[1]).astype(BF16)


def _mla_attn(q, k, v, batch, n, ctx_len, tq):
    pairs = MLA_HEADS // 2
    return pl.pallas_call(
        _mla_body,
        grid=(batch, pairs, n // tq),
        in_specs=[pl.BlockSpec((None, tq, 2 * LANES), lambda b, h, i: (b, i, h)),
                  pl.BlockSpec((None, n, 2 * LANES), lambda b, h, i: (b, 0, h)),
                  pl.BlockSpec((None, ctx_len, 2 * LANES), lambda b, h, i: (batch, b, h)),
                  pl.BlockSpec((None, n, 2 * MLA_V), lambda b, h, i: (b, 0, h)),
                  pl.BlockSpec((None, ctx_len, 2 * MLA_V), lambda b, h, i: (batch, b, h))],
        out_specs=pl.BlockSpec((None, tq, 2 * MLA_V), lambda b, h, i: (b, i, h)),
        out_shape=jax.ShapeDtypeStruct((batch, n, MLA_HEADS * MLA_V), BF16),
        compiler_params=_cparams("arbitrary", "arbitrary", "arbitrary"),
        name="mla_attn",
    )(q, k, k, v, v)


def _dup_half(x, kv):
    lane = lax.broadcasted_iota(jnp.int32, x.shape, 1)
    swapped = pltpu.roll(x, HEAD_DIM, axis=1)
    lo = lane < HEAD_DIM
    return jnp.where(lo, x, swapped) if kv == 0 else jnp.where(lo, swapped, x)


def _win_body(sink_ref, q_ref, kp_ref, kc_ref, kn_ref, kx_ref, vp_ref, vc_ref, vn_ref, vx_ref, o_ref):
    i = pl.program_id(1)
    nb = pl.num_programs(1)
    w = WINDOW
    g = GQA_HEADS // GQA_KV
    ri = lax.broadcasted_iota(jnp.int32, (g * w, w), 0) % w
    ci = lax.broadcasted_iota(jnp.int32, (g * w, w), 1)
    ok_prev = (ci >= ri) & (i > 0)
    ok_next = (ci <= ri) & (i < nb - 1)
    lane = lax.broadcasted_iota(jnp.int32, (w, LANES), 1)
    lo = lane < HEAD_DIM
    neg = -jnp.inf
    for kv in range(GQA_KV):
        rows = []
        for hh in range(g):
            hd = kv * g + hh
            blk = q_ref[:, (hd // 2) * LANES:(hd // 2 + 1) * LANES].astype(F32)
            rows.append(jnp.where(lo if hd % 2 == 0 else ~lo, blk, 0.0).astype(BF16))
        q = jnp.concatenate(rows, axis=0)
        ks = [_dup_half(r[...].astype(F32), kv).astype(BF16) for r in (kp_ref, kc_ref, kn_ref, kx_ref)]
        vs = [_dup_half(r[...].astype(F32), kv).astype(BF16) for r in (vp_ref, vc_ref, vn_ref, vx_ref)]
        s_p = jnp.where(ok_prev, _dot_nt(q, ks[0]), neg)
        s_c = _dot_nt(q, ks[1])
        s_n = jnp.where(ok_next, _dot_nt(q, ks[2]), neg)
        s_x = _dot_nt(q, ks[3])
        hrow = lax.broadcasted_iota(jnp.int32, (g * w, 1), 0) // w
        sk = jnp.zeros((g * w, 1), F32)
        for hh in range(g):
            sk = jnp.where(hrow == hh, sink_ref[kv * g + hh], sk)
        m = jnp.maximum(jnp.maximum(jnp.max(s_p, axis=-1, keepdims=True), jnp.max(s_c, axis=-1, keepdims=True)),
                        jnp.maximum(jnp.max(s_n, axis=-1, keepdims=True), jnp.max(s_x, axis=-1, keepdims=True)))
        m = jnp.maximum(m, sk)
        ps = [jnp.exp(s - m) for s in (s_p, s_c, s_n, s_x)]
        den = jnp.exp(sk - m)
        for p in ps:
            den = den + jnp.sum(p, axis=-1, keepdims=True)
        acc = jnp.zeros((g * w, LANES), F32)
        for p, v in zip(ps, vs):
            acc = acc + _dot(p.astype(BF16), v)
        res = acc / den
        for pr in range(g // 2):
            blk = (kv * g) // 2 + pr
            o_ref[:, blk * LANES:(blk + 1) * LANES] = jnp.where(
                lo, res[(2 * pr) * w:(2 * pr + 1) * w, :], res[(2 * pr + 1) * w:(2 * pr + 2) * w, :]).astype(BF16)


def _win_attn(q, k, v, sink, batch, n, ctx_len):
    w = WINDOW
    nb = n // w
    gq, gk = GQA_HEADS * HEAD_DIM, GQA_KV * HEAD_DIM
    prev = lambda b, i: (b, jnp.maximum(i - 1, 0), 0)
    cur = lambda b, i: (b, i, 0)
    nxt = lambda b, i: (b, jnp.minimum(i + 1, nb - 1), 0)
    ctx = lambda b, i: (batch, b, 0)
    kspec = lambda rows, im: pl.BlockSpec((None, rows, gk), im)
    return pl.pallas_call(
        _win_body,
        grid=(batch, nb),
        in_specs=[pl.BlockSpec(memory_space=pltpu.SMEM),
                  pl.BlockSpec((None, w, gq), cur),
                  kspec(w, prev), kspec(w, cur), kspec(w, nxt), kspec(ctx_len, ctx),
                  kspec(w, prev), kspec(w, cur), kspec(w, nxt), kspec(ctx_len, ctx)],
        out_specs=pl.BlockSpec((None, w, gq), cur),
        out_shape=jax.ShapeDtypeStruct((batch, n, gq), BF16),
        compiler_params=_cparams("arbitrary", "arbitrary"),
        name="win_attn",
    )(sink, q, k, k, k, k, v, v, v, v)


def _route_body(h_ref, mod_ref, g_ref, r_ref, a_ref, info_ref, cnt_ref, carry_ref):
    first = (pl.program_id(0) == 0) & (pl.program_id(1) == 0)

    @pl.when(first)
    def _():
        carry_ref[...] = jnp.zeros_like(carry_ref)

    a = _norm_mod(h_ref[...], g_ref[...], mod_ref[3:4, :], mod_ref[4:5, :])
    a_ref[...] = a
    tm = a.shape[0]
    logits = jnp.dot(a, r_ref[...], precision=HIGHEST, preferred_element_type=F32)
    lane = lax.broadcasted_iota(jnp.int32, logits.shape, 1)
    neg = -jnp.inf
    logits = jnp.where(lane < N_EXPERTS, logits, neg)
    m1 = jnp.max(logits, axis=-1, keepdims=True)
    i1 = jnp.min(jnp.where(logits == m1, lane, LANES), axis=-1, keepdims=True)
    rest = jnp.where(lane == i1, neg, logits)
    m2 = jnp.max(rest, axis=-1, keepdims=True)
    i2 = jnp.min(jnp.where(rest == m2, lane, LANES), axis=-1, keepdims=True)
    e2 = jnp.exp(m2 - m1)
    w1 = 1.0 / (1.0 + e2)
    w2 = e2 / (1.0 + e2)
    chosen = ((lane == i1) | (lane == i2)).astype(F32)
    ri = lax.broadcasted_iota(jnp.int32, (tm, tm), 0)
    ci = lax.broadcasted_iota(jnp.int32, (tm, tm), 1)
    tri = (ci < ri).astype(BF16)
    before = _dot(tri, chosen.astype(BF16)) + carry_ref[...]
    p1 = jnp.sum(jnp.where(lane == i1, before, 0.0), axis=-1, keepdims=True)
    p2 = jnp.sum(jnp.where(lane == i2, before, 0.0), axis=-1, keepdims=True)
    carry_ref[...] += jnp.sum(chosen, axis=0, keepdims=True)
    cnt_ref[...] = carry_ref[...]
    vals = (i1.astype(F32), i2.astype(F32), w1, w2, p1, p2)
    info = jnp.zeros(logits.shape, F32)
    for j, v in enumerate(vals):
        info = jnp.where(lane == j, v, info)
    info_ref[...] = info


def _route(h_lat, mod, g_pre, router, tm):
    batch, n, d = h_lat.shape
    rp = jnp.pad(router, ((0, 0), (0, LANES - N_EXPERTS)))
    nt = n // tm
    return pl.pallas_call(
        _route_body,
        grid=(batch, nt),
        in_specs=[pl.BlockSpec((None, tm, d), lambda g, t: (g, t, 0)),
                  pl.BlockSpec((None, 6, d), lambda g, t: (g, 0, 0)),
                  pl.BlockSpec((1, d), lambda g, t: (0, 0)),
                  pl.BlockSpec((d, LANES), lambda g, t: (0, 0))],
        out_specs=[pl.BlockSpec((tm, d), lambda g, t: (g * nt + t, 0)),
                   pl.BlockSpec((tm, LANES), lambda g, t: (g * nt + t, 0)),
                   pl.BlockSpec((1, LANES), lambda g, t: (0, 0))],
        out_shape=[jax.ShapeDtypeStruct((batch * n, d), F32),
                   jax.ShapeDtypeStruct((batch * n, LANES), F32),
                   jax.ShapeDtypeStruct((1, LANES), F32)],
        scratch_shapes=[pltpu.VMEM((1, LANES), F32)],
        compiler_params=_cparams("arbitrary", "arbitrary"),
        name="moe_route",
    )(h_lat, mod, g_pre, rp)


def _row_copy(src_ref, src_row, dst_ref, dst_row, sem):
    return pltpu.make_async_copy(src_ref.at[pl.ds(src_row, 1), :], dst_ref.at[pl.ds(dst_row, 1), :], sem)


def _scatter_body(slot_ref, a_ref, xs_in_ref, xs_ref, sem, *, ts, n_tok):
    del xs_in_ref
    base = pl.program_id(0) * ts

    def issue(r, carry):
        for k in range(2):
            _row_copy(a_ref, r, xs_ref, slot_ref[k * n_tok + base + r], sem).start()
        return carry

    def drain(r, carry):
        _row_copy(a_ref, 0, xs_ref, 0, sem).wait()
        return carry

    lax.fori_loop(0, ts, issue, 0)
    lax.fori_loop(0, 2 * ts, drain, 0)


def _scatter(slots, a, n_slots, ts):
    n_tok, d = a.shape
    zeros = jnp.zeros((n_slots, d), F32)
    return pl.pallas_call(
        functools.partial(_scatter_body, ts=ts, n_tok=n_tok),
        grid_spec=pltpu.PrefetchScalarGridSpec(
            num_scalar_prefetch=1,
            grid=(n_tok // ts,),
            in_specs=[pl.BlockSpec((ts, d), lambda i, s: (i, 0)),
                      pl.BlockSpec(memory_space=pl.ANY)],
            out_specs=pl.BlockSpec(memory_space=pl.ANY),
            scratch_shapes=[pltpu.SemaphoreType.DMA(())]),
        out_shape=jax.ShapeDtypeStruct((n_slots, d), F32),
        input_output_aliases={2: 0},
        compiler_params=_cparams("arbitrary"),
        name="moe_scatter",
    )(slots, a, zeros)


def _experts_body(te_ref, tv_ref, x_ref, wg_ref, wu_ref, wd_ref, y_ref, acc_ref):
    del te_ref
    j = pl.program_id(0)
    f = pl.program_id(1)
    valid = tv_ref[j]

    @pl.when(f == 0)
    def _():
        acc_ref[...] = jnp.zeros_like(acc_ref)

    wg = wg_ref[...].astype(BF16)
    wu = wu_ref[...].astype(BF16)
    wd = wd_ref[...].astype(BF16)
    for sub in range(MOE_TILE // MOE_SUB):
        rows = slice(sub * MOE_SUB, (sub + 1) * MOE_SUB)

        @pl.when(sub * MOE_SUB < valid)
        def _(rows=rows):
            x = x_ref[rows, :].astype(BF16)
            hid = (_silu(_dot(x, wg)) * _dot(x, wu)).astype(BF16)
            acc_ref[rows, :] += _dot(hid, wd)

    @pl.when(f == pl.num_programs(1) - 1)
    def _():
        y_ref[...] = acc_ref[...]


def _experts(tile_expert, tile_valid, xs, wg, wu, wd):
    n_slots, d = xs.shape
    ff = wg.shape[-1]
    n_tiles = n_slots // MOE_TILE
    tf = MOE_FCHUNK
    return pl.pallas_call(
        _experts_body,
        grid_spec=pltpu.PrefetchScalarGridSpec(
            num_scalar_prefetch=2,
            grid=(n_tiles, ff // tf),
            in_specs=[pl.BlockSpec((MOE_TILE, d), lambda j, f, te, tv: (j, 0)),
                      pl.BlockSpec((None, d, tf), lambda j, f, te, tv: (te[j], 0, f)),
                      pl.BlockSpec((None, d, tf), lambda j, f, te, tv: (te[j], 0, f)),
                      pl.BlockSpec((None, tf, d), lambda j, f, te, tv: (te[j], f, 0))],
            out_specs=pl.BlockSpec((MOE_TILE, d), lambda j, f, te, tv: (j, 0)),
            scratch_shapes=[pltpu.VMEM((MOE_TILE, d), F32)]),
        out_shape=jax.ShapeDtypeStruct((n_slots, d), F32),
        compiler_params=_cparams("arbitrary", "arbitrary"),
        name="moe_experts",
    )(tile_expert, tile_valid, xs, wg, wu, wd)


def _combine_body(slot_ref, ys_ref, info_ref, h_ref, mod_ref, pg_ref, o_ref, buf_ref, sem, *, tc, n_tok):
    base = (pl.program_id(0) * pl.num_programs(1) + pl.program_id(1)) * tc

    def issue(r, carry):
        for k in range(2):
            _row_copy(ys_ref, slot_ref[k * n_tok + base + r], buf_ref.at[k], r, sem).start()
        return carry

    def drain(r, carry):
        _row_copy(ys_ref, 0, buf_ref.at[0], 0, sem).wait()
        return carry

    lax.fori_loop(0, tc, issue, 0)
    lax.fori_loop(0, 2 * tc, drain, 0)
    f = info_ref[:, 2:3] * buf_ref[0] + info_ref[:, 3:4] * buf_ref[1]
    o_ref[...] = h_ref[...] + mod_ref[5:6, :] * _rms(f, pg_ref[...])


def _combine(slots, ys, info, h_lat, mod, post_g, tc):
    batch, n, d = h_lat.shape
    nt = n // tc
    return pl.pallas_call(
        functools.partial(_combine_body, tc=tc, n_tok=batch * n),
        grid_spec=pltpu.PrefetchScalarGridSpec(
            num_scalar_prefetch=1,
            grid=(batch, nt),
            in_specs=[pl.BlockSpec(memory_space=pl.ANY),
                      pl.BlockSpec((tc, LANES), lambda g, t, s: (g * nt + t, 0)),
                      pl.BlockSpec((None, tc, d), lambda g, t, s: (g, t, 0)),
                      pl.BlockSpec((None, 6, d), lambda g, t, s: (g, 0, 0)),
                      pl.BlockSpec((1, d), lambda g, t, s: (0, 0))],
            out_specs=pl.BlockSpec((None, tc, d), lambda g, t, s: (g, t, 0)),
            scratch_shapes=[pltpu.VMEM((2, tc, d), F32), pltpu.SemaphoreType.DMA(())]),
        out_shape=jax.ShapeDtypeStruct((batch, n, d), F32),
        compiler_params=_cparams("arbitrary", "arbitrary"),
        name="moe_combine",
    )(slots, ys, info, h_lat, mod, post_g)


def _moe(h_lat, mod, g_pre, router, wg, wu, wd, post_g, tm):
    batch, n, d = h_lat.shape
    n_tok = batch * n
    a, info, counts = _route(h_lat, mod, g_pre, router, tm)
    cnt = counts[0, :N_EXPERTS].astype(jnp.int32)
    padded = ((cnt + MOE_TILE - 1) // MOE_TILE) * MOE_TILE
    ends = jnp.cumsum(padded)
    starts = ends - padded
    i1, i2 = info[:, 0].astype(jnp.int32), info[:, 1].astype(jnp.int32)
    p1, p2 = info[:, 4].astype(jnp.int32), info[:, 5].astype(jnp.int32)
    slots = jnp.concatenate([starts[i1] + p1, starts[i2] + p2])
    n_tiles = (2 * n_tok + N_EXPERTS * (MOE_TILE - 1)) // MOE_TILE
    tile_row = jnp.arange(n_tiles, dtype=jnp.int32) * MOE_TILE
    tile_expert = jnp.minimum(jnp.sum(tile_row[:, None] >= ends[None, :], axis=1), N_EXPERTS - 1).astype(jnp.int32)
    tile_valid = jnp.clip(cnt[tile_expert] - (tile_row - starts[tile_expert]), 0, MOE_TILE)
    tile_valid = jnp.where(tile_row < ends[-1], tile_valid, 0).astype(jnp.int32)
    xs = _scatter(slots, a, n_tiles * MOE_TILE, tm)
    ys = _experts(tile_expert, tile_valid, xs, wg, wu, wd)
    return _combine(slots, ys, info, h_lat, mod, post_g, tm)


def kernel(x, c, ctx, c_ctx, ada_w, ada_b, mix_pre_g, mix_post_g, ffn_pre_g, ffn_post_g, ev_w_in, ev_hy_conv_w, ev_hy_conv_b, ev_hy_filt_w1, ev_hy_filt_b1, ev_hy_filt_w2, ev_hy_filt_b2, ev_hy_filt_w3, ev_hy_freq, ev_hy_bias, ev_ssm_conv_w, ev_ssm_conv_b, ev_ssm_dt_bias, ev_ssm_a_log, ev_ssm_d, ev_ssm_norm_g, ev_w_out, ev_ffn_w_gate, ev_ffn_w_up, ev_ffn_w_down, od_w_in, od_mla_q_norm_g, od_mla_w_q_up, od_mla_kv_norm_g, od_mla_w_kv_up, od_gqa_sink, od_w_out, od_router, od_moe_w_gate, od_moe_w_up, od_moe_w_down):
    batch, n, d = x.shape
    ctx_len = ctx.shape[1]
    assert batch * ctx_len == n and ada_w.shape[0] == 2
    G = batch + 1
    tm = min(n, 512)

    cond = jnp.concatenate([c, c_ctx[None, :], jnp.zeros((16 - G, d), F32)], axis=0)
    mods = _ada(cond, ada_w, ada_b).reshape(2, 16, 6, d)
    h_all = jnp.concatenate([x, ctx.reshape(1, n, d)], axis=0)

    n_main = 3 * HY_CH + SSM_INNER + SSM_CONV_CH
    w_in = ev_w_in[0]
    w_main = w_in[:, :n_main].astype(BF16)
    w_dt = jnp.pad(w_in[:, n_main:], ((0, 0), (0, LANES - 2 * SSM_HEADS))).astype(BF16)
    proj, dt = _even_in(h_all, mods[0], mix_pre_g[0][None, :], w_main, w_dt, tm)
    y_hy_l, y_hy_c = _hyena(proj, batch, n, ctx_len,
                            (ev_hy_conv_w[0], ev_hy_conv_b[0], ev_hy_filt_w1[0], ev_hy_filt_b1[0], ev_hy_filt_w2[0],
                             ev_hy_filt_b2[0], ev_hy_filt_w3[0], ev_hy_freq[0], ev_hy_bias[0]))
    y_ss_l, y_ss_c = _ssd(proj, dt, batch, n, ctx_len,
                          (ev_ssm_conv_w[0], ev_ssm_conv_b[0], ev_ssm_dt_bias[0], ev_ssm_a_log[0], ev_ssm_d[0]))
    h_all = _mix_out_even(y_hy_l, y_hy_c, y_ss_l, y_ss_c, proj, ev_ssm_norm_g[0][None, :],
                          ev_w_out[0].astype(BF16), h_all, mods[0], mix_post_g[0][None, :], tm)
    h_all = _ffn(h_all, mods[0], ffn_pre_g[0][None, :], ev_ffn_w_gate[0].astype(BF16), ev_ffn_w_up[0].astype(BF16),
                 ev_ffn_w_down[0].astype(BF16), ffn_post_g[0][None, :], tm)

    q_m, k_m, v_m, q_w, k_w, v_w = _odd_in(h_all, mods[1], mix_pre_g[1][None, :], od_w_in[0], od_mla_q_norm_g[0],
                                           od_mla_w_q_up[0], od_mla_kv_norm_g[0], od_mla_w_kv_up[0], tm)
    o_mla = _mla_attn(q_m, k_m, v_m, batch, n, ctx_len, tm)
    o_win = _win_attn(q_w, k_w, v_w, od_gqa_sink[0], batch, n, ctx_len)
    h_lat = _mix_out_odd(o_mla, o_win, od_w_out[0].astype(BF16), h_all, mods[1], mix_post_g[1][None, :], tm)
    return _moe(h_lat, mods[1], ffn_pre_g[1][None, :], od_router[0], od_moe_w_gate[0], od_moe_w_up[0],
                od_moe_w_down[0], ffn_post_g[1][None, :], tm)
```

```python
import functools
import math

import jax
import jax.numpy as jnp
from jax import lax
from jax.experimental import pallas as pl
from jax.experimental.pallas import tpu as pltpu

F32 = jnp.float32
BF16 = jnp.bfloat16
HIGHEST = lax.Precision.HIGHEST

D_MODEL = 1024
GRID_W = 64
EPS = 1e-6
HEAD_DIM = 64
GROUP_WIDTH = D_MODEL // 2

HY_CH = GROUP_WIDTH
HY_BANDS = 16
HY_EMB = 2 * HY_BANDS + 1
HY_FILT_HID = 64
HY_FAST_DECAY = 0.3
HY_SLOW_DECAY = 1.5
HY_DECAY_TARGET = 1e-2

SSM_INNER = GROUP_WIDTH
SSM_HEADDIM = 64
SSM_HEADS = SSM_INNER // SSM_HEADDIM
SSM_GROUPS = 2
SSM_STATE = 128
SSM_CHUNK = 128
SSM_CONV_CH = SSM_INNER + 2 * SSM_GROUPS * SSM_STATE

MLA_HEADS = GROUP_WIDTH // HEAD_DIM
MLA_NOPE = 64
MLA_ROPE = 32
MLA_V = 64
MLA_Q_LORA = D_MODEL // 4
MLA_KV_LORA = D_MODEL // 8

GQA_HEADS = GROUP_WIDTH // HEAD_DIM
GQA_KV = GQA_HEADS // 4
WINDOW = 128
ROPE_BASE = 10000.0

D_FF = ((8 * D_MODEL // 3 + 127) // 128) * 128
N_EXPERTS = 8
D_FF_EXPERT = 7 * D_MODEL // 2

LANES = 128
VMEM_LIMIT = 56 * 1024 * 1024
MOE_TILE = 1024
MOE_SUB = 256
MOE_FCHUNK = 512


def _cparams(*sem):
    return pltpu.CompilerParams(dimension_semantics=sem, vmem_limit_bytes=VMEM_LIMIT)


def _silu(x):
    return x / (1.0 + jnp.exp(-x))


def _rms(x, g):
    return x * lax.rsqrt(jnp.mean(x * x, axis=-1, keepdims=True) + EPS) * g


def _norm_mod(h, g, shift, scale):
    return _rms(h, g) * (1.0 + scale) + shift


def _dot(a, b):
    return jnp.dot(a, b, preferred_element_type=F32)


def _dot_nt(a, b):
    return lax.dot_general(a, b, (((1,), (1,)), ((), ())), preferred_element_type=F32)


def _ada_body(c_ref, w_ref, b_ref, o_ref):
    s = _silu(c_ref[...]).astype(BF16)
    o_ref[0] = _dot(s, w_ref[0].astype(BF16)) + b_ref[0]


def _ada(cc, ada_w, ada_b):
    depth, d, n6 = ada_w.shape
    rows = cc.shape[0]
    tn = 1536
    return pl.pallas_call(
        _ada_body,
        grid=(depth, n6 // tn),
        in_specs=[pl.BlockSpec((rows, d), lambda l, j: (0, 0)),
                  pl.BlockSpec((1, d, tn), lambda l, j: (l, 0, j)),
                  pl.BlockSpec((1, 1, tn), lambda l, j: (l, 0, j))],
        out_specs=pl.BlockSpec((1, rows, tn), lambda l, j: (l, 0, j)),
        out_shape=jax.ShapeDtypeStruct((depth, rows, n6), F32),
        compiler_params=_cparams("arbitrary", "arbitrary"),
        name="ada",
    )(cc, ada_w, ada_b.reshape(depth, 1, n6))


def _lat_ctx_specs(batch, nt, tm, d):
    lat = pl.BlockSpec((None, tm, d), lambda g, t: (jnp.minimum(g, batch - 1), jnp.where(g < batch, t, nt - 1), 0))
    ctx = pl.BlockSpec((None, tm, d), lambda g, t: (0, jnp.where(g < batch, 0, t), 0))
    return lat, ctx


def _even_in_body(x_ref, c_ref, mod_ref, g_ref, w_ref, wdt_ref, o_ref, dt_ref, *, batch):
    h = jnp.where(pl.program_id(0) == batch, c_ref[...], x_ref[...])
    a = _norm_mod(h, g_ref[...], mod_ref[0:1, :], mod_ref[1:2, :]).astype(BF16)
    n_out = o_ref.shape[-1]
    for j in range(0, n_out, 512):
        o_ref[:, j:j + 512] = _dot(a, w_ref[:, j:j + 512]).astype(BF16)
    dt_ref[...] = _dot(a, wdt_ref[...])


def _even_in(x, ctx3, mod, g_pre, w_main, w_dt, tm):
    batch, n, d = x.shape
    G = batch + 1
    n_main = w_main.shape[1]
    lat_spec, ctx_spec = _lat_ctx_specs(batch, n // tm, tm, d)
    return pl.pallas_call(
        functools.partial(_even_in_body, batch=batch),
        grid=(G, n // tm),
        in_specs=[lat_spec, ctx_spec,
                  pl.BlockSpec((None, 6, d), lambda g, t: (g, 0, 0)),
                  pl.BlockSpec((1, d), lambda g, t: (0, 0)),
                  pl.BlockSpec((d, n_main), lambda g, t: (0, 0)),
                  pl.BlockSpec((d, LANES), lambda g, t: (0, 0))],
        out_specs=[pl.BlockSpec((None, tm, n_main), lambda g, t: (g, t, 0)),
                   pl.BlockSpec((None, tm, LANES), lambda g, t: (g, t, 0))],
        out_shape=[jax.ShapeDtypeStruct((G, n, n_main), BF16),
                   jax.ShapeDtypeStruct((G, n, LANES), F32)],
        compiler_params=_cparams("arbitrary", "arbitrary"),
        name="even_in",
    )(x, ctx3, mod, g_pre, w_main, w_dt)


def _dwconv3(x, w, b):
    L = x.shape[0]
    row = lax.broadcasted_iota(jnp.int32, x.shape, 0)
    prev = jnp.where(row >= 1, pltpu.roll(x, 1, axis=0), 0.0)
    nxt = jnp.where(row < L - 1, pltpu.roll(x, L - 1, axis=0), 0.0)
    return prev * w[0:1, :] + x * w[1:2, :] + nxt * w[2:3, :] + b


def _hy_pre_body(p_ref, cw_ref, cb_ref, u_ref, x0_ref):
    c = HY_CH
    parts = []
    for k in range(3):
        x = p_ref[:, k * c:(k + 1) * c].astype(F32)
        parts.append(_dwconv3(x, cw_ref[:, k * c:(k + 1) * c], cb_ref[:, k * c:(k + 1) * c]))
    x0_ref[...] = parts[0].astype(BF16)
    u_ref[...] = (parts[1] * parts[2]).astype(BF16)


def _hy_pre(proj, nseq, L, index_map, cw, cb):
    c3 = 3 * HY_CH
    return pl.pallas_call(
        _hy_pre_body,
        grid=(nseq,),
        in_specs=[pl.BlockSpec((None, L, c3), index_map),
                  pl.BlockSpec((3, c3), lambda s: (0, 0)),
                  pl.BlockSpec((1, c3), lambda s: (0, 0))],
        out_specs=[pl.BlockSpec((None, L, HY_CH), lambda s: (s, 0, 0)),
                   pl.BlockSpec((None, L, HY_CH), lambda s: (s, 0, 0))],
        out_shape=[jax.ShapeDtypeStruct((nseq, L, HY_CH), BF16),
                   jax.ShapeDtypeStruct((nseq, L, HY_CH), BF16)],
        compiler_params=_cparams("arbitrary"),
        name="hy_pre",
    )(proj, cw, cb)


def _hy_filter_body(z_ref, t_ref, w1_ref, b1_ref, w2_ref, b2_ref, w3_ref, f_ref, dl_ref, o_ref):
    f = f_ref[...]
    h = jnp.sin(f * (jnp.dot(z_ref[...], w1_ref[...], precision=HIGHEST, preferred_element_type=F32) + b1_ref[...]))
    h = jnp.sin(f * (jnp.dot(h, w2_ref[...], precision=HIGHEST, preferred_element_type=F32) + b2_ref[...]))
    h = jnp.dot(h, w3_ref[...], precision=HIGHEST, preferred_element_type=F32)
    decay = jnp.exp(-t_ref[...] * dl_ref[...])
    hf = h[:, :HY_CH] * decay
    hb = h[:, HY_CH:] * decay
    row = lax.broadcasted_iota(jnp.int32, hb.shape, 0)
    hb = jnp.where(row == 0, 0.0, hb)
    o_ref[:, :HY_CH] = hf.astype(BF16)
    o_ref[:, HY_CH:] = hb.astype(BF16)


def _hy_filter(L, w1, b1, w2, b2, w3, freq):
    pos = jnp.arange(L, dtype=F32)
    t = pos / (L - 1)
    omega = 2.0 * math.pi * pos / L
    bands = jnp.linspace(1e-4, HY_BANDS - 1, HY_BANDS, dtype=F32)
    z = jnp.concatenate([t[:, None], jnp.cos(omega[:, None] * bands), -jnp.sin(omega[:, None] * bands)], axis=-1)
    z = jnp.pad(z, ((0, 0), (0, LANES - HY_EMB)))
    w1p = jnp.pad(w1, ((0, LANES - HY_EMB), (0, 0)))
    deltas = jnp.abs(jnp.linspace(math.log(HY_DECAY_TARGET) / HY_SLOW_DECAY,
                                  math.log(HY_DECAY_TARGET) / HY_FAST_DECAY, HY_CH, dtype=F32))
    return pl.pallas_call(
        _hy_filter_body,
        out_shape=jax.ShapeDtypeStruct((L, 2 * HY_CH), BF16),
        compiler_params=pltpu.CompilerParams(vmem_limit_bytes=VMEM_LIMIT),
        name="hy_filter",
    )(z, t[:, None], w1p, b1[None, :], w2, b2[None, :], w3, freq[None, :], deltas[None, :])


def _dft_tables(L):
    split = 64
    t = jnp.arange(L, dtype=jnp.int32)[None, :]
    k1 = jnp.arange(L // split, dtype=jnp.int32)[:, None] * split
    k0 = jnp.arange(split, dtype=jnp.int32)[:, None]
    ang1 = ((k1 * t) % (2 * L)).astype(F32) * (math.pi / L)
    ang0 = ((k0 * t) % (2 * L)).astype(F32) * (math.pi / L)
    c1, s1 = jnp.cos(ang1)[:, None, :], jnp.sin(ang1)[:, None, :]
    c0, s0 = jnp.cos(ang0)[None, :, :], jnp.sin(ang0)[None, :, :]
    fre = (c1 * c0 - s1 * s0).reshape(L, L)
    nsin = -(s1 * c0 + c1 * s0).reshape(L, L)
    k = jnp.arange(L, dtype=jnp.int32)[:, None]
    alt_t = jnp.where(t % 2 == 0, 1.0, -1.0).astype(F32)
    alt_k = jnp.where(k % 2 == 0, 1.0, -1.0).astype(F32)
    f2 = jnp.stack([fre, jnp.where(k == 0, alt_t, nsin)]).astype(BF16)
    fimt = jnp.where(t == 0, alt_k, nsin).astype(BF16)
    return f2, fimt


def _filt_dft_body(f_ref, x_ref, o_ref):
    o_ref[...] = _dot(f_ref[...], x_ref[...])


def _filt_dft(f2, hfb):
    _, L, _ = f2.shape
    cols = hfb.shape[1]
    tr = min(L, 1024)
    tc = 512
    return pl.pallas_call(
        _filt_dft_body,
        grid=(2, L // tr, cols // tc),
        in_specs=[pl.BlockSpec((None, tr, L), lambda p, i, j: (p, i, 0)),
                  pl.BlockSpec((L, tc), lambda p, i, j: (0, j))],
        out_specs=pl.BlockSpec((None, tr, tc), lambda p, i, j: (p, i, j)),
        out_shape=jax.ShapeDtypeStruct((2, L, cols), F32),
        compiler_params=_cparams("arbitrary", "arbitrary", "arbitrary"),
        name="filt_dft",
    )(f2, hfb)


def _hy_conv_body(u_ref, x0_ref, f2_ref, fret_ref, fimt_ref, s_ref, bias_ref, o_ref, acc_ref, *, L, fc):
    k = pl.program_id(1)
    c = HY_CH

    @pl.when(k == 0)
    def _():
        acc_ref[...] = jnp.zeros_like(acc_ref)

    u = u_ref[...]
    ure = _dot(f2_ref[0], u)
    uim = _dot(f2_ref[1], u)
    srf = s_ref[0, :, :c]
    srb = s_ref[0, :, c:]
    sif = s_ref[1, :, :c]
    sib = s_ref[1, :, c:]
    is0 = (lax.broadcasted_iota(jnp.int32, (fc, c), 0) + k * fc) == 0
    wgt = jnp.where(is0, 1.0 / (2 * L), 1.0 / L)
    hre = srf + srb
    a_m = hre * wgt
    b_m = jnp.where(is0, 0.0, sif - sib) * wgt
    d_m = jnp.where(is0, sif + sib, hre) * wgt
    yre = (ure * a_m - uim * b_m).astype(BF16)
    yim = (ure * b_m + uim * d_m).astype(BF16)
    acc_ref[...] += _dot(fret_ref[...], yre) + _dot(fimt_ref[...], yim)

    @pl.when(k == pl.num_programs(1) - 1)
    def _():
        y = acc_ref[...] + u.astype(F32) * bias_ref[...]
        o_ref[...] = (x0_ref[...].astype(F32) * y).astype(BF16)


def _hy_conv(u, x0, f2, fimt, s, bias):
    nseq, L, c = u.shape
    fc = min(L, 512)
    return pl.pallas_call(
        functools.partial(_hy_conv_body, L=L, fc=fc),
        grid=(nseq, L // fc),
        in_specs=[pl.BlockSpec((None, L, c), lambda s_, k: (s_, 0, 0)),
                  pl.BlockSpec((None, L, c), lambda s_, k: (s_, 0, 0)),
                  pl.BlockSpec((2, fc, L), lambda s_, k: (0, k, 0)),
                  pl.BlockSpec((None, L, fc), lambda s_, k: (0, 0, k)),
                  pl.BlockSpec((L, fc), lambda s_, k: (0, k)),
                  pl.BlockSpec((2, fc, 2 * c), lambda s_, k: (0, k, 0)),
                  pl.BlockSpec((1, c), lambda s_, k: (0, 0))],
        out_specs=pl.BlockSpec((None, L, c), lambda s_, k: (s_, 0, 0)),
        out_shape=jax.ShapeDtypeStruct((nseq, L, c), BF16),
        scratch_shapes=[pltpu.VMEM((L, c), F32)],
        compiler_params=_cparams("arbitrary", "arbitrary"),
        name="hy_conv",
    )(u, x0, f2, f2, fimt, s, bias)


def _hyena(proj, batch, n, ctx_len, p):
    cw, cb, w1, b1, w2, b2, w3, freq, bias = p
    outs = []
    for L, imap in ((n, lambda s: (s, 0, 0)), (ctx_len, lambda s: (batch, s, 0))):
        u, x0 = _hy_pre(proj, batch, L, imap, cw, cb[None, :])
        hfb = _hy_filter(L, w1, b1, w2, b2, w3, freq)
        f2, fimt = _dft_tables(L)
        s = _filt_dft(f2, hfb)
        outs.append(_hy_conv(u, x0, f2, fimt, s, bias[None, :]))
    return outs


def _softplus(x):
    return jnp.maximum(x, 0.0) + jnp.log(1.0 + jnp.exp(-jnp.abs(x)))


def _chunk_cumsum(a, axis, reverse):
    q = SSM_CHUNK
    size = a.shape[axis]
    pos = lax.broadcasted_iota(jnp.int32, a.shape, axis) % q
    s = 1
    while s < q:
        if reverse:
            a = a + jnp.where(pos < q - s, pltpu.roll(a, size - s, axis=axis), 0.0)
        else:
            a = a + jnp.where(pos >= s, pltpu.roll(a, s, axis=axis), 0.0)
        s *= 2
    return a


def _ssd_prepare(L, xbc_ref, dtc_ref, dtr_ref, cw_ref, cb_ref, dtbc_ref, alc_ref, dtbr_ref, alr_ref,
                 x_s, c_s, bt_s, csc_s, csr_s, dtr_s):
    nc = L // SSM_CHUNK
    nh = SSM_HEADS
    inner = SSM_INNER
    gs = SSM_GROUPS * SSM_STATE
    xs = _silu(_dwconv3(xbc_ref[:, :inner].astype(F32), cw_ref[:, :inner], cb_ref[:, :inner]))
    x_s[0:L, :] = xs.astype(BF16)
    bm = _silu(_dwconv3(xbc_ref[:, inner:inner + gs].astype(F32), cw_ref[:, inner:inner + gs],
                        cb_ref[:, inner:inner + gs]))
    for c in range(nc):
        bt_s[c] = bm[c * SSM_CHUNK:(c + 1) * SSM_CHUNK, :].T.astype(BF16)
    cm = _silu(_dwconv3(xbc_ref[:, inner + gs:].astype(F32), cw_ref[:, inner + gs:], cb_ref[:, inner + gs:]))
    c_s[0:L, :] = cm.astype(BF16)
    a_col = _softplus(dtc_ref[...] + dtbc_ref[...]) * (-jnp.exp(alc_ref[...]))
    lane = lax.broadcasted_iota(jnp.int32, a_col.shape, 1)
    csc_s[0:L, :] = jnp.where(lane < nh, _chunk_cumsum(a_col, 0, False), _chunk_cumsum(a_col, 0, True))
    dt_row = _softplus(dtr_ref[...] + dtbr_ref[...])
    a_row = (dt_row * (-jnp.exp(alr_ref[...]))).reshape(2 * nh * nc, SSM_CHUNK)
    rown = lax.broadcasted_iota(jnp.int32, a_row.shape, 0)
    csr_s[0:2 * nh * nc, :] = jnp.where(rown < nh * nc, _chunk_cumsum(a_row, 1, False),
                                        _chunk_cumsum(a_row, 1, True))
    dtr_s[0:2 * nh * nc, :] = dt_row.reshape(2 * nh * nc, SSM_CHUNK)


def _ssd_chunk(c, d, nc, first, x_s, c_s, bt_s, csc_s, csr_s, dtr_s, st_s, y_s):
    q = SSM_CHUNK
    ns = SSM_STATE
    hp = SSM_HEADDIM
    hpg = SSM_HEADS // SSM_GROUPS
    gw = hpg * hp
    r0 = pl.multiple_of(c * q, q)
    ri = lax.broadcasted_iota(jnp.int32, (q, q), 0)
    ci = lax.broadcasted_iota(jnp.int32, (q, q), 1)
    keep = (ci <= ri) if d == 0 else (ci >= ri)
    lane_blk = lax.broadcasted_iota(jnp.int32, (q, gw), 1) // hp
    cc = c_s[pl.ds(r0, q), :]
    xc = x_s[pl.ds(r0, q), :]
    btc = bt_s[c]
    ys = []
    for g in range(SSM_GROUPS):
        cg = cc[:, g * ns:(g + 1) * ns]
        btg = btc[g * ns:(g + 1) * ns, :]
        xg = xc[:, g * gw:(g + 1) * gw]
        s_cb = _dot(cg, btg)
        btg32 = btg.astype(F32)
        m_rows, w_rows, e_cols, tots = [], [], [], []
        for hh in range(hpg):
            idx = d * SSM_HEADS + g * hpg + hh
            csc = csc_s[pl.ds(r0, q), idx:idx + 1]
            csr = csr_s[pl.ds(idx * nc + c, 1), :]
            dtr = dtr_s[pl.ds(idx * nc + c, 1), :]
            lmat = jnp.where(keep, jnp.exp(csc - csr), 0.0)
            m_rows.append((s_cb * lmat * dtr).astype(BF16))
            tot = csr[:, q - 1:q] if d == 0 else csr[:, 0:1]
            w_rows.append((btg32 * (jnp.exp(tot - csr) * dtr)).astype(BF16))
            e_cols.append(jnp.exp(csc))
            tots.append(jnp.exp(tot))
        yd = _dot(jnp.concatenate(m_rows, axis=0), xg)
        ds = _dot(jnp.concatenate(w_rows, axis=0), xg)
        st = st_s[d, g]
        yo = _dot(cg, st.astype(BF16))
        y_g = jnp.zeros((q, gw), F32)
        st_new = jnp.zeros((ns, gw), F32)
        for hh in range(hpg):
            sel = lane_blk == hh
            y_g = jnp.where(sel, yd[hh * q:(hh + 1) * q, :] + yo * e_cols[hh], y_g)
            st_new = jnp.where(sel, st * tots[hh] + ds[hh * ns:(hh + 1) * ns, :], st_new)
        st_s[d, g] = st_new
        ys.append(y_g)
    y = jnp.concatenate(ys, axis=1)
    if first:
        y_s[pl.ds(r0, q), :] = y
    else:
        y_s[pl.ds(r0, q), :] += y


def _ssd_body(xl_ref, xc_ref, dcl_ref, dcc_ref, drl_ref, drc_ref, cw_ref, cb_ref, dtbc_ref, alc_ref,
              dtbr_ref, alr_ref, dsk_ref, yl_ref, yc_ref,
              x_s, c_s, bt_s, csc_s, csr_s, dtr_s, st_s, y_s, *, n, ctx_len):
    st_s[...] = jnp.zeros_like(st_s)
    for L, xbc_ref, dtc_ref, dtr_ref, o_ref in ((ctx_len, xc_ref, dcc_ref, drc_ref, yc_ref),
                                                 (n, xl_ref, dcl_ref, drl_ref, yl_ref)):
        nc = L // SSM_CHUNK
        _ssd_prepare(L, xbc_ref, dtc_ref, dtr_ref, cw_ref, cb_ref, dtbc_ref, alc_ref, dtbr_ref, alr_ref,
                     x_s, c_s, bt_s, csc_s, csr_s, dtr_s)
        args = (x_s, c_s, bt_s, csc_s, csr_s, dtr_s, st_s, y_s)

        def fwd(c, carry, nc=nc, args=args):
            _ssd_chunk(c, 0, nc, True, *args)
            return carry

        def bwd(i, carry, nc=nc, args=args):
            _ssd_chunk(nc - 1 - i, 1, nc, False, *args)
            return carry

        lax.fori_loop(0, nc, fwd, 0)
        lax.fori_loop(0, nc, bwd, 0)
        o_ref[...] = (y_s[0:L, :] + x_s[0:L, :].astype(F32) * dsk_ref[...]).astype(BF16)


def _ssd(proj, dt, batch, n, ctx_len, p):
    conv_w, conv_b, dt_bias, a_log, d_skip = p
    nh = SSM_HEADS
    ncl, ncc = n // SSM_CHUNK, ctx_len // SSM_CHUNK
    cch = SSM_CONV_CH
    xbc_blk = (SSM_INNER * 3 + SSM_INNER) // cch
    dt16 = dt[:, :, :2 * nh]
    dtr_l = dt16[:batch].transpose(0, 2, 1).reshape(batch, 2 * nh, ncl, SSM_CHUNK)
    dtr_c = dt16[batch].reshape(batch, ctx_len, 2 * nh).transpose(0, 2, 1).reshape(batch, 2 * nh, ncc, SSM_CHUNK)
    pad = LANES - 2 * nh
    dtb_col = jnp.pad(dt_bias.reshape(1, 2 * nh), ((0, 0), (0, pad)))
    al_col = jnp.pad(a_log.reshape(1, 2 * nh), ((0, 0), (0, pad)))
    dtb_row = jnp.broadcast_to(dt_bias.reshape(2 * nh, 1, 1), (2 * nh, 1, SSM_CHUNK))
    al_row = jnp.broadcast_to(a_log.reshape(2 * nh, 1, 1), (2 * nh, 1, SSM_CHUNK))
    dsk = jnp.repeat(d_skip, SSM_HEADDIM)[None, :]
    full = lambda shape: pl.BlockSpec(shape, lambda b: (0,) * len(shape))
    return pl.pallas_call(
        functools.partial(_ssd_body, n=n, ctx_len=ctx_len),
        grid=(batch,),
        in_specs=[pl.BlockSpec((None, n, cch), lambda b: (b, 0, xbc_blk)),
                  pl.BlockSpec((None, ctx_len, cch), lambda b: (batch, b, xbc_blk)),
                  pl.BlockSpec((None, n, LANES), lambda b: (b, 0, 0)),
                  pl.BlockSpec((None, ctx_len, LANES), lambda b: (batch, b, 0)),
                  pl.BlockSpec((None, 2 * nh, ncl, SSM_CHUNK), lambda b: (b, 0, 0, 0)),
                  pl.BlockSpec((None, 2 * nh, ncc, SSM_CHUNK), lambda b: (b, 0, 0, 0)),
                  full((3, cch)), full((1, cch)), full((1, LANES)), full((1, LANES)),
                  full((2 * nh, 1, SSM_CHUNK)), full((2 * nh, 1, SSM_CHUNK)), full((1, SSM_INNER))],
        out_specs=[pl.BlockSpec((None, n, SSM_INNER), lambda b: (b, 0, 0)),
                   pl.BlockSpec((None, ctx_len, SSM_INNER), lambda b: (b, 0, 0))],
        out_shape=[jax.ShapeDtypeStruct((batch, n, SSM_INNER), BF16),
                   jax.ShapeDtypeStruct((batch, ctx_len, SSM_INNER), BF16)],
        scratch_shapes=[pltpu.VMEM((n, SSM_INNER), BF16),
                        pltpu.VMEM((n, SSM_GROUPS * SSM_STATE), BF16),
                        pltpu.VMEM((ncl, SSM_GROUPS * SSM_STATE, SSM_CHUNK), BF16),
                        pltpu.VMEM((n, LANES), F32),
                        pltpu.VMEM((2 * nh * ncl, SSM_CHUNK), F32),
                        pltpu.VMEM((2 * nh * ncl, SSM_CHUNK), F32),
                        pltpu.VMEM((2, SSM_GROUPS, SSM_STATE, SSM_INNER // SSM_GROUPS), F32),
                        pltpu.VMEM((n, SSM_INNER), F32)],
        compiler_params=_cparams("arbitrary"),
        name="ssd",
    )(proj, proj, dt, dt, dtr_l, dtr_c, conv_w, conv_b[None, :], dtb_col, al_col, dtb_row, al_row, dsk)


def _mix_out_even_body(yhl_ref, yhc_ref, ysl_ref, ysc_ref, z_ref, ng_ref, w_ref, x_ref, c_ref, mod_ref, pg_ref, o_ref,
                       *, batch):
    is_ctx = pl.program_id(0) == batch
    h = jnp.where(is_ctx, c_ref[...], x_ref[...])
    y_hy = jnp.where(is_ctx, yhc_ref[...], yhl_ref[...])
    ys = jnp.where(is_ctx, ysc_ref[...], ysl_ref[...]).astype(F32) * _silu(z_ref[...].astype(F32))
    gw = SSM_INNER // SSM_GROUPS
    parts = []
    for g in range(SSM_GROUPS):
        yg = ys[:, g * gw:(g + 1) * gw]
        parts.append(yg * lax.rsqrt(jnp.mean(yg * yg, axis=-1, keepdims=True) + EPS))
    y_ssm = (jnp.concatenate(parts, axis=1) * ng_ref[...]).astype(BF16)
    m = _dot(y_hy, w_ref[0:HY_CH, :]) + _dot(y_ssm, w_ref[HY_CH:, :])
    o_ref[...] = h + mod_ref[2:3, :] * _rms(m, pg_ref[...])


def _mix_out_even(y_hy_l, y_hy_c, y_ss_l, y_ss_c, proj, norm_g, w_out, x, ctx3, mod, post_g, tm):
    batch, n, d = x.shape
    G = batch + 1
    c = HY_CH
    z_blk = (3 * HY_CH) // SSM_INNER
    lat_c, ctx_c = _lat_ctx_specs(batch, n // tm, tm, c)
    lat_d, ctx_d = _lat_ctx_specs(batch, n // tm, tm, d)
    return pl.pallas_call(
        functools.partial(_mix_out_even_body, batch=batch),
        grid=(G, n // tm),
        in_specs=[lat_c, ctx_c, lat_c, ctx_c,
                  pl.BlockSpec((None, tm, SSM_INNER), lambda g, t: (g, t, z_blk)),
                  pl.BlockSpec((1, SSM_INNER), lambda g, t: (0, 0)),
                  pl.BlockSpec((2 * c, d), lambda g, t: (0, 0)),
                  lat_d, ctx_d,
                  pl.BlockSpec((None, 6, d), lambda g, t: (g, 0, 0)),
                  pl.BlockSpec((1, d), lambda g, t: (0, 0))],
        out_specs=pl.BlockSpec((None, tm, d), lambda g, t: (g, t, 0)),
        out_shape=jax.ShapeDtypeStruct((G, n, d), F32),
        compiler_params=_cparams("arbitrary", "arbitrary"),
        name="mix_out_even",
    )(y_hy_l, y_hy_c.reshape(1, n, c), y_ss_l, y_ss_c.reshape(1, n, c), proj, norm_g, w_out, x, ctx3, mod, post_g)


def _mix_out_odd_body(a_ref, b_ref, w_ref, h_ref, mod_ref, pg_ref, o_ref):
    half = a_ref.shape[-1]
    m = _dot(a_ref[...], w_ref[0:half, :]) + _dot(b_ref[...], w_ref[half:, :])
    o_ref[...] = h_ref[...] + mod_ref[2:3, :] * _rms(m, pg_ref[...])


def _mix_out_odd(o_a, o_b, w_out, h_all, mod, post_g, tm):
    batch, n, c = o_a.shape
    d = h_all.shape[-1]
    return pl.pallas_call(
        _mix_out_odd_body,
        grid=(batch, n // tm),
        in_specs=[pl.BlockSpec((None, tm, c), lambda g, t: (g, t, 0)),
                  pl.BlockSpec((None, tm, c), lambda g, t: (g, t, 0)),
                  pl.BlockSpec((2 * c, d), lambda g, t: (0, 0)),
                  pl.BlockSpec((None, tm, d), lambda g, t: (g, t, 0)),
                  pl.BlockSpec((None, 6, d), lambda g, t: (g, 0, 0)),
                  pl.BlockSpec((1, d), lambda g, t: (0, 0))],
        out_specs=pl.BlockSpec((None, tm, d), lambda g, t: (g, t, 0)),
        out_shape=jax.ShapeDtypeStruct((batch, n, d), F32),
        compiler_params=_cparams("arbitrary", "arbitrary"),
        name="mix_out_odd",
    )(o_a, o_b, w_out, h_all, mod, post_g)


def _ffn_body(h_ref, mod_ref, g_ref, wg_ref, wu_ref, wd_ref, pg_ref, o_ref, *, fchunk):
    h = h_ref[...]
    a = _norm_mod(h, g_ref[...], mod_ref[3:4, :], mod_ref[4:5, :]).astype(BF16)
    ff = wg_ref.shape[1]
    f = jnp.zeros(h.shape, F32)
    for j in range(0, ff, fchunk):
        gate = _dot(a, wg_ref[:, j:j + fchunk])
        up = _dot(a, wu_ref[:, j:j + fchunk])
        f = f + _dot((_silu(gate) * up).astype(BF16), wd_ref[j:j + fchunk, :])
    o_ref[...] = h + mod_ref[5:6, :] * _rms(f, pg_ref[...])


def _ffn(h_all, mod, g_pre, wg, wu, wd, post_g, tm):
    G, n, d = h_all.shape
    ff = wg.shape[1]
    return pl.pallas_call(
        functools.partial(_ffn_body, fchunk=256),
        grid=(G, n // tm),
        in_specs=[pl.BlockSpec((None, tm, d), lambda g, t: (g, t, 0)),
                  pl.BlockSpec((None, 6, d), lambda g, t: (g, 0, 0)),
                  pl.BlockSpec((1, d), lambda g, t: (0, 0)),
                  pl.BlockSpec((d, ff), lambda g, t: (0, 0), pipeline_mode=pl.Buffered(1)),
                  pl.BlockSpec((d, ff), lambda g, t: (0, 0), pipeline_mode=pl.Buffered(1)),
                  pl.BlockSpec((ff, d), lambda g, t: (0, 0), pipeline_mode=pl.Buffered(1)),
                  pl.BlockSpec((1, d), lambda g, t: (0, 0))],
        out_specs=pl.BlockSpec((None, tm, d), lambda g, t: (g, t, 0)),
        out_shape=jax.ShapeDtypeStruct((G, n, d), F32),
        compiler_params=_cparams("arbitrary", "arbitrary"),
        name="ffn",
    )(h_all, mod, g_pre, wg, wu, wd, post_g)


def _rope(x, cos, sin_a, sin_b, half):
    return x * cos + pltpu.roll(x, LANES - half, axis=1) * sin_a + pltpu.roll(x, half, axis=1) * sin_b


def _odd_in_body(h_ref, mod_ref, g_ref, w_ref, qg_ref, wq_ref, kg_ref, wk_ref, wv_ref, rp_ref,
                 qm_ref, km_ref, vm_ref, qw_ref, kw_ref, vw_ref):
    a = _norm_mod(h_ref[...], g_ref[...], mod_ref[0:1, :], mod_ref[1:2, :]).astype(BF16)
    ql, kl = MLA_Q_LORA, MLA_KV_LORA
    o_qg = ql + kl
    o_kg = o_qg + GQA_HEADS * HEAD_DIM
    o_vg = o_kg + GQA_KV * HEAD_DIM
    o_kr = o_vg + GQA_KV * HEAD_DIM
    cm, s1m, s2m = rp_ref[:, 0:LANES], rp_ref[:, LANES:2 * LANES], rp_ref[:, 2 * LANES:3 * LANES]
    cg, s1g, s2g = rp_ref[:, 3 * LANES:4 * LANES], rp_ref[:, 4 * LANES:5 * LANES], rp_ref[:, 5 * LANES:6 * LANES]
    mla_scale = (MLA_NOPE + MLA_ROPE) ** -0.5
    gqa_scale = HEAD_DIM ** -0.5
    q_lat = _rms(_dot(a, w_ref[:, 0:ql]), qg_ref[...]).astype(BF16)
    q = _dot(q_lat, wq_ref[...])
    kv_lat = _rms(_dot(a, w_ref[:, ql:o_qg]), kg_ref[...]).astype(BF16)
    k = _dot(kv_lat, wk_ref[...])
    vm_ref[...] = _dot(kv_lat, wv_ref[...]).astype(BF16)
    k_rope = _rope(_dot(a, w_ref[:, o_kr:o_kr + LANES]), cm, s1m, s2m, MLA_ROPE // 2)
    for hd in range(MLA_HEADS):
        sl = slice(hd * LANES, (hd + 1) * LANES)
        qm_ref[:, sl] = (_rope(q[:, sl], cm, s1m, s2m, MLA_ROPE // 2) * mla_scale).astype(BF16)
        km_ref[:, sl] = (k[:, sl] + k_rope).astype(BF16)
    qg = _dot(a, w_ref[:, o_qg:o_kg])
    for blk in range(GQA_HEADS * HEAD_DIM // LANES):
        sl = slice(blk * LANES, (blk + 1) * LANES)
        qw_ref[:, sl] = (_rope(qg[:, sl], cg, s1g, s2g, HEAD_DIM // 2) * gqa_scale).astype(BF16)
    kw_ref[...] = _rope(_dot(a, w_ref[:, o_kg:o_vg]), cg, s1g, s2g, HEAD_DIM // 2).astype(BF16)
    vw_ref[...] = _dot(a, w_ref[:, o_vg:o_kr]).astype(BF16)


def _rope_tables(n):
    rows = (jnp.arange(n) // GRID_W).astype(F32)
    cols = (jnp.arange(n) % GRID_W).astype(F32)

    def cs(rot_dim):
        nf = rot_dim // 4
        inv = ROPE_BASE ** (-jnp.arange(nf, dtype=F32) / nf)
        ang = jnp.concatenate([rows[:, None] * inv, cols[:, None] * inv], axis=-1)
        return jnp.cos(ang), jnp.sin(ang)

    one = jnp.ones((n, 1), F32)
    zero = jnp.zeros((n, 1), F32)
    c, s = cs(MLA_ROPE)
    hm = MLA_ROPE // 2
    pad_n, pad_t = MLA_NOPE, LANES - MLA_NOPE - MLA_ROPE
    cm = jnp.concatenate([jnp.tile(one, (1, pad_n)), c, c, jnp.tile(one, (1, pad_t))], axis=1)
    s1m = jnp.concatenate([jnp.tile(zero, (1, pad_n)), -s, jnp.tile(zero, (1, hm + pad_t))], axis=1)
    s2m = jnp.concatenate([jnp.tile(zero, (1, pad_n + hm)), s, jnp.tile(zero, (1, pad_t))], axis=1)
    c, s = cs(HEAD_DIM)
    z32 = jnp.zeros_like(s)
    cg = jnp.concatenate([c, c, c, c], axis=1)
    s1g = jnp.concatenate([-s, z32, -s, z32], axis=1)
    s2g = jnp.concatenate([z32, s, z32, s], axis=1)
    lat = jnp.concatenate([cm, s1m, s2m, cg, s1g, s2g], axis=1)
    ident = jnp.concatenate([jnp.ones((n, LANES), F32), jnp.zeros((n, 2 * LANES), F32)] * 2, axis=1)
    return jnp.stack([lat, ident])


def _odd_in(h_all, mod, g_pre, w_in, q_norm_g, w_q_up, kv_norm_g, w_kv_up, tm):
    G, n, d = h_all.shape
    batch = G - 1
    ql, kl, rp = MLA_Q_LORA, MLA_KV_LORA, MLA_ROPE
    gq, gk = GQA_HEADS * HEAD_DIM, GQA_KV * HEAD_DIM
    o = [0, ql, ql + kl, ql + kl + rp, ql + kl + rp + gq, ql + kl + rp + gq + gk]
    kr_cols = jnp.pad(w_in[:, o[2]:o[3]], ((0, 0), (MLA_NOPE, LANES - MLA_NOPE - rp)))
    w = jnp.concatenate([w_in[:, o[0]:o[2]], w_in[:, o[3]:], kr_cols], axis=1).astype(BF16)
    hq = MLA_NOPE + MLA_ROPE
    wq = jnp.pad(w_q_up.reshape(ql, MLA_HEADS, hq), ((0, 0), (0, 0), (0, LANES - hq)))
    wq = wq.reshape(ql, MLA_HEADS * LANES).astype(BF16)
    wkv = w_kv_up.reshape(kl, MLA_HEADS, MLA_NOPE + MLA_V)
    wk = jnp.pad(wkv[:, :, :MLA_NOPE], ((0, 0), (0, 0), (0, LANES - MLA_NOPE)))
    wk = wk.reshape(kl, MLA_HEADS * LANES).astype(BF16)
    wv = wkv[:, :, MLA_NOPE:].reshape(kl, MLA_HEADS * MLA_V).astype(BF16)
    ropes = _rope_tables(n)
    nw = w.shape[1]
    full = lambda shape: pl.BlockSpec(shape, lambda g, t: (0,) * len(shape))
    blk = lambda width: pl.BlockSpec((None, tm, width), lambda g, t: (g, t, 0))
    widths = [MLA_HEADS * LANES, MLA_HEADS * LANES, MLA_HEADS * MLA_V, gq, gk, gk]
    return pl.pallas_call(
        _odd_in_body,
        grid=(G, n // tm),
        in_specs=[blk(d), pl.BlockSpec((None, 6, d), lambda g, t: (g, 0, 0)), full((1, d)), full((d, nw)),
                  full((1, ql)), full((ql, MLA_HEADS * LANES)), full((1, kl)), full((kl, MLA_HEADS * LANES)),
                  full((kl, MLA_HEADS * MLA_V)),
                  pl.BlockSpec((None, tm, 6 * LANES), lambda g, t: (g // batch, t, 0))],
        out_specs=[blk(wd) for wd in widths],
        out_shape=[jax.ShapeDtypeStruct((G, n, wd), BF16) for wd in widths],
        compiler_params=_cparams("arbitrary", "arbitrary"),
        name="odd_in",
    )(h_all, mod, g_pre, w, q_norm_g[None, :], wq, kv_norm_g[None, :], wk, wv, ropes)


def _mla_body(q_ref, kl_ref, kc_ref, vl_ref, vc_ref, o_ref, *, hps):
    lane = lax.broadcasted_iota(jnp.int32, (o_ref.shape[0], LANES), 1)
    outs = []
    for hd in range(hps):
        sl = slice(hd * LANES, (hd + 1) * LANES)
        vs = slice((hd // 2) * LANES, (hd // 2 + 1) * LANES)
        q = q_ref[:, sl]
        s_l = _dot_nt(q, kl_ref[:, sl])
        s_c = _dot_nt(q, kc_ref[:, sl])
        m = jnp.maximum(jnp.max(s_l, axis=-1, keepdims=True), jnp.max(s_c, axis=-1, keepdims=True))
        p_l = jnp.exp(s_l - m)
        p_c = jnp.exp(s_c - m)
        den = jnp.sum(p_l, axis=-1, keepdims=True) + jnp.sum(p_c, axis=-1, keepdims=True)
        pv = _dot(p_l.astype(BF16), vl_ref[:, vs]) + _dot(p_c.astype(BF16), vc_ref[:, vs])
        outs.append(pv / den)
    for pr in range(hps // 2):
        o_ref[:, pr * LANES:(pr + 1) * LANES] = jnp.where(lane < MLA_V, outs[2 * pr], outs[2 * pr + 1]).astype(BF16)


def _mla_attn(q, k, v, batch, n, ctx_len, tq, hps):
    return pl.pallas_call(
        functools.partial(_mla_body, hps=hps),
        grid=(batch, MLA_HEADS // hps, n // tq),
        in_specs=[pl.BlockSpec((None, tq, hps * LANES), lambda b, h, i: (b, i, h)),
                  pl.BlockSpec((None, n, hps * LANES), lambda b, h, i: (b, 0, h)),
                  pl.BlockSpec((None, ctx_len, hps * LANES), lambda b, h, i: (batch, b, h)),
                  pl.BlockSpec((None, n, hps * MLA_V), lambda b, h, i: (b, 0, h)),
                  pl.BlockSpec((None, ctx_len, hps * MLA_V), lambda b, h, i: (batch, b, h))],
        out_specs=pl.BlockSpec((None, tq, hps * MLA_V), lambda b, h, i: (b, i, h)),
        out_shape=jax.ShapeDtypeStruct((batch, n, MLA_HEADS * MLA_V), BF16),
        compiler_params=_cparams("arbitrary", "arbitrary", "arbitrary"),
        name="mla_attn",
    )(q, k, k, v, v)


def _dup_half(x, kv):
    lane = lax.broadcasted_iota(jnp.int32, x.shape, 1)
    swapped = pltpu.roll(x, HEAD_DIM, axis=1)
    lo = lane < HEAD_DIM
    return jnp.where(lo, x, swapped) if kv == 0 else jnp.where(lo, swapped, x)


def _win_body(sink_ref, q_ref, kp_ref, kc_ref, kn_ref, kx_ref, vp_ref, vc_ref, vn_ref, vx_ref, o_ref):
    i = pl.program_id(1)
    nb = pl.num_programs(1)
    w = WINDOW
    g = GQA_HEADS // GQA_KV
    ri = lax.broadcasted_iota(jnp.int32, (g * w, w), 0) % w
    ci = lax.broadcasted_iota(jnp.int32, (g * w, w), 1)
    ok_prev = (ci >= ri) & (i > 0)
    ok_next = (ci <= ri) & (i < nb - 1)
    lane = lax.broadcasted_iota(jnp.int32, (w, LANES), 1)
    lo = lane < HEAD_DIM
    neg = -jnp.inf
    for kv in range(GQA_KV):
        rows = []
        for hh in range(g):
            hd = kv * g + hh
            blk = q_ref[:, (hd // 2) * LANES:(hd // 2 + 1) * LANES].astype(F32)
            rows.append(jnp.where(lo if hd % 2 == 0 else ~lo, blk, 0.0).astype(BF16))
        q = jnp.concatenate(rows, axis=0)
        ks = [_dup_half(r[...].astype(F32), kv).astype(BF16) for r in (kp_ref, kc_ref, kn_ref, kx_ref)]
        vs = [_dup_half(r[...].astype(F32), kv).astype(BF16) for r in (vp_ref, vc_ref, vn_ref, vx_ref)]
        s_p = jnp.where(ok_prev, _dot_nt(q, ks[0]), neg)
        s_c = _dot_nt(q, ks[1])
        s_n = jnp.where(ok_next, _dot_nt(q, ks[2]), neg)
        s_x = _dot_nt(q, ks[3])
        hrow = lax.broadcasted_iota(jnp.int32, (g * w, 1), 0) // w
        sk = jnp.zeros((g * w, 1), F32)
        for hh in range(g):
            sk = jnp.where(hrow == hh, sink_ref[kv * g + hh], sk)
        m = jnp.maximum(jnp.maximum(jnp.max(s_p, axis=-1, keepdims=True), jnp.max(s_c, axis=-1, keepdims=True)),
                        jnp.maximum(jnp.max(s_n, axis=-1, keepdims=True), jnp.max(s_x, axis=-1, keepdims=True)))
        m = jnp.maximum(m, sk)
        ps = [jnp.exp(s - m) for s in (s_p, s_c, s_n, s_x)]
        den = jnp.exp(sk - m)
        for p in ps:
            den = den + jnp.sum(p, axis=-1, keepdims=True)
        acc = jnp.zeros((g * w, LANES), F32)
        for p, v in zip(ps, vs):
            acc = acc + _dot(p.astype(BF16), v)
        res = acc / den
        for pr in range(g // 2):
            blk = (kv * g) // 2 + pr
            o_ref[:, blk * LANES:(blk + 1) * LANES] = jnp.where(
                lo, res[(2 * pr) * w:(2 * pr + 1) * w, :], res[(2 * pr + 1) * w:(2 * pr + 2) * w, :]).astype(BF16)


def _win_attn(q, k, v, sink, batch, n, ctx_len):
    w = WINDOW
    nb = n // w
    gq, gk = GQA_HEADS * HEAD_DIM, GQA_KV * HEAD_DIM
    prev = lambda b, i: (b, jnp.maximum(i - 1, 0), 0)
    cur = lambda b, i: (b, i, 0)
    nxt = lambda b, i: (b, jnp.minimum(i + 1, nb - 1), 0)
    ctx = lambda b, i: (batch, b, 0)
    kspec = lambda rows, im: pl.BlockSpec((None, rows, gk), im)
    return pl.pallas_call(
        _win_body,
        grid=(batch, nb),
        in_specs=[pl.BlockSpec(memory_space=pltpu.SMEM),
                  pl.BlockSpec((None, w, gq), cur),
                  kspec(w, prev), kspec(w, cur), kspec(w, nxt), kspec(ctx_len, ctx),
                  kspec(w, prev), kspec(w, cur), kspec(w, nxt), kspec(ctx_len, ctx)],
        out_specs=pl.BlockSpec((None, w, gq), cur),
        out_shape=jax.ShapeDtypeStruct((batch, n, gq), BF16),
        compiler_params=_cparams("arbitrary", "arbitrary"),
        name="win_attn",
    )(sink, q, k, k, k, k, v, v, v, v)


def _route_body(h_ref, mod_ref, g_ref, r_ref, a_ref, info_ref, cnt_ref, carry_ref):
    first = (pl.program_id(0) == 0) & (pl.program_id(1) == 0)

    @pl.when(first)
    def _():
        carry_ref[...] = jnp.zeros_like(carry_ref)

    a = _norm_mod(h_ref[...], g_ref[...], mod_ref[3:4, :], mod_ref[4:5, :])
    a_ref[...] = a
    tm = a.shape[0]
    logits = jnp.dot(a, r_ref[...], precision=HIGHEST, preferred_element_type=F32)
    lane = lax.broadcasted_iota(jnp.int32, logits.shape, 1)
    neg = -jnp.inf
    logits = jnp.where(lane < N_EXPERTS, logits, neg)
    m1 = jnp.max(logits, axis=-1, keepdims=True)
    i1 = jnp.min(jnp.where(logits == m1, lane, LANES), axis=-1, keepdims=True)
    rest = jnp.where(lane == i1, neg, logits)
    m2 = jnp.max(rest, axis=-1, keepdims=True)
    i2 = jnp.min(jnp.where(rest == m2, lane, LANES), axis=-1, keepdims=True)
    e2 = jnp.exp(m2 - m1)
    w1 = 1.0 / (1.0 + e2)
    w2 = e2 / (1.0 + e2)
    chosen = ((lane == i1) | (lane == i2)).astype(F32)
    ri = lax.broadcasted_iota(jnp.int32, (tm, tm), 0)
    ci = lax.broadcasted_iota(jnp.int32, (tm, tm), 1)
    tri = (ci < ri).astype(BF16)
    before = _dot(tri, chosen.astype(BF16)) + carry_ref[...]
    p1 = jnp.sum(jnp.where(lane == i1, before, 0.0), axis=-1, keepdims=True)
    p2 = jnp.sum(jnp.where(lane == i2, before, 0.0), axis=-1, keepdims=True)
    carry_ref[...] += jnp.sum(chosen, axis=0, keepdims=True)
    cnt_ref[...] = carry_ref[...]
    vals = (i1.astype(F32), i2.astype(F32), w1, w2, p1, p2)
    info = jnp.zeros(logits.shape, F32)
    for j, v in enumerate(vals):
        info = jnp.where(lane == j, v, info)
    info_ref[...] = info


def _route(h_lat, mod, g_pre, router, tm):
    batch, n, d = h_lat.shape
    rp = jnp.pad(router, ((0, 0), (0, LANES - N_EXPERTS)))
    nt = n // tm
    return pl.pallas_call(
        _route_body,
        grid=(batch, nt),
        in_specs=[pl.BlockSpec((None, tm, d), lambda g, t: (g, t, 0)),
                  pl.BlockSpec((None, 6, d), lambda g, t: (g, 0, 0)),
                  pl.BlockSpec((1, d), lambda g, t: (0, 0)),
                  pl.BlockSpec((d, LANES), lambda g, t: (0, 0))],
        out_specs=[pl.BlockSpec((tm, d), lambda g, t: (g * nt + t, 0)),
                   pl.BlockSpec((tm, LANES), lambda g, t: (g * nt + t, 0)),
                   pl.BlockSpec((1, LANES), lambda g, t: (0, 0))],
        out_shape=[jax.ShapeDtypeStruct((batch * n, d), F32),
                   jax.ShapeDtypeStruct((batch * n, LANES), F32),
                   jax.ShapeDtypeStruct((1, LANES), F32)],
        scratch_shapes=[pltpu.VMEM((1, LANES), F32)],
        compiler_params=_cparams("arbitrary", "arbitrary"),
        name="moe_route",
    )(h_lat, mod, g_pre, rp)


def _row_copy(src_ref, src_row, dst_ref, dst_row, sem):
    return pltpu.make_async_copy(src_ref.at[pl.ds(src_row, 1), :], dst_ref.at[pl.ds(dst_row, 1), :], sem)


def _scatter_body(slot_ref, a_ref, xs_in_ref, xs_ref, sem, *, ts, n_tok):
    del xs_in_ref
    base = pl.program_id(0) * ts

    def issue(r, carry):
        for k in range(2):
            _row_copy(a_ref, r, xs_ref, slot_ref[k * n_tok + base + r], sem).start(priority=k)
        return carry

    lax.fori_loop(0, ts, issue, 0, unroll=8)
    for k in range(2):
        pltpu.make_async_copy(a_ref, xs_ref.at[pl.ds(0, ts), :], sem).wait()


def _scatter(slots, a, n_slots, ts):
    n_tok, d = a.shape
    zeros = jnp.zeros((n_slots, d), F32)
    return pl.pallas_call(
        functools.partial(_scatter_body, ts=ts, n_tok=n_tok),
        grid_spec=pltpu.PrefetchScalarGridSpec(
            num_scalar_prefetch=1,
            grid=(n_tok // ts,),
            in_specs=[pl.BlockSpec((ts, d), lambda i, s: (i, 0)),
                      pl.BlockSpec(memory_space=pl.ANY)],
            out_specs=pl.BlockSpec(memory_space=pl.ANY),
            scratch_shapes=[pltpu.SemaphoreType.DMA(())]),
        out_shape=jax.ShapeDtypeStruct((n_slots, d), F32),
        input_output_aliases={2: 0},
        compiler_params=_cparams("arbitrary"),
        name="moe_scatter",
    )(slots, a, zeros)


def _experts_body(te_ref, tv_ref, x_ref, wg_ref, wu_ref, wd_ref, y_ref, acc_ref):
    del te_ref
    j = pl.program_id(0)
    f = pl.program_id(1)
    valid = tv_ref[j]

    @pl.when(f == 0)
    def _():
        acc_ref[...] = jnp.zeros_like(acc_ref)

    n_sub = (valid + MOE_SUB - 1) // MOE_SUB
    for ns in range(1, MOE_TILE // MOE_SUB + 1):
        rows = slice(0, ns * MOE_SUB)

        @pl.when(n_sub == ns)
        def _(rows=rows):
            x = x_ref[rows, :].astype(BF16)
            gate = _dot(x, wg_ref[...].astype(BF16))
            up = _dot(x, wu_ref[...].astype(BF16))
            hid = (_silu(gate) * up).astype(BF16)
            acc_ref[rows, :] += _dot(hid, wd_ref[...].astype(BF16))

    @pl.when(f == pl.num_programs(1) - 1)
    def _():
        y_ref[...] = acc_ref[...]


def _experts(tile_expert, tile_valid, xs, wg, wu, wd):
    n_slots, d = xs.shape
    ff = wg.shape[-1]
    n_tiles = n_slots // MOE_TILE
    tf = MOE_FCHUNK
    return pl.pallas_call(
        _experts_body,
        grid_spec=pltpu.PrefetchScalarGridSpec(
            num_scalar_prefetch=2,
            grid=(n_tiles, ff // tf),
            in_specs=[pl.BlockSpec((MOE_TILE, d), lambda j, f, te, tv: (j, 0)),
                      pl.BlockSpec((None, d, tf), lambda j, f, te, tv: (te[j], 0, f)),
                      pl.BlockSpec((None, d, tf), lambda j, f, te, tv: (te[j], 0, f)),
                      pl.BlockSpec((None, tf, d), lambda j, f, te, tv: (te[j], f, 0))],
            out_specs=pl.BlockSpec((MOE_TILE, d), lambda j, f, te, tv: (j, 0)),
            scratch_shapes=[pltpu.VMEM((MOE_TILE, d), F32)]),
        out_shape=jax.ShapeDtypeStruct((n_slots, d), F32),
        compiler_params=_cparams("arbitrary", "arbitrary"),
        name="moe_experts",
    )(tile_expert, tile_valid, xs, wg, wu, wd)


def _combine_body(slot_ref, ys_ref, info_ref, h_ref, mod_ref, pg_ref, o_ref, buf_ref, sem, *, tc, n_tok):
    base = (pl.program_id(0) * pl.num_programs(1) + pl.program_id(1)) * tc

    def issue(r, carry):
        for k in range(2):
            _row_copy(ys_ref, slot_ref[k * n_tok + base + r], buf_ref.at[k], r, sem).start(priority=k)
        return carry

    lax.fori_loop(0, tc, issue, 0, unroll=8)
    for k in range(2):
        pltpu.make_async_copy(ys_ref.at[pl.ds(0, tc), :], buf_ref.at[k], sem).wait()
    f = info_ref[:, 2:3] * buf_ref[0] + info_ref[:, 3:4] * buf_ref[1]
    o_ref[...] = h_ref[...] + mod_ref[5:6, :] * _rms(f, pg_ref[...])


def _combine(slots, ys, info, h_lat, mod, post_g, tc):
    batch, n, d = h_lat.shape
    nt = n // tc
    return pl.pallas_call(
        functools.partial(_combine_body, tc=tc, n_tok=batch * n),
        grid_spec=pltpu.PrefetchScalarGridSpec(
            num_scalar_prefetch=1,
            grid=(batch, nt),
            in_specs=[pl.BlockSpec(memory_space=pl.ANY),
                      pl.BlockSpec((tc, LANES), lambda g, t, s: (g * nt + t, 0)),
                      pl.BlockSpec((None, tc, d), lambda g, t, s: (g, t, 0)),
                      pl.BlockSpec((None, 6, d), lambda g, t, s: (g, 0, 0)),
                      pl.BlockSpec((1, d), lambda g, t, s: (0, 0))],
            out_specs=pl.BlockSpec((None, tc, d), lambda g, t, s: (g, t, 0)),
            scratch_shapes=[pltpu.VMEM((2, tc, d), F32), pltpu.SemaphoreType.DMA(())]),
        out_shape=jax.ShapeDtypeStruct((batch, n, d), F32),
        compiler_params=_cparams("arbitrary", "arbitrary"),
        name="moe_combine",
    )(slots, ys, info, h_lat, mod, post_g)


def _moe(h_lat, mod, g_pre, router, wg, wu, wd, post_g, tm):
    batch, n, d = h_lat.shape
    n_tok = batch * n
    a, info, counts = _route(h_lat, mod, g_pre, router, tm)
    cnt = counts[0, :N_EXPERTS].astype(jnp.int32)
    padded = ((cnt + MOE_TILE - 1) // MOE_TILE) * MOE_TILE
    ends = jnp.cumsum(padded)
    starts = ends - padded
    i1, i2 = info[:, 0].astype(jnp.int32), info[:, 1].astype(jnp.int32)
    p1, p2 = info[:, 4].astype(jnp.int32), info[:, 5].astype(jnp.int32)
    slots = jnp.concatenate([starts[i1] + p1, starts[i2] + p2])
    n_tiles = (2 * n_tok + N_EXPERTS * (MOE_TILE - 1)) // MOE_TILE
    tile_row = jnp.arange(n_tiles, dtype=jnp.int32) * MOE_TILE
    tile_expert = jnp.minimum(jnp.sum(tile_row[:, None] >= ends[None, :], axis=1), N_EXPERTS - 1).astype(jnp.int32)
    tile_valid = jnp.clip(cnt[tile_expert] - (tile_row - starts[tile_expert]), 0, MOE_TILE)
    tile_valid = jnp.where(tile_row < ends[-1], tile_valid, 0).astype(jnp.int32)
    xs = _scatter(slots, a, n_tiles * MOE_TILE, tm)
    ys = _experts(tile_expert, tile_valid, xs, wg, wu, wd)
    return _combine(slots, ys, info, h_lat, mod, post_g, tm)


def kernel(x, c, ctx, c_ctx, ada_w, ada_b, mix_pre_g, mix_post_g, ffn_pre_g, ffn_post_g, ev_w_in, ev_hy_conv_w, ev_hy_conv_b, ev_hy_filt_w1, ev_hy_filt_b1, ev_hy_filt_w2, ev_hy_filt_b2, ev_hy_filt_w3, ev_hy_freq, ev_hy_bias, ev_ssm_conv_w, ev_ssm_conv_b, ev_ssm_dt_bias, ev_ssm_a_log, ev_ssm_d, ev_ssm_norm_g, ev_w_out, ev_ffn_w_gate, ev_ffn_w_up, ev_ffn_w_down, od_w_in, od_mla_q_norm_g, od_mla_w_q_up, od_mla_kv_norm_g, od_mla_w_kv_up, od_gqa_sink, od_w_out, od_router, od_moe_w_gate, od_moe_w_up, od_moe_w_down):
    batch, n, d = x.shape
    ctx_len = ctx.shape[1]
    assert batch * ctx_len == n and ada_w.shape[0] == 2
    G = batch + 1
    tm = min(n, 512)

    cond = jnp.concatenate([c, c_ctx[None, :], jnp.zeros((16 - G, d), F32)], axis=0)
    mods = _ada(cond, ada_w, ada_b).reshape(2, 16, 6, d)
    ctx3 = ctx.reshape(1, n, d)

    n_main = 3 * HY_CH + SSM_INNER + SSM_CONV_CH
    w_in = ev_w_in[0]
    w_main = w_in[:, :n_main].astype(BF16)
    w_dt = jnp.pad(w_in[:, n_main:], ((0, 0), (0, LANES - 2 * SSM_HEADS))).astype(BF16)
    proj, dt = _even_in(x, ctx3, mods[0], mix_pre_g[0][None, :], w_main, w_dt, tm)
    y_hy_l, y_hy_c = _hyena(proj, batch, n, ctx_len,
                            (ev_hy_conv_w[0], ev_hy_conv_b[0], ev_hy_filt_w1[0], ev_hy_filt_b1[0], ev_hy_filt_w2[0],
                             ev_hy_filt_b2[0], ev_hy_filt_w3[0], ev_hy_freq[0], ev_hy_bias[0]))
    y_ss_l, y_ss_c = _ssd(proj, dt, batch, n, ctx_len,
                          (ev_ssm_conv_w[0], ev_ssm_conv_b[0], ev_ssm_dt_bias[0], ev_ssm_a_log[0], ev_ssm_d[0]))
    h_all = _mix_out_even(y_hy_l, y_hy_c, y_ss_l, y_ss_c, proj, ev_ssm_norm_g[0][None, :],
                          ev_w_out[0].astype(BF16), x, ctx3, mods[0], mix_post_g[0][None, :], tm)
    h_all = _ffn(h_all, mods[0], ffn_pre_g[0][None, :], ev_ffn_w_gate[0].astype(BF16), ev_ffn_w_up[0].astype(BF16),
                 ev_ffn_w_down[0].astype(BF16), ffn_post_g[0][None, :], min(n, 1024))

    q_m, k_m, v_m, q_w, k_w, v_w = _odd_in(h_all, mods[1], mix_pre_g[1][None, :], od_w_in[0], od_mla_q_norm_g[0],
                                           od_mla_w_q_up[0], od_mla_kv_norm_g[0], od_mla_w_kv_up[0], tm)
    o_mla = _mla_attn(q_m, k_m, v_m, batch, n, ctx_len, min(n, 512), 4)
    o_win = _win_attn(q_w, k_w, v_w, od_gqa_sink[0], batch, n, ctx_len)
    h_lat = _mix_out_odd(o_mla, o_win, od_w_out[0].astype(BF16), h_all, mods[1], mix_post_g[1][None, :], tm)
    return _moe(h_lat, mods[1], ffn_pre_g[1][None, :], od_router[0], od_moe_w_gate[0], od_moe_w_up[0],
                od_moe_w_down[0], ffn_post_g[1][None, :], tm)
```

```python
import functools
import math

import jax
import jax.numpy as jnp
from jax import lax
from jax.experimental import pallas as pl
from jax.experimental.pallas import tpu as pltpu

F32 = jnp.float32
BF16 = jnp.bfloat16
HIGHEST = lax.Precision.HIGHEST

D_MODEL = 1024
GRID_W = 64
EPS = 1e-6
HEAD_DIM = 64
GROUP_WIDTH = D_MODEL // 2

HY_CH = GROUP_WIDTH
HY_BANDS = 16
HY_EMB = 2 * HY_BANDS + 1
HY_FILT_HID = 64
HY_FAST_DECAY = 0.3
HY_SLOW_DECAY = 1.5
HY_DECAY_TARGET = 1e-2

SSM_INNER = GROUP_WIDTH
SSM_HEADDIM = 64
SSM_HEADS = SSM_INNER // SSM_HEADDIM
SSM_GROUPS = 2
SSM_STATE = 128
SSM_CHUNK = 128
SSM_CONV_CH = SSM_INNER + 2 * SSM_GROUPS * SSM_STATE

MLA_HEADS = GROUP_WIDTH // HEAD_DIM
MLA_NOPE = 64
MLA_ROPE = 32
MLA_V = 64
MLA_Q_LORA = D_MODEL // 4
MLA_KV_LORA = D_MODEL // 8

GQA_HEADS = GROUP_WIDTH // HEAD_DIM
GQA_KV = GQA_HEADS // 4
WINDOW = 128
ROPE_BASE = 10000.0

D_FF = ((8 * D_MODEL // 3 + 127) // 128) * 128
N_EXPERTS = 8
D_FF_EXPERT = 7 * D_MODEL // 2

LANES = 128
VMEM_LIMIT = 56 * 1024 * 1024
MOE_TILE = 1024
MOE_SUB = 256
MOE_FCHUNK = 512


def _cparams(*sem):
    return pltpu.CompilerParams(dimension_semantics=sem, vmem_limit_bytes=VMEM_LIMIT)


def _silu(x):
    return x / (1.0 + jnp.exp(-x))


def _rms(x, g):
    return x * lax.rsqrt(jnp.mean(x * x, axis=-1, keepdims=True) + EPS) * g


def _norm_mod(h, g, shift, scale):
    return _rms(h, g) * (1.0 + scale) + shift


def _dot(a, b):
    return jnp.dot(a, b, preferred_element_type=F32)


def _dot_nt(a, b):
    return lax.dot_general(a, b, (((1,), (1,)), ((), ())), preferred_element_type=F32)


def _ada_body(c_ref, w_ref, b_ref, o_ref):
    s = _silu(c_ref[...]).astype(BF16)
    o_ref[0] = _dot(s, w_ref[0].astype(BF16)) + b_ref[0]


def _ada(cc, ada_w, ada_b):
    depth, d, n6 = ada_w.shape
    rows = cc.shape[0]
    tn = 1536
    return pl.pallas_call(
        _ada_body,
        grid=(depth, n6 // tn),
        in_specs=[pl.BlockSpec((rows, d), lambda l, j: (0, 0)),
                  pl.BlockSpec((1, d, tn), lambda l, j: (l, 0, j)),
                  pl.BlockSpec((1, 1, tn), lambda l, j: (l, 0, j))],
        out_specs=pl.BlockSpec((1, rows, tn), lambda l, j: (l, 0, j)),
        out_shape=jax.ShapeDtypeStruct((depth, rows, n6), F32),
        compiler_params=_cparams("arbitrary", "arbitrary"),
        name="ada",
    )(cc, ada_w, ada_b.reshape(depth, 1, n6))


def _lat_ctx_specs(batch, nt, tm, d):
    lat = pl.BlockSpec((None, tm, d), lambda g, t: (jnp.minimum(g, batch - 1), jnp.where(g < batch, t, nt - 1), 0))
    ctx = pl.BlockSpec((None, tm, d), lambda g, t: (0, jnp.where(g < batch, 0, t), 0))
    return lat, ctx


def _even_in_body(x_ref, c_ref, mod_ref, g_ref, w_ref, wdt_ref, o_ref, dt_ref, *, batch):
    h = jnp.where(pl.program_id(0) == batch, c_ref[...], x_ref[...])
    a = _norm_mod(h, g_ref[...], mod_ref[0:1, :], mod_ref[1:2, :]).astype(BF16)
    n_out = o_ref.shape[-1]
    for j in range(0, n_out, 512):
        o_ref[:, j:j + 512] = _dot(a, w_ref[:, j:j + 512]).astype(BF16)
    dt_ref[...] = _dot(a, wdt_ref[...])


def _even_in(x, ctx3, mod, g_pre, w_main, w_dt, tm):
    batch, n, d = x.shape
    G = batch + 1
    n_main = w_main.shape[1]
    lat_spec, ctx_spec = _lat_ctx_specs(batch, n // tm, tm, d)
    return pl.pallas_call(
        functools.partial(_even_in_body, batch=batch),
        grid=(G, n // tm),
        in_specs=[lat_spec, ctx_spec,
                  pl.BlockSpec((None, 6, d), lambda g, t: (g, 0, 0)),
                  pl.BlockSpec((1, d), lambda g, t: (0, 0)),
                  pl.BlockSpec((d, n_main), lambda g, t: (0, 0)),
                  pl.BlockSpec((d, LANES), lambda g, t: (0, 0))],
        out_specs=[pl.BlockSpec((None, tm, n_main), lambda g, t: (g, t, 0)),
                   pl.BlockSpec((None, tm, LANES), lambda g, t: (g, t, 0))],
        out_shape=[jax.ShapeDtypeStruct((G, n, n_main), BF16),
                   jax.ShapeDtypeStruct((G, n, LANES), F32)],
        compiler_params=_cparams("arbitrary", "arbitrary"),
        name="even_in",
    )(x, ctx3, mod, g_pre, w_main, w_dt)


def _dwconv3(x, w, b):
    L = x.shape[0]
    row = lax.broadcasted_iota(jnp.int32, x.shape, 0)
    prev = jnp.where(row >= 1, pltpu.roll(x, 1, axis=0), 0.0)
    nxt = jnp.where(row < L - 1, pltpu.roll(x, L - 1, axis=0), 0.0)
    return prev * w[0:1, :] + x * w[1:2, :] + nxt * w[2:3, :] + b


def _hy_pre_body(p_ref, cw_ref, cb_ref, u_ref, x0_ref):
    c = HY_CH
    parts = []
    for k in range(3):
        x = p_ref[:, k * c:(k + 1) * c].astype(F32)
        parts.append(_dwconv3(x, cw_ref[:, k * c:(k + 1) * c], cb_ref[:, k * c:(k + 1) * c]))
    x0_ref[...] = parts[0].astype(BF16)
    u_ref[...] = (parts[1] * parts[2]).astype(BF16)


def _hy_pre(proj, nseq, L, index_map, cw, cb):
    c3 = 3 * HY_CH
    return pl.pallas_call(
        _hy_pre_body,
        grid=(nseq,),
        in_specs=[pl.BlockSpec((None, L, c3), index_map),
                  pl.BlockSpec((3, c3), lambda s: (0, 0)),
                  pl.BlockSpec((1, c3), lambda s: (0, 0))],
        out_specs=[pl.BlockSpec((None, L, HY_CH), lambda s: (s, 0, 0)),
                   pl.BlockSpec((None, L, HY_CH), lambda s: (s, 0, 0))],
        out_shape=[jax.ShapeDtypeStruct((nseq, L, HY_CH), BF16),
                   jax.ShapeDtypeStruct((nseq, L, HY_CH), BF16)],
        compiler_params=_cparams("arbitrary"),
        name="hy_pre",
    )(proj, cw, cb)


def _hy_filter_body(z_ref, t_ref, w1_ref, b1_ref, w2_ref, b2_ref, w3_ref, f_ref, dl_ref, o_ref):
    f = f_ref[...]
    h = jnp.sin(f * (jnp.dot(z_ref[...], w1_ref[...], precision=HIGHEST, preferred_element_type=F32) + b1_ref[...]))
    h = jnp.sin(f * (jnp.dot(h, w2_ref[...], precision=HIGHEST, preferred_element_type=F32) + b2_ref[...]))
    h = jnp.dot(h, w3_ref[...], precision=HIGHEST, preferred_element_type=F32)
    decay = jnp.exp(-t_ref[...] * dl_ref[...])
    hf = h[:, :HY_CH] * decay
    hb = h[:, HY_CH:] * decay
    row = lax.broadcasted_iota(jnp.int32, hb.shape, 0)
    hb = jnp.where(row == 0, 0.0, hb)
    o_ref[:, :HY_CH] = hf.astype(BF16)
    o_ref[:, HY_CH:] = hb.astype(BF16)


def _hy_filter(L, w1, b1, w2, b2, w3, freq):
    pos = jnp.arange(L, dtype=F32)
    t = pos / (L - 1)
    omega = 2.0 * math.pi * pos / L
    bands = jnp.linspace(1e-4, HY_BANDS - 1, HY_BANDS, dtype=F32)
    z = jnp.concatenate([t[:, None], jnp.cos(omega[:, None] * bands), -jnp.sin(omega[:, None] * bands)], axis=-1)
    z = jnp.pad(z, ((0, 0), (0, LANES - HY_EMB)))
    w1p = jnp.pad(w1, ((0, LANES - HY_EMB), (0, 0)))
    deltas = jnp.abs(jnp.linspace(math.log(HY_DECAY_TARGET) / HY_SLOW_DECAY,
                                  math.log(HY_DECAY_TARGET) / HY_FAST_DECAY, HY_CH, dtype=F32))
    return pl.pallas_call(
        _hy_filter_body,
        out_shape=jax.ShapeDtypeStruct((L, 2 * HY_CH), BF16),
        compiler_params=pltpu.CompilerParams(vmem_limit_bytes=VMEM_LIMIT),
        name="hy_filter",
    )(z, t[:, None], w1p, b1[None, :], w2, b2[None, :], w3, freq[None, :], deltas[None, :])


def _dft_tables(L):
    split = 64
    t = jnp.arange(L, dtype=jnp.int32)[None, :]
    k1 = jnp.arange(L // split, dtype=jnp.int32)[:, None] * split
    k0 = jnp.arange(split, dtype=jnp.int32)[:, None]
    ang1 = ((k1 * t) % (2 * L)).astype(F32) * (math.pi / L)
    ang0 = ((k0 * t) % (2 * L)).astype(F32) * (math.pi / L)
    c1, s1 = jnp.cos(ang1)[:, None, :], jnp.sin(ang1)[:, None, :]
    c0, s0 = jnp.cos(ang0)[None, :, :], jnp.sin(ang0)[None, :, :]
    fre = (c1 * c0 - s1 * s0).reshape(L, L)
    nsin = -(s1 * c0 + c1 * s0).reshape(L, L)
    k = jnp.arange(L, dtype=jnp.int32)[:, None]
    alt_t = jnp.where(t % 2 == 0, 1.0, -1.0).astype(F32)
    alt_k = jnp.where(k % 2 == 0, 1.0, -1.0).astype(F32)
    f2 = jnp.stack([fre, jnp.where(k == 0, alt_t, nsin)]).astype(BF16)
    fimt = jnp.where(t == 0, alt_k, nsin).astype(BF16)
    return f2, fimt


def _filt_dft_body(f_ref, x_ref, o_ref, *, L, tr):
    c = HY_CH
    x = x_ref[...]
    sre = _dot(f_ref[0], x)
    sim = _dot(f_ref[1], x)
    is0 = (lax.broadcasted_iota(jnp.int32, (tr, c), 0) + pl.program_id(0) * tr) == 0
    wgt = jnp.where(is0, 1.0 / (2 * L), 1.0 / L)
    hre = sre[:, :c] + sre[:, c:]
    o_ref[0] = (hre * wgt).astype(BF16)
    o_ref[1] = (jnp.where(is0, 0.0, sim[:, :c] - sim[:, c:]) * wgt).astype(BF16)
    o_ref[2] = (jnp.where(is0, sim[:, :c] + sim[:, c:], hre) * wgt).astype(BF16)


def _filt_dft(f2, hfb):
    _, L, _ = f2.shape
    cols = hfb.shape[1]
    tr = min(L, 512)
    return pl.pallas_call(
        functools.partial(_filt_dft_body, L=L, tr=tr),
        grid=(L // tr,),
        in_specs=[pl.BlockSpec((2, tr, L), lambda i: (0, i, 0)),
                  pl.BlockSpec((L, cols), lambda i: (0, 0))],
        out_specs=pl.BlockSpec((3, tr, cols // 2), lambda i: (0, i, 0)),
        out_shape=jax.ShapeDtypeStruct((3, L, cols // 2), BF16),
        compiler_params=_cparams("arbitrary"),
        name="filt_dft",
    )(f2, hfb)


def _hy_conv_body(u_ref, x0_ref, f2_ref, fimt_ref, s_ref, bias_ref, o_ref, *, L, fc):
    u = u_ref[...]
    acc = None
    for k in range(L // fc):
        ks = slice(k * fc, (k + 1) * fc)
        ure = _dot(f2_ref[0, ks, :], u)
        uim = _dot(f2_ref[1, ks, :], u)
        a_m = s_ref[0, ks, :].astype(F32)
        b_m = s_ref[1, ks, :].astype(F32)
        d_m = s_ref[2, ks, :].astype(F32)
        yre = (ure * a_m - uim * b_m).astype(BF16)
        yim = (ure * b_m + uim * d_m).astype(BF16)
        part = _dot(f2_ref[0, :, ks], yre) + _dot(fimt_ref[:, ks], yim)
        acc = part if acc is None else acc + part
    y = acc + u.astype(F32) * bias_ref[...]
    o_ref[...] = (x0_ref[...].astype(F32) * y).astype(BF16)


def _hy_conv(u, x0, f2, fimt, s, bias):
    nseq, L, c = u.shape
    fc = min(L, 512)
    resident = lambda shape: pl.BlockSpec(shape, lambda s_: (0,) * len(shape), pipeline_mode=pl.Buffered(1))
    return pl.pallas_call(
        functools.partial(_hy_conv_body, L=L, fc=fc),
        grid=(nseq,),
        in_specs=[pl.BlockSpec((None, L, c), lambda s_: (s_, 0, 0)),
                  pl.BlockSpec((None, L, c), lambda s_: (s_, 0, 0)),
                  resident((2, L, L)), resident((L, L)), resident((3, L, c)),
                  pl.BlockSpec((1, c), lambda s_: (0, 0))],
        out_specs=pl.BlockSpec((None, L, c), lambda s_: (s_, 0, 0)),
        out_shape=jax.ShapeDtypeStruct((nseq, L, c), BF16),
        compiler_params=_cparams("arbitrary"),
        name="hy_conv",
    )(u, x0, f2, fimt, s, bias)


def _hyena(proj, batch, n, ctx_len, p):
    cw, cb, w1, b1, w2, b2, w3, freq, bias = p
    outs = []
    for L, imap in ((n, lambda s: (s, 0, 0)), (ctx_len, lambda s: (batch, s, 0))):
        u, x0 = _hy_pre(proj, batch, L, imap, cw, cb[None, :])
        hfb = _hy_filter(L, w1, b1, w2, b2, w3, freq)
        f2, fimt = _dft_tables(L)
        s = _filt_dft(f2, hfb)
        outs.append(_hy_conv(u, x0, f2, fimt, s, bias[None, :]))
    return outs


def _softplus(x):
    return jnp.maximum(x, 0.0) + jnp.log(1.0 + jnp.exp(-jnp.abs(x)))


def _chunk_cumsum(a, axis, reverse):
    q = SSM_CHUNK
    size = a.shape[axis]
    pos = lax.broadcasted_iota(jnp.int32, a.shape, axis) % q
    s = 1
    while s < q:
        if reverse:
            a = a + jnp.where(pos < q - s, pltpu.roll(a, size - s, axis=axis), 0.0)
        else:
            a = a + jnp.where(pos >= s, pltpu.roll(a, s, axis=axis), 0.0)
        s *= 2
    return a


def _ssd_prepare(L, xbc_ref, dtc_ref, dtr_ref, cw_ref, cb_ref, dtbc_ref, alc_ref, dtbr_ref, alr_ref,
                 x_s, c_s, bt_s, csc_s, csr_s, dtr_s):
    nc = L // SSM_CHUNK
    nh = SSM_HEADS
    inner = SSM_INNER
    gs = SSM_GROUPS * SSM_STATE
    xs = _silu(_dwconv3(xbc_ref[:, :inner].astype(F32), cw_ref[:, :inner], cb_ref[:, :inner]))
    x_s[0:L, :] = xs.astype(BF16)
    bm = _silu(_dwconv3(xbc_ref[:, inner:inner + gs].astype(F32), cw_ref[:, inner:inner + gs],
                        cb_ref[:, inner:inner + gs]))
    for c in range(nc):
        bt_s[c] = bm[c * SSM_CHUNK:(c + 1) * SSM_CHUNK, :].T.astype(BF16)
    cm = _silu(_dwconv3(xbc_ref[:, inner + gs:].astype(F32), cw_ref[:, inner + gs:], cb_ref[:, inner + gs:]))
    c_s[0:L, :] = cm.astype(BF16)
    a_col = _softplus(dtc_ref[...] + dtbc_ref[...]) * (-jnp.exp(alc_ref[...]))
    lane = lax.broadcasted_iota(jnp.int32, a_col.shape, 1)
    csc_s[0:L, :] = jnp.where(lane < nh, _chunk_cumsum(a_col, 0, False), _chunk_cumsum(a_col, 0, True))
    dt_row = _softplus(dtr_ref[...] + dtbr_ref[...])
    a_row = (dt_row * (-jnp.exp(alr_ref[...]))).reshape(2 * nh * nc, SSM_CHUNK)
    rown = lax.broadcasted_iota(jnp.int32, a_row.shape, 0)
    csr_s[0:2 * nh * nc, :] = jnp.where(rown < nh * nc, _chunk_cumsum(a_row, 1, False),
                                        _chunk_cumsum(a_row, 1, True))
    dtr_s[0:2 * nh * nc, :] = dt_row.reshape(2 * nh * nc, SSM_CHUNK)


def _ssd_chunk(c, d, nc, first, x_s, c_s, bt_s, csc_s, csr_s, dtr_s, st_s, y_s):
    q = SSM_CHUNK
    ns = SSM_STATE
    hp = SSM_HEADDIM
    hpg = SSM_HEADS // SSM_GROUPS
    gw = hpg * hp
    r0 = pl.multiple_of(c * q, q)
    ri = lax.broadcasted_iota(jnp.int32, (q, q), 0)
    ci = lax.broadcasted_iota(jnp.int32, (q, q), 1)
    keep = (ci <= ri) if d == 0 else (ci >= ri)
    lane_blk = lax.broadcasted_iota(jnp.int32, (q, gw), 1) // hp
    cc = c_s[pl.ds(r0, q), :]
    xc = x_s[pl.ds(r0, q), :]
    btc = bt_s[c]
    ys = []
    for g in range(SSM_GROUPS):
        cg = cc[:, g * ns:(g + 1) * ns]
        btg = btc[g * ns:(g + 1) * ns, :]
        xg = xc[:, g * gw:(g + 1) * gw]
        s_cb = _dot(cg, btg)
        btg32 = btg.astype(F32)
        m_rows, w_rows, e_cols, tots = [], [], [], []
        for hh in range(hpg):
            idx = d * SSM_HEADS + g * hpg + hh
            csc = csc_s[pl.ds(r0, q), idx:idx + 1]
            csr = csr_s[pl.ds(idx * nc + c, 1), :]
            dtr = dtr_s[pl.ds(idx * nc + c, 1), :]
            lmat = jnp.where(keep, jnp.exp(csc - csr), 0.0)
            m_rows.append((s_cb * lmat * dtr).astype(BF16))
            tot = csr[:, q - 1:q] if d == 0 else csr[:, 0:1]
            w_rows.append((btg32 * (jnp.exp(tot - csr) * dtr)).astype(BF16))
            e_cols.append(jnp.exp(csc))
            tots.append(jnp.exp(tot))
        yd = _dot(jnp.concatenate(m_rows, axis=0), xg)
        ds = _dot(jnp.concatenate(w_rows, axis=0), xg)
        st = st_s[d, g]
        yo = _dot(cg, st.astype(BF16))
        y_g = jnp.zeros((q, gw), F32)
        st_new = jnp.zeros((ns, gw), F32)
        for hh in range(hpg):
            sel = lane_blk == hh
            y_g = jnp.where(sel, yd[hh * q:(hh + 1) * q, :] + yo * e_cols[hh], y_g)
            st_new = jnp.where(sel, st * tots[hh] + ds[hh * ns:(hh + 1) * ns, :], st_new)
        st_s[d, g] = st_new
        ys.append(y_g)
    y = jnp.concatenate(ys, axis=1)
    if first:
        y_s[pl.ds(r0, q), :] = y
    else:
        y_s[pl.ds(r0, q), :] += y


def _ssd_body(xl_ref, xc_ref, dcl_ref, dcc_ref, drl_ref, drc_ref, cw_ref, cb_ref, dtbc_ref, alc_ref,
              dtbr_ref, alr_ref, dsk_ref, yl_ref, yc_ref,
              x_s, c_s, bt_s, csc_s, csr_s, dtr_s, st_s, y_s, *, n, ctx_len):
    st_s[...] = jnp.zeros_like(st_s)
    for L, xbc_ref, dtc_ref, dtr_ref, o_ref in ((ctx_len, xc_ref, dcc_ref, drc_ref, yc_ref),
                                                 (n, xl_ref, dcl_ref, drl_ref, yl_ref)):
        nc = L // SSM_CHUNK
        _ssd_prepare(L, xbc_ref, dtc_ref, dtr_ref, cw_ref, cb_ref, dtbc_ref, alc_ref, dtbr_ref, alr_ref,
                     x_s, c_s, bt_s, csc_s, csr_s, dtr_s)
        args = (x_s, c_s, bt_s, csc_s, csr_s, dtr_s, st_s, y_s)

        def fwd(c, carry, nc=nc, args=args):
            _ssd_chunk(c, 0, nc, True, *args)
            return carry

        def bwd(i, carry, nc=nc, args=args):
            _ssd_chunk(nc - 1 - i, 1, nc, False, *args)
            return carry

        lax.fori_loop(0, nc, fwd, 0)
        lax.fori_loop(0, nc, bwd, 0)
        o_ref[...] = (y_s[0:L, :] + x_s[0:L, :].astype(F32) * dsk_ref[...]).astype(BF16)


def _ssd(proj, dt, batch, n, ctx_len, p):
    conv_w, conv_b, dt_bias, a_log, d_skip = p
    nh = SSM_HEADS
    ncl, ncc = n // SSM_CHUNK, ctx_len // SSM_CHUNK
    cch = SSM_CONV_CH
    xbc_blk = (SSM_INNER * 3 + SSM_INNER) // cch
    dt16 = dt[:, :, :2 * nh]
    dtr_l = dt16[:batch].transpose(0, 2, 1).reshape(batch, 2 * nh, ncl, SSM_CHUNK)
    dtr_c = dt16[batch].reshape(batch, ctx_len, 2 * nh).transpose(0, 2, 1).reshape(batch, 2 * nh, ncc, SSM_CHUNK)
    pad = LANES - 2 * nh
    dtb_col = jnp.pad(dt_bias.reshape(1, 2 * nh), ((0, 0), (0, pad)))
    al_col = jnp.pad(a_log.reshape(1, 2 * nh), ((0, 0), (0, pad)))
    dtb_row = jnp.broadcast_to(dt_bias.reshape(2 * nh, 1, 1), (2 * nh, 1, SSM_CHUNK))
    al_row = jnp.broadcast_to(a_log.reshape(2 * nh, 1, 1), (2 * nh, 1, SSM_CHUNK))
    dsk = jnp.repeat(d_skip, SSM_HEADDIM)[None, :]
    full = lambda shape: pl.BlockSpec(shape, lambda b: (0,) * len(shape))
    return pl.pallas_call(
        functools.partial(_ssd_body, n=n, ctx_len=ctx_len),
        grid=(batch,),
        in_specs=[pl.BlockSpec((None, n, cch), lambda b: (b, 0, xbc_blk)),
                  pl.BlockSpec((None, ctx_len, cch), lambda b: (batch, b, xbc_blk)),
                  pl.BlockSpec((None, n, LANES), lambda b: (b, 0, 0)),
                  pl.BlockSpec((None, ctx_len, LANES), lambda b: (batch, b, 0)),
                  pl.BlockSpec((None, 2 * nh, ncl, SSM_CHUNK), lambda b: (b, 0, 0, 0)),
                  pl.BlockSpec((None, 2 * nh, ncc, SSM_CHUNK), lambda b: (b, 0, 0, 0)),
                  full((3, cch)), full((1, cch)), full((1, LANES)), full((1, LANES)),
                  full((2 * nh, 1, SSM_CHUNK)), full((2 * nh, 1, SSM_CHUNK)), full((1, SSM_INNER))],
        out_specs=[pl.BlockSpec((None, n, SSM_INNER), lambda b: (b, 0, 0)),
                   pl.BlockSpec((None, ctx_len, SSM_INNER), lambda b: (b, 0, 0))],
        out_shape=[jax.ShapeDtypeStruct((batch, n, SSM_INNER), BF16),
                   jax.ShapeDtypeStruct((batch, ctx_len, SSM_INNER), BF16)],
        scratch_shapes=[pltpu.VMEM((n, SSM_INNER), BF16),
                        pltpu.VMEM((n, SSM_GROUPS * SSM_STATE), BF16),
                        pltpu.VMEM((ncl, SSM_GROUPS * SSM_STATE, SSM_CHUNK), BF16),
                        pltpu.VMEM((n, LANES), F32),
                        pltpu.VMEM((2 * nh * ncl, SSM_CHUNK), F32),
                        pltpu.VMEM((2 * nh * ncl, SSM_CHUNK), F32),
                        pltpu.VMEM((2, SSM_GROUPS, SSM_STATE, SSM_INNER // SSM_GROUPS), F32),
                        pltpu.VMEM((n, SSM_INNER), F32)],
        compiler_params=_cparams("arbitrary"),
        name="ssd",
    )(proj, proj, dt, dt, dtr_l, dtr_c, conv_w, conv_b[None, :], dtb_col, al_col, dtb_row, al_row, dsk)


def _mix_out_even_body(yhl_ref, yhc_ref, ysl_ref, ysc_ref, z_ref, ng_ref, w_ref, x_ref, c_ref, mod_ref, pg_ref, o_ref,
                       *, batch):
    is_ctx = pl.program_id(0) == batch
    h = jnp.where(is_ctx, c_ref[...], x_ref[...])
    y_hy = jnp.where(is_ctx, yhc_ref[...], yhl_ref[...])
    ys = jnp.where(is_ctx, ysc_ref[...], ysl_ref[...]).astype(F32) * _silu(z_ref[...].astype(F32))
    gw = SSM_INNER // SSM_GROUPS
    parts = []
    for g in range(SSM_GROUPS):
        yg = ys[:, g * gw:(g + 1) * gw]
        parts.append(yg * lax.rsqrt(jnp.mean(yg * yg, axis=-1, keepdims=True) + EPS))
    y_ssm = (jnp.concatenate(parts, axis=1) * ng_ref[...]).astype(BF16)
    m = _dot(y_hy, w_ref[0:HY_CH, :]) + _dot(y_ssm, w_ref[HY_CH:, :])
    o_ref[...] = h + mod_ref[2:3, :] * _rms(m, pg_ref[...])


def _mix_out_even(y_hy_l, y_hy_c, y_ss_l, y_ss_c, proj, norm_g, w_out, x, ctx3, mod, post_g, tm):
    batch, n, d = x.shape
    G = batch + 1
    c = HY_CH
    z_blk = (3 * HY_CH) // SSM_INNER
    lat_c, ctx_c = _lat_ctx_specs(batch, n // tm, tm, c)
    lat_d, ctx_d = _lat_ctx_specs(batch, n // tm, tm, d)
    return pl.pallas_call(
        functools.partial(_mix_out_even_body, batch=batch),
        grid=(G, n // tm),
        in_specs=[lat_c, ctx_c, lat_c, ctx_c,
                  pl.BlockSpec((None, tm, SSM_INNER), lambda g, t: (g, t, z_blk)),
                  pl.BlockSpec((1, SSM_INNER), lambda g, t: (0, 0)),
                  pl.BlockSpec((2 * c, d), lambda g, t: (0, 0)),
                  lat_d, ctx_d,
                  pl.BlockSpec((None, 6, d), lambda g, t: (g, 0, 0)),
                  pl.BlockSpec((1, d), lambda g, t: (0, 0))],
        out_specs=pl.BlockSpec((None, tm, d), lambda g, t: (g, t, 0)),
        out_shape=jax.ShapeDtypeStruct((G, n, d), F32),
        compiler_params=_cparams("arbitrary", "arbitrary"),
        name="mix_out_even",
    )(y_hy_l, y_hy_c.reshape(1, n, c), y_ss_l, y_ss_c.reshape(1, n, c), proj, norm_g, w_out, x, ctx3, mod, post_g)


def _mix_out_odd_body(a_ref, b_ref, w_ref, h_ref, mod_ref, pg_ref, o_ref):
    half = a_ref.shape[-1]
    m = _dot(a_ref[...], w_ref[0:half, :]) + _dot(b_ref[...], w_ref[half:, :])
    o_ref[...] = h_ref[...] + mod_ref[2:3, :] * _rms(m, pg_ref[...])


def _mix_out_odd(o_a, o_b, w_out, h_all, mod, post_g, tm):
    batch, n, c = o_a.shape
    d = h_all.shape[-1]
    return pl.pallas_call(
        _mix_out_odd_body,
        grid=(batch, n // tm),
        in_specs=[pl.BlockSpec((None, tm, c), lambda g, t: (g, t, 0)),
                  pl.BlockSpec((None, tm, c), lambda g, t: (g, t, 0)),
                  pl.BlockSpec((2 * c, d), lambda g, t: (0, 0)),
                  pl.BlockSpec((None, tm, d), lambda g, t: (g, t, 0)),
                  pl.BlockSpec((None, 6, d), lambda g, t: (g, 0, 0)),
                  pl.BlockSpec((1, d), lambda g, t: (0, 0))],
        out_specs=pl.BlockSpec((None, tm, d), lambda g, t: (g, t, 0)),
        out_shape=jax.ShapeDtypeStruct((batch, n, d), F32),
        compiler_params=_cparams("arbitrary", "arbitrary"),
        name="mix_out_odd",
    )(o_a, o_b, w_out, h_all, mod, post_g)


def _ffn_body(h_ref, mod_ref, g_ref, wg_ref, wu_ref, wd_ref, pg_ref, o_ref, *, fchunk):
    h = h_ref[...]
    a = _norm_mod(h, g_ref[...], mod_ref[3:4, :], mod_ref[4:5, :]).astype(BF16)
    ff = wg_ref.shape[1]
    f = jnp.zeros(h.shape, F32)
    for j in range(0, ff, fchunk):
        gate = _dot(a, wg_ref[:, j:j + fchunk])
        up = _dot(a, wu_ref[:, j:j + fchunk])
        f = f + _dot((_silu(gate) * up).astype(BF16), wd_ref[j:j + fchunk, :])
    o_ref[...] = h + mod_ref[5:6, :] * _rms(f, pg_ref[...])


def _ffn(h_all, mod, g_pre, wg, wu, wd, post_g, tm):
    G, n, d = h_all.shape
    ff = wg.shape[1]
    return pl.pallas_call(
        functools.partial(_ffn_body, fchunk=256),
        grid=(G, n // tm),
        in_specs=[pl.BlockSpec((None, tm, d), lambda g, t: (g, t, 0)),
                  pl.BlockSpec((None, 6, d), lambda g, t: (g, 0, 0)),
                  pl.BlockSpec((1, d), lambda g, t: (0, 0)),
                  pl.BlockSpec((d, ff), lambda g, t: (0, 0), pipeline_mode=pl.Buffered(1)),
                  pl.BlockSpec((d, ff), lambda g, t: (0, 0), pipeline_mode=pl.Buffered(1)),
                  pl.BlockSpec((ff, d), lambda g, t: (0, 0), pipeline_mode=pl.Buffered(1)),
                  pl.BlockSpec((1, d), lambda g, t: (0, 0))],
        out_specs=pl.BlockSpec((None, tm, d), lambda g, t: (g, t, 0)),
        out_shape=jax.ShapeDtypeStruct((G, n, d), F32),
        compiler_params=_cparams("arbitrary", "arbitrary"),
        name="ffn",
    )(h_all, mod, g_pre, wg, wu, wd, post_g)


def _rope(x, cos, sin_a, sin_b, half):
    return x * cos + pltpu.roll(x, LANES - half, axis=1) * sin_a + pltpu.roll(x, half, axis=1) * sin_b


def _odd_in_body(h_ref, mod_ref, g_ref, w_ref, qg_ref, wq_ref, kg_ref, wk_ref, wv_ref, rp_ref,
                 qm_ref, km_ref, vm_ref, qw_ref, kw_ref, vw_ref):
    a = _norm_mod(h_ref[...], g_ref[...], mod_ref[0:1, :], mod_ref[1:2, :]).astype(BF16)
    ql, kl = MLA_Q_LORA, MLA_KV_LORA
    o_qg = ql + kl
    o_kg = o_qg + GQA_HEADS * HEAD_DIM
    o_vg = o_kg + GQA_KV * HEAD_DIM
    o_kr = o_vg + GQA_KV * HEAD_DIM
    cm, s1m, s2m = rp_ref[:, 0:LANES], rp_ref[:, LANES:2 * LANES], rp_ref[:, 2 * LANES:3 * LANES]
    cg, s1g, s2g = rp_ref[:, 3 * LANES:4 * LANES], rp_ref[:, 4 * LANES:5 * LANES], rp_ref[:, 5 * LANES:6 * LANES]
    mla_scale = (MLA_NOPE + MLA_ROPE) ** -0.5
    gqa_scale = HEAD_DIM ** -0.5
    q_lat = _rms(_dot(a, w_ref[:, 0:ql]), qg_ref[...]).astype(BF16)
    q = _dot(q_lat, wq_ref[...])
    kv_lat = _rms(_dot(a, w_ref[:, ql:o_qg]), kg_ref[...]).astype(BF16)
    k = _dot(kv_lat, wk_ref[...])
    vm_ref[...] = _dot(kv_lat, wv_ref[...]).astype(BF16)
    k_rope = _rope(_dot(a, w_ref[:, o_kr:o_kr + LANES]), cm, s1m, s2m, MLA_ROPE // 2)
    for hd in range(MLA_HEADS):
        sl = slice(hd * LANES, (hd + 1) * LANES)
        qm_ref[:, sl] = (_rope(q[:, sl], cm, s1m, s2m, MLA_ROPE // 2) * mla_scale).astype(BF16)
        km_ref[:, sl] = (k[:, sl] + k_rope).astype(BF16)
    qg = _dot(a, w_ref[:, o_qg:o_kg])
    for blk in range(GQA_HEADS * HEAD_DIM // LANES):
        sl = slice(blk * LANES, (blk + 1) * LANES)
        qw_ref[:, sl] = (_rope(qg[:, sl], cg, s1g, s2g, HEAD_DIM // 2) * gqa_scale).astype(BF16)
    kw_ref[...] = _rope(_dot(a, w_ref[:, o_kg:o_vg]), cg, s1g, s2g, HEAD_DIM // 2).astype(BF16)
    vw_ref[...] = _dot(a, w_ref[:, o_vg:o_kr]).astype(BF16)


def _rope_tables(n):
    rows = (jnp.arange(n) // GRID_W).astype(F32)
    cols = (jnp.arange(n) % GRID_W).astype(F32)

    def cs(rot_dim):
        nf = rot_dim // 4
        inv = ROPE_BASE ** (-jnp.arange(nf, dtype=F32) / nf)
        ang = jnp.concatenate([rows[:, None] * inv, cols[:, None] * inv], axis=-1)
        return jnp.cos(ang), jnp.sin(ang)

    one = jnp.ones((n, 1), F32)
    zero = jnp.zeros((n, 1), F32)
    c, s = cs(MLA_ROPE)
    hm = MLA_ROPE // 2
    pad_n, pad_t = MLA_NOPE, LANES - MLA_NOPE - MLA_ROPE
    cm = jnp.concatenate([jnp.tile(one, (1, pad_n)), c, c, jnp.tile(one, (1, pad_t))], axis=1)
    s1m = jnp.concatenate([jnp.tile(zero, (1, pad_n)), -s, jnp.tile(zero, (1, hm + pad_t))], axis=1)
    s2m = jnp.concatenate([jnp.tile(zero, (1, pad_n + hm)), s, jnp.tile(zero, (1, pad_t))], axis=1)
    c, s = cs(HEAD_DIM)
    z32 = jnp.zeros_like(s)
    cg = jnp.concatenate([c, c, c, c], axis=1)
    s1g = jnp.concatenate([-s, z32, -s, z32], axis=1)
    s2g = jnp.concatenate([z32, s, z32, s], axis=1)
    lat = jnp.concatenate([cm, s1m, s2m, cg, s1g, s2g], axis=1)
    ident = jnp.concatenate([jnp.ones((n, LANES), F32), jnp.zeros((n, 2 * LANES), F32)] * 2, axis=1)
    return jnp.stack([lat, ident])


def _odd_in(h_all, mod, g_pre, w_in, q_norm_g, w_q_up, kv_norm_g, w_kv_up, tm):
    G, n, d = h_all.shape
    batch = G - 1
    ql, kl, rp = MLA_Q_LORA, MLA_KV_LORA, MLA_ROPE
    gq, gk = GQA_HEADS * HEAD_DIM, GQA_KV * HEAD_DIM
    o = [0, ql, ql + kl, ql + kl + rp, ql + kl + rp + gq, ql + kl + rp + gq + gk]
    kr_cols = jnp.pad(w_in[:, o[2]:o[3]], ((0, 0), (MLA_NOPE, LANES - MLA_NOPE - rp)))
    w = jnp.concatenate([w_in[:, o[0]:o[2]], w_in[:, o[3]:], kr_cols], axis=1).astype(BF16)
    hq = MLA_NOPE + MLA_ROPE
    wq = jnp.pad(w_q_up.reshape(ql, MLA_HEADS, hq), ((0, 0), (0, 0), (0, LANES - hq)))
    wq = wq.reshape(ql, MLA_HEADS * LANES).astype(BF16)
    wkv = w_kv_up.reshape(kl, MLA_HEADS, MLA_NOPE + MLA_V)
    wk = jnp.pad(wkv[:, :, :MLA_NOPE], ((0, 0), (0, 0), (0, LANES - MLA_NOPE)))
    wk = wk.reshape(kl, MLA_HEADS * LANES).astype(BF16)
    wv = wkv[:, :, MLA_NOPE:].reshape(kl, MLA_HEADS * MLA_V).astype(BF16)
    ropes = _rope_tables(n)
    nw = w.shape[1]
    full = lambda shape: pl.BlockSpec(shape, lambda g, t: (0,) * len(shape))
    blk = lambda width: pl.BlockSpec((None, tm, width), lambda g, t: (g, t, 0))
    widths = [MLA_HEADS * LANES, MLA_HEADS * LANES, MLA_HEADS * MLA_V, gq, gk, gk]
    return pl.pallas_call(
        _odd_in_body,
        grid=(G, n // tm),
        in_specs=[blk(d), pl.BlockSpec((None, 6, d), lambda g, t: (g, 0, 0)), full((1, d)), full((d, nw)),
                  full((1, ql)), full((ql, MLA_HEADS * LANES)), full((1, kl)), full((kl, MLA_HEADS * LANES)),
                  full((kl, MLA_HEADS * MLA_V)),
                  pl.BlockSpec((None, tm, 6 * LANES), lambda g, t: (g // batch, t, 0))],
        out_specs=[blk(wd) for wd in widths],
        out_shape=[jax.ShapeDtypeStruct((G, n, wd), BF16) for wd in widths],
        compiler_params=_cparams("arbitrary", "arbitrary"),
        name="odd_in",
    )(h_all, mod, g_pre, w, q_norm_g[None, :], wq, kv_norm_g[None, :], wk, wv, ropes)


def _mla_body(q_ref, kl_ref, kc_ref, vl_ref, vc_ref, o_ref, *, hps):
    lane = lax.broadcasted_iota(jnp.int32, (o_ref.shape[0], LANES), 1)
    outs = []
    for hd in range(hps):
        sl = slice(hd * LANES, (hd + 1) * LANES)
        vs = slice((hd // 2) * LANES, (hd // 2 + 1) * LANES)
        q = q_ref[:, sl]
        s_l = _dot_nt(q, kl_ref[:, sl])
        s_c = _dot_nt(q, kc_ref[:, sl])
        m = jnp.maximum(jnp.max(s_l, axis=-1, keepdims=True), jnp.max(s_c, axis=-1, keepdims=True))
        p_l = jnp.exp(s_l - m)
        p_c = jnp.exp(s_c - m)
        den = jnp.sum(p_l, axis=-1, keepdims=True) + jnp.sum(p_c, axis=-1, keepdims=True)
        pv = _dot(p_l.astype(BF16), vl_ref[:, vs]) + _dot(p_c.astype(BF16), vc_ref[:, vs])
        outs.append(pv / den)
    for pr in range(hps // 2):
        o_ref[:, pr * LANES:(pr + 1) * LANES] = jnp.where(lane < MLA_V, outs[2 * pr], outs[2 * pr + 1]).astype(BF16)


def _mla_attn(q, k, v, batch, n, ctx_len, tq, hps):
    return pl.pallas_call(
        functools.partial(_mla_body, hps=hps),
        grid=(batch, MLA_HEADS // hps, n // tq),
        in_specs=[pl.BlockSpec((None, tq, hps * LANES), lambda b, h, i: (b, i, h)),
                  pl.BlockSpec((None, n, hps * LANES), lambda b, h, i: (b, 0, h)),
                  pl.BlockSpec((None, ctx_len, hps * LANES), lambda b, h, i: (batch, b, h)),
                  pl.BlockSpec((None, n, hps * MLA_V), lambda b, h, i: (b, 0, h)),
                  pl.BlockSpec((None, ctx_len, hps * MLA_V), lambda b, h, i: (batch, b, h))],
        out_specs=pl.BlockSpec((None, tq, hps * MLA_V), lambda b, h, i: (b, i, h)),
        out_shape=jax.ShapeDtypeStruct((batch, n, MLA_HEADS * MLA_V), BF16),
        compiler_params=_cparams("arbitrary", "arbitrary", "arbitrary"),
        name="mla_attn",
    )(q, k, k, v, v)


def _dup_half(x, kv):
    lane = lax.broadcasted_iota(jnp.int32, x.shape, 1)
    swapped = pltpu.roll(x, HEAD_DIM, axis=1)
    lo = lane < HEAD_DIM
    return jnp.where(lo, x, swapped) if kv == 0 else jnp.where(lo, swapped, x)


def _win_body(sink_ref, q_ref, kp_ref, kc_ref, kn_ref, kx_ref, vp_ref, vc_ref, vn_ref, vx_ref, o_ref, *, tq):
    i = pl.program_id(1)
    nb = pl.num_programs(1)
    w = WINDOW
    g = GQA_HEADS // GQA_KV
    ri = lax.broadcasted_iota(jnp.int32, (g * tq, w), 0) % tq
    ci = lax.broadcasted_iota(jnp.int32, (g * tq, w), 1)
    ok_prev = (ci >= ri) & (i > 0)
    ok_next = (ci <= ri - (tq - w)) & (i < nb - 1)
    rc = lax.broadcasted_iota(jnp.int32, (g * tq, tq), 0) % tq
    cc = lax.broadcasted_iota(jnp.int32, (g * tq, tq), 1)
    ok_cur = jnp.abs(cc - rc) <= w
    lane = lax.broadcasted_iota(jnp.int32, (tq, LANES), 1)
    lo = lane < HEAD_DIM
    neg = -jnp.inf
    for kv in range(GQA_KV):
        rows = []
        for hh in range(g):
            hd = kv * g + hh
            blk = q_ref[:, (hd // 2) * LANES:(hd // 2 + 1) * LANES].astype(F32)
            rows.append(jnp.where(lo if hd % 2 == 0 else ~lo, blk, 0.0).astype(BF16))
        q = jnp.concatenate(rows, axis=0)
        ks = [_dup_half(r[...].astype(F32), kv).astype(BF16) for r in (kp_ref, kc_ref, kn_ref, kx_ref)]
        vs = [_dup_half(r[...].astype(F32), kv).astype(BF16) for r in (vp_ref, vc_ref, vn_ref, vx_ref)]
        s_p = jnp.where(ok_prev, _dot_nt(q, ks[0]), neg)
        s_c = jnp.where(ok_cur, _dot_nt(q, ks[1]), neg)
        s_n = jnp.where(ok_next, _dot_nt(q, ks[2]), neg)
        s_x = _dot_nt(q, ks[3])
        hrow = lax.broadcasted_iota(jnp.int32, (g * tq, 1), 0) // tq
        sk = jnp.zeros((g * tq, 1), F32)
        for hh in range(g):
            sk = jnp.where(hrow == hh, sink_ref[kv * g + hh], sk)
        m = jnp.maximum(jnp.maximum(jnp.max(s_p, axis=-1, keepdims=True), jnp.max(s_c, axis=-1, keepdims=True)),
                        jnp.maximum(jnp.max(s_n, axis=-1, keepdims=True), jnp.max(s_x, axis=-1, keepdims=True)))
        m = jnp.maximum(m, sk)
        ps = [jnp.exp(s - m) for s in (s_p, s_c, s_n, s_x)]
        den = jnp.exp(sk - m)
        for p in ps:
            den = den + jnp.sum(p, axis=-1, keepdims=True)
        acc = jnp.zeros((g * tq, LANES), F32)
        for p, v in zip(ps, vs):
            acc = acc + _dot(p.astype(BF16), v)
        res = acc / den
        for pr in range(g // 2):
            blk = (kv * g) // 2 + pr
            o_ref[:, blk * LANES:(blk + 1) * LANES] = jnp.where(
                lo, res[(2 * pr) * tq:(2 * pr + 1) * tq, :], res[(2 * pr + 1) * tq:(2 * pr + 2) * tq, :]).astype(BF16)


def _win_attn(q, k, v, sink, batch, n, ctx_len, tq):
    w = WINDOW
    per = tq // w
    gq, gk = GQA_HEADS * HEAD_DIM, GQA_KV * HEAD_DIM
    prev = lambda b, i: (b, jnp.maximum(i * per - 1, 0), 0)
    cur = lambda b, i: (b, i, 0)
    nxt = lambda b, i: (b, jnp.minimum((i + 1) * per, n // w - 1), 0)
    ctx = lambda b, i: (batch, b, 0)
    kspec = lambda rows, im: pl.BlockSpec((None, rows, gk), im)
    return pl.pallas_call(
        functools.partial(_win_body, tq=tq),
        grid=(batch, n // tq),
        in_specs=[pl.BlockSpec(memory_space=pltpu.SMEM),
                  pl.BlockSpec((None, tq, gq), cur),
                  kspec(w, prev), kspec(tq, cur), kspec(w, nxt), kspec(ctx_len, ctx),
                  kspec(w, prev), kspec(tq, cur), kspec(w, nxt), kspec(ctx_len, ctx)],
        out_specs=pl.BlockSpec((None, tq, gq), cur),
        out_shape=jax.ShapeDtypeStruct((batch, n, gq), BF16),
        compiler_params=_cparams("arbitrary", "arbitrary"),
        name="win_attn",
    )(sink, q, k, k, k, k, v, v, v, v)


def _route_body(h_ref, mod_ref, g_ref, r_ref, a_ref, info_ref, cnt_ref, carry_ref):
    first = (pl.program_id(0) == 0) & (pl.program_id(1) == 0)

    @pl.when(first)
    def _():
        carry_ref[...] = jnp.zeros_like(carry_ref)

    a = _norm_mod(h_ref[...], g_ref[...], mod_ref[3:4, :], mod_ref[4:5, :])
    a_ref[...] = a
    tm = a.shape[0]
    logits = jnp.dot(a, r_ref[...], precision=HIGHEST, preferred_element_type=F32)
    lane = lax.broadcasted_iota(jnp.int32, logits.shape, 1)
    neg = -jnp.inf
    logits = jnp.where(lane < N_EXPERTS, logits, neg)
    m1 = jnp.max(logits, axis=-1, keepdims=True)
    i1 = jnp.min(jnp.where(logits == m1, lane, LANES), axis=-1, keepdims=True)
    rest = jnp.where(lane == i1, neg, logits)
    m2 = jnp.max(rest, axis=-1, keepdims=True)
    i2 = jnp.min(jnp.where(rest == m2, lane, LANES), axis=-1, keepdims=True)
    e2 = jnp.exp(m2 - m1)
    w1 = 1.0 / (1.0 + e2)
    w2 = e2 / (1.0 + e2)
    chosen = ((lane == i1) | (lane == i2)).astype(F32)
    ri = lax.broadcasted_iota(jnp.int32, (tm, tm), 0)
    ci = lax.broadcasted_iota(jnp.int32, (tm, tm), 1)
    tri = (ci < ri).astype(BF16)
    before = _dot(tri, chosen.astype(BF16)) + carry_ref[...]
    p1 = jnp.sum(jnp.where(lane == i1, before, 0.0), axis=-1, keepdims=True)
    p2 = jnp.sum(jnp.where(lane == i2, before, 0.0), axis=-1, keepdims=True)
    carry_ref[...] += jnp.sum(chosen, axis=0, keepdims=True)
    cnt_ref[...] = carry_ref[...]
    vals = (i1.astype(F32), i2.astype(F32), w1, w2, p1, p2)
    info = jnp.zeros(logits.shape, F32)
    for j, v in enumerate(vals):
        info = jnp.where(lane == j, v, info)
    info_ref[...] = info


def _route(h_lat, mod, g_pre, router, tm):
    batch, n, d = h_lat.shape
    rp = jnp.pad(router, ((0, 0), (0, LANES - N_EXPERTS)))
    nt = n // tm
    return pl.pallas_call(
        _route_body,
        grid=(batch, nt),
        in_specs=[pl.BlockSpec((None, tm, d), lambda g, t: (g, t, 0)),
                  pl.BlockSpec((None, 6, d), lambda g, t: (g, 0, 0)),
                  pl.BlockSpec((1, d), lambda g, t: (0, 0)),
                  pl.BlockSpec((d, LANES), lambda g, t: (0, 0))],
        out_specs=[pl.BlockSpec((tm, d), lambda g, t: (g * nt + t, 0)),
                   pl.BlockSpec((tm, LANES), lambda g, t: (g * nt + t, 0)),
                   pl.BlockSpec((1, LANES), lambda g, t: (0, 0))],
        out_shape=[jax.ShapeDtypeStruct((batch * n, d), F32),
                   jax.ShapeDtypeStruct((batch * n, LANES), F32),
                   jax.ShapeDtypeStruct((1, LANES), F32)],
        scratch_shapes=[pltpu.VMEM((1, LANES), F32)],
        compiler_params=_cparams("arbitrary", "arbitrary"),
        name="moe_route",
    )(h_lat, mod, g_pre, rp)


def _row_copy(src_ref, src_row, dst_ref, dst_row, sem):
    return pltpu.make_async_copy(src_ref.at[pl.ds(src_row, 1), :], dst_ref.at[pl.ds(dst_row, 1), :], sem)


def _slot(start_ref, idx_ref, pos_ref, j):
    return start_ref[idx_ref[j]] + pos_ref[j]


def _scatter_body(start_ref, idx_ref, pos_ref, a_ref, xs_in_ref, xs_ref, sem, *, ts, n_tok):
    del xs_in_ref
    base = pl.program_id(0) * ts

    def issue(r, carry):
        for k in range(2):
            slot = _slot(start_ref, idx_ref, pos_ref, k * n_tok + base + r)
            _row_copy(a_ref, r, xs_ref, slot, sem).start(priority=k)
        return carry

    lax.fori_loop(0, ts, issue, 0, unroll=8)
    for k in range(2):
        pltpu.make_async_copy(a_ref, xs_ref.at[pl.ds(0, ts), :], sem).wait()


def _scatter(starts, idx, pos, a, n_slots, ts):
    n_tok, d = a.shape
    zeros = jnp.zeros((n_slots, d), F32)
    return pl.pallas_call(
        functools.partial(_scatter_body, ts=ts, n_tok=n_tok),
        grid_spec=pltpu.PrefetchScalarGridSpec(
            num_scalar_prefetch=3,
            grid=(n_tok // ts,),
            in_specs=[pl.BlockSpec((ts, d), lambda i, *_: (i, 0)),
                      pl.BlockSpec(memory_space=pl.ANY)],
            out_specs=pl.BlockSpec(memory_space=pl.ANY),
            scratch_shapes=[pltpu.SemaphoreType.DMA(())]),
        out_shape=jax.ShapeDtypeStruct((n_slots, d), F32),
        input_output_aliases={4: 0},
        compiler_params=_cparams("arbitrary"),
        name="moe_scatter",
    )(starts, idx, pos, a, zeros)


def _experts_body(te_ref, tv_ref, x_ref, wg_ref, wu_ref, wd_ref, y_ref, acc_ref):
    del te_ref
    j = pl.program_id(0)
    f = pl.program_id(1)
    valid = tv_ref[j]

    @pl.when(f == 0)
    def _():
        acc_ref[...] = jnp.zeros_like(acc_ref)

    n_sub = (valid + MOE_SUB - 1) // MOE_SUB
    for ns in range(1, MOE_TILE // MOE_SUB + 1):
        rows = slice(0, ns * MOE_SUB)

        @pl.when(n_sub == ns)
        def _(rows=rows):
            x = x_ref[rows, :].astype(BF16)
            gate = _dot(x, wg_ref[...].astype(BF16))
            up = _dot(x, wu_ref[...].astype(BF16))
            hid = (_silu(gate) * up).astype(BF16)
            acc_ref[rows, :] += _dot(hid, wd_ref[...].astype(BF16))

    @pl.when(f == pl.num_programs(1) - 1)
    def _():
        y_ref[...] = acc_ref[...]


def _experts(tile_expert, tile_valid, xs, wg, wu, wd):
    n_slots, d = xs.shape
    ff = wg.shape[-1]
    n_tiles = n_slots // MOE_TILE
    tf = MOE_FCHUNK
    return pl.pallas_call(
        _experts_body,
        grid_spec=pltpu.PrefetchScalarGridSpec(
            num_scalar_prefetch=2,
            grid=(n_tiles, ff // tf),
            in_specs=[pl.BlockSpec((MOE_TILE, d), lambda j, f, te, tv: (j, 0)),
                      pl.BlockSpec((None, d, tf), lambda j, f, te, tv: (te[j], 0, f)),
                      pl.BlockSpec((None, d, tf), lambda j, f, te, tv: (te[j], 0, f)),
                      pl.BlockSpec((None, tf, d), lambda j, f, te, tv: (te[j], f, 0))],
            out_specs=pl.BlockSpec((MOE_TILE, d), lambda j, f, te, tv: (j, 0)),
            scratch_shapes=[pltpu.VMEM((MOE_TILE, d), F32)]),
        out_shape=jax.ShapeDtypeStruct((n_slots, d), F32),
        compiler_params=_cparams("arbitrary", "arbitrary"),
        name="moe_experts",
    )(tile_expert, tile_valid, xs, wg, wu, wd)


def _combine_body(start_ref, idx_ref, pos_ref, ys_ref, info_ref, h_ref, mod_ref, pg_ref, o_ref, buf_ref, sem,
                  *, tc, n_tok):
    base = (pl.program_id(0) * pl.num_programs(1) + pl.program_id(1)) * tc

    def issue(r, carry):
        for k in range(2):
            slot = _slot(start_ref, idx_ref, pos_ref, k * n_tok + base + r)
            _row_copy(ys_ref, slot, buf_ref.at[k], r, sem).start(priority=k)
        return carry

    lax.fori_loop(0, tc, issue, 0, unroll=8)
    for k in range(2):
        pltpu.make_async_copy(ys_ref.at[pl.ds(0, tc), :], buf_ref.at[k], sem).wait()
    f = info_ref[:, 2:3] * buf_ref[0] + info_ref[:, 3:4] * buf_ref[1]
    o_ref[...] = h_ref[...] + mod_ref[5:6, :] * _rms(f, pg_ref[...])


def _combine(starts, idx, pos, ys, info, h_lat, mod, post_g, tc):
    batch, n, d = h_lat.shape
    nt = n // tc
    return pl.pallas_call(
        functools.partial(_combine_body, tc=tc, n_tok=batch * n),
        grid_spec=pltpu.PrefetchScalarGridSpec(
            num_scalar_prefetch=3,
            grid=(batch, nt),
            in_specs=[pl.BlockSpec(memory_space=pl.ANY),
                      pl.BlockSpec((tc, LANES), lambda g, t, *_: (g * nt + t, 0)),
                      pl.BlockSpec((None, tc, d), lambda g, t, *_: (g, t, 0)),
                      pl.BlockSpec((None, 6, d), lambda g, t, *_: (g, 0, 0)),
                      pl.BlockSpec((1, d), lambda g, t, *_: (0, 0))],
            out_specs=pl.BlockSpec((None, tc, d), lambda g, t, *_: (g, t, 0)),
            scratch_shapes=[pltpu.VMEM((2, tc, d), F32), pltpu.SemaphoreType.DMA(())]),
        out_shape=jax.ShapeDtypeStruct((batch, n, d), F32),
        compiler_params=_cparams("arbitrary", "arbitrary"),
        name="moe_combine",
    )(starts, idx, pos, ys, info, h_lat, mod, post_g)


def _moe(h_lat, mod, g_pre, router, wg, wu, wd, post_g, tm):
    batch, n, d = h_lat.shape
    n_tok = batch * n
    a, info, counts = _route(h_lat, mod, g_pre, router, tm)
    cnt = counts[0, :N_EXPERTS].astype(jnp.int32)
    padded = ((cnt + MOE_TILE - 1) // MOE_TILE) * MOE_TILE
    ends = jnp.cumsum(padded)
    starts = (ends - padded).astype(jnp.int32)
    idx = info[:, 0:2].T.reshape(2 * n_tok).astype(jnp.int32)
    pos = info[:, 4:6].T.reshape(2 * n_tok).astype(jnp.int32)
    n_tiles = (2 * n_tok + N_EXPERTS * (MOE_TILE - 1)) // MOE_TILE
    tile_row = jnp.arange(n_tiles, dtype=jnp.int32) * MOE_TILE
    tile_expert = jnp.minimum(jnp.sum(tile_row[:, None] >= ends[None, :], axis=1), N_EXPERTS - 1).astype(jnp.int32)
    tile_valid = jnp.clip(cnt[tile_expert] - (tile_row - starts[tile_expert]), 0, MOE_TILE)
    tile_valid = jnp.where(tile_row < ends[-1], tile_valid, 0).astype(jnp.int32)
    xs = _scatter(starts, idx, pos, a, n_tiles * MOE_TILE, tm)
    ys = _experts(tile_expert, tile_valid, xs, wg, wu, wd)
    return _combine(starts, idx, pos, ys, info, h_lat, mod, post_g, tm)


def kernel(x, c, ctx, c_ctx, ada_w, ada_b, mix_pre_g, mix_post_g, ffn_pre_g, ffn_post_g, ev_w_in, ev_hy_conv_w, ev_hy_conv_b, ev_hy_filt_w1, ev_hy_filt_b1, ev_hy_filt_w2, ev_hy_filt_b2, ev_hy_filt_w3, ev_hy_freq, ev_hy_bias, ev_ssm_conv_w, ev_ssm_conv_b, ev_ssm_dt_bias, ev_ssm_a_log, ev_ssm_d, ev_ssm_norm_g, ev_w_out, ev_ffn_w_gate, ev_ffn_w_up, ev_ffn_w_down, od_w_in, od_mla_q_norm_g, od_mla_w_q_up, od_mla_kv_norm_g, od_mla_w_kv_up, od_gqa_sink, od_w_out, od_router, od_moe_w_gate, od_moe_w_up, od_moe_w_down):
    batch, n, d = x.shape
    ctx_len = ctx.shape[1]
    assert batch * ctx_len == n and ada_w.shape[0] == 2
    G = batch + 1
    tm = min(n, 512)

    cond = jnp.concatenate([c, c_ctx[None, :], jnp.zeros((16 - G, d), F32)], axis=0)
    mods = _ada(cond, ada_w, ada_b).reshape(2, 16, 6, d)
    ctx3 = ctx.reshape(1, n, d)

    n_main = 3 * HY_CH + SSM_INNER + SSM_CONV_CH
    w_in = ev_w_in[0]
    w_main = w_in[:, :n_main].astype(BF16)
    w_dt = jnp.pad(w_in[:, n_main:], ((0, 0), (0, LANES - 2 * SSM_HEADS))).astype(BF16)
    proj, dt = _even_in(x, ctx3, mods[0], mix_pre_g[0][None, :], w_main, w_dt, tm)
    y_hy_l, y_hy_c = _hyena(proj, batch, n, ctx_len,
                            (ev_hy_conv_w[0], ev_hy_conv_b[0], ev_hy_filt_w1[0], ev_hy_filt_b1[0], ev_hy_filt_w2[0],
                             ev_hy_filt_b2[0], ev_hy_filt_w3[0], ev_hy_freq[0], ev_hy_bias[0]))
    y_ss_l, y_ss_c = _ssd(proj, dt, batch, n, ctx_len,
                          (ev_ssm_conv_w[0], ev_ssm_conv_b[0], ev_ssm_dt_bias[0], ev_ssm_a_log[0], ev_ssm_d[0]))
    h_all = _mix_out_even(y_hy_l, y_hy_c, y_ss_l, y_ss_c, proj, ev_ssm_norm_g[0][None, :],
                          ev_w_out[0].astype(BF16), x, ctx3, mods[0], mix_post_g[0][None, :], tm)
    h_all = _ffn(h_all, mods[0], ffn_pre_g[0][None, :], ev_ffn_w_gate[0].astype(BF16), ev_ffn_w_up[0].astype(BF16),
                 ev_ffn_w_down[0].astype(BF16), ffn_post_g[0][None, :], min(n, 1024))

    q_m, k_m, v_m, q_w, k_w, v_w = _odd_in(h_all, mods[1], mix_pre_g[1][None, :], od_w_in[0], od_mla_q_norm_g[0],
                                           od_mla_w_q_up[0], od_mla_kv_norm_g[0], od_mla_w_kv_up[0], tm)
    o_mla = _mla_attn(q_m, k_m, v_m, batch, n, ctx_len, min(n, 512), 4)
    o_win = _win_attn(q_w, k_w, v_w, od_gqa_sink[0], batch, n, ctx_len, min(n, 2 * WINDOW))
    h_lat = _mix_out_odd(o_mla, o_win, od_w_out[0].astype(BF16), h_all, mods[1], mix_post_g[1][None, :], tm)
    return _moe(h_lat, mods[1], ffn_pre_g[1][None, :], od_router[0], od_moe_w_gate[0], od_moe_w_up[0],
                od_moe_w_down[0], ffn_post_g[1][None, :], tm)
```

```python
import functools
import math

import jax
import jax.numpy as jnp
from jax import lax
from jax.experimental import pallas as pl
from jax.experimental.pallas import tpu as pltpu

F32 = jnp.float32
BF16 = jnp.bfloat16
HIGHEST = lax.Precision.HIGHEST

D_MODEL = 1024
GRID_W = 64
EPS = 1e-6
HEAD_DIM = 64
GROUP_WIDTH = D_MODEL // 2

HY_CH = GROUP_WIDTH
HY_BANDS = 16
HY_EMB = 2 * HY_BANDS + 1
HY_FILT_HID = 64
HY_FAST_DECAY = 0.3
HY_SLOW_DECAY = 1.5
HY_DECAY_TARGET = 1e-2

SSM_INNER = GROUP_WIDTH
SSM_HEADDIM = 64
SSM_HEADS = SSM_INNER // SSM_HEADDIM
SSM_GROUPS = 2
SSM_STATE = 128
SSM_CHUNK = 128
SSM_CONV_CH = SSM_INNER + 2 * SSM_GROUPS * SSM_STATE

MLA_HEADS = GROUP_WIDTH // HEAD_DIM
MLA_NOPE = 64
MLA_ROPE = 32
MLA_V = 64
MLA_Q_LORA = D_MODEL // 4
MLA_KV_LORA = D_MODEL // 8

GQA_HEADS = GROUP_WIDTH // HEAD_DIM
GQA_KV = GQA_HEADS // 4
WINDOW = 128
ROPE_BASE = 10000.0

D_FF = ((8 * D_MODEL // 3 + 127) // 128) * 128
N_EXPERTS = 8
D_FF_EXPERT = 7 * D_MODEL // 2

LANES = 128
VMEM_LIMIT = 56 * 1024 * 1024
MOE_TILE = 1024
MOE_SUB = 256
MOE_FCHUNK = 512


def _cparams(*sem):
    return pltpu.CompilerParams(dimension_semantics=sem, vmem_limit_bytes=VMEM_LIMIT)


def _silu(x):
    return x / (1.0 + jnp.exp(-x))


def _rms(x, g):
    return x * lax.rsqrt(jnp.mean(x * x, axis=-1, keepdims=True) + EPS) * g


def _norm_mod(h, g, shift, scale):
    return _rms(h, g) * (1.0 + scale) + shift


def _dot(a, b):
    return jnp.dot(a, b, preferred_element_type=F32)


def _dot_nt(a, b):
    return lax.dot_general(a, b, (((1,), (1,)), ((), ())), preferred_element_type=F32)


def _ada_body(c_ref, w_ref, b_ref, o_ref):
    s = _silu(c_ref[...]).astype(BF16)
    o_ref[0] = _dot(s, w_ref[0].astype(BF16)) + b_ref[0]


def _ada(cc, ada_w, ada_b):
    depth, d, n6 = ada_w.shape
    rows = cc.shape[0]
    tn = 1536
    return pl.pallas_call(
        _ada_body,
        grid=(depth, n6 // tn),
        in_specs=[pl.BlockSpec((rows, d), lambda l, j: (0, 0)),
                  pl.BlockSpec((1, d, tn), lambda l, j: (l, 0, j)),
                  pl.BlockSpec((1, 1, tn), lambda l, j: (l, 0, j))],
        out_specs=pl.BlockSpec((1, rows, tn), lambda l, j: (l, 0, j)),
        out_shape=jax.ShapeDtypeStruct((depth, rows, n6), F32),
        compiler_params=_cparams("arbitrary", "arbitrary"),
        name="ada",
    )(cc, ada_w, ada_b.reshape(depth, 1, n6))


def _lat_ctx_specs(batch, nt, tm, d):
    lat = pl.BlockSpec((None, tm, d), lambda g, t: (jnp.minimum(g, batch - 1), jnp.where(g < batch, t, nt - 1), 0))
    ctx = pl.BlockSpec((None, tm, d), lambda g, t: (0, jnp.where(g < batch, 0, t), 0))
    return lat, ctx


def _even_in_body(x_ref, c_ref, mod_ref, g_ref, w_ref, wdt_ref, o_ref, dt_ref, *, batch):
    h = jnp.where(pl.program_id(0) == batch, c_ref[...], x_ref[...])
    a = _norm_mod(h, g_ref[...], mod_ref[0:1, :], mod_ref[1:2, :]).astype(BF16)
    n_out = o_ref.shape[-1]
    for j in range(0, n_out, 512):
        o_ref[:, j:j + 512] = _dot(a, w_ref[:, j:j + 512]).astype(BF16)
    dt_ref[...] = _dot(a, wdt_ref[...])


def _even_in(x, ctx3, mod, g_pre, w_main, w_dt, tm):
    batch, n, d = x.shape
    G = batch + 1
    n_main = w_main.shape[1]
    lat_spec, ctx_spec = _lat_ctx_specs(batch, n // tm, tm, d)
    return pl.pallas_call(
        functools.partial(_even_in_body, batch=batch),
        grid=(G, n // tm),
        in_specs=[lat_spec, ctx_spec,
                  pl.BlockSpec((None, 6, d), lambda g, t: (g, 0, 0)),
                  pl.BlockSpec((1, d), lambda g, t: (0, 0)),
                  pl.BlockSpec((d, n_main), lambda g, t: (0, 0)),
                  pl.BlockSpec((d, LANES), lambda g, t: (0, 0))],
        out_specs=[pl.BlockSpec((None, tm, n_main), lambda g, t: (g, t, 0)),
                   pl.BlockSpec((None, tm, LANES), lambda g, t: (g, t, 0))],
        out_shape=[jax.ShapeDtypeStruct((G, n, n_main), BF16),
                   jax.ShapeDtypeStruct((G, n, LANES), F32)],
        compiler_params=_cparams("arbitrary", "arbitrary"),
        name="even_in",
    )(x, ctx3, mod, g_pre, w_main, w_dt)


def _dwconv3(x, w, b):
    L = x.shape[0]
    row = lax.broadcasted_iota(jnp.int32, x.shape, 0)
    prev = jnp.where(row >= 1, pltpu.roll(x, 1, axis=0), 0.0)
    nxt = jnp.where(row < L - 1, pltpu.roll(x, L - 1, axis=0), 0.0)
    return prev * w[0:1, :] + x * w[1:2, :] + nxt * w[2:3, :] + b


def _hy_pre_body(p_ref, cw_ref, cb_ref, u_ref, x0_ref):
    c = HY_CH
    parts = []
    for k in range(3):
        x = p_ref[:, k * c:(k + 1) * c].astype(F32)
        parts.append(_dwconv3(x, cw_ref[:, k * c:(k + 1) * c], cb_ref[:, k * c:(k + 1) * c]))
    x0_ref[...] = parts[0].astype(BF16)
    u_ref[...] = (parts[1] * parts[2]).astype(BF16)


def _hy_pre(proj, nseq, L, index_map, cw, cb):
    c3 = 3 * HY_CH
    return pl.pallas_call(
        _hy_pre_body,
        grid=(nseq,),
        in_specs=[pl.BlockSpec((None, L, c3), index_map),
                  pl.BlockSpec((3, c3), lambda s: (0, 0)),
                  pl.BlockSpec((1, c3), lambda s: (0, 0))],
        out_specs=[pl.BlockSpec((None, L, HY_CH), lambda s: (s, 0, 0)),
                   pl.BlockSpec((None, L, HY_CH), lambda s: (s, 0, 0))],
        out_shape=[jax.ShapeDtypeStruct((nseq, L, HY_CH), BF16),
                   jax.ShapeDtypeStruct((nseq, L, HY_CH), BF16)],
        compiler_params=_cparams("arbitrary"),
        name="hy_pre",
    )(proj, cw, cb)


def _hy_filter_body(z_ref, t_ref, w1_ref, b1_ref, w2_ref, b2_ref, w3_ref, f_ref, dl_ref, o_ref):
    f = f_ref[...]
    h = jnp.sin(f * (jnp.dot(z_ref[...], w1_ref[...], precision=HIGHEST, preferred_element_type=F32) + b1_ref[...]))
    h = jnp.sin(f * (jnp.dot(h, w2_ref[...], precision=HIGHEST, preferred_element_type=F32) + b2_ref[...]))
    h = jnp.dot(h, w3_ref[...], precision=HIGHEST, preferred_element_type=F32)
    decay = jnp.exp(-t_ref[...] * dl_ref[...])
    hf = h[:, :HY_CH] * decay
    hb = h[:, HY_CH:] * decay
    row = lax.broadcasted_iota(jnp.int32, hb.shape, 0)
    hb = jnp.where(row == 0, 0.0, hb)
    o_ref[:, :HY_CH] = hf.astype(BF16)
    o_ref[:, HY_CH:] = hb.astype(BF16)


def _hy_filter(L, w1, b1, w2, b2, w3, freq):
    pos = jnp.arange(L, dtype=F32)
    t = pos / (L - 1)
    omega = 2.0 * math.pi * pos / L
    bands = jnp.linspace(1e-4, HY_BANDS - 1, HY_BANDS, dtype=F32)
    z = jnp.concatenate([t[:, None], jnp.cos(omega[:, None] * bands), -jnp.sin(omega[:, None] * bands)], axis=-1)
    z = jnp.pad(z, ((0, 0), (0, LANES - HY_EMB)))
    w1p = jnp.pad(w1, ((0, LANES - HY_EMB), (0, 0)))
    deltas = jnp.abs(jnp.linspace(math.log(HY_DECAY_TARGET) / HY_SLOW_DECAY,
                                  math.log(HY_DECAY_TARGET) / HY_FAST_DECAY, HY_CH, dtype=F32))
    return pl.pallas_call(
        _hy_filter_body,
        out_shape=jax.ShapeDtypeStruct((L, 2 * HY_CH), BF16),
        compiler_params=pltpu.CompilerParams(vmem_limit_bytes=VMEM_LIMIT),
        name="hy_filter",
    )(z, t[:, None], w1p, b1[None, :], w2, b2[None, :], w3, freq[None, :], deltas[None, :])


def _dft_tables(L):
    split = 64
    t = jnp.arange(L, dtype=jnp.int32)[None, :]
    k1 = jnp.arange(L // split, dtype=jnp.int32)[:, None] * split
    k0 = jnp.arange(split, dtype=jnp.int32)[:, None]
    ang1 = ((k1 * t) % (2 * L)).astype(F32) * (math.pi / L)
    ang0 = ((k0 * t) % (2 * L)).astype(F32) * (math.pi / L)
    c1, s1 = jnp.cos(ang1)[:, None, :], jnp.sin(ang1)[:, None, :]
    c0, s0 = jnp.cos(ang0)[None, :, :], jnp.sin(ang0)[None, :, :]
    fre = (c1 * c0 - s1 * s0).reshape(L, L)
    nsin = -(s1 * c0 + c1 * s0).reshape(L, L)
    k = jnp.arange(L, dtype=jnp.int32)[:, None]
    alt_t = jnp.where(t % 2 == 0, 1.0, -1.0).astype(F32)
    alt_k = jnp.where(k % 2 == 0, 1.0, -1.0).astype(F32)
    f2 = jnp.stack([fre, jnp.where(k == 0, alt_t, nsin)]).astype(BF16)
    fimt = jnp.where(t == 0, alt_k, nsin).astype(BF16)
    return f2, fimt


def _filt_dft_body(f_ref, x_ref, o_ref, *, L, tr):
    c = HY_CH
    x = x_ref[...]
    sre = _dot(f_ref[0], x)
    sim = _dot(f_ref[1], x)
    is0 = (lax.broadcasted_iota(jnp.int32, (tr, c), 0) + pl.program_id(0) * tr) == 0
    wgt = jnp.where(is0, 1.0 / (2 * L), 1.0 / L)
    hre = sre[:, :c] + sre[:, c:]
    o_ref[0] = (hre * wgt).astype(BF16)
    o_ref[1] = (jnp.where(is0, 0.0, sim[:, :c] - sim[:, c:]) * wgt).astype(BF16)
    o_ref[2] = (jnp.where(is0, sim[:, :c] + sim[:, c:], hre) * wgt).astype(BF16)


def _filt_dft(f2, hfb):
    _, L, _ = f2.shape
    cols = hfb.shape[1]
    tr = min(L, 512)
    return pl.pallas_call(
        functools.partial(_filt_dft_body, L=L, tr=tr),
        grid=(L // tr,),
        in_specs=[pl.BlockSpec((2, tr, L), lambda i: (0, i, 0)),
                  pl.BlockSpec((L, cols), lambda i: (0, 0))],
        out_specs=pl.BlockSpec((3, tr, cols // 2), lambda i: (0, i, 0)),
        out_shape=jax.ShapeDtypeStruct((3, L, cols // 2), BF16),
        compiler_params=_cparams("arbitrary"),
        name="filt_dft",
    )(f2, hfb)


def _hy_conv_body(u_ref, x0_ref, f2_ref, fimt_ref, s_ref, bias_ref, o_ref, *, L, fc):
    u = u_ref[...]
    acc = None
    for k in range(L // fc):
        ks = slice(k * fc, (k + 1) * fc)
        ure = _dot(f2_ref[0, ks, :], u)
        uim = _dot(f2_ref[1, ks, :], u)
        a_m = s_ref[0, ks, :].astype(F32)
        b_m = s_ref[1, ks, :].astype(F32)
        d_m = s_ref[2, ks, :].astype(F32)
        yre = (ure * a_m - uim * b_m).astype(BF16)
        yim = (ure * b_m + uim * d_m).astype(BF16)
        part = _dot(f2_ref[0, :, ks], yre) + _dot(fimt_ref[:, ks], yim)
        acc = part if acc is None else acc + part
    y = acc + u.astype(F32) * bias_ref[...]
    o_ref[...] = (x0_ref[...].astype(F32) * y).astype(BF16)


def _hy_conv(u, x0, f2, fimt, s, bias):
    nseq, L, c = u.shape
    fc = min(L, 512)
    resident = lambda shape: pl.BlockSpec(shape, lambda s_: (0,) * len(shape), pipeline_mode=pl.Buffered(1))
    return pl.pallas_call(
        functools.partial(_hy_conv_body, L=L, fc=fc),
        grid=(nseq,),
        in_specs=[pl.BlockSpec((None, L, c), lambda s_: (s_, 0, 0)),
                  pl.BlockSpec((None, L, c), lambda s_: (s_, 0, 0)),
                  resident((2, L, L)), resident((L, L)), resident((3, L, c)),
                  pl.BlockSpec((1, c), lambda s_: (0, 0))],
        out_specs=pl.BlockSpec((None, L, c), lambda s_: (s_, 0, 0)),
        out_shape=jax.ShapeDtypeStruct((nseq, L, c), BF16),
        compiler_params=_cparams("arbitrary"),
        name="hy_conv",
    )(u, x0, f2, fimt, s, bias)


def _hyena(proj, batch, n, ctx_len, p):
    cw, cb, w1, b1, w2, b2, w3, freq, bias = p
    outs = []
    for L, imap in ((n, lambda s: (s, 0, 0)), (ctx_len, lambda s: (batch, s, 0))):
        u, x0 = _hy_pre(proj, batch, L, imap, cw, cb[None, :])
        hfb = _hy_filter(L, w1, b1, w2, b2, w3, freq)
        f2, fimt = _dft_tables(L)
        s = _filt_dft(f2, hfb)
        outs.append(_hy_conv(u, x0, f2, fimt, s, bias[None, :]))
    return outs


def _softplus(x):
    return jnp.maximum(x, 0.0) + jnp.log(1.0 + jnp.exp(-jnp.abs(x)))


def _chunk_cumsum(a, axis, reverse):
    q = SSM_CHUNK
    size = a.shape[axis]
    pos = lax.broadcasted_iota(jnp.int32, a.shape, axis) % q
    s = 1
    while s < q:
        if reverse:
            a = a + jnp.where(pos < q - s, pltpu.roll(a, size - s, axis=axis), 0.0)
        else:
            a = a + jnp.where(pos >= s, pltpu.roll(a, s, axis=axis), 0.0)
        s *= 2
    return a


def _ssd_prepare(L, xbc_ref, dtc_ref, dtr_ref, cw_ref, cb_ref, dtbc_ref, alc_ref, dtbr_ref, alr_ref,
                 x_s, c_s, bt_s, csc_s, csr_s, dtr_s):
    nc = L // SSM_CHUNK
    nh = SSM_HEADS
    inner = SSM_INNER
    gs = SSM_GROUPS * SSM_STATE
    xs = _silu(_dwconv3(xbc_ref[:, :inner].astype(F32), cw_ref[:, :inner], cb_ref[:, :inner]))
    x_s[0:L, :] = xs.astype(BF16)
    bm = _silu(_dwconv3(xbc_ref[:, inner:inner + gs].astype(F32), cw_ref[:, inner:inner + gs],
                        cb_ref[:, inner:inner + gs]))
    for c in range(nc):
        bt_s[c] = bm[c * SSM_CHUNK:(c + 1) * SSM_CHUNK, :].T.astype(BF16)
    cm = _silu(_dwconv3(xbc_ref[:, inner + gs:].astype(F32), cw_ref[:, inner + gs:], cb_ref[:, inner + gs:]))
    c_s[0:L, :] = cm.astype(BF16)
    a_col = _softplus(dtc_ref[...] + dtbc_ref[...]) * (-jnp.exp(alc_ref[...]))
    lane = lax.broadcasted_iota(jnp.int32, a_col.shape, 1)
    csc_s[0:L, :] = jnp.where(lane < nh, _chunk_cumsum(a_col, 0, False), _chunk_cumsum(a_col, 0, True))
    dt_row = _softplus(dtr_ref[...] + dtbr_ref[...])
    a_row = (dt_row * (-jnp.exp(alr_ref[...]))).reshape(2 * nh * nc, SSM_CHUNK)
    rown = lax.broadcasted_iota(jnp.int32, a_row.shape, 0)
    csr_s[0:2 * nh * nc, :] = jnp.where(rown < nh * nc, _chunk_cumsum(a_row, 1, False),
                                        _chunk_cumsum(a_row, 1, True))
    dtr_s[0:2 * nh * nc, :] = dt_row.reshape(2 * nh * nc, SSM_CHUNK)


def _ssd_chunk(c, d, nc, first, x_s, c_s, bt_s, csc_s, csr_s, dtr_s, st_s, y_s):
    q = SSM_CHUNK
    ns = SSM_STATE
    hp = SSM_HEADDIM
    hpg = SSM_HEADS // SSM_GROUPS
    gw = hpg * hp
    r0 = pl.multiple_of(c * q, q)
    ri = lax.broadcasted_iota(jnp.int32, (q, q), 0)
    ci = lax.broadcasted_iota(jnp.int32, (q, q), 1)
    keep = (ci <= ri) if d == 0 else (ci >= ri)
    lane_blk = lax.broadcasted_iota(jnp.int32, (q, gw), 1) // hp
    cc = c_s[pl.ds(r0, q), :]
    xc = x_s[pl.ds(r0, q), :]
    btc = bt_s[c]
    ys = []
    for g in range(SSM_GROUPS):
        cg = cc[:, g * ns:(g + 1) * ns]
        btg = btc[g * ns:(g + 1) * ns, :]
        xg = xc[:, g * gw:(g + 1) * gw]
        s_cb = _dot(cg, btg)
        btg32 = btg.astype(F32)
        m_rows, w_rows, e_cols, tots = [], [], [], []
        for hh in range(hpg):
            idx = d * SSM_HEADS + g * hpg + hh
            csc = csc_s[pl.ds(r0, q), idx:idx + 1]
            csr = csr_s[pl.ds(idx * nc + c, 1), :]
            dtr = dtr_s[pl.ds(idx * nc + c, 1), :]
            lmat = jnp.where(keep, jnp.exp(csc - csr), 0.0)
            m_rows.append((s_cb * lmat * dtr).astype(BF16))
            tot = csr[:, q - 1:q] if d == 0 else csr[:, 0:1]
            w_rows.append((btg32 * (jnp.exp(tot - csr) * dtr)).astype(BF16))
            e_cols.append(jnp.exp(csc))
            tots.append(jnp.exp(tot))
        yd = _dot(jnp.concatenate(m_rows, axis=0), xg)
        ds = _dot(jnp.concatenate(w_rows, axis=0), xg)
        st = st_s[d, g]
        yo = _dot(cg, st.astype(BF16))
        y_g = jnp.zeros((q, gw), F32)
        st_new = jnp.zeros((ns, gw), F32)
        for hh in range(hpg):
            sel = lane_blk == hh
            y_g = jnp.where(sel, yd[hh * q:(hh + 1) * q, :] + yo * e_cols[hh], y_g)
            st_new = jnp.where(sel, st * tots[hh] + ds[hh * ns:(hh + 1) * ns, :], st_new)
        st_s[d, g] = st_new
        ys.append(y_g)
    y = jnp.concatenate(ys, axis=1)
    if first:
        y_s[pl.ds(r0, q), :] = y
    else:
        y_s[pl.ds(r0, q), :] += y


def _ssd_body(xl_ref, xc_ref, dcl_ref, dcc_ref, drl_ref, drc_ref, cw_ref, cb_ref, dtbc_ref, alc_ref,
              dtbr_ref, alr_ref, dsk_ref, yl_ref, yc_ref,
              x_s, c_s, bt_s, csc_s, csr_s, dtr_s, st_s, y_s, *, n, ctx_len):
    st_s[...] = jnp.zeros_like(st_s)
    for L, xbc_ref, dtc_ref, dtr_ref, o_ref in ((ctx_len, xc_ref, dcc_ref, drc_ref, yc_ref),
                                                 (n, xl_ref, dcl_ref, drl_ref, yl_ref)):
        nc = L // SSM_CHUNK
        _ssd_prepare(L, xbc_ref, dtc_ref, dtr_ref, cw_ref, cb_ref, dtbc_ref, alc_ref, dtbr_ref, alr_ref,
                     x_s, c_s, bt_s, csc_s, csr_s, dtr_s)
        args = (x_s, c_s, bt_s, csc_s, csr_s, dtr_s, st_s, y_s)

        def fwd(c, carry, nc=nc, args=args):
            _ssd_chunk(c, 0, nc, True, *args)
            return carry

        def bwd(i, carry, nc=nc, args=args):
            _ssd_chunk(nc - 1 - i, 1, nc, False, *args)
            return carry

        lax.fori_loop(0, nc, fwd, 0)
        lax.fori_loop(0, nc, bwd, 0)
        o_ref[...] = (y_s[0:L, :] + x_s[0:L, :].astype(F32) * dsk_ref[...]).astype(BF16)


def _ssd(proj, dt, batch, n, ctx_len, p):
    conv_w, conv_b, dt_bias, a_log, d_skip = p
    nh = SSM_HEADS
    ncl, ncc = n // SSM_CHUNK, ctx_len // SSM_CHUNK
    cch = SSM_CONV_CH
    xbc_blk = (SSM_INNER * 3 + SSM_INNER) // cch
    dt16 = dt[:, :, :2 * nh]
    dtr_l = dt16[:batch].transpose(0, 2, 1).reshape(batch, 2 * nh, ncl, SSM_CHUNK)
    dtr_c = dt16[batch].reshape(batch, ctx_len, 2 * nh).transpose(0, 2, 1).reshape(batch, 2 * nh, ncc, SSM_CHUNK)
    pad = LANES - 2 * nh
    dtb_col = jnp.pad(dt_bias.reshape(1, 2 * nh), ((0, 0), (0, pad)))
    al_col = jnp.pad(a_log.reshape(1, 2 * nh), ((0, 0), (0, pad)))
    dtb_row = jnp.broadcast_to(dt_bias.reshape(2 * nh, 1, 1), (2 * nh, 1, SSM_CHUNK))
    al_row = jnp.broadcast_to(a_log.reshape(2 * nh, 1, 1), (2 * nh, 1, SSM_CHUNK))
    dsk = jnp.repeat(d_skip, SSM_HEADDIM)[None, :]
    full = lambda shape: pl.BlockSpec(shape, lambda b: (0,) * len(shape))
    return pl.pallas_call(
        functools.partial(_ssd_body, n=n, ctx_len=ctx_len),
        grid=(batch,),
        in_specs=[pl.BlockSpec((None, n, cch), lambda b: (b, 0, xbc_blk)),
                  pl.BlockSpec((None, ctx_len, cch), lambda b: (batch, b, xbc_blk)),
                  pl.BlockSpec((None, n, LANES), lambda b: (b, 0, 0)),
                  pl.BlockSpec((None, ctx_len, LANES), lambda b: (batch, b, 0)),
                  pl.BlockSpec((None, 2 * nh, ncl, SSM_CHUNK), lambda b: (b, 0, 0, 0)),
                  pl.BlockSpec((None, 2 * nh, ncc, SSM_CHUNK), lambda b: (b, 0, 0, 0)),
                  full((3, cch)), full((1, cch)), full((1, LANES)), full((1, LANES)),
                  full((2 * nh, 1, SSM_CHUNK)), full((2 * nh, 1, SSM_CHUNK)), full((1, SSM_INNER))],
        out_specs=[pl.BlockSpec((None, n, SSM_INNER), lambda b: (b, 0, 0)),
                   pl.BlockSpec((None, ctx_len, SSM_INNER), lambda b: (b, 0, 0))],
        out_shape=[jax.ShapeDtypeStruct((batch, n, SSM_INNER), BF16),
                   jax.ShapeDtypeStruct((batch, ctx_len, SSM_INNER), BF16)],
        scratch_shapes=[pltpu.VMEM((n, SSM_INNER), BF16),
                        pltpu.VMEM((n, SSM_GROUPS * SSM_STATE), BF16),
                        pltpu.VMEM((ncl, SSM_GROUPS * SSM_STATE, SSM_CHUNK), BF16),
                        pltpu.VMEM((n, LANES), F32),
                        pltpu.VMEM((2 * nh * ncl, SSM_CHUNK), F32),
                        pltpu.VMEM((2 * nh * ncl, SSM_CHUNK), F32),
                        pltpu.VMEM((2, SSM_GROUPS, SSM_STATE, SSM_INNER // SSM_GROUPS), F32),
                        pltpu.VMEM((n, SSM_INNER), F32)],
        compiler_params=_cparams("arbitrary"),
        name="ssd",
    )(proj, proj, dt, dt, dtr_l, dtr_c, conv_w, conv_b[None, :], dtb_col, al_col, dtb_row, al_row, dsk)


def _mix_out_even_body(yhl_ref, yhc_ref, ysl_ref, ysc_ref, z_ref, ng_ref, w_ref, x_ref, c_ref, mod_ref, pg_ref, o_ref,
                       *, batch):
    is_ctx = pl.program_id(0) == batch
    h = jnp.where(is_ctx, c_ref[...], x_ref[...])
    y_hy = jnp.where(is_ctx, yhc_ref[...], yhl_ref[...])
    ys = jnp.where(is_ctx, ysc_ref[...], ysl_ref[...]).astype(F32) * _silu(z_ref[...].astype(F32))
    gw = SSM_INNER // SSM_GROUPS
    parts = []
    for g in range(SSM_GROUPS):
        yg = ys[:, g * gw:(g + 1) * gw]
        parts.append(yg * lax.rsqrt(jnp.mean(yg * yg, axis=-1, keepdims=True) + EPS))
    y_ssm = (jnp.concatenate(parts, axis=1) * ng_ref[...]).astype(BF16)
    m = _dot(y_hy, w_ref[0:HY_CH, :]) + _dot(y_ssm, w_ref[HY_CH:, :])
    o_ref[...] = h + mod_ref[2:3, :] * _rms(m, pg_ref[...])


def _mix_out_even(y_hy_l, y_hy_c, y_ss_l, y_ss_c, proj, norm_g, w_out, x, ctx3, mod, post_g, tm):
    batch, n, d = x.shape
    G = batch + 1
    c = HY_CH
    z_blk = (3 * HY_CH) // SSM_INNER
    lat_c, ctx_c = _lat_ctx_specs(batch, n // tm, tm, c)
    lat_d, ctx_d = _lat_ctx_specs(batch, n // tm, tm, d)
    return pl.pallas_call(
        functools.partial(_mix_out_even_body, batch=batch),
        grid=(G, n // tm),
        in_specs=[lat_c, ctx_c, lat_c, ctx_c,
                  pl.BlockSpec((None, tm, SSM_INNER), lambda g, t: (g, t, z_blk)),
                  pl.BlockSpec((1, SSM_INNER), lambda g, t: (0, 0)),
                  pl.BlockSpec((2 * c, d), lambda g, t: (0, 0)),
                  lat_d, ctx_d,
                  pl.BlockSpec((None, 6, d), lambda g, t: (g, 0, 0)),
                  pl.BlockSpec((1, d), lambda g, t: (0, 0))],
        out_specs=pl.BlockSpec((None, tm, d), lambda g, t: (g, t, 0)),
        out_shape=jax.ShapeDtypeStruct((G, n, d), F32),
        compiler_params=_cparams("arbitrary", "arbitrary"),
        name="mix_out_even",
    )(y_hy_l, y_hy_c.reshape(1, n, c), y_ss_l, y_ss_c.reshape(1, n, c), proj, norm_g, w_out, x, ctx3, mod, post_g)


def _mix_out_odd_body(a_ref, b_ref, w_ref, h_ref, mod_ref, pg_ref, o_ref):
    half = a_ref.shape[-1]
    m = _dot(a_ref[...], w_ref[0:half, :]) + _dot(b_ref[...], w_ref[half:, :])
    o_ref[...] = h_ref[...] + mod_ref[2:3, :] * _rms(m, pg_ref[...])


def _mix_out_odd(o_a, o_b, w_out, h_all, mod, post_g, tm):
    batch, n, c = o_a.shape
    d = h_all.shape[-1]
    return pl.pallas_call(
        _mix_out_odd_body,
        grid=(batch, n // tm),
        in_specs=[pl.BlockSpec((None, tm, c), lambda g, t: (g, t, 0)),
                  pl.BlockSpec((None, tm, c), lambda g, t: (g, t, 0)),
                  pl.BlockSpec((2 * c, d), lambda g, t: (0, 0)),
                  pl.BlockSpec((None, tm, d), lambda g, t: (g, t, 0)),
                  pl.BlockSpec((None, 6, d), lambda g, t: (g, 0, 0)),
                  pl.BlockSpec((1, d), lambda g, t: (0, 0))],
        out_specs=pl.BlockSpec((None, tm, d), lambda g, t: (g, t, 0)),
        out_shape=jax.ShapeDtypeStruct((batch, n, d), F32),
        compiler_params=_cparams("arbitrary", "arbitrary"),
        name="mix_out_odd",
    )(o_a, o_b, w_out, h_all, mod, post_g)


def _ffn_body(h_ref, mod_ref, g_ref, wg_ref, wu_ref, wd_ref, pg_ref, o_ref, *, fchunk):
    h = h_ref[...]
    a = _norm_mod(h, g_ref[...], mod_ref[3:4, :], mod_ref[4:5, :]).astype(BF16)
    ff = wg_ref.shape[1]
    f = jnp.zeros(h.shape, F32)
    for j in range(0, ff, fchunk):
        gate = _dot(a, wg_ref[:, j:j + fchunk])
        up = _dot(a, wu_ref[:, j:j + fchunk])
        f = f + _dot((_silu(gate) * up).astype(BF16), wd_ref[j:j + fchunk, :])
    o_ref[...] = h + mod_ref[5:6, :] * _rms(f, pg_ref[...])


def _ffn(h_all, mod, g_pre, wg, wu, wd, post_g, tm):
    G, n, d = h_all.shape
    ff = wg.shape[1]
    return pl.pallas_call(
        functools.partial(_ffn_body, fchunk=256),
        grid=(G, n // tm),
        in_specs=[pl.BlockSpec((None, tm, d), lambda g, t: (g, t, 0)),
                  pl.BlockSpec((None, 6, d), lambda g, t: (g, 0, 0)),
                  pl.BlockSpec((1, d), lambda g, t: (0, 0)),
                  pl.BlockSpec((d, ff), lambda g, t: (0, 0), pipeline_mode=pl.Buffered(1)),
                  pl.BlockSpec((d, ff), lambda g, t: (0, 0), pipeline_mode=pl.Buffered(1)),
                  pl.BlockSpec((ff, d), lambda g, t: (0, 0), pipeline_mode=pl.Buffered(1)),
                  pl.BlockSpec((1, d), lambda g, t: (0, 0))],
        out_specs=pl.BlockSpec((None, tm, d), lambda g, t: (g, t, 0)),
        out_shape=jax.ShapeDtypeStruct((G, n, d), F32),
        compiler_params=_cparams("arbitrary", "arbitrary"),
        name="ffn",
    )(h_all, mod, g_pre, wg, wu, wd, post_g)


def _rope(x, cos, sin_a, sin_b, half):
    return x * cos + pltpu.roll(x, LANES - half, axis=1) * sin_a + pltpu.roll(x, half, axis=1) * sin_b


def _odd_in_body(h_ref, mod_ref, g_ref, w_ref, qg_ref, wq_ref, kg_ref, wk_ref, wv_ref, rp_ref,
                 qm_ref, km_ref, vm_ref, qw_ref, kw_ref, vw_ref):
    a = _norm_mod(h_ref[...], g_ref[...], mod_ref[0:1, :], mod_ref[1:2, :]).astype(BF16)
    ql, kl = MLA_Q_LORA, MLA_KV_LORA
    o_qg = ql + kl
    o_kg = o_qg + GQA_HEADS * HEAD_DIM
    o_vg = o_kg + GQA_KV * HEAD_DIM
    o_kr = o_vg + GQA_KV * HEAD_DIM
    cm, s1m, s2m = rp_ref[:, 0:LANES], rp_ref[:, LANES:2 * LANES], rp_ref[:, 2 * LANES:3 * LANES]
    cg, s1g, s2g = rp_ref[:, 3 * LANES:4 * LANES], rp_ref[:, 4 * LANES:5 * LANES], rp_ref[:, 5 * LANES:6 * LANES]
    mla_scale = (MLA_NOPE + MLA_ROPE) ** -0.5
    gqa_scale = HEAD_DIM ** -0.5
    q_lat = _rms(_dot(a, w_ref[:, 0:ql]), qg_ref[...]).astype(BF16)
    q = _dot(q_lat, wq_ref[...])
    kv_lat = _rms(_dot(a, w_ref[:, ql:o_qg]), kg_ref[...]).astype(BF16)
    k = _dot(kv_lat, wk_ref[...])
    vm_ref[...] = _dot(kv_lat, wv_ref[...]).astype(BF16)
    k_rope = _rope(_dot(a, w_ref[:, o_kr:o_kr + LANES]), cm, s1m, s2m, MLA_ROPE // 2)
    for hd in range(MLA_HEADS):
        sl = slice(hd * LANES, (hd + 1) * LANES)
        qm_ref[:, sl] = (_rope(q[:, sl], cm, s1m, s2m, MLA_ROPE // 2) * mla_scale).astype(BF16)
        km_ref[:, sl] = (k[:, sl] + k_rope).astype(BF16)
    qg = _dot(a, w_ref[:, o_qg:o_kg])
    for blk in range(GQA_HEADS * HEAD_DIM // LANES):
        sl = slice(blk * LANES, (blk + 1) * LANES)
        qw_ref[:, sl] = (_rope(qg[:, sl], cg, s1g, s2g, HEAD_DIM // 2) * gqa_scale).astype(BF16)
    kw_ref[...] = _rope(_dot(a, w_ref[:, o_kg:o_vg]), cg, s1g, s2g, HEAD_DIM // 2).astype(BF16)
    vw_ref[...] = _dot(a, w_ref[:, o_vg:o_kr]).astype(BF16)


def _rope_tables(n):
    rows = (jnp.arange(n) // GRID_W).astype(F32)
    cols = (jnp.arange(n) % GRID_W).astype(F32)

    def cs(rot_dim):
        nf = rot_dim // 4
        inv = ROPE_BASE ** (-jnp.arange(nf, dtype=F32) / nf)
        ang = jnp.concatenate([rows[:, None] * inv, cols[:, None] * inv], axis=-1)
        return jnp.cos(ang), jnp.sin(ang)

    one = jnp.ones((n, 1), F32)
    zero = jnp.zeros((n, 1), F32)
    c, s = cs(MLA_ROPE)
    hm = MLA_ROPE // 2
    pad_n, pad_t = MLA_NOPE, LANES - MLA_NOPE - MLA_ROPE
    cm = jnp.concatenate([jnp.tile(one, (1, pad_n)), c, c, jnp.tile(one, (1, pad_t))], axis=1)
    s1m = jnp.concatenate([jnp.tile(zero, (1, pad_n)), -s, jnp.tile(zero, (1, hm + pad_t))], axis=1)
    s2m = jnp.concatenate([jnp.tile(zero, (1, pad_n + hm)), s, jnp.tile(zero, (1, pad_t))], axis=1)
    c, s = cs(HEAD_DIM)
    z32 = jnp.zeros_like(s)
    cg = jnp.concatenate([c, c, c, c], axis=1)
    s1g = jnp.concatenate([-s, z32, -s, z32], axis=1)
    s2g = jnp.concatenate([z32, s, z32, s], axis=1)
    lat = jnp.concatenate([cm, s1m, s2m, cg, s1g, s2g], axis=1)
    ident = jnp.concatenate([jnp.ones((n, LANES), F32), jnp.zeros((n, 2 * LANES), F32)] * 2, axis=1)
    return jnp.stack([lat, ident])


def _odd_in(h_all, mod, g_pre, w_in, q_norm_g, w_q_up, kv_norm_g, w_kv_up, tm):
    G, n, d = h_all.shape
    batch = G - 1
    ql, kl, rp = MLA_Q_LORA, MLA_KV_LORA, MLA_ROPE
    gq, gk = GQA_HEADS * HEAD_DIM, GQA_KV * HEAD_DIM
    o = [0, ql, ql + kl, ql + kl + rp, ql + kl + rp + gq, ql + kl + rp + gq + gk]
    kr_cols = jnp.pad(w_in[:, o[2]:o[3]], ((0, 0), (MLA_NOPE, LANES - MLA_NOPE - rp)))
    w = jnp.concatenate([w_in[:, o[0]:o[2]], w_in[:, o[3]:], kr_cols], axis=1).astype(BF16)
    hq = MLA_NOPE + MLA_ROPE
    wq = jnp.pad(w_q_up.reshape(ql, MLA_HEADS, hq), ((0, 0), (0, 0), (0, LANES - hq)))
    wq = wq.reshape(ql, MLA_HEADS * LANES).astype(BF16)
    wkv = w_kv_up.reshape(kl, MLA_HEADS, MLA_NOPE + MLA_V)
    wk = jnp.pad(wkv[:, :, :MLA_NOPE], ((0, 0), (0, 0), (0, LANES - MLA_NOPE)))
    wk = wk.reshape(kl, MLA_HEADS * LANES).astype(BF16)
    wv = wkv[:, :, MLA_NOPE:].reshape(kl, MLA_HEADS * MLA_V).astype(BF16)
    ropes = _rope_tables(n)
    nw = w.shape[1]
    full = lambda shape: pl.BlockSpec(shape, lambda g, t: (0,) * len(shape))
    blk = lambda width: pl.BlockSpec((None, tm, width), lambda g, t: (g, t, 0))
    widths = [MLA_HEADS * LANES, MLA_HEADS * LANES, MLA_HEADS * MLA_V, gq, gk, gk]
    return pl.pallas_call(
        _odd_in_body,
        grid=(G, n // tm),
        in_specs=[blk(d), pl.BlockSpec((None, 6, d), lambda g, t: (g, 0, 0)), full((1, d)), full((d, nw)),
                  full((1, ql)), full((ql, MLA_HEADS * LANES)), full((1, kl)), full((kl, MLA_HEADS * LANES)),
                  full((kl, MLA_HEADS * MLA_V)),
                  pl.BlockSpec((None, tm, 6 * LANES), lambda g, t: (g // batch, t, 0))],
        out_specs=[blk(wd) for wd in widths],
        out_shape=[jax.ShapeDtypeStruct((G, n, wd), BF16) for wd in widths],
        compiler_params=_cparams("arbitrary", "arbitrary"),
        name="odd_in",
    )(h_all, mod, g_pre, w, q_norm_g[None, :], wq, kv_norm_g[None, :], wk, wv, ropes)


def _mla_body(q_ref, kl_ref, kc_ref, vl_ref, vc_ref, o_ref, *, hps):
    lane = lax.broadcasted_iota(jnp.int32, (o_ref.shape[0], LANES), 1)
    for pr in range(hps // 2):
        vs = slice(pr * LANES, (pr + 1) * LANES)
        v_l = jnp.concatenate([vl_ref[:, vs], jnp.ones((vl_ref.shape[0], LANES), BF16)], axis=1)
        v_c = jnp.concatenate([vc_ref[:, vs], jnp.ones((vc_ref.shape[0], LANES), BF16)], axis=1)
        outs = []
        for hd in (2 * pr, 2 * pr + 1):
            sl = slice(hd * LANES, (hd + 1) * LANES)
            q = q_ref[:, sl]
            s_l = _dot_nt(q, kl_ref[:, sl])
            s_c = _dot_nt(q, kc_ref[:, sl])
            m = jnp.maximum(jnp.max(s_l, axis=-1, keepdims=True), jnp.max(s_c, axis=-1, keepdims=True))
            p_l = jnp.exp((s_l - m).astype(BF16))
            p_c = jnp.exp((s_c - m).astype(BF16))
            pv = _dot(p_l, v_l) + _dot(p_c, v_c)
            outs.append(pv[:, :LANES] / pv[:, LANES:])
        o_ref[:, vs] = jnp.where(lane < MLA_V, outs[0], outs[1]).astype(BF16)


def _mla_attn(q, k, v, batch, n, ctx_len, tq, hps):
    return pl.pallas_call(
        functools.partial(_mla_body, hps=hps),
        grid=(batch, MLA_HEADS // hps, n // tq),
        in_specs=[pl.BlockSpec((None, tq, hps * LANES), lambda b, h, i: (b, i, h)),
                  pl.BlockSpec((None, n, hps * LANES), lambda b, h, i: (b, 0, h)),
                  pl.BlockSpec((None, ctx_len, hps * LANES), lambda b, h, i: (batch, b, h)),
                  pl.BlockSpec((None, n, hps * MLA_V), lambda b, h, i: (b, 0, h)),
                  pl.BlockSpec((None, ctx_len, hps * MLA_V), lambda b, h, i: (batch, b, h))],
        out_specs=pl.BlockSpec((None, tq, hps * MLA_V), lambda b, h, i: (b, i, h)),
        out_shape=jax.ShapeDtypeStruct((batch, n, MLA_HEADS * MLA_V), BF16),
        compiler_params=_cparams("arbitrary", "arbitrary", "arbitrary"),
        name="mla_attn",
    )(q, k, k, v, v)


def _dup_half(x, kv):
    lane = lax.broadcasted_iota(jnp.int32, x.shape, 1)
    swapped = pltpu.roll(x, HEAD_DIM, axis=1)
    lo = lane < HEAD_DIM
    return jnp.where(lo, x, swapped) if kv == 0 else jnp.where(lo, swapped, x)


def _win_body(sink_ref, q_ref, kp_ref, kc_ref, kn_ref, kx_ref, vp_ref, vc_ref, vn_ref, vx_ref, o_ref, *, tq):
    i = pl.program_id(1)
    nb = pl.num_programs(1)
    w = WINDOW
    g = GQA_HEADS // GQA_KV
    ri = lax.broadcasted_iota(jnp.int32, (g * tq, w), 0) % tq
    ci = lax.broadcasted_iota(jnp.int32, (g * tq, w), 1)
    ok_prev = (ci >= ri) & (i > 0)
    ok_next = (ci <= ri - (tq - w)) & (i < nb - 1)
    rc = lax.broadcasted_iota(jnp.int32, (g * tq, tq), 0) % tq
    cc = lax.broadcasted_iota(jnp.int32, (g * tq, tq), 1)
    ok_cur = jnp.abs(cc - rc) <= w
    lane = lax.broadcasted_iota(jnp.int32, (tq, LANES), 1)
    lo = lane < HEAD_DIM
    neg = -jnp.inf
    for kv in range(GQA_KV):
        rows = []
        for hh in range(g):
            hd = kv * g + hh
            blk = q_ref[:, (hd // 2) * LANES:(hd // 2 + 1) * LANES].astype(F32)
            rows.append(jnp.where(lo if hd % 2 == 0 else ~lo, blk, 0.0).astype(BF16))
        q = jnp.concatenate(rows, axis=0)
        ks = [_dup_half(r[...].astype(F32), kv).astype(BF16) for r in (kp_ref, kc_ref, kn_ref, kx_ref)]
        vs = [_dup_half(r[...].astype(F32), kv).astype(BF16) for r in (vp_ref, vc_ref, vn_ref, vx_ref)]
        s_p = jnp.where(ok_prev, _dot_nt(q, ks[0]), neg)
        s_c = jnp.where(ok_cur, _dot_nt(q, ks[1]), neg)
        s_n = jnp.where(ok_next, _dot_nt(q, ks[2]), neg)
        s_x = _dot_nt(q, ks[3])
        hrow = lax.broadcasted_iota(jnp.int32, (g * tq, 1), 0) // tq
        sk = jnp.zeros((g * tq, 1), F32)
        for hh in range(g):
            sk = jnp.where(hrow == hh, sink_ref[kv * g + hh], sk)
        m = jnp.maximum(jnp.maximum(jnp.max(s_p, axis=-1, keepdims=True), jnp.max(s_c, axis=-1, keepdims=True)),
                        jnp.maximum(jnp.max(s_n, axis=-1, keepdims=True), jnp.max(s_x, axis=-1, keepdims=True)))
        m = jnp.maximum(m, sk)
        ps = [jnp.exp(s - m) for s in (s_p, s_c, s_n, s_x)]
        den = jnp.exp(sk - m)
        for p in ps:
            den = den + jnp.sum(p, axis=-1, keepdims=True)
        acc = jnp.zeros((g * tq, LANES), F32)
        for p, v in zip(ps, vs):
            acc = acc + _dot(p.astype(BF16), v)
        res = acc / den
        for pr in range(g // 2):
            blk = (kv * g) // 2 + pr
            o_ref[:, blk * LANES:(blk + 1) * LANES] = jnp.where(
                lo, res[(2 * pr) * tq:(2 * pr + 1) * tq, :], res[(2 * pr + 1) * tq:(2 * pr + 2) * tq, :]).astype(BF16)


def _win_attn(q, k, v, sink, batch, n, ctx_len, tq):
    w = WINDOW
    per = tq // w
    gq, gk = GQA_HEADS * HEAD_DIM, GQA_KV * HEAD_DIM
    prev = lambda b, i: (b, jnp.maximum(i * per - 1, 0), 0)
    cur = lambda b, i: (b, i, 0)
    nxt = lambda b, i: (b, jnp.minimum((i + 1) * per, n // w - 1), 0)
    ctx = lambda b, i: (batch, b, 0)
    kspec = lambda rows, im: pl.BlockSpec((None, rows, gk), im)
    return pl.pallas_call(
        functools.partial(_win_body, tq=tq),
        grid=(batch, n // tq),
        in_specs=[pl.BlockSpec(memory_space=pltpu.SMEM),
                  pl.BlockSpec((None, tq, gq), cur),
                  kspec(w, prev), kspec(tq, cur), kspec(w, nxt), kspec(ctx_len, ctx),
                  kspec(w, prev), kspec(tq, cur), kspec(w, nxt), kspec(ctx_len, ctx)],
        out_specs=pl.BlockSpec((None, tq, gq), cur),
        out_shape=jax.ShapeDtypeStruct((batch, n, gq), BF16),
        compiler_params=_cparams("arbitrary", "arbitrary"),
        name="win_attn",
    )(sink, q, k, k, k, k, v, v, v, v)


def _route_body(h_ref, mod_ref, g_ref, r_ref, a_ref, info_ref, cnt_ref, rows_ref, carry_ref):
    first = (pl.program_id(0) == 0) & (pl.program_id(1) == 0)

    @pl.when(first)
    def _():
        carry_ref[...] = jnp.zeros_like(carry_ref)

    a = _norm_mod(h_ref[...], g_ref[...], mod_ref[3:4, :], mod_ref[4:5, :])
    a_ref[...] = a
    tm = a.shape[0]
    logits = jnp.dot(a, r_ref[...], precision=HIGHEST, preferred_element_type=F32)
    lane = lax.broadcasted_iota(jnp.int32, logits.shape, 1)
    neg = -jnp.inf
    logits = jnp.where(lane < N_EXPERTS, logits, neg)
    m1 = jnp.max(logits, axis=-1, keepdims=True)
    i1 = jnp.min(jnp.where(logits == m1, lane, LANES), axis=-1, keepdims=True)
    rest = jnp.where(lane == i1, neg, logits)
    m2 = jnp.max(rest, axis=-1, keepdims=True)
    i2 = jnp.min(jnp.where(rest == m2, lane, LANES), axis=-1, keepdims=True)
    e2 = jnp.exp(m2 - m1)
    w1 = 1.0 / (1.0 + e2)
    w2 = e2 / (1.0 + e2)
    chosen = ((lane == i1) | (lane == i2)).astype(F32)
    ri = lax.broadcasted_iota(jnp.int32, (tm, tm), 0)
    ci = lax.broadcasted_iota(jnp.int32, (tm, tm), 1)
    tri = (ci < ri).astype(BF16)
    before = _dot(tri, chosen.astype(BF16)) + carry_ref[...]
    p1 = jnp.sum(jnp.where(lane == i1, before, 0.0), axis=-1, keepdims=True)
    p2 = jnp.sum(jnp.where(lane == i2, before, 0.0), axis=-1, keepdims=True)
    carry_ref[...] += jnp.sum(chosen, axis=0, keepdims=True)
    cnt_ref[...] = carry_ref[...]
    vals = (i1.astype(F32), i2.astype(F32), w1, w2, p1, p2)
    info = jnp.zeros(logits.shape, F32)
    for j, v in enumerate(vals):
        info = jnp.where(lane == j, v, info)
    info_ref[...] = info
    src = (0, 1, 4, 5)
    sr = lax.broadcasted_iota(jnp.int32, (SUBLANES, LANES), 0)
    sl = lax.broadcasted_iota(jnp.int32, (SUBLANES, LANES), 1)
    sel = jnp.zeros((SUBLANES, LANES), F32)
    for r, c in enumerate(src):
        sel = jnp.where((sr == r) & (sl == c), 1.0, sel)
    rows_ref[...] = lax.dot_general(sel, info, (((1,), (1,)), ((), ())), precision=HIGHEST,
                                    preferred_element_type=F32)


def _route(h_lat, mod, g_pre, router, tm):
    batch, n, d = h_lat.shape
    rp = jnp.pad(router, ((0, 0), (0, LANES - N_EXPERTS)))
    nt = n // tm
    return pl.pallas_call(
        _route_body,
        grid=(batch, nt),
        in_specs=[pl.BlockSpec((None, tm, d), lambda g, t: (g, t, 0)),
                  pl.BlockSpec((None, 6, d), lambda g, t: (g, 0, 0)),
                  pl.BlockSpec((1, d), lambda g, t: (0, 0)),
                  pl.BlockSpec((d, LANES), lambda g, t: (0, 0))],
        out_specs=[pl.BlockSpec((tm, d), lambda g, t: (g * nt + t, 0)),
                   pl.BlockSpec((tm, LANES), lambda g, t: (g * nt + t, 0)),
                   pl.BlockSpec((1, LANES), lambda g, t: (0, 0)),
                   pl.BlockSpec((SUBLANES, tm), lambda g, t: (0, g * nt + t))],
        out_shape=[jax.ShapeDtypeStruct((batch * n, d), F32),
                   jax.ShapeDtypeStruct((batch * n, LANES), F32),
                   jax.ShapeDtypeStruct((1, LANES), F32),
                   jax.ShapeDtypeStruct((SUBLANES, batch * n), F32)],
        scratch_shapes=[pltpu.VMEM((1, LANES), F32)],
        compiler_params=_cparams("arbitrary", "arbitrary"),
        name="moe_route",
    )(h_lat, mod, g_pre, rp)


SUBLANES = 8


def _scatter_body(slot_ref, a_ref, xs_in_ref, xs_ref, sem, *, ts, n_tok):
    del xs_in_ref
    base = pl.program_id(0) * ts

    def issue(i, carry):
        for u in range(SUBLANES):
            for k in range(2):
                slot = slot_ref[k * n_tok + base + i * SUBLANES + u]
                pltpu.make_async_copy(a_ref.at[i, pl.ds(u, 1), :], xs_ref.at[pl.ds(slot, 1), :],
                                      sem).start(priority=k)
        return carry

    lax.fori_loop(0, ts // SUBLANES, issue, 0)
    for k in range(2):
        pltpu.make_async_copy(xs_ref.at[pl.ds(0, ts), :], xs_ref.at[pl.ds(0, ts), :], sem).wait()


def _scatter(slots, a, n_slots, ts):
    n_tok, d = a.shape
    zeros = jnp.zeros((n_slots, d), F32)
    return pl.pallas_call(
        functools.partial(_scatter_body, ts=ts, n_tok=n_tok),
        grid_spec=pltpu.PrefetchScalarGridSpec(
            num_scalar_prefetch=1,
            grid=(n_tok // ts,),
            in_specs=[pl.BlockSpec((ts // SUBLANES, SUBLANES, d), lambda i, s: (i, 0, 0)),
                      pl.BlockSpec(memory_space=pl.ANY)],
            out_specs=pl.BlockSpec(memory_space=pl.ANY),
            scratch_shapes=[pltpu.SemaphoreType.DMA(())]),
        out_shape=jax.ShapeDtypeStruct((n_slots, d), F32),
        input_output_aliases={2: 0},
        compiler_params=_cparams("arbitrary"),
        name="moe_scatter",
    )(slots, a.reshape(n_tok // SUBLANES, SUBLANES, d), zeros)


def _experts_body(te_ref, tv_ref, x_ref, wg_ref, wu_ref, wd_ref, y_ref, acc_ref):
    del te_ref
    j = pl.program_id(0)
    f = pl.program_id(1)
    valid = tv_ref[j]

    @pl.when(f == 0)
    def _():
        acc_ref[...] = jnp.zeros_like(acc_ref)

    n_sub = (valid + MOE_SUB - 1) // MOE_SUB
    for ns in range(1, MOE_TILE // MOE_SUB + 1):
        rows = slice(0, ns * MOE_SUB)

        @pl.when(n_sub == ns)
        def _(rows=rows):
            x = x_ref[rows, :].astype(BF16)
            gate = _dot(x, wg_ref[...].astype(BF16))
            up = _dot(x, wu_ref[...].astype(BF16))
            hid = (_silu(gate) * up).astype(BF16)
            acc_ref[rows, :] += _dot(hid, wd_ref[...].astype(BF16))

    @pl.when(f == pl.num_programs(1) - 1)
    def _():
        y_ref[...] = acc_ref[...]


def _experts(tile_expert, tile_valid, xs, wg, wu, wd):
    n_slots, d = xs.shape
    ff = wg.shape[-1]
    n_tiles = n_slots // MOE_TILE
    tf = MOE_FCHUNK
    return pl.pallas_call(
        _experts_body,
        grid_spec=pltpu.PrefetchScalarGridSpec(
            num_scalar_prefetch=2,
            grid=(n_tiles, ff // tf),
            in_specs=[pl.BlockSpec((MOE_TILE, d), lambda j, f, te, tv: (j, 0)),
                      pl.BlockSpec((None, d, tf), lambda j, f, te, tv: (te[j], 0, f)),
                      pl.BlockSpec((None, d, tf), lambda j, f, te, tv: (te[j], 0, f)),
                      pl.BlockSpec((None, tf, d), lambda j, f, te, tv: (te[j], f, 0))],
            out_specs=pl.BlockSpec((MOE_TILE, d), lambda j, f, te, tv: (j, 0)),
            scratch_shapes=[pltpu.VMEM((MOE_TILE, d), F32)]),
        out_shape=jax.ShapeDtypeStruct((n_slots, d), F32),
        compiler_params=_cparams("arbitrary", "arbitrary"),
        name="moe_experts",
    )(tile_expert, tile_valid, xs, wg, wu, wd)


def _combine_body(slot_ref, ys_ref, info_ref, h_ref, mod_ref, pg_ref, o_ref, buf_ref, sem, *, tc, n_tok):
    base = (pl.program_id(0) * pl.num_programs(1) + pl.program_id(1)) * tc

    def issue(i, carry):
        for u in range(SUBLANES):
            for k in range(2):
                slot = slot_ref[k * n_tok + base + i * SUBLANES + u]
                pltpu.make_async_copy(ys_ref.at[pl.ds(slot, 1), :], buf_ref.at[k, i, pl.ds(u, 1), :],
                                      sem).start(priority=k)
        return carry

    lax.fori_loop(0, tc // SUBLANES, issue, 0)
    for k in range(2):
        pltpu.make_async_copy(ys_ref.at[pl.ds(0, tc), :], ys_ref.at[pl.ds(0, tc), :], sem).wait()
    d = o_ref.shape[-1]
    f = info_ref[:, 2:3] * buf_ref[0].reshape(tc, d) + info_ref[:, 3:4] * buf_ref[1].reshape(tc, d)
    o_ref[...] = h_ref[...] + mod_ref[5:6, :] * _rms(f, pg_ref[...])


def _combine(slots, ys, info, h_lat, mod, post_g, tc):
    batch, n, d = h_lat.shape
    nt = n // tc
    return pl.pallas_call(
        functools.partial(_combine_body, tc=tc, n_tok=batch * n),
        grid_spec=pltpu.PrefetchScalarGridSpec(
            num_scalar_prefetch=1,
            grid=(batch, nt),
            in_specs=[pl.BlockSpec(memory_space=pl.ANY),
                      pl.BlockSpec((tc, LANES), lambda g, t, *_: (g * nt + t, 0)),
                      pl.BlockSpec((None, tc, d), lambda g, t, *_: (g, t, 0)),
                      pl.BlockSpec((None, 6, d), lambda g, t, *_: (g, 0, 0)),
                      pl.BlockSpec((1, d), lambda g, t, *_: (0, 0))],
            out_specs=pl.BlockSpec((None, tc, d), lambda g, t, *_: (g, t, 0)),
            scratch_shapes=[pltpu.VMEM((2, tc // SUBLANES, SUBLANES, d), F32), pltpu.SemaphoreType.DMA(())]),
        out_shape=jax.ShapeDtypeStruct((batch, n, d), F32),
        compiler_params=_cparams("arbitrary", "arbitrary"),
        name="moe_combine",
    )(slots, ys, info, h_lat, mod, post_g)


def _moe(h_lat, mod, g_pre, router, wg, wu, wd, post_g, tm):
    batch, n, d = h_lat.shape
    n_tok = batch * n
    a, info, counts, rt = _route(h_lat, mod, g_pre, router, tm)
    cnt = counts[0, :N_EXPERTS].astype(jnp.int32)
    padded = ((cnt + MOE_TILE - 1) // MOE_TILE) * MOE_TILE
    ends = jnp.cumsum(padded)
    starts = ends - padded
    seg = jnp.zeros((2, n_tok), jnp.int32)
    for e in range(N_EXPERTS):
        seg = jnp.where(rt[0:2] == e, starts[e], seg)
    slots = (seg + rt[2:4].astype(jnp.int32)).reshape(2 * n_tok)
    n_tiles = (2 * n_tok + N_EXPERTS * (MOE_TILE - 1)) // MOE_TILE
    tile_row = jnp.arange(n_tiles, dtype=jnp.int32) * MOE_TILE
    tile_expert = jnp.minimum(jnp.sum(tile_row[:, None] >= ends[None, :], axis=1), N_EXPERTS - 1).astype(jnp.int32)
    tile_valid = jnp.clip(cnt[tile_expert] - (tile_row - starts[tile_expert]), 0, MOE_TILE)
    tile_valid = jnp.where(tile_row < ends[-1], tile_valid, 0).astype(jnp.int32)
    xs = _scatter(slots, a, n_tiles * MOE_TILE, tm)
    ys = _experts(tile_expert, tile_valid, xs, wg, wu, wd)
    return _combine(slots, ys, info, h_lat, mod, post_g, tm)


def kernel(x, c, ctx, c_ctx, ada_w, ada_b, mix_pre_g, mix_post_g, ffn_pre_g, ffn_post_g, ev_w_in, ev_hy_conv_w, ev_hy_conv_b, ev_hy_filt_w1, ev_hy_filt_b1, ev_hy_filt_w2, ev_hy_filt_b2, ev_hy_filt_w3, ev_hy_freq, ev_hy_bias, ev_ssm_conv_w, ev_ssm_conv_b, ev_ssm_dt_bias, ev_ssm_a_log, ev_ssm_d, ev_ssm_norm_g, ev_w_out, ev_ffn_w_gate, ev_ffn_w_up, ev_ffn_w_down, od_w_in, od_mla_q_norm_g, od_mla_w_q_up, od_mla_kv_norm_g, od_mla_w_kv_up, od_gqa_sink, od_w_out, od_router, od_moe_w_gate, od_moe_w_up, od_moe_w_down):
    batch, n, d = x.shape
    ctx_len = ctx.shape[1]
    assert batch * ctx_len == n and ada_w.shape[0] == 2
    G = batch + 1
    tm = min(n, 512)

    cond = jnp.concatenate([c, c_ctx[None, :], jnp.zeros((16 - G, d), F32)], axis=0)
    mods = _ada(cond, ada_w, ada_b).reshape(2, 16, 6, d)
    ctx3 = ctx.reshape(1, n, d)

    n_main = 3 * HY_CH + SSM_INNER + SSM_CONV_CH
    w_in = ev_w_in[0]
    w_main = w_in[:, :n_main].astype(BF16)
    w_dt = jnp.pad(w_in[:, n_main:], ((0, 0), (0, LANES - 2 * SSM_HEADS))).astype(BF16)
    proj, dt = _even_in(x, ctx3, mods[0], mix_pre_g[0][None, :], w_main, w_dt, tm)
    y_hy_l, y_hy_c = _hyena(proj, batch, n, ctx_len,
                            (ev_hy_conv_w[0], ev_hy_conv_b[0], ev_hy_filt_w1[0], ev_hy_filt_b1[0], ev_hy_filt_w2[0],
                             ev_hy_filt_b2[0], ev_hy_filt_w3[0], ev_hy_freq[0], ev_hy_bias[0]))
    y_ss_l, y_ss_c = _ssd(proj, dt, batch, n, ctx_len,
                          (ev_ssm_conv_w[0], ev_ssm_conv_b[0], ev_ssm_dt_bias[0], ev_ssm_a_log[0], ev_ssm_d[0]))
    h_all = _mix_out_even(y_hy_l, y_hy_c, y_ss_l, y_ss_c, proj, ev_ssm_norm_g[0][None, :],
                          ev_w_out[0].astype(BF16), x, ctx3, mods[0], mix_post_g[0][None, :], tm)
    h_all = _ffn(h_all, mods[0], ffn_pre_g[0][None, :], ev_ffn_w_gate[0].astype(BF16), ev_ffn_w_up[0].astype(BF16),
                 ev_ffn_w_down[0].astype(BF16), ffn_post_g[0][None, :], min(n, 1024))

    q_m, k_m, v_m, q_w, k_w, v_w = _odd_in(h_all, mods[1], mix_pre_g[1][None, :], od_w_in[0], od_mla_q_norm_g[0],
                                           od_mla_w_q_up[0], od_mla_kv_norm_g[0], od_mla_w_kv_up[0], tm)
    o_mla = _mla_attn(q_m, k_m, v_m, batch, n, ctx_len, min(n, 512), 4)
    o_win = _win_attn(q_w, k_w, v_w, od_gqa_sink[0], batch, n, ctx_len, min(n, 2 * WINDOW))
    h_lat = _mix_out_odd(o_mla, o_win, od_w_out[0].astype(BF16), h_all, mods[1], mix_post_g[1][None, :], tm)
    return _moe(h_lat, mods[1], ffn_pre_g[1][None, :], od_router[0], od_moe_w_gate[0], od_moe_w_up[0],
                od_moe_w_down[0], ffn_post_g[1][None, :], tm)
```

```python
import functools
import math

import jax
import jax.numpy as jnp
from jax import lax
from jax.experimental import pallas as pl
from jax.experimental.pallas import tpu as pltpu

F32 = jnp.float32
BF16 = jnp.bfloat16
HIGHEST = lax.Precision.HIGHEST

D_MODEL = 1024
GRID_W = 64
EPS = 1e-6
HEAD_DIM = 64
GROUP_WIDTH = D_MODEL // 2

HY_CH = GROUP_WIDTH
HY_BANDS = 16
HY_EMB = 2 * HY_BANDS + 1
HY_FILT_HID = 64
HY_FAST_DECAY = 0.3
HY_SLOW_DECAY = 1.5
HY_DECAY_TARGET = 1e-2

SSM_INNER = GROUP_WIDTH
SSM_HEADDIM = 64
SSM_HEADS = SSM_INNER // SSM_HEADDIM
SSM_GROUPS = 2
SSM_STATE = 128
SSM_CHUNK = 128
SSM_CONV_CH = SSM_INNER + 2 * SSM_GROUPS * SSM_STATE

MLA_HEADS = GROUP_WIDTH // HEAD_DIM
MLA_NOPE = 64
MLA_ROPE = 32
MLA_V = 64
MLA_Q_LORA = D_MODEL // 4
MLA_KV_LORA = D_MODEL // 8

GQA_HEADS = GROUP_WIDTH // HEAD_DIM
GQA_KV = GQA_HEADS // 4
WINDOW = 128
ROPE_BASE = 10000.0

D_FF = ((8 * D_MODEL // 3 + 127) // 128) * 128
N_EXPERTS = 8
D_FF_EXPERT = 7 * D_MODEL // 2

LANES = 128
VMEM_LIMIT = 56 * 1024 * 1024
MOE_TILE = 1024
MOE_SUB = 256
MOE_FCHUNK = 512


def _cparams(*sem):
    return pltpu.CompilerParams(dimension_semantics=sem, vmem_limit_bytes=VMEM_LIMIT)


def _silu(x):
    return x / (1.0 + jnp.exp(-x))


def _rms(x, g):
    return x * lax.rsqrt(jnp.mean(x * x, axis=-1, keepdims=True) + EPS) * g


def _norm_mod(h, g, shift, scale):
    return _rms(h, g) * (1.0 + scale) + shift


def _dot(a, b):
    return jnp.dot(a, b, preferred_element_type=F32)


def _row_max(*blocks):
    tiles = [b[:, j:j + LANES] for b in blocks for j in range(0, b.shape[1], LANES)]
    m = tiles[0]
    for t in tiles[1:]:
        m = jnp.maximum(m, t)
    return jnp.max(m, axis=-1, keepdims=True)


def _dot_nt(a, b):
    return lax.dot_general(a, b, (((1,), (1,)), ((), ())), preferred_element_type=F32)


def _ada_body(c_ref, w_ref, b_ref, o_ref):
    s = _silu(c_ref[...]).astype(BF16)
    o_ref[0] = _dot(s, w_ref[0].astype(BF16)) + b_ref[0]


def _ada(cc, ada_w, ada_b):
    depth, d, n6 = ada_w.shape
    rows = cc.shape[0]
    tn = 1536
    return pl.pallas_call(
        _ada_body,
        grid=(depth, n6 // tn),
        in_specs=[pl.BlockSpec((rows, d), lambda l, j: (0, 0)),
                  pl.BlockSpec((1, d, tn), lambda l, j: (l, 0, j)),
                  pl.BlockSpec((1, 1, tn), lambda l, j: (l, 0, j))],
        out_specs=pl.BlockSpec((1, rows, tn), lambda l, j: (l, 0, j)),
        out_shape=jax.ShapeDtypeStruct((depth, rows, n6), F32),
        compiler_params=_cparams("arbitrary", "arbitrary"),
        name="ada",
    )(cc, ada_w, ada_b.reshape(depth, 1, n6))


def _lat_ctx_specs(batch, nt, tm, d):
    lat = pl.BlockSpec((None, tm, d), lambda g, t: (jnp.minimum(g, batch - 1), jnp.where(g < batch, t, nt - 1), 0))
    ctx = pl.BlockSpec((None, tm, d), lambda g, t: (0, jnp.where(g < batch, 0, t), 0))
    return lat, ctx


def _even_in_body(x_ref, c_ref, mod_ref, g_ref, w_ref, wdt_ref, o_ref, dt_ref, *, batch):
    h = jnp.where(pl.program_id(0) == batch, c_ref[...], x_ref[...])
    a = _norm_mod(h, g_ref[...], mod_ref[0:1, :], mod_ref[1:2, :]).astype(BF16)
    n_out = o_ref.shape[-1]
    for j in range(0, n_out, 512):
        o_ref[:, j:j + 512] = _dot(a, w_ref[:, j:j + 512]).astype(BF16)
    dt_ref[...] = _dot(a, wdt_ref[...])


def _even_in(x, ctx3, mod, g_pre, w_main, w_dt, tm):
    batch, n, d = x.shape
    G = batch + 1
    n_main = w_main.shape[1]
    lat_spec, ctx_spec = _lat_ctx_specs(batch, n // tm, tm, d)
    return pl.pallas_call(
        functools.partial(_even_in_body, batch=batch),
        grid=(G, n // tm),
        in_specs=[lat_spec, ctx_spec,
                  pl.BlockSpec((None, 6, d), lambda g, t: (g, 0, 0)),
                  pl.BlockSpec((1, d), lambda g, t: (0, 0)),
                  pl.BlockSpec((d, n_main), lambda g, t: (0, 0), pipeline_mode=pl.Buffered(1)),
                  pl.BlockSpec((d, LANES), lambda g, t: (0, 0))],
        out_specs=[pl.BlockSpec((None, tm, n_main), lambda g, t: (g, t, 0)),
                   pl.BlockSpec((None, tm, LANES), lambda g, t: (g, t, 0))],
        out_shape=[jax.ShapeDtypeStruct((G, n, n_main), BF16),
                   jax.ShapeDtypeStruct((G, n, LANES), F32)],
        compiler_params=_cparams("arbitrary", "arbitrary"),
        name="even_in",
    )(x, ctx3, mod, g_pre, w_main, w_dt)


def _dwconv3(x, w, b):
    L = x.shape[0]
    row = lax.broadcasted_iota(jnp.int32, x.shape, 0)
    prev = jnp.where(row >= 1, pltpu.roll(x, 1, axis=0), 0.0)
    nxt = jnp.where(row < L - 1, pltpu.roll(x, L - 1, axis=0), 0.0)
    return prev * w[0:1, :] + x * w[1:2, :] + nxt * w[2:3, :] + b


def _hy_pre_body(p_ref, cw_ref, cb_ref, u_ref, x0_ref):
    c = HY_CH
    parts = []
    for k in range(3):
        x = p_ref[:, k * c:(k + 1) * c].astype(F32)
        parts.append(_dwconv3(x, cw_ref[:, k * c:(k + 1) * c], cb_ref[:, k * c:(k + 1) * c]))
    x0_ref[...] = parts[0].astype(BF16)
    u_ref[...] = (parts[1] * parts[2]).astype(BF16)


def _hy_pre(proj, nseq, L, index_map, cw, cb):
    c3 = 3 * HY_CH
    return pl.pallas_call(
        _hy_pre_body,
        grid=(nseq,),
        in_specs=[pl.BlockSpec((None, L, c3), index_map),
                  pl.BlockSpec((3, c3), lambda s: (0, 0)),
                  pl.BlockSpec((1, c3), lambda s: (0, 0))],
        out_specs=[pl.BlockSpec((None, L, HY_CH), lambda s: (s, 0, 0)),
                   pl.BlockSpec((None, L, HY_CH), lambda s: (s, 0, 0))],
        out_shape=[jax.ShapeDtypeStruct((nseq, L, HY_CH), BF16),
                   jax.ShapeDtypeStruct((nseq, L, HY_CH), BF16)],
        compiler_params=_cparams("arbitrary"),
        name="hy_pre",
    )(proj, cw, cb)


def _hy_filter_body(z_ref, t_ref, w1_ref, b1_ref, w2_ref, b2_ref, w3_ref, f_ref, dl_ref, o_ref):
    f = f_ref[...]
    h = jnp.sin(f * (jnp.dot(z_ref[...], w1_ref[...], precision=HIGHEST, preferred_element_type=F32) + b1_ref[...]))
    h = jnp.sin(f * (jnp.dot(h, w2_ref[...], precision=HIGHEST, preferred_element_type=F32) + b2_ref[...]))
    h = jnp.dot(h, w3_ref[...], precision=HIGHEST, preferred_element_type=F32)
    decay = jnp.exp(-t_ref[...] * dl_ref[...])
    hf = h[:, :HY_CH] * decay
    hb = h[:, HY_CH:] * decay
    row = lax.broadcasted_iota(jnp.int32, hb.shape, 0)
    hb = jnp.where(row == 0, 0.0, hb)
    o_ref[:, :HY_CH] = hf.astype(BF16)
    o_ref[:, HY_CH:] = hb.astype(BF16)


def _hy_filter(L, w1, b1, w2, b2, w3, freq):
    pos = jnp.arange(L, dtype=F32)
    t = pos / (L - 1)
    omega = 2.0 * math.pi * pos / L
    bands = jnp.linspace(1e-4, HY_BANDS - 1, HY_BANDS, dtype=F32)
    z = jnp.concatenate([t[:, None], jnp.cos(omega[:, None] * bands), -jnp.sin(omega[:, None] * bands)], axis=-1)
    z = jnp.pad(z, ((0, 0), (0, LANES - HY_EMB)))
    w1p = jnp.pad(w1, ((0, LANES - HY_EMB), (0, 0)))
    deltas = jnp.abs(jnp.linspace(math.log(HY_DECAY_TARGET) / HY_SLOW_DECAY,
                                  math.log(HY_DECAY_TARGET) / HY_FAST_DECAY, HY_CH, dtype=F32))
    return pl.pallas_call(
        _hy_filter_body,
        out_shape=jax.ShapeDtypeStruct((L, 2 * HY_CH), BF16),
        compiler_params=pltpu.CompilerParams(vmem_limit_bytes=VMEM_LIMIT),
        name="hy_filter",
    )(z, t[:, None], w1p, b1[None, :], w2, b2[None, :], w3, freq[None, :], deltas[None, :])


def _dft_tables(L):
    split = 64
    t = jnp.arange(L, dtype=jnp.int32)[None, :]
    k1 = jnp.arange(L // split, dtype=jnp.int32)[:, None] * split
    k0 = jnp.arange(split, dtype=jnp.int32)[:, None]
    ang1 = ((k1 * t) % (2 * L)).astype(F32) * (math.pi / L)
    ang0 = ((k0 * t) % (2 * L)).astype(F32) * (math.pi / L)
    c1, s1 = jnp.cos(ang1)[:, None, :], jnp.sin(ang1)[:, None, :]
    c0, s0 = jnp.cos(ang0)[None, :, :], jnp.sin(ang0)[None, :, :]
    fre = (c1 * c0 - s1 * s0).reshape(L, L)
    nsin = -(s1 * c0 + c1 * s0).reshape(L, L)
    k = jnp.arange(L, dtype=jnp.int32)[:, None]
    alt_t = jnp.where(t % 2 == 0, 1.0, -1.0).astype(F32)
    alt_k = jnp.where(k % 2 == 0, 1.0, -1.0).astype(F32)
    f2 = jnp.stack([fre, jnp.where(k == 0, alt_t, nsin)]).astype(BF16)
    fimt = jnp.where(t == 0, alt_k, nsin).astype(BF16)
    return f2, fimt


def _filt_dft_body(f_ref, x_ref, o_ref, *, L, tr):
    c = HY_CH
    x = x_ref[...]
    sre = _dot(f_ref[0], x)
    sim = _dot(f_ref[1], x)
    is0 = (lax.broadcasted_iota(jnp.int32, (tr, c), 0) + pl.program_id(0) * tr) == 0
    wgt = jnp.where(is0, 1.0 / (2 * L), 1.0 / L)
    hre = sre[:, :c] + sre[:, c:]
    o_ref[0] = (hre * wgt).astype(BF16)
    o_ref[1] = (jnp.where(is0, 0.0, sim[:, :c] - sim[:, c:]) * wgt).astype(BF16)
    o_ref[2] = (jnp.where(is0, sim[:, :c] + sim[:, c:], hre) * wgt).astype(BF16)


def _filt_dft(f2, hfb):
    _, L, _ = f2.shape
    cols = hfb.shape[1]
    tr = min(L, 512)
    return pl.pallas_call(
        functools.partial(_filt_dft_body, L=L, tr=tr),
        grid=(L // tr,),
        in_specs=[pl.BlockSpec((2, tr, L), lambda i: (0, i, 0)),
                  pl.BlockSpec((L, cols), lambda i: (0, 0))],
        out_specs=pl.BlockSpec((3, tr, cols // 2), lambda i: (0, i, 0)),
        out_shape=jax.ShapeDtypeStruct((3, L, cols // 2), BF16),
        compiler_params=_cparams("arbitrary"),
        name="filt_dft",
    )(f2, hfb)


def _hy_conv_body(u_ref, x0_ref, f2_ref, fimt_ref, s_ref, bias_ref, o_ref, *, L, fc):
    u = u_ref[...]
    acc = None
    for k in range(L // fc):
        ks = slice(k * fc, (k + 1) * fc)
        ure = _dot(f2_ref[0, ks, :], u)
        uim = _dot(f2_ref[1, ks, :], u)
        a_m = s_ref[0, ks, :].astype(F32)
        b_m = s_ref[1, ks, :].astype(F32)
        d_m = s_ref[2, ks, :].astype(F32)
        yre = (ure * a_m - uim * b_m).astype(BF16)
        yim = (ure * b_m + uim * d_m).astype(BF16)
        part = _dot(f2_ref[0, :, ks], yre) + _dot(fimt_ref[:, ks], yim)
        acc = part if acc is None else acc + part
    y = acc + u.astype(F32) * bias_ref[...]
    o_ref[...] = (x0_ref[...].astype(F32) * y).astype(BF16)


def _hy_conv(u, x0, f2, fimt, s, bias):
    nseq, L, c = u.shape
    fc = min(L, 512)
    resident = lambda shape: pl.BlockSpec(shape, lambda s_: (0,) * len(shape), pipeline_mode=pl.Buffered(1))
    return pl.pallas_call(
        functools.partial(_hy_conv_body, L=L, fc=fc),
        grid=(nseq,),
        in_specs=[pl.BlockSpec((None, L, c), lambda s_: (s_, 0, 0)),
                  pl.BlockSpec((None, L, c), lambda s_: (s_, 0, 0)),
                  resident((2, L, L)), resident((L, L)), resident((3, L, c)),
                  pl.BlockSpec((1, c), lambda s_: (0, 0))],
        out_specs=pl.BlockSpec((None, L, c), lambda s_: (s_, 0, 0)),
        out_shape=jax.ShapeDtypeStruct((nseq, L, c), BF16),
        compiler_params=_cparams("arbitrary"),
        name="hy_conv",
    )(u, x0, f2, fimt, s, bias)


def _hyena(proj, batch, n, ctx_len, p):
    cw, cb, w1, b1, w2, b2, w3, freq, bias = p
    outs = []
    for L, imap in ((n, lambda s: (s, 0, 0)), (ctx_len, lambda s: (batch, s, 0))):
        u, x0 = _hy_pre(proj, batch, L, imap, cw, cb[None, :])
        hfb = _hy_filter(L, w1, b1, w2, b2, w3, freq)
        f2, fimt = _dft_tables(L)
        s = _filt_dft(f2, hfb)
        outs.append(_hy_conv(u, x0, f2, fimt, s, bias[None, :]))
    return outs


def _softplus(x):
    return jnp.maximum(x, 0.0) + jnp.log(1.0 + jnp.exp(-jnp.abs(x)))


def _chunk_cumsum(a, axis, reverse):
    q = SSM_CHUNK
    size = a.shape[axis]
    pos = lax.broadcasted_iota(jnp.int32, a.shape, axis) % q
    s = 1
    while s < q:
        if reverse:
            a = a + jnp.where(pos < q - s, pltpu.roll(a, size - s, axis=axis), 0.0)
        else:
            a = a + jnp.where(pos >= s, pltpu.roll(a, s, axis=axis), 0.0)
        s *= 2
    return a


def _ssd_prepare(L, xbc_ref, dtc_ref, dtr_ref, cw_ref, cb_ref, dtbc_ref, alc_ref, dtbr_ref, alr_ref,
                 x_s, c_s, bt_s, csc_s, csr_s, dtr_s):
    nc = L // SSM_CHUNK
    nh = SSM_HEADS
    inner = SSM_INNER
    gs = SSM_GROUPS * SSM_STATE
    xs = _silu(_dwconv3(xbc_ref[:, :inner].astype(F32), cw_ref[:, :inner], cb_ref[:, :inner]))
    x_s[0:L, :] = xs.astype(BF16)
    bm = _silu(_dwconv3(xbc_ref[:, inner:inner + gs].astype(F32), cw_ref[:, inner:inner + gs],
                        cb_ref[:, inner:inner + gs]))
    for c in range(nc):
        bt_s[c] = bm[c * SSM_CHUNK:(c + 1) * SSM_CHUNK, :].T.astype(BF16)
    cm = _silu(_dwconv3(xbc_ref[:, inner + gs:].astype(F32), cw_ref[:, inner + gs:], cb_ref[:, inner + gs:]))
    c_s[0:L, :] = cm.astype(BF16)
    a_col = _softplus(dtc_ref[...] + dtbc_ref[...]) * (-jnp.exp(alc_ref[...]))
    lane = lax.broadcasted_iota(jnp.int32, a_col.shape, 1)
    csc_s[0:L, :] = jnp.where(lane < nh, _chunk_cumsum(a_col, 0, False), _chunk_cumsum(a_col, 0, True))
    dt_row = _softplus(dtr_ref[...] + dtbr_ref[...])
    a_row = (dt_row * (-jnp.exp(alr_ref[...]))).reshape(2 * nh * nc, SSM_CHUNK)
    rown = lax.broadcasted_iota(jnp.int32, a_row.shape, 0)
    csr_s[0:2 * nh * nc, :] = jnp.where(rown < nh * nc, _chunk_cumsum(a_row, 1, False),
                                        _chunk_cumsum(a_row, 1, True))
    dtr_s[0:2 * nh * nc, :] = dt_row.reshape(2 * nh * nc, SSM_CHUNK)


def _ssd_chunk(c, d, nc, first, x_s, c_s, bt_s, csc_s, csr_s, dtr_s, st_s, y_s):
    q = SSM_CHUNK
    ns = SSM_STATE
    hp = SSM_HEADDIM
    hpg = SSM_HEADS // SSM_GROUPS
    gw = hpg * hp
    r0 = pl.multiple_of(c * q, q)
    ri = lax.broadcasted_iota(jnp.int32, (q, q), 0)
    ci = lax.broadcasted_iota(jnp.int32, (q, q), 1)
    keep = (ci <= ri) if d == 0 else (ci >= ri)
    lane_blk = lax.broadcasted_iota(jnp.int32, (q, gw), 1) // hp
    cc = c_s[pl.ds(r0, q), :]
    xc = x_s[pl.ds(r0, q), :]
    btc = bt_s[c]
    ys = []
    for g in range(SSM_GROUPS):
        cg = cc[:, g * ns:(g + 1) * ns]
        btg = btc[g * ns:(g + 1) * ns, :]
        xg = xc[:, g * gw:(g + 1) * gw]
        s_cb = _dot(cg, btg)
        btg32 = btg.astype(F32)
        m_rows, w_rows, e_cols, tots = [], [], [], []
        for hh in range(hpg):
            idx = d * SSM_HEADS + g * hpg + hh
            csc = csc_s[pl.ds(r0, q), idx:idx + 1]
            csr = csr_s[pl.ds(idx * nc + c, 1), :]
            dtr = dtr_s[pl.ds(idx * nc + c, 1), :]
            lmat = jnp.where(keep, jnp.exp(csc - csr), 0.0)
            m_rows.append((s_cb * lmat * dtr).astype(BF16))
            tot = csr[:, q - 1:q] if d == 0 else csr[:, 0:1]
            w_rows.append((btg32 * (jnp.exp(tot - csr) * dtr)).astype(BF16))
            e_cols.append(jnp.exp(csc))
            tots.append(jnp.exp(tot))
        yd = _dot(jnp.concatenate(m_rows, axis=0), xg)
        ds = _dot(jnp.concatenate(w_rows, axis=0), xg)
        st = st_s[d, g]
        yo = _dot(cg, st.astype(BF16))
        y_g = jnp.zeros((q, gw), F32)
        st_new = jnp.zeros((ns, gw), F32)
        for hh in range(hpg):
            sel = lane_blk == hh
            y_g = jnp.where(sel, yd[hh * q:(hh + 1) * q, :] + yo * e_cols[hh], y_g)
            st_new = jnp.where(sel, st * tots[hh] + ds[hh * ns:(hh + 1) * ns, :], st_new)
        st_s[d, g] = st_new
        ys.append(y_g)
    y = jnp.concatenate(ys, axis=1)
    if first:
        y_s[pl.ds(r0, q), :] = y
    else:
        y_s[pl.ds(r0, q), :] += y


def _ssd_body(xl_ref, xc_ref, dcl_ref, dcc_ref, drl_ref, drc_ref, cw_ref, cb_ref, dtbc_ref, alc_ref,
              dtbr_ref, alr_ref, dsk_ref, yl_ref, yc_ref,
              x_s, c_s, bt_s, csc_s, csr_s, dtr_s, st_s, y_s, *, n, ctx_len):
    st_s[...] = jnp.zeros_like(st_s)
    for L, xbc_ref, dtc_ref, dtr_ref, o_ref in ((ctx_len, xc_ref, dcc_ref, drc_ref, yc_ref),
                                                 (n, xl_ref, dcl_ref, drl_ref, yl_ref)):
        nc = L // SSM_CHUNK
        _ssd_prepare(L, xbc_ref, dtc_ref, dtr_ref, cw_ref, cb_ref, dtbc_ref, alc_ref, dtbr_ref, alr_ref,
                     x_s, c_s, bt_s, csc_s, csr_s, dtr_s)
        args = (x_s, c_s, bt_s, csc_s, csr_s, dtr_s, st_s, y_s)

        def fwd(c, carry, nc=nc, args=args):
            _ssd_chunk(c, 0, nc, True, *args)
            return carry

        def bwd(i, carry, nc=nc, args=args):
            _ssd_chunk(nc - 1 - i, 1, nc, False, *args)
            return carry

        lax.fori_loop(0, nc, fwd, 0)
        lax.fori_loop(0, nc, bwd, 0)
        o_ref[...] = (y_s[0:L, :] + x_s[0:L, :].astype(F32) * dsk_ref[...]).astype(BF16)


def _ssd(proj, dt, batch, n, ctx_len, p):
    conv_w, conv_b, dt_bias, a_log, d_skip = p
    nh = SSM_HEADS
    ncl, ncc = n // SSM_CHUNK, ctx_len // SSM_CHUNK
    cch = SSM_CONV_CH
    xbc_blk = (SSM_INNER * 3 + SSM_INNER) // cch
    dt16 = dt[:, :, :2 * nh]
    dtr_l = dt16[:batch].transpose(0, 2, 1).reshape(batch, 2 * nh, ncl, SSM_CHUNK)
    dtr_c = dt16[batch].reshape(batch, ctx_len, 2 * nh).transpose(0, 2, 1).reshape(batch, 2 * nh, ncc, SSM_CHUNK)
    pad = LANES - 2 * nh
    dtb_col = jnp.pad(dt_bias.reshape(1, 2 * nh), ((0, 0), (0, pad)))
    al_col = jnp.pad(a_log.reshape(1, 2 * nh), ((0, 0), (0, pad)))
    dtb_row = jnp.broadcast_to(dt_bias.reshape(2 * nh, 1, 1), (2 * nh, 1, SSM_CHUNK))
    al_row = jnp.broadcast_to(a_log.reshape(2 * nh, 1, 1), (2 * nh, 1, SSM_CHUNK))
    dsk = jnp.repeat(d_skip, SSM_HEADDIM)[None, :]
    full = lambda shape: pl.BlockSpec(shape, lambda b: (0,) * len(shape))
    return pl.pallas_call(
        functools.partial(_ssd_body, n=n, ctx_len=ctx_len),
        grid=(batch,),
        in_specs=[pl.BlockSpec((None, n, cch), lambda b: (b, 0, xbc_blk)),
                  pl.BlockSpec((None, ctx_len, cch), lambda b: (batch, b, xbc_blk)),
                  pl.BlockSpec((None, n, LANES), lambda b: (b, 0, 0)),
                  pl.BlockSpec((None, ctx_len, LANES), lambda b: (batch, b, 0)),
                  pl.BlockSpec((None, 2 * nh, ncl, SSM_CHUNK), lambda b: (b, 0, 0, 0)),
                  pl.BlockSpec((None, 2 * nh, ncc, SSM_CHUNK), lambda b: (b, 0, 0, 0)),
                  full((3, cch)), full((1, cch)), full((1, LANES)), full((1, LANES)),
                  full((2 * nh, 1, SSM_CHUNK)), full((2 * nh, 1, SSM_CHUNK)), full((1, SSM_INNER))],
        out_specs=[pl.BlockSpec((None, n, SSM_INNER), lambda b: (b, 0, 0)),
                   pl.BlockSpec((None, ctx_len, SSM_INNER), lambda b: (b, 0, 0))],
        out_shape=[jax.ShapeDtypeStruct((batch, n, SSM_INNER), BF16),
                   jax.ShapeDtypeStruct((batch, ctx_len, SSM_INNER), BF16)],
        scratch_shapes=[pltpu.VMEM((n, SSM_INNER), BF16),
                        pltpu.VMEM((n, SSM_GROUPS * SSM_STATE), BF16),
                        pltpu.VMEM((ncl, SSM_GROUPS * SSM_STATE, SSM_CHUNK), BF16),
                        pltpu.VMEM((n, LANES), F32),
                        pltpu.VMEM((2 * nh * ncl, SSM_CHUNK), F32),
                        pltpu.VMEM((2 * nh * ncl, SSM_CHUNK), F32),
                        pltpu.VMEM((2, SSM_GROUPS, SSM_STATE, SSM_INNER // SSM_GROUPS), F32),
                        pltpu.VMEM((n, SSM_INNER), F32)],
        compiler_params=_cparams("arbitrary"),
        name="ssd",
    )(proj, proj, dt, dt, dtr_l, dtr_c, conv_w, conv_b[None, :], dtb_col, al_col, dtb_row, al_row, dsk)


def _mix_out_even_body(yhl_ref, yhc_ref, ysl_ref, ysc_ref, z_ref, ng_ref, w_ref, x_ref, c_ref, mod_ref, pg_ref, o_ref,
                       *, batch):
    is_ctx = pl.program_id(0) == batch
    h = jnp.where(is_ctx, c_ref[...], x_ref[...])
    y_hy = jnp.where(is_ctx, yhc_ref[...], yhl_ref[...])
    ys = jnp.where(is_ctx, ysc_ref[...], ysl_ref[...]).astype(F32) * _silu(z_ref[...].astype(F32))
    gw = SSM_INNER // SSM_GROUPS
    parts = []
    for g in range(SSM_GROUPS):
        yg = ys[:, g * gw:(g + 1) * gw]
        parts.append(yg * lax.rsqrt(jnp.mean(yg * yg, axis=-1, keepdims=True) + EPS))
    y_ssm = (jnp.concatenate(parts, axis=1) * ng_ref[...]).astype(BF16)
    m = _dot(y_hy, w_ref[0:HY_CH, :]) + _dot(y_ssm, w_ref[HY_CH:, :])
    o_ref[...] = h + mod_ref[2:3, :] * _rms(m, pg_ref[...])


def _mix_out_even(y_hy_l, y_hy_c, y_ss_l, y_ss_c, proj, norm_g, w_out, x, ctx3, mod, post_g, tm):
    batch, n, d = x.shape
    G = batch + 1
    c = HY_CH
    z_blk = (3 * HY_CH) // SSM_INNER
    lat_c, ctx_c = _lat_ctx_specs(batch, n // tm, tm, c)
    lat_d, ctx_d = _lat_ctx_specs(batch, n // tm, tm, d)
    return pl.pallas_call(
        functools.partial(_mix_out_even_body, batch=batch),
        grid=(G, n // tm),
        in_specs=[lat_c, ctx_c, lat_c, ctx_c,
                  pl.BlockSpec((None, tm, SSM_INNER), lambda g, t: (g, t, z_blk)),
                  pl.BlockSpec((1, SSM_INNER), lambda g, t: (0, 0)),
                  pl.BlockSpec((2 * c, d), lambda g, t: (0, 0)),
                  lat_d, ctx_d,
                  pl.BlockSpec((None, 6, d), lambda g, t: (g, 0, 0)),
                  pl.BlockSpec((1, d), lambda g, t: (0, 0))],
        out_specs=pl.BlockSpec((None, tm, d), lambda g, t: (g, t, 0)),
        out_shape=jax.ShapeDtypeStruct((G, n, d), F32),
        compiler_params=_cparams("arbitrary", "arbitrary"),
        name="mix_out_even",
    )(y_hy_l, y_hy_c.reshape(1, n, c), y_ss_l, y_ss_c.reshape(1, n, c), proj, norm_g, w_out, x, ctx3, mod, post_g)


def _mix_out_odd_body(a_ref, b_ref, w_ref, h_ref, mod_ref, pg_ref, o_ref):
    half = a_ref.shape[-1]
    m = _dot(a_ref[...], w_ref[0:half, :]) + _dot(b_ref[...], w_ref[half:, :])
    o_ref[...] = h_ref[...] + mod_ref[2:3, :] * _rms(m, pg_ref[...])


def _mix_out_odd(o_a, o_b, w_out, h_all, mod, post_g, tm):
    batch, n, c = o_a.shape
    d = h_all.shape[-1]
    return pl.pallas_call(
        _mix_out_odd_body,
        grid=(batch, n // tm),
        in_specs=[pl.BlockSpec((None, tm, c), lambda g, t: (g, t, 0)),
                  pl.BlockSpec((None, tm, c), lambda g, t: (g, t, 0)),
                  pl.BlockSpec((2 * c, d), lambda g, t: (0, 0)),
                  pl.BlockSpec((None, tm, d), lambda g, t: (g, t, 0)),
                  pl.BlockSpec((None, 6, d), lambda g, t: (g, 0, 0)),
                  pl.BlockSpec((1, d), lambda g, t: (0, 0))],
        out_specs=pl.BlockSpec((None, tm, d), lambda g, t: (g, t, 0)),
        out_shape=jax.ShapeDtypeStruct((batch, n, d), F32),
        compiler_params=_cparams("arbitrary", "arbitrary"),
        name="mix_out_odd",
    )(o_a, o_b, w_out, h_all, mod, post_g)


def _ffn_body(h_ref, mod_ref, g_ref, wg_ref, wu_ref, wd_ref, pg_ref, o_ref, *, fchunk):
    h = h_ref[...]
    a = _norm_mod(h, g_ref[...], mod_ref[3:4, :], mod_ref[4:5, :]).astype(BF16)
    ff = wg_ref.shape[1]
    f = jnp.zeros(h.shape, F32)
    for j in range(0, ff, fchunk):
        gate = _dot(a, wg_ref[:, j:j + fchunk])
        up = _dot(a, wu_ref[:, j:j + fchunk])
        f = f + _dot((_silu(gate) * up).astype(BF16), wd_ref[j:j + fchunk, :])
    o_ref[...] = h + mod_ref[5:6, :] * _rms(f, pg_ref[...])


def _ffn(h_all, mod, g_pre, wg, wu, wd, post_g, tm):
    G, n, d = h_all.shape
    ff = wg.shape[1]
    return pl.pallas_call(
        functools.partial(_ffn_body, fchunk=256),
        grid=(G, n // tm),
        in_specs=[pl.BlockSpec((None, tm, d), lambda g, t: (g, t, 0)),
                  pl.BlockSpec((None, 6, d), lambda g, t: (g, 0, 0)),
                  pl.BlockSpec((1, d), lambda g, t: (0, 0)),
                  pl.BlockSpec((d, ff), lambda g, t: (0, 0), pipeline_mode=pl.Buffered(1)),
                  pl.BlockSpec((d, ff), lambda g, t: (0, 0), pipeline_mode=pl.Buffered(1)),
                  pl.BlockSpec((ff, d), lambda g, t: (0, 0), pipeline_mode=pl.Buffered(1)),
                  pl.BlockSpec((1, d), lambda g, t: (0, 0))],
        out_specs=pl.BlockSpec((None, tm, d), lambda g, t: (g, t, 0)),
        out_shape=jax.ShapeDtypeStruct((G, n, d), F32),
        compiler_params=_cparams("arbitrary", "arbitrary"),
        name="ffn",
    )(h_all, mod, g_pre, wg, wu, wd, post_g)


def _rope(x, cos, sin_a, sin_b, half):
    return x * cos + pltpu.roll(x, LANES - half, axis=1) * sin_a + pltpu.roll(x, half, axis=1) * sin_b


def _odd_in_body(h_ref, mod_ref, g_ref, w_ref, qg_ref, wq_ref, kg_ref, wk_ref, wv_ref, rp_ref,
                 qm_ref, km_ref, vm_ref, qw_ref, kw_ref, vw_ref):
    a = _norm_mod(h_ref[...], g_ref[...], mod_ref[0:1, :], mod_ref[1:2, :]).astype(BF16)
    ql, kl = MLA_Q_LORA, MLA_KV_LORA
    o_qg = ql + kl
    o_kg = o_qg + GQA_HEADS * HEAD_DIM
    o_vg = o_kg + GQA_KV * HEAD_DIM
    o_kr = o_vg + GQA_KV * HEAD_DIM
    cm, s1m, s2m = rp_ref[:, 0:LANES], rp_ref[:, LANES:2 * LANES], rp_ref[:, 2 * LANES:3 * LANES]
    cg, s1g, s2g = rp_ref[:, 3 * LANES:4 * LANES], rp_ref[:, 4 * LANES:5 * LANES], rp_ref[:, 5 * LANES:6 * LANES]
    mla_scale = (MLA_NOPE + MLA_ROPE) ** -0.5
    gqa_scale = HEAD_DIM ** -0.5
    q_lat = _rms(_dot(a, w_ref[:, 0:ql]), qg_ref[...]).astype(BF16)
    q = _dot(q_lat, wq_ref[...])
    kv_lat = _rms(_dot(a, w_ref[:, ql:o_qg]), kg_ref[...]).astype(BF16)
    k = _dot(kv_lat, wk_ref[...])
    vm_ref[...] = _dot(kv_lat, wv_ref[...]).astype(BF16)
    k_rope = _rope(_dot(a, w_ref[:, o_kr:o_kr + LANES]), cm, s1m, s2m, MLA_ROPE // 2)
    for hd in range(MLA_HEADS):
        sl = slice(hd * LANES, (hd + 1) * LANES)
        qm_ref[:, sl] = (_rope(q[:, sl], cm, s1m, s2m, MLA_ROPE // 2) * mla_scale).astype(BF16)
        km_ref[:, sl] = (k[:, sl] + k_rope).astype(BF16)
    qg = _dot(a, w_ref[:, o_qg:o_kg])
    for blk in range(GQA_HEADS * HEAD_DIM // LANES):
        sl = slice(blk * LANES, (blk + 1) * LANES)
        qw_ref[:, sl] = (_rope(qg[:, sl], cg, s1g, s2g, HEAD_DIM // 2) * gqa_scale).astype(BF16)
    kw_ref[...] = _rope(_dot(a, w_ref[:, o_kg:o_vg]), cg, s1g, s2g, HEAD_DIM // 2).astype(BF16)
    vw_ref[...] = _dot(a, w_ref[:, o_vg:o_kr]).astype(BF16)


def _rope_tables(n):
    rows = (jnp.arange(n) // GRID_W).astype(F32)
    cols = (jnp.arange(n) % GRID_W).astype(F32)

    def cs(rot_dim):
        nf = rot_dim // 4
        inv = ROPE_BASE ** (-jnp.arange(nf, dtype=F32) / nf)
        ang = jnp.concatenate([rows[:, None] * inv, cols[:, None] * inv], axis=-1)
        return jnp.cos(ang), jnp.sin(ang)

    one = jnp.ones((n, 1), F32)
    zero = jnp.zeros((n, 1), F32)
    c, s = cs(MLA_ROPE)
    hm = MLA_ROPE // 2
    pad_n, pad_t = MLA_NOPE, LANES - MLA_NOPE - MLA_ROPE
    cm = jnp.concatenate([jnp.tile(one, (1, pad_n)), c, c, jnp.tile(one, (1, pad_t))], axis=1)
    s1m = jnp.concatenate([jnp.tile(zero, (1, pad_n)), -s, jnp.tile(zero, (1, hm + pad_t))], axis=1)
    s2m = jnp.concatenate([jnp.tile(zero, (1, pad_n + hm)), s, jnp.tile(zero, (1, pad_t))], axis=1)
    c, s = cs(HEAD_DIM)
    z32 = jnp.zeros_like(s)
    cg = jnp.concatenate([c, c, c, c], axis=1)
    s1g = jnp.concatenate([-s, z32, -s, z32], axis=1)
    s2g = jnp.concatenate([z32, s, z32, s], axis=1)
    lat = jnp.concatenate([cm, s1m, s2m, cg, s1g, s2g], axis=1)
    ident = jnp.concatenate([jnp.ones((n, LANES), F32), jnp.zeros((n, 2 * LANES), F32)] * 2, axis=1)
    return jnp.stack([lat, ident])


def _odd_in(h_all, mod, g_pre, w_in, q_norm_g, w_q_up, kv_norm_g, w_kv_up, tm):
    G, n, d = h_all.shape
    batch = G - 1
    ql, kl, rp = MLA_Q_LORA, MLA_KV_LORA, MLA_ROPE
    gq, gk = GQA_HEADS * HEAD_DIM, GQA_KV * HEAD_DIM
    o = [0, ql, ql + kl, ql + kl + rp, ql + kl + rp + gq, ql + kl + rp + gq + gk]
    kr_cols = jnp.pad(w_in[:, o[2]:o[3]], ((0, 0), (MLA_NOPE, LANES - MLA_NOPE - rp)))
    w = jnp.concatenate([w_in[:, o[0]:o[2]], w_in[:, o[3]:], kr_cols], axis=1).astype(BF16)
    hq = MLA_NOPE + MLA_ROPE
    wq = jnp.pad(w_q_up.reshape(ql, MLA_HEADS, hq), ((0, 0), (0, 0), (0, LANES - hq)))
    wq = wq.reshape(ql, MLA_HEADS * LANES).astype(BF16)
    wkv = w_kv_up.reshape(kl, MLA_HEADS, MLA_NOPE + MLA_V)
    wk = jnp.pad(wkv[:, :, :MLA_NOPE], ((0, 0), (0, 0), (0, LANES - MLA_NOPE)))
    wk = wk.reshape(kl, MLA_HEADS * LANES).astype(BF16)
    wv = wkv[:, :, MLA_NOPE:].reshape(kl, MLA_HEADS * MLA_V).astype(BF16)
    ropes = _rope_tables(n)
    nw = w.shape[1]
    full = lambda shape: pl.BlockSpec(shape, lambda g, t: (0,) * len(shape))
    blk = lambda width: pl.BlockSpec((None, tm, width), lambda g, t: (g, t, 0))
    widths = [MLA_HEADS * LANES, MLA_HEADS * LANES, MLA_HEADS * MLA_V, gq, gk, gk]
    return pl.pallas_call(
        _odd_in_body,
        grid=(G, n // tm),
        in_specs=[blk(d), pl.BlockSpec((None, 6, d), lambda g, t: (g, 0, 0)), full((1, d)), full((d, nw)),
                  full((1, ql)), full((ql, MLA_HEADS * LANES)), full((1, kl)), full((kl, MLA_HEADS * LANES)),
                  full((kl, MLA_HEADS * MLA_V)),
                  pl.BlockSpec((None, tm, 6 * LANES), lambda g, t: (g // batch, t, 0))],
        out_specs=[blk(wd) for wd in widths],
        out_shape=[jax.ShapeDtypeStruct((G, n, wd), BF16) for wd in widths],
        compiler_params=_cparams("arbitrary", "arbitrary"),
        name="odd_in",
    )(h_all, mod, g_pre, w, q_norm_g[None, :], wq, kv_norm_g[None, :], wk, wv, ropes)


def _mla_body(q_ref, kl_ref, kc_ref, vl_ref, vc_ref, o_ref, *, hps):
    lane = lax.broadcasted_iota(jnp.int32, (o_ref.shape[0], LANES), 1)
    for pr in range(hps // 2):
        vs = slice(pr * LANES, (pr + 1) * LANES)
        v_l = jnp.concatenate([vl_ref[:, vs], jnp.ones((vl_ref.shape[0], LANES), BF16)], axis=1)
        v_c = jnp.concatenate([vc_ref[:, vs], jnp.ones((vc_ref.shape[0], LANES), BF16)], axis=1)
        outs = []
        for hd in (2 * pr, 2 * pr + 1):
            sl = slice(hd * LANES, (hd + 1) * LANES)
            q = q_ref[:, sl]
            s_l = _dot_nt(q, kl_ref[:, sl])
            s_c = _dot_nt(q, kc_ref[:, sl])
            m = jnp.maximum(jnp.max(s_l, axis=-1, keepdims=True), jnp.max(s_c, axis=-1, keepdims=True))
            p_l = jnp.exp((s_l - m).astype(BF16))
            p_c = jnp.exp((s_c - m).astype(BF16))
            pv = _dot(p_l, v_l) + _dot(p_c, v_c)
            outs.append(pv[:, :LANES] / pv[:, LANES:])
        o_ref[:, vs] = jnp.where(lane < MLA_V, outs[0], outs[1]).astype(BF16)


def _mla_attn(q, k, v, batch, n, ctx_len, tq, hps):
    return pl.pallas_call(
        functools.partial(_mla_body, hps=hps),
        grid=(batch, MLA_HEADS // hps, n // tq),
        in_specs=[pl.BlockSpec((None, tq, hps * LANES), lambda b, h, i: (b, i, h)),
                  pl.BlockSpec((None, n, hps * LANES), lambda b, h, i: (b, 0, h)),
                  pl.BlockSpec((None, ctx_len, hps * LANES), lambda b, h, i: (batch, b, h)),
                  pl.BlockSpec((None, n, hps * MLA_V), lambda b, h, i: (b, 0, h)),
                  pl.BlockSpec((None, ctx_len, hps * MLA_V), lambda b, h, i: (batch, b, h))],
        out_specs=pl.BlockSpec((None, tq, hps * MLA_V), lambda b, h, i: (b, i, h)),
        out_shape=jax.ShapeDtypeStruct((batch, n, MLA_HEADS * MLA_V), BF16),
        compiler_params=_cparams("arbitrary", "arbitrary", "arbitrary"),
        name="mla_attn",
    )(q, k, k, v, v)


def _dup_half(x, kv):
    lane = lax.broadcasted_iota(jnp.int32, x.shape, 1)
    swapped = pltpu.roll(x, HEAD_DIM, axis=1)
    lo = lane < HEAD_DIM
    return jnp.where(lo, x, swapped) if kv == 0 else jnp.where(lo, swapped, x)


def _win_body(sink_ref, q_ref, kp_ref, kc_ref, kn_ref, kx_ref, vp_ref, vc_ref, vn_ref, vx_ref, o_ref, bias_ref,
              *, tq):
    i = pl.program_id(1)
    nb = pl.num_programs(1)
    w = WINDOW
    g = GQA_HEADS // GQA_KV
    n_ctx = kx_ref.shape[0]
    n_key = 2 * w + tq + n_ctx

    @pl.when((pl.program_id(0) == 0) & (i == 0))
    def _():
        r = lax.broadcasted_iota(jnp.int32, (g * tq, n_key), 0) % tq
        c = lax.broadcasted_iota(jnp.int32, (g * tq, n_key), 1)
        rel = c - w - r
        visible = (jnp.abs(rel) <= w) | (c >= 2 * w + tq)
        bias_ref[...] = jnp.where(visible, 0.0, -jnp.inf)

    edge = jnp.full((1, LANES), -jnp.inf, F32)
    zero = jnp.zeros((1, LANES), F32)
    bias_prev = jnp.where(i > 0, zero, edge)
    bias_next = jnp.where(i < nb - 1, zero, edge)
    lane = lax.broadcasted_iota(jnp.int32, (tq, LANES), 1)
    lo = lane < HEAD_DIM
    for kv in range(GQA_KV):
        rows = []
        for hh in range(g):
            hd = kv * g + hh
            blk = q_ref[:, (hd // 2) * LANES:(hd // 2 + 1) * LANES].astype(F32)
            rows.append(jnp.where(lo if hd % 2 == 0 else ~lo, blk, 0.0).astype(BF16))
        q = jnp.concatenate(rows, axis=0)
        k_all = jnp.concatenate([_dup_half(ref[...].astype(F32), kv).astype(BF16)
                                 for ref in (kp_ref, kc_ref, kn_ref, kx_ref)], axis=0)
        v_all = jnp.concatenate([_dup_half(ref[...].astype(F32), kv).astype(BF16)
                                 for ref in (vp_ref, vc_ref, vn_ref, vx_ref)], axis=0)
        v_all = jnp.concatenate([v_all, jnp.ones((n_key, LANES), BF16)], axis=1)
        s = _dot_nt(q, k_all) + bias_ref[...]
        s = jnp.concatenate([s[:, :w] + bias_prev, s[:, w:w + tq], s[:, w + tq:2 * w + tq] + bias_next,
                             s[:, 2 * w + tq:]], axis=1)
        hrow = lax.broadcasted_iota(jnp.int32, (g * tq, 1), 0) // tq
        sk = jnp.zeros((g * tq, 1), F32)
        for hh in range(g):
            sk = jnp.where(hrow == hh, sink_ref[kv * g + hh], sk)
        m = jnp.maximum(_row_max(s), sk)
        acc = _dot(jnp.exp((s - m).astype(BF16)), v_all)
        res = acc[:, :LANES] / (acc[:, LANES:] + jnp.exp(sk - m))
        for pr in range(g // 2):
            blk = (kv * g) // 2 + pr
            o_ref[:, blk * LANES:(blk + 1) * LANES] = jnp.where(
                lo, res[(2 * pr) * tq:(2 * pr + 1) * tq, :], res[(2 * pr + 1) * tq:(2 * pr + 2) * tq, :]).astype(BF16)


def _win_attn(q, k, v, sink, batch, n, ctx_len, tq):
    w = WINDOW
    per = tq // w
    gq, gk = GQA_HEADS * HEAD_DIM, GQA_KV * HEAD_DIM
    prev = lambda b, i: (b, jnp.maximum(i * per - 1, 0), 0)
    cur = lambda b, i: (b, i, 0)
    nxt = lambda b, i: (b, jnp.minimum((i + 1) * per, n // w - 1), 0)
    ctx = lambda b, i: (batch, b, 0)
    kspec = lambda rows, im: pl.BlockSpec((None, rows, gk), im)
    return pl.pallas_call(
        functools.partial(_win_body, tq=tq),
        grid=(batch, n // tq),
        in_specs=[pl.BlockSpec(memory_space=pltpu.SMEM),
                  pl.BlockSpec((None, tq, gq), cur),
                  kspec(w, prev), kspec(tq, cur), kspec(w, nxt), kspec(ctx_len, ctx),
                  kspec(w, prev), kspec(tq, cur), kspec(w, nxt), kspec(ctx_len, ctx)],
        out_specs=pl.BlockSpec((None, tq, gq), cur),
        out_shape=jax.ShapeDtypeStruct((batch, n, gq), BF16),
        scratch_shapes=[pltpu.VMEM((GQA_HEADS // GQA_KV * tq, 2 * w + tq + ctx_len), F32)],
        compiler_params=_cparams("arbitrary", "arbitrary"),
        name="win_attn",
    )(sink, q, k, k, k, k, v, v, v, v)


def _route_body(h_ref, mod_ref, g_ref, r_ref, a_ref, info_ref, cnt_ref, rows_ref, carry_ref):
    first = (pl.program_id(0) == 0) & (pl.program_id(1) == 0)

    @pl.when(first)
    def _():
        carry_ref[...] = jnp.zeros_like(carry_ref)

    a = _norm_mod(h_ref[...], g_ref[...], mod_ref[3:4, :], mod_ref[4:5, :])
    a_ref[...] = a
    tm = a.shape[0]
    logits = jnp.dot(a, r_ref[...], precision=HIGHEST, preferred_element_type=F32)
    lane = lax.broadcasted_iota(jnp.int32, logits.shape, 1)
    neg = -jnp.inf
    logits = jnp.where(lane < N_EXPERTS, logits, neg)
    m1 = jnp.max(logits, axis=-1, keepdims=True)
    i1 = jnp.min(jnp.where(logits == m1, lane, LANES), axis=-1, keepdims=True)
    rest = jnp.where(lane == i1, neg, logits)
    m2 = jnp.max(rest, axis=-1, keepdims=True)
    i2 = jnp.min(jnp.where(rest == m2, lane, LANES), axis=-1, keepdims=True)
    e2 = jnp.exp(m2 - m1)
    w1 = 1.0 / (1.0 + e2)
    w2 = e2 / (1.0 + e2)
    chosen = ((lane == i1) | (lane == i2)).astype(F32)
    ri = lax.broadcasted_iota(jnp.int32, (tm, tm), 0)
    ci = lax.broadcasted_iota(jnp.int32, (tm, tm), 1)
    tri = (ci < ri).astype(BF16)
    before = _dot(tri, chosen.astype(BF16)) + carry_ref[...]
    p1 = jnp.sum(jnp.where(lane == i1, before, 0.0), axis=-1, keepdims=True)
    p2 = jnp.sum(jnp.where(lane == i2, before, 0.0), axis=-1, keepdims=True)
    carry_ref[...] += jnp.sum(chosen, axis=0, keepdims=True)
    cnt_ref[...] = carry_ref[...]
    vals = (i1.astype(F32), i2.astype(F32), w1, w2, p1, p2)
    info = jnp.zeros(logits.shape, F32)
    for j, v in enumerate(vals):
        info = jnp.where(lane == j, v, info)
    info_ref[...] = info
    src = (0, 1, 4, 5)
    sr = lax.broadcasted_iota(jnp.int32, (SUBLANES, LANES), 0)
    sl = lax.broadcasted_iota(jnp.int32, (SUBLANES, LANES), 1)
    sel = jnp.zeros((SUBLANES, LANES), F32)
    for r, c in enumerate(src):
        sel = jnp.where((sr == r) & (sl == c), 1.0, sel)
    rows_ref[...] = lax.dot_general(sel, info, (((1,), (1,)), ((), ())), precision=HIGHEST,
                                    preferred_element_type=F32)


def _route(h_lat, mod, g_pre, router, tm):
    batch, n, d = h_lat.shape
    rp = jnp.pad(router, ((0, 0), (0, LANES - N_EXPERTS)))
    nt = n // tm
    return pl.pallas_call(
        _route_body,
        grid=(batch, nt),
        in_specs=[pl.BlockSpec((None, tm, d), lambda g, t: (g, t, 0)),
                  pl.BlockSpec((None, 6, d), lambda g, t: (g, 0, 0)),
                  pl.BlockSpec((1, d), lambda g, t: (0, 0)),
                  pl.BlockSpec((d, LANES), lambda g, t: (0, 0))],
        out_specs=[pl.BlockSpec((tm, d), lambda g, t: (g * nt + t, 0)),
                   pl.BlockSpec((tm, LANES), lambda g, t: (g * nt + t, 0)),
                   pl.BlockSpec((1, LANES), lambda g, t: (0, 0)),
                   pl.BlockSpec((SUBLANES, tm), lambda g, t: (0, g * nt + t))],
        out_shape=[jax.ShapeDtypeStruct((batch * n, d), F32),
                   jax.ShapeDtypeStruct((batch * n, LANES), F32),
                   jax.ShapeDtypeStruct((1, LANES), F32),
                   jax.ShapeDtypeStruct((SUBLANES, batch * n), F32)],
        scratch_shapes=[pltpu.VMEM((1, LANES), F32)],
        compiler_params=_cparams("arbitrary", "arbitrary"),
        name="moe_route",
    )(h_lat, mod, g_pre, rp)


SUBLANES = 8


def _scatter_body(slot_ref, a_ref, xs_in_ref, xs_ref, sem, *, ts, n_tok):
    del xs_in_ref
    base = pl.program_id(0) * ts

    def issue(i, carry):
        for u in range(SUBLANES):
            for k in range(2):
                slot = slot_ref[k * n_tok + base + i * SUBLANES + u]
                pltpu.make_async_copy(a_ref.at[i, pl.ds(u, 1), :], xs_ref.at[pl.ds(slot, 1), :],
                                      sem).start(priority=k)
        return carry

    lax.fori_loop(0, ts // SUBLANES, issue, 0)
    for k in range(2):
        pltpu.make_async_copy(xs_ref.at[pl.ds(0, ts), :], xs_ref.at[pl.ds(0, ts), :], sem).wait()


def _scatter(slots, a, n_slots, ts):
    n_tok, d = a.shape
    zeros = jnp.zeros((n_slots, d), F32)
    return pl.pallas_call(
        functools.partial(_scatter_body, ts=ts, n_tok=n_tok),
        grid_spec=pltpu.PrefetchScalarGridSpec(
            num_scalar_prefetch=1,
            grid=(n_tok // ts,),
            in_specs=[pl.BlockSpec((ts // SUBLANES, SUBLANES, d), lambda i, s: (i, 0, 0)),
                      pl.BlockSpec(memory_space=pl.ANY)],
            out_specs=pl.BlockSpec(memory_space=pl.ANY),
            scratch_shapes=[pltpu.SemaphoreType.DMA(())]),
        out_shape=jax.ShapeDtypeStruct((n_slots, d), F32),
        input_output_aliases={2: 0},
        compiler_params=_cparams("arbitrary"),
        name="moe_scatter",
    )(slots, a.reshape(n_tok // SUBLANES, SUBLANES, d), zeros)


def _experts_body(te_ref, tv_ref, x_ref, wg_ref, wu_ref, wd_ref, y_ref):
    del te_ref
    j = pl.program_id(0)
    f = pl.program_id(1)
    valid = tv_ref[j]

    @pl.when(f == 0)
    def _():
        y_ref[...] = jnp.zeros_like(y_ref)

    n_sub = (valid + MOE_SUB - 1) // MOE_SUB
    for ns in range(1, MOE_TILE // MOE_SUB + 1):
        rows = slice(0, ns * MOE_SUB)

        @pl.when(n_sub == ns)
        def _(rows=rows):
            x = x_ref[rows, :].astype(BF16)
            gate = _dot(x, wg_ref[...].astype(BF16))
            up = _dot(x, wu_ref[...].astype(BF16))
            hid = (_silu(gate) * up).astype(BF16)
            y_ref[rows, :] += _dot(hid, wd_ref[...].astype(BF16))


def _experts(tile_expert, tile_valid, xs, wg, wu, wd):
    n_slots, d = xs.shape
    ff = wg.shape[-1]
    n_tiles = n_slots // MOE_TILE
    tf = MOE_FCHUNK
    return pl.pallas_call(
        _experts_body,
        grid_spec=pltpu.PrefetchScalarGridSpec(
            num_scalar_prefetch=2,
            grid=(n_tiles, ff // tf),
            in_specs=[pl.BlockSpec((MOE_TILE, d), lambda j, f, te, tv: (j, 0)),
                      pl.BlockSpec((None, d, tf), lambda j, f, te, tv: (te[j], 0, f)),
                      pl.BlockSpec((None, d, tf), lambda j, f, te, tv: (te[j], 0, f)),
                      pl.BlockSpec((None, tf, d), lambda j, f, te, tv: (te[j], f, 0))],
            out_specs=pl.BlockSpec((MOE_TILE, d), lambda j, f, te, tv: (j, 0))),
        out_shape=jax.ShapeDtypeStruct((n_slots, d), F32),
        compiler_params=_cparams("arbitrary", "arbitrary"),
        name="moe_experts",
    )(tile_expert, tile_valid, xs, wg, wu, wd)


def _combine_body(slot_ref, ys_ref, info_ref, h_ref, mod_ref, pg_ref, o_ref, buf_ref, sem, *, tc, n_tok):
    base = (pl.program_id(0) * pl.num_programs(1) + pl.program_id(1)) * tc

    def issue(i, carry):
        for u in range(SUBLANES):
            for k in range(2):
                slot = slot_ref[k * n_tok + base + i * SUBLANES + u]
                pltpu.make_async_copy(ys_ref.at[pl.ds(slot, 1), :], buf_ref.at[k, i, pl.ds(u, 1), :],
                                      sem).start(priority=k)
        return carry

    lax.fori_loop(0, tc // SUBLANES, issue, 0)
    for k in range(2):
        pltpu.make_async_copy(ys_ref.at[pl.ds(0, tc), :], ys_ref.at[pl.ds(0, tc), :], sem).wait()
    d = o_ref.shape[-1]
    f = info_ref[:, 2:3] * buf_ref[0].reshape(tc, d) + info_ref[:, 3:4] * buf_ref[1].reshape(tc, d)
    o_ref[...] = h_ref[...] + mod_ref[5:6, :] * _rms(f, pg_ref[...])


def _combine(slots, ys, info, h_lat, mod, post_g, tc):
    batch, n, d = h_lat.shape
    nt = n // tc
    return pl.pallas_call(
        functools.partial(_combine_body, tc=tc, n_tok=batch * n),
        grid_spec=pltpu.PrefetchScalarGridSpec(
            num_scalar_prefetch=1,
            grid=(batch, nt),
            in_specs=[pl.BlockSpec(memory_space=pl.ANY),
                      pl.BlockSpec((tc, LANES), lambda g, t, *_: (g * nt + t, 0)),
                      pl.BlockSpec((None, tc, d), lambda g, t, *_: (g, t, 0)),
                      pl.BlockSpec((None, 6, d), lambda g, t, *_: (g, 0, 0)),
                      pl.BlockSpec((1, d), lambda g, t, *_: (0, 0))],
            out_specs=pl.BlockSpec((None, tc, d), lambda g, t, *_: (g, t, 0)),
            scratch_shapes=[pltpu.VMEM((2, tc // SUBLANES, SUBLANES, d), F32), pltpu.SemaphoreType.DMA(())]),
        out_shape=jax.ShapeDtypeStruct((batch, n, d), F32),
        compiler_params=_cparams("arbitrary", "arbitrary"),
        name="moe_combine",
    )(slots, ys, info, h_lat, mod, post_g)


def _moe(h_lat, mod, g_pre, router, wg, wu, wd, post_g, tm):
    batch, n, d = h_lat.shape
    n_tok = batch * n
    a, info, counts, rt = _route(h_lat, mod, g_pre, router, tm)
    cnt = counts[0, :N_EXPERTS].astype(jnp.int32)
    padded = ((cnt + MOE_TILE - 1) // MOE_TILE) * MOE_TILE
    ends = jnp.cumsum(padded)
    starts = ends - padded
    seg = jnp.zeros((2, n_tok), jnp.int32)
    for e in range(N_EXPERTS):
        seg = jnp.where(rt[0:2] == e, starts[e], seg)
    slots = (seg + rt[2:4].astype(jnp.int32)).reshape(2 * n_tok)
    n_tiles = (2 * n_tok + N_EXPERTS * (MOE_TILE - 1)) // MOE_TILE
    tile_row = jnp.arange(n_tiles, dtype=jnp.int32) * MOE_TILE
    tile_expert = jnp.minimum(jnp.sum(tile_row[:, None] >= ends[None, :], axis=1), N_EXPERTS - 1).astype(jnp.int32)
    tile_valid = jnp.clip(cnt[tile_expert] - (tile_row - starts[tile_expert]), 0, MOE_TILE)
    tile_valid = jnp.where(tile_row < ends[-1], tile_valid, 0).astype(jnp.int32)
    xs = _scatter(slots, a, n_tiles * MOE_TILE, tm)
    ys = _experts(tile_expert, tile_valid, xs, wg, wu, wd)
    return _combine(slots, ys, info, h_lat, mod, post_g, tm)


def kernel(x, c, ctx, c_ctx, ada_w, ada_b, mix_pre_g, mix_post_g, ffn_pre_g, ffn_post_g, ev_w_in, ev_hy_conv_w, ev_hy_conv_b, ev_hy_filt_w1, ev_hy_filt_b1, ev_hy_filt_w2, ev_hy_filt_b2, ev_hy_filt_w3, ev_hy_freq, ev_hy_bias, ev_ssm_conv_w, ev_ssm_conv_b, ev_ssm_dt_bias, ev_ssm_a_log, ev_ssm_d, ev_ssm_norm_g, ev_w_out, ev_ffn_w_gate, ev_ffn_w_up, ev_ffn_w_down, od_w_in, od_mla_q_norm_g, od_mla_w_q_up, od_mla_kv_norm_g, od_mla_w_kv_up, od_gqa_sink, od_w_out, od_router, od_moe_w_gate, od_moe_w_up, od_moe_w_down):
    batch, n, d = x.shape
    ctx_len = ctx.shape[1]
    assert batch * ctx_len == n and ada_w.shape[0] == 2
    G = batch + 1
    tm = min(n, 512)

    cond = jnp.concatenate([c, c_ctx[None, :], jnp.zeros((16 - G, d), F32)], axis=0)
    mods = _ada(cond, ada_w, ada_b).reshape(2, 16, 6, d)
    ctx3 = ctx.reshape(1, n, d)

    n_main = 3 * HY_CH + SSM_INNER + SSM_CONV_CH
    w_in = ev_w_in[0]
    w_main = w_in[:, :n_main].astype(BF16)
    w_dt = jnp.pad(w_in[:, n_main:], ((0, 0), (0, LANES - 2 * SSM_HEADS))).astype(BF16)
    proj, dt = _even_in(x, ctx3, mods[0], mix_pre_g[0][None, :], w_main, w_dt, min(n, 1024))
    y_hy_l, y_hy_c = _hyena(proj, batch, n, ctx_len,
                            (ev_hy_conv_w[0], ev_hy_conv_b[0], ev_hy_filt_w1[0], ev_hy_filt_b1[0], ev_hy_filt_w2[0],
                             ev_hy_filt_b2[0], ev_hy_filt_w3[0], ev_hy_freq[0], ev_hy_bias[0]))
    y_ss_l, y_ss_c = _ssd(proj, dt, batch, n, ctx_len,
                          (ev_ssm_conv_w[0], ev_ssm_conv_b[0], ev_ssm_dt_bias[0], ev_ssm_a_log[0], ev_ssm_d[0]))
    h_all = _mix_out_even(y_hy_l, y_hy_c, y_ss_l, y_ss_c, proj, ev_ssm_norm_g[0][None, :],
                          ev_w_out[0].astype(BF16), x, ctx3, mods[0], mix_post_g[0][None, :], tm)
    h_all = _ffn(h_all, mods[0], ffn_pre_g[0][None, :], ev_ffn_w_gate[0].astype(BF16), ev_ffn_w_up[0].astype(BF16),
                 ev_ffn_w_down[0].astype(BF16), ffn_post_g[0][None, :], min(n, 1024))

    q_m, k_m, v_m, q_w, k_w, v_w = _odd_in(h_all, mods[1], mix_pre_g[1][None, :], od_w_in[0], od_mla_q_norm_g[0],
                                           od_mla_w_q_up[0], od_mla_kv_norm_g[0], od_mla_w_kv_up[0], tm)
    o_mla = _mla_attn(q_m, k_m, v_m, batch, n, ctx_len, min(n, 512), 4)
    o_win = _win_attn(q_w, k_w, v_w, od_gqa_sink[0], batch, n, ctx_len, min(n, 2 * WINDOW))
    h_lat = _mix_out_odd(o_mla, o_win, od_w_out[0].astype(BF16), h_all, mods[1], mix_post_g[1][None, :], tm)
    return _moe(h_lat, mods[1], ffn_pre_g[1][None, :], od_router[0], od_moe_w_gate[0], od_moe_w_up[0],
                od_moe_w_down[0], ffn_post_g[1][None, :], tm)
```

```python
import functools
import math

import jax
import jax.numpy as jnp
from jax import lax
from jax.experimental import pallas as pl
from jax.experimental.pallas import tpu as pltpu

F32 = jnp.float32
BF16 = jnp.bfloat16
HIGHEST = lax.Precision.HIGHEST

D_MODEL = 1024
GRID_W = 64
EPS = 1e-6
HEAD_DIM = 64
GROUP_WIDTH = D_MODEL // 2

HY_CH = GROUP_WIDTH
HY_BANDS = 16
HY_EMB = 2 * HY_BANDS + 1
HY_FILT_HID = 64
HY_FAST_DECAY = 0.3
HY_SLOW_DECAY = 1.5
HY_DECAY_TARGET = 1e-2

SSM_INNER = GROUP_WIDTH
SSM_HEADDIM = 64
SSM_HEADS = SSM_INNER // SSM_HEADDIM
SSM_GROUPS = 2
SSM_STATE = 128
SSM_CHUNK = 128
SSM_CONV_CH = SSM_INNER + 2 * SSM_GROUPS * SSM_STATE

MLA_HEADS = GROUP_WIDTH // HEAD_DIM
MLA_NOPE = 64
MLA_ROPE = 32
MLA_V = 64
MLA_Q_LORA = D_MODEL // 4
MLA_KV_LORA = D_MODEL // 8

GQA_HEADS = GROUP_WIDTH // HEAD_DIM
GQA_KV = GQA_HEADS // 4
WINDOW = 128
ROPE_BASE = 10000.0

D_FF = ((8 * D_MODEL // 3 + 127) // 128) * 128
N_EXPERTS = 8
D_FF_EXPERT = 7 * D_MODEL // 2

LANES = 128
VMEM_LIMIT = 56 * 1024 * 1024
MOE_TILE = 1024
MOE_SUB = 256
MOE_FCHUNK = 512


def _cparams(*sem):
    return pltpu.CompilerParams(dimension_semantics=sem, vmem_limit_bytes=VMEM_LIMIT)


def _silu(x):
    return x / (1.0 + jnp.exp(-x))


def _rms(x, g):
    return x * lax.rsqrt(jnp.mean(x * x, axis=-1, keepdims=True) + EPS) * g


def _norm_mod(h, g, shift, scale):
    return _rms(h, g) * (1.0 + scale) + shift


def _dot(a, b):
    return jnp.dot(a, b, preferred_element_type=F32)


def _row_max(*blocks):
    tiles = [b[:, j:j + LANES] for b in blocks for j in range(0, b.shape[1], LANES)]
    m = tiles[0]
    for t in tiles[1:]:
        m = jnp.maximum(m, t)
    return jnp.max(m, axis=-1, keepdims=True)


def _dot_nt(a, b):
    return lax.dot_general(a, b, (((1,), (1,)), ((), ())), preferred_element_type=F32)


def _ada_body(c_ref, w_ref, b_ref, o_ref):
    s = _silu(c_ref[...]).astype(BF16)
    o_ref[0] = _dot(s, w_ref[0].astype(BF16)) + b_ref[0]


def _ada(cc, ada_w, ada_b):
    depth, d, n6 = ada_w.shape
    rows = cc.shape[0]
    tn = 1536
    return pl.pallas_call(
        _ada_body,
        grid=(depth, n6 // tn),
        in_specs=[pl.BlockSpec((rows, d), lambda l, j: (0, 0)),
                  pl.BlockSpec((1, d, tn), lambda l, j: (l, 0, j)),
                  pl.BlockSpec((1, 1, tn), lambda l, j: (l, 0, j))],
        out_specs=pl.BlockSpec((1, rows, tn), lambda l, j: (l, 0, j)),
        out_shape=jax.ShapeDtypeStruct((depth, rows, n6), F32),
        compiler_params=_cparams("arbitrary", "arbitrary"),
        name="ada",
    )(cc, ada_w, ada_b.reshape(depth, 1, n6))


def _lat_ctx_specs(batch, nt, tm, d):
    lat = pl.BlockSpec((None, tm, d), lambda g, t: (jnp.minimum(g, batch - 1), jnp.where(g < batch, t, nt - 1), 0))
    ctx = pl.BlockSpec((None, tm, d), lambda g, t: (0, jnp.where(g < batch, 0, t), 0))
    return lat, ctx


def _even_in_body(x_ref, c_ref, mod_ref, g_ref, w_ref, wdt_ref, o_ref, dt_ref, *, batch):
    h = jnp.where(pl.program_id(0) == batch, c_ref[...], x_ref[...])
    a = _norm_mod(h, g_ref[...], mod_ref[0:1, :], mod_ref[1:2, :]).astype(BF16)
    n_out = o_ref.shape[-1]
    for j in range(0, n_out, 512):
        o_ref[:, j:j + 512] = _dot(a, w_ref[:, j:j + 512]).astype(BF16)
    dt_ref[...] = _dot(a, wdt_ref[...])


def _even_in(x, ctx3, mod, g_pre, w_main, w_dt, tm):
    batch, n, d = x.shape
    G = batch + 1
    n_main = w_main.shape[1]
    lat_spec, ctx_spec = _lat_ctx_specs(batch, n // tm, tm, d)
    return pl.pallas_call(
        functools.partial(_even_in_body, batch=batch),
        grid=(G, n // tm),
        in_specs=[lat_spec, ctx_spec,
                  pl.BlockSpec((None, 6, d), lambda g, t: (g, 0, 0)),
                  pl.BlockSpec((1, d), lambda g, t: (0, 0)),
                  pl.BlockSpec((d, n_main), lambda g, t: (0, 0), pipeline_mode=pl.Buffered(1)),
                  pl.BlockSpec((d, LANES), lambda g, t: (0, 0))],
        out_specs=[pl.BlockSpec((None, tm, n_main), lambda g, t: (g, t, 0)),
                   pl.BlockSpec((None, tm, LANES), lambda g, t: (g, t, 0))],
        out_shape=[jax.ShapeDtypeStruct((G, n, n_main), BF16),
                   jax.ShapeDtypeStruct((G, n, LANES), F32)],
        compiler_params=_cparams("arbitrary", "arbitrary"),
        name="even_in",
    )(x, ctx3, mod, g_pre, w_main, w_dt)


def _dwconv3(x, w, b):
    L = x.shape[0]
    row = lax.broadcasted_iota(jnp.int32, x.shape, 0)
    prev = jnp.where(row >= 1, pltpu.roll(x, 1, axis=0), 0.0)
    nxt = jnp.where(row < L - 1, pltpu.roll(x, L - 1, axis=0), 0.0)
    return prev * w[0:1, :] + x * w[1:2, :] + nxt * w[2:3, :] + b


def _hy_pre_body(p_ref, cw_ref, cb_ref, u_ref, x0_ref):
    c = HY_CH
    parts = []
    for k in range(3):
        x = p_ref[:, k * c:(k + 1) * c].astype(F32)
        parts.append(_dwconv3(x, cw_ref[:, k * c:(k + 1) * c], cb_ref[:, k * c:(k + 1) * c]))
    x0_ref[...] = parts[0].astype(BF16)
    u_ref[...] = (parts[1] * parts[2]).astype(BF16)


def _hy_pre(proj, nseq, L, index_map, cw, cb):
    c3 = 3 * HY_CH
    return pl.pallas_call(
        _hy_pre_body,
        grid=(nseq,),
        in_specs=[pl.BlockSpec((None, L, c3), index_map),
                  pl.BlockSpec((3, c3), lambda s: (0, 0)),
                  pl.BlockSpec((1, c3), lambda s: (0, 0))],
        out_specs=[pl.BlockSpec((None, L, HY_CH), lambda s: (s, 0, 0)),
                   pl.BlockSpec((None, L, HY_CH), lambda s: (s, 0, 0))],
        out_shape=[jax.ShapeDtypeStruct((nseq, L, HY_CH), BF16),
                   jax.ShapeDtypeStruct((nseq, L, HY_CH), BF16)],
        compiler_params=_cparams("arbitrary"),
        name="hy_pre",
    )(proj, cw, cb)


def _hy_filter_body(z_ref, t_ref, w1_ref, b1_ref, w2_ref, b2_ref, w3_ref, f_ref, dl_ref, o_ref):
    f = f_ref[...]
    h = jnp.sin(f * (jnp.dot(z_ref[...], w1_ref[...], precision=HIGHEST, preferred_element_type=F32) + b1_ref[...]))
    h = jnp.sin(f * (jnp.dot(h, w2_ref[...], precision=HIGHEST, preferred_element_type=F32) + b2_ref[...]))
    h = jnp.dot(h, w3_ref[...], precision=HIGHEST, preferred_element_type=F32)
    decay = jnp.exp(-t_ref[...] * dl_ref[...])
    hf = h[:, :HY_CH] * decay
    hb = h[:, HY_CH:] * decay
    row = lax.broadcasted_iota(jnp.int32, hb.shape, 0)
    hb = jnp.where(row == 0, 0.0, hb)
    o_ref[:, :HY_CH] = hf.astype(BF16)
    o_ref[:, HY_CH:] = hb.astype(BF16)


def _hy_filter(L, w1, b1, w2, b2, w3, freq):
    pos = jnp.arange(L, dtype=F32)
    t = pos / (L - 1)
    omega = 2.0 * math.pi * pos / L
    bands = jnp.linspace(1e-4, HY_BANDS - 1, HY_BANDS, dtype=F32)
    z = jnp.concatenate([t[:, None], jnp.cos(omega[:, None] * bands), -jnp.sin(omega[:, None] * bands)], axis=-1)
    z = jnp.pad(z, ((0, 0), (0, LANES - HY_EMB)))
    w1p = jnp.pad(w1, ((0, LANES - HY_EMB), (0, 0)))
    deltas = jnp.abs(jnp.linspace(math.log(HY_DECAY_TARGET) / HY_SLOW_DECAY,
                                  math.log(HY_DECAY_TARGET) / HY_FAST_DECAY, HY_CH, dtype=F32))
    return pl.pallas_call(
        _hy_filter_body,
        out_shape=jax.ShapeDtypeStruct((L, 2 * HY_CH), BF16),
        compiler_params=pltpu.CompilerParams(vmem_limit_bytes=VMEM_LIMIT),
        name="hy_filter",
    )(z, t[:, None], w1p, b1[None, :], w2, b2[None, :], w3, freq[None, :], deltas[None, :])


def _dft_tables(L):
    split = 64
    t = jnp.arange(L, dtype=jnp.int32)[None, :]
    k1 = jnp.arange(L // split, dtype=jnp.int32)[:, None] * split
    k0 = jnp.arange(split, dtype=jnp.int32)[:, None]
    ang1 = ((k1 * t) % (2 * L)).astype(F32) * (math.pi / L)
    ang0 = ((k0 * t) % (2 * L)).astype(F32) * (math.pi / L)
    c1, s1 = jnp.cos(ang1)[:, None, :], jnp.sin(ang1)[:, None, :]
    c0, s0 = jnp.cos(ang0)[None, :, :], jnp.sin(ang0)[None, :, :]
    fre = (c1 * c0 - s1 * s0).reshape(L, L)
    nsin = -(s1 * c0 + c1 * s0).reshape(L, L)
    k = jnp.arange(L, dtype=jnp.int32)[:, None]
    alt_t = jnp.where(t % 2 == 0, 1.0, -1.0).astype(F32)
    alt_k = jnp.where(k % 2 == 0, 1.0, -1.0).astype(F32)
    f2 = jnp.stack([fre, jnp.where(k == 0, alt_t, nsin)]).astype(BF16)
    fimt = jnp.where(t == 0, alt_k, nsin).astype(BF16)
    return f2, fimt


def _filt_dft_body(f_ref, x_ref, o_ref, *, L, tr):
    c = HY_CH
    x = x_ref[...]
    sre = _dot(f_ref[0], x)
    sim = _dot(f_ref[1], x)
    is0 = (lax.broadcasted_iota(jnp.int32, (tr, c), 0) + pl.program_id(0) * tr) == 0
    wgt = jnp.where(is0, 1.0 / (2 * L), 1.0 / L)
    hre = sre[:, :c] + sre[:, c:]
    o_ref[0] = (hre * wgt).astype(BF16)
    o_ref[1] = (jnp.where(is0, 0.0, sim[:, :c] - sim[:, c:]) * wgt).astype(BF16)
    o_ref[2] = (jnp.where(is0, sim[:, :c] + sim[:, c:], hre) * wgt).astype(BF16)


def _filt_dft(f2, hfb):
    _, L, _ = f2.shape
    cols = hfb.shape[1]
    tr = min(L, 512)
    return pl.pallas_call(
        functools.partial(_filt_dft_body, L=L, tr=tr),
        grid=(L // tr,),
        in_specs=[pl.BlockSpec((2, tr, L), lambda i: (0, i, 0)),
                  pl.BlockSpec((L, cols), lambda i: (0, 0))],
        out_specs=pl.BlockSpec((3, tr, cols // 2), lambda i: (0, i, 0)),
        out_shape=jax.ShapeDtypeStruct((3, L, cols // 2), BF16),
        compiler_params=_cparams("arbitrary"),
        name="filt_dft",
    )(f2, hfb)


def _hy_conv_body(u_ref, x0_ref, f2_ref, fimt_ref, s_ref, bias_ref, o_ref, *, L, fc):
    u = u_ref[...]
    acc = None
    for k in range(L // fc):
        ks = slice(k * fc, (k + 1) * fc)
        ure = _dot(f2_ref[0, ks, :], u)
        uim = _dot(f2_ref[1, ks, :], u)
        a_m = s_ref[0, ks, :].astype(F32)
        b_m = s_ref[1, ks, :].astype(F32)
        d_m = s_ref[2, ks, :].astype(F32)
        yre = (ure * a_m - uim * b_m).astype(BF16)
        yim = (ure * b_m + uim * d_m).astype(BF16)
        part = _dot(f2_ref[0, :, ks], yre) + _dot(fimt_ref[:, ks], yim)
        acc = part if acc is None else acc + part
    y = acc + u.astype(F32) * bias_ref[...]
    o_ref[...] = (x0_ref[...].astype(F32) * y).astype(BF16)


def _hy_conv(u, x0, f2, fimt, s, bias):
    nseq, L, c = u.shape
    fc = min(L, 512)
    resident = lambda shape: pl.BlockSpec(shape, lambda s_: (0,) * len(shape), pipeline_mode=pl.Buffered(1))
    return pl.pallas_call(
        functools.partial(_hy_conv_body, L=L, fc=fc),
        grid=(nseq,),
        in_specs=[pl.BlockSpec((None, L, c), lambda s_: (s_, 0, 0)),
                  pl.BlockSpec((None, L, c), lambda s_: (s_, 0, 0)),
                  resident((2, L, L)), resident((L, L)), resident((3, L, c)),
                  pl.BlockSpec((1, c), lambda s_: (0, 0))],
        out_specs=pl.BlockSpec((None, L, c), lambda s_: (s_, 0, 0)),
        out_shape=jax.ShapeDtypeStruct((nseq, L, c), BF16),
        compiler_params=_cparams("arbitrary"),
        name="hy_conv",
    )(u, x0, f2, fimt, s, bias)


def _hyena(proj, batch, n, ctx_len, p):
    cw, cb, w1, b1, w2, b2, w3, freq, bias = p
    outs = []
    for L, imap in ((n, lambda s: (s, 0, 0)), (ctx_len, lambda s: (batch, s, 0))):
        u, x0 = _hy_pre(proj, batch, L, imap, cw, cb[None, :])
        hfb = _hy_filter(L, w1, b1, w2, b2, w3, freq)
        f2, fimt = _dft_tables(L)
        s = _filt_dft(f2, hfb)
        outs.append(_hy_conv(u, x0, f2, fimt, s, bias[None, :]))
    return outs


def _softplus(x):
    return jnp.maximum(x, 0.0) + jnp.log(1.0 + jnp.exp(-jnp.abs(x)))


def _chunk_cumsum(a, axis, reverse):
    q = SSM_CHUNK
    size = a.shape[axis]
    pos = lax.broadcasted_iota(jnp.int32, a.shape, axis) % q
    s = 1
    while s < q:
        if reverse:
            a = a + jnp.where(pos < q - s, pltpu.roll(a, size - s, axis=axis), 0.0)
        else:
            a = a + jnp.where(pos >= s, pltpu.roll(a, s, axis=axis), 0.0)
        s *= 2
    return a


def _ssd_prepare(L, xbc_ref, dtc_ref, dtr_ref, cw_ref, cb_ref, dtbc_ref, alc_ref, dtbr_ref, alr_ref,
                 x_s, c_s, bt_s, csc_s, csr_s, dtr_s):
    nc = L // SSM_CHUNK
    nh = SSM_HEADS
    inner = SSM_INNER
    gs = SSM_GROUPS * SSM_STATE
    xs = _silu(_dwconv3(xbc_ref[:, :inner].astype(F32), cw_ref[:, :inner], cb_ref[:, :inner]))
    x_s[0:L, :] = xs.astype(BF16)
    bm = _silu(_dwconv3(xbc_ref[:, inner:inner + gs].astype(F32), cw_ref[:, inner:inner + gs],
                        cb_ref[:, inner:inner + gs]))
    for c in range(nc):
        bt_s[c] = bm[c * SSM_CHUNK:(c + 1) * SSM_CHUNK, :].T.astype(BF16)
    cm = _silu(_dwconv3(xbc_ref[:, inner + gs:].astype(F32), cw_ref[:, inner + gs:], cb_ref[:, inner + gs:]))
    c_s[0:L, :] = cm.astype(BF16)
    a_col = _softplus(dtc_ref[...] + dtbc_ref[...]) * (-jnp.exp(alc_ref[...]))
    lane = lax.broadcasted_iota(jnp.int32, a_col.shape, 1)
    csc_s[0:L, :] = jnp.where(lane < nh, _chunk_cumsum(a_col, 0, False), _chunk_cumsum(a_col, 0, True))
    dt_row = _softplus(dtr_ref[...] + dtbr_ref[...])
    a_row = (dt_row * (-jnp.exp(alr_ref[...]))).reshape(2 * nh * nc, SSM_CHUNK)
    rown = lax.broadcasted_iota(jnp.int32, a_row.shape, 0)
    csr_s[0:2 * nh * nc, :] = jnp.where(rown < nh * nc, _chunk_cumsum(a_row, 1, False),
                                        _chunk_cumsum(a_row, 1, True))
    dtr_s[0:2 * nh * nc, :] = dt_row.reshape(2 * nh * nc, SSM_CHUNK)


def _ssd_chunk(c, d, nc, x_s, c_s, bt_s, csc_s, csr_s, dtr_s, st_s, y_s):
    q = SSM_CHUNK
    ns = SSM_STATE
    hp = SSM_HEADDIM
    hpg = SSM_HEADS // SSM_GROUPS
    gw = hpg * hp
    r0 = pl.multiple_of(c * q, q)
    ri = lax.broadcasted_iota(jnp.int32, (q, q), 0)
    ci = lax.broadcasted_iota(jnp.int32, (q, q), 1)
    keep = (ci <= ri) if d == 0 else (ci >= ri)
    lane_blk = lax.broadcasted_iota(jnp.int32, (q, gw), 1) // hp
    cc = c_s[pl.ds(r0, q), :]
    xc = x_s[pl.ds(r0, q), :]
    btc = bt_s[c]
    ys = []
    for g in range(SSM_GROUPS):
        cg = cc[:, g * ns:(g + 1) * ns]
        btg = btc[g * ns:(g + 1) * ns, :]
        xg = xc[:, g * gw:(g + 1) * gw]
        s_cb = _dot(cg, btg)
        btg32 = btg.astype(F32)
        m_rows, w_rows, e_cols, tots = [], [], [], []
        for hh in range(hpg):
            idx = d * SSM_HEADS + g * hpg + hh
            csc = csc_s[pl.ds(r0, q), idx:idx + 1]
            csr = csr_s[pl.ds(idx * nc + c, 1), :]
            dtr = dtr_s[pl.ds(idx * nc + c, 1), :]
            lmat = jnp.where(keep, jnp.exp(csc - csr), 0.0)
            m_rows.append((s_cb * lmat * dtr).astype(BF16))
            tot = csr[:, q - 1:q] if d == 0 else csr[:, 0:1]
            w_rows.append((btg32 * (jnp.exp(tot - csr) * dtr)).astype(BF16))
            e_cols.append(jnp.exp(csc))
            tots.append(jnp.exp(tot))
        yd = _dot(jnp.concatenate(m_rows, axis=0), xg)
        ds = _dot(jnp.concatenate(w_rows, axis=0), xg)
        st = st_s[d, g]
        yo = _dot(cg, st.astype(BF16))
        y_g = jnp.zeros((q, gw), F32)
        st_new = jnp.zeros((ns, gw), F32)
        for hh in range(hpg):
            sel = lane_blk == hh
            y_g = jnp.where(sel, yd[hh * q:(hh + 1) * q, :] + yo * e_cols[hh], y_g)
            st_new = jnp.where(sel, st * tots[hh] + ds[hh * ns:(hh + 1) * ns, :], st_new)
        st_s[d, g] = st_new
        ys.append(y_g)
    y_s[pl.ds(r0, q), :] += jnp.concatenate(ys, axis=1)


def _ssd_body(xl_ref, xc_ref, dcl_ref, dcc_ref, drl_ref, drc_ref, cw_ref, cb_ref, dtbc_ref, alc_ref,
              dtbr_ref, alr_ref, dsk_ref, yl_ref, yc_ref,
              x_s, c_s, bt_s, csc_s, csr_s, dtr_s, st_s, y_s, *, n, ctx_len):
    st_s[...] = jnp.zeros_like(st_s)
    for L, xbc_ref, dtc_ref, dtr_ref, o_ref in ((ctx_len, xc_ref, dcc_ref, drc_ref, yc_ref),
                                                 (n, xl_ref, dcl_ref, drl_ref, yl_ref)):
        nc = L // SSM_CHUNK
        _ssd_prepare(L, xbc_ref, dtc_ref, dtr_ref, cw_ref, cb_ref, dtbc_ref, alc_ref, dtbr_ref, alr_ref,
                     x_s, c_s, bt_s, csc_s, csr_s, dtr_s)
        args = (x_s, c_s, bt_s, csc_s, csr_s, dtr_s, st_s, y_s)

        y_s[0:L, :] = jnp.zeros((L, SSM_INNER), F32)

        def both(i, carry, nc=nc, args=args):
            _ssd_chunk(i, 0, nc, *args)
            _ssd_chunk(nc - 1 - i, 1, nc, *args)
            return carry

        lax.fori_loop(0, nc, both, 0)
        o_ref[...] = (y_s[0:L, :] + x_s[0:L, :].astype(F32) * dsk_ref[...]).astype(BF16)


def _ssd(proj, dt, batch, n, ctx_len, p):
    conv_w, conv_b, dt_bias, a_log, d_skip = p
    nh = SSM_HEADS
    ncl, ncc = n // SSM_CHUNK, ctx_len // SSM_CHUNK
    cch = SSM_CONV_CH
    xbc_blk = (SSM_INNER * 3 + SSM_INNER) // cch
    dt16 = dt[:, :, :2 * nh]
    dtr_l = dt16[:batch].transpose(0, 2, 1).reshape(batch, 2 * nh, ncl, SSM_CHUNK)
    dtr_c = dt16[batch].reshape(batch, ctx_len, 2 * nh).transpose(0, 2, 1).reshape(batch, 2 * nh, ncc, SSM_CHUNK)
    pad = LANES - 2 * nh
    dtb_col = jnp.pad(dt_bias.reshape(1, 2 * nh), ((0, 0), (0, pad)))
    al_col = jnp.pad(a_log.reshape(1, 2 * nh), ((0, 0), (0, pad)))
    dtb_row = jnp.broadcast_to(dt_bias.reshape(2 * nh, 1, 1), (2 * nh, 1, SSM_CHUNK))
    al_row = jnp.broadcast_to(a_log.reshape(2 * nh, 1, 1), (2 * nh, 1, SSM_CHUNK))
    dsk = jnp.repeat(d_skip, SSM_HEADDIM)[None, :]
    full = lambda shape: pl.BlockSpec(shape, lambda b: (0,) * len(shape))
    return pl.pallas_call(
        functools.partial(_ssd_body, n=n, ctx_len=ctx_len),
        grid=(batch,),
        in_specs=[pl.BlockSpec((None, n, cch), lambda b: (b, 0, xbc_blk)),
                  pl.BlockSpec((None, ctx_len, cch), lambda b: (batch, b, xbc_blk)),
                  pl.BlockSpec((None, n, LANES), lambda b: (b, 0, 0)),
                  pl.BlockSpec((None, ctx_len, LANES), lambda b: (batch, b, 0)),
                  pl.BlockSpec((None, 2 * nh, ncl, SSM_CHUNK), lambda b: (b, 0, 0, 0)),
                  pl.BlockSpec((None, 2 * nh, ncc, SSM_CHUNK), lambda b: (b, 0, 0, 0)),
                  full((3, cch)), full((1, cch)), full((1, LANES)), full((1, LANES)),
                  full((2 * nh, 1, SSM_CHUNK)), full((2 * nh, 1, SSM_CHUNK)), full((1, SSM_INNER))],
        out_specs=[pl.BlockSpec((None, n, SSM_INNER), lambda b: (b, 0, 0)),
                   pl.BlockSpec((None, ctx_len, SSM_INNER), lambda b: (b, 0, 0))],
        out_shape=[jax.ShapeDtypeStruct((batch, n, SSM_INNER), BF16),
                   jax.ShapeDtypeStruct((batch, ctx_len, SSM_INNER), BF16)],
        scratch_shapes=[pltpu.VMEM((n, SSM_INNER), BF16),
                        pltpu.VMEM((n, SSM_GROUPS * SSM_STATE), BF16),
                        pltpu.VMEM((ncl, SSM_GROUPS * SSM_STATE, SSM_CHUNK), BF16),
                        pltpu.VMEM((n, LANES), F32),
                        pltpu.VMEM((2 * nh * ncl, SSM_CHUNK), F32),
                        pltpu.VMEM((2 * nh * ncl, SSM_CHUNK), F32),
                        pltpu.VMEM((2, SSM_GROUPS, SSM_STATE, SSM_INNER // SSM_GROUPS), F32),
                        pltpu.VMEM((n, SSM_INNER), F32)],
        compiler_params=_cparams("arbitrary"),
        name="ssd",
    )(proj, proj, dt, dt, dtr_l, dtr_c, conv_w, conv_b[None, :], dtb_col, al_col, dtb_row, al_row, dsk)


def _mix_out_even_body(yhl_ref, yhc_ref, ysl_ref, ysc_ref, z_ref, ng_ref, w_ref, x_ref, c_ref, mod_ref, pg_ref, o_ref,
                       *, batch):
    is_ctx = pl.program_id(0) == batch
    h = jnp.where(is_ctx, c_ref[...], x_ref[...])
    y_hy = jnp.where(is_ctx, yhc_ref[...], yhl_ref[...])
    ys = jnp.where(is_ctx, ysc_ref[...], ysl_ref[...]).astype(F32) * _silu(z_ref[...].astype(F32))
    gw = SSM_INNER // SSM_GROUPS
    parts = []
    for g in range(SSM_GROUPS):
        yg = ys[:, g * gw:(g + 1) * gw]
        parts.append(yg * lax.rsqrt(jnp.mean(yg * yg, axis=-1, keepdims=True) + EPS))
    y_ssm = (jnp.concatenate(parts, axis=1) * ng_ref[...]).astype(BF16)
    m = _dot(y_hy, w_ref[0:HY_CH, :]) + _dot(y_ssm, w_ref[HY_CH:, :])
    o_ref[...] = h + mod_ref[2:3, :] * _rms(m, pg_ref[...])


def _mix_out_even(y_hy_l, y_hy_c, y_ss_l, y_ss_c, proj, norm_g, w_out, x, ctx3, mod, post_g, tm):
    batch, n, d = x.shape
    G = batch + 1
    c = HY_CH
    z_blk = (3 * HY_CH) // SSM_INNER
    lat_c, ctx_c = _lat_ctx_specs(batch, n // tm, tm, c)
    lat_d, ctx_d = _lat_ctx_specs(batch, n // tm, tm, d)
    return pl.pallas_call(
        functools.partial(_mix_out_even_body, batch=batch),
        grid=(G, n // tm),
        in_specs=[lat_c, ctx_c, lat_c, ctx_c,
                  pl.BlockSpec((None, tm, SSM_INNER), lambda g, t: (g, t, z_blk)),
                  pl.BlockSpec((1, SSM_INNER), lambda g, t: (0, 0)),
                  pl.BlockSpec((2 * c, d), lambda g, t: (0, 0)),
                  lat_d, ctx_d,
                  pl.BlockSpec((None, 6, d), lambda g, t: (g, 0, 0)),
                  pl.BlockSpec((1, d), lambda g, t: (0, 0))],
        out_specs=pl.BlockSpec((None, tm, d), lambda g, t: (g, t, 0)),
        out_shape=jax.ShapeDtypeStruct((G, n, d), F32),
        compiler_params=_cparams("arbitrary", "arbitrary"),
        name="mix_out_even",
    )(y_hy_l, y_hy_c.reshape(1, n, c), y_ss_l, y_ss_c.reshape(1, n, c), proj, norm_g, w_out, x, ctx3, mod, post_g)


def _mix_out_odd_body(a_ref, b_ref, w_ref, h_ref, mod_ref, pg_ref, o_ref):
    half = a_ref.shape[-1]
    m = _dot(a_ref[...], w_ref[0:half, :]) + _dot(b_ref[...], w_ref[half:, :])
    o_ref[...] = h_ref[...] + mod_ref[2:3, :] * _rms(m, pg_ref[...])


def _mix_out_odd(o_a, o_b, w_out, h_all, mod, post_g, tm):
    batch, n, c = o_a.shape
    d = h_all.shape[-1]
    return pl.pallas_call(
        _mix_out_odd_body,
        grid=(batch, n // tm),
        in_specs=[pl.BlockSpec((None, tm, c), lambda g, t: (g, t, 0)),
                  pl.BlockSpec((None, tm, c), lambda g, t: (g, t, 0)),
                  pl.BlockSpec((2 * c, d), lambda g, t: (0, 0)),
                  pl.BlockSpec((None, tm, d), lambda g, t: (g, t, 0)),
                  pl.BlockSpec((None, 6, d), lambda g, t: (g, 0, 0)),
                  pl.BlockSpec((1, d), lambda g, t: (0, 0))],
        out_specs=pl.BlockSpec((None, tm, d), lambda g, t: (g, t, 0)),
        out_shape=jax.ShapeDtypeStruct((batch, n, d), F32),
        compiler_params=_cparams("arbitrary", "arbitrary"),
        name="mix_out_odd",
    )(o_a, o_b, w_out, h_all, mod, post_g)


def _ffn_body(h_ref, mod_ref, g_ref, wg_ref, wu_ref, wd_ref, pg_ref, o_ref, *, fchunk):
    h = h_ref[...]
    a = _norm_mod(h, g_ref[...], mod_ref[3:4, :], mod_ref[4:5, :]).astype(BF16)
    ff = wg_ref.shape[1]
    f = jnp.zeros(h.shape, F32)
    for j in range(0, ff, fchunk):
        gate = _dot(a, wg_ref[:, j:j + fchunk])
        up = _dot(a, wu_ref[:, j:j + fchunk])
        f = f + _dot((_silu(gate) * up).astype(BF16), wd_ref[j:j + fchunk, :])
    o_ref[...] = h + mod_ref[5:6, :] * _rms(f, pg_ref[...])


def _ffn(h_all, mod, g_pre, wg, wu, wd, post_g, tm):
    G, n, d = h_all.shape
    ff = wg.shape[1]
    return pl.pallas_call(
        functools.partial(_ffn_body, fchunk=256),
        grid=(G, n // tm),
        in_specs=[pl.BlockSpec((None, tm, d), lambda g, t: (g, t, 0)),
                  pl.BlockSpec((None, 6, d), lambda g, t: (g, 0, 0)),
                  pl.BlockSpec((1, d), lambda g, t: (0, 0)),
                  pl.BlockSpec((d, ff), lambda g, t: (0, 0), pipeline_mode=pl.Buffered(1)),
                  pl.BlockSpec((d, ff), lambda g, t: (0, 0), pipeline_mode=pl.Buffered(1)),
                  pl.BlockSpec((ff, d), lambda g, t: (0, 0), pipeline_mode=pl.Buffered(1)),
                  pl.BlockSpec((1, d), lambda g, t: (0, 0))],
        out_specs=pl.BlockSpec((None, tm, d), lambda g, t: (g, t, 0)),
        out_shape=jax.ShapeDtypeStruct((G, n, d), F32),
        compiler_params=_cparams("arbitrary", "arbitrary"),
        name="ffn",
    )(h_all, mod, g_pre, wg, wu, wd, post_g)


def _rope(x, cos, sin_a, sin_b, half):
    return x * cos + pltpu.roll(x, LANES - half, axis=1) * sin_a + pltpu.roll(x, half, axis=1) * sin_b


def _odd_in_body(h_ref, mod_ref, g_ref, w_ref, qg_ref, wq_ref, kg_ref, wk_ref, wv_ref, rp_ref,
                 qm_ref, km_ref, vm_ref, qw_ref, kw_ref, vw_ref):
    a = _norm_mod(h_ref[...], g_ref[...], mod_ref[0:1, :], mod_ref[1:2, :]).astype(BF16)
    ql, kl = MLA_Q_LORA, MLA_KV_LORA
    o_qg = ql + kl
    o_kg = o_qg + GQA_HEADS * HEAD_DIM
    o_vg = o_kg + GQA_KV * HEAD_DIM
    o_kr = o_vg + GQA_KV * HEAD_DIM
    cm, s1m, s2m = rp_ref[:, 0:LANES], rp_ref[:, LANES:2 * LANES], rp_ref[:, 2 * LANES:3 * LANES]
    cg, s1g, s2g = rp_ref[:, 3 * LANES:4 * LANES], rp_ref[:, 4 * LANES:5 * LANES], rp_ref[:, 5 * LANES:6 * LANES]
    mla_scale = (MLA_NOPE + MLA_ROPE) ** -0.5
    gqa_scale = HEAD_DIM ** -0.5
    q_lat = _rms(_dot(a, w_ref[:, 0:ql]), qg_ref[...]).astype(BF16)
    q = _dot(q_lat, wq_ref[...])
    kv_lat = _rms(_dot(a, w_ref[:, ql:o_qg]), kg_ref[...]).astype(BF16)
    k = _dot(kv_lat, wk_ref[...])
    vm_ref[...] = _dot(kv_lat, wv_ref[...]).astype(BF16)
    k_rope = _rope(_dot(a, w_ref[:, o_kr:o_kr + LANES]), cm, s1m, s2m, MLA_ROPE // 2)
    for hd in range(MLA_HEADS):
        sl = slice(hd * LANES, (hd + 1) * LANES)
        qm_ref[:, sl] = (_rope(q[:, sl], cm, s1m, s2m, MLA_ROPE // 2) * mla_scale).astype(BF16)
        km_ref[:, sl] = (k[:, sl] + k_rope).astype(BF16)
    qg = _dot(a, w_ref[:, o_qg:o_kg])
    for blk in range(GQA_HEADS * HEAD_DIM // LANES):
        sl = slice(blk * LANES, (blk + 1) * LANES)
        qw_ref[:, sl] = (_rope(qg[:, sl], cg, s1g, s2g, HEAD_DIM // 2) * gqa_scale).astype(BF16)
    kw_ref[...] = _rope(_dot(a, w_ref[:, o_kg:o_vg]), cg, s1g, s2g, HEAD_DIM // 2).astype(BF16)
    vw_ref[...] = _dot(a, w_ref[:, o_vg:o_kr]).astype(BF16)


def _rope_tables(n):
    rows = (jnp.arange(n) // GRID_W).astype(F32)
    cols = (jnp.arange(n) % GRID_W).astype(F32)

    def cs(rot_dim):
        nf = rot_dim // 4
        inv = ROPE_BASE ** (-jnp.arange(nf, dtype=F32) / nf)
        ang = jnp.concatenate([rows[:, None] * inv, cols[:, None] * inv], axis=-1)
        return jnp.cos(ang), jnp.sin(ang)

    one = jnp.ones((n, 1), F32)
    zero = jnp.zeros((n, 1), F32)
    c, s = cs(MLA_ROPE)
    hm = MLA_ROPE // 2
    pad_n, pad_t = MLA_NOPE, LANES - MLA_NOPE - MLA_ROPE
    cm = jnp.concatenate([jnp.tile(one, (1, pad_n)), c, c, jnp.tile(one, (1, pad_t))], axis=1)
    s1m = jnp.concatenate([jnp.tile(zero, (1, pad_n)), -s, jnp.tile(zero, (1, hm + pad_t))], axis=1)
    s2m = jnp.concatenate([jnp.tile(zero, (1, pad_n + hm)), s, jnp.tile(zero, (1, pad_t))], axis=1)
    c, s = cs(HEAD_DIM)
    z32 = jnp.zeros_like(s)
    cg = jnp.concatenate([c, c, c, c], axis=1)
    s1g = jnp.concatenate([-s, z32, -s, z32], axis=1)
    s2g = jnp.concatenate([z32, s, z32, s], axis=1)
    lat = jnp.concatenate([cm, s1m, s2m, cg, s1g, s2g], axis=1)
    ident = jnp.concatenate([jnp.ones((n, LANES), F32), jnp.zeros((n, 2 * LANES), F32)] * 2, axis=1)
    return jnp.stack([lat, ident])


def _odd_in(h_all, mod, g_pre, w_in, q_norm_g, w_q_up, kv_norm_g, w_kv_up, tm):
    G, n, d = h_all.shape
    batch = G - 1
    ql, kl, rp = MLA_Q_LORA, MLA_KV_LORA, MLA_ROPE
    gq, gk = GQA_HEADS * HEAD_DIM, GQA_KV * HEAD_DIM
    o = [0, ql, ql + kl, ql + kl + rp, ql + kl + rp + gq, ql + kl + rp + gq + gk]
    kr_cols = jnp.pad(w_in[:, o[2]:o[3]], ((0, 0), (MLA_NOPE, LANES - MLA_NOPE - rp)))
    w = jnp.concatenate([w_in[:, o[0]:o[2]], w_in[:, o[3]:], kr_cols], axis=1).astype(BF16)
    hq = MLA_NOPE + MLA_ROPE
    wq = jnp.pad(w_q_up.reshape(ql, MLA_HEADS, hq), ((0, 0), (0, 0), (0, LANES - hq)))
    wq = wq.reshape(ql, MLA_HEADS * LANES).astype(BF16)
    wkv = w_kv_up.reshape(kl, MLA_HEADS, MLA_NOPE + MLA_V)
    wk = jnp.pad(wkv[:, :, :MLA_NOPE], ((0, 0), (0, 0), (0, LANES - MLA_NOPE)))
    wk = wk.reshape(kl, MLA_HEADS * LANES).astype(BF16)
    wv = wkv[:, :, MLA_NOPE:].reshape(kl, MLA_HEADS * MLA_V).astype(BF16)
    ropes = _rope_tables(n)
    nw = w.shape[1]
    full = lambda shape: pl.BlockSpec(shape, lambda g, t: (0,) * len(shape))
    blk = lambda width: pl.BlockSpec((None, tm, width), lambda g, t: (g, t, 0))
    widths = [MLA_HEADS * LANES, MLA_HEADS * LANES, MLA_HEADS * MLA_V, gq, gk, gk]
    return pl.pallas_call(
        _odd_in_body,
        grid=(G, n // tm),
        in_specs=[blk(d), pl.BlockSpec((None, 6, d), lambda g, t: (g, 0, 0)), full((1, d)), full((d, nw)),
                  full((1, ql)), full((ql, MLA_HEADS * LANES)), full((1, kl)), full((kl, MLA_HEADS * LANES)),
                  full((kl, MLA_HEADS * MLA_V)),
                  pl.BlockSpec((None, tm, 6 * LANES), lambda g, t: (g // batch, t, 0))],
        out_specs=[blk(wd) for wd in widths],
        out_shape=[jax.ShapeDtypeStruct((G, n, wd), BF16) for wd in widths],
        compiler_params=_cparams("arbitrary", "arbitrary"),
        name="odd_in",
    )(h_all, mod, g_pre, w, q_norm_g[None, :], wq, kv_norm_g[None, :], wk, wv, ropes)


def _mla_body(q_ref, kl_ref, kc_ref, vl_ref, vc_ref, o_ref, *, hps):
    lane = lax.broadcasted_iota(jnp.int32, (o_ref.shape[0], LANES), 1)
    for pr in range(hps // 2):
        vs = slice(pr * LANES, (pr + 1) * LANES)
        v_l = jnp.concatenate([vl_ref[:, vs], jnp.ones((vl_ref.shape[0], LANES), BF16)], axis=1)
        v_c = jnp.concatenate([vc_ref[:, vs], jnp.ones((vc_ref.shape[0], LANES), BF16)], axis=1)
        outs = []
        for hd in (2 * pr, 2 * pr + 1):
            sl = slice(hd * LANES, (hd + 1) * LANES)
            q = q_ref[:, sl]
            s_l = _dot_nt(q, kl_ref[:, sl])
            s_c = _dot_nt(q, kc_ref[:, sl])
            m = jnp.maximum(jnp.max(s_l, axis=-1, keepdims=True), jnp.max(s_c, axis=-1, keepdims=True))
            p_l = jnp.exp((s_l - m).astype(BF16))
            p_c = jnp.exp((s_c - m).astype(BF16))
            pv = _dot(p_l, v_l) + _dot(p_c, v_c)
            outs.append(pv[:, :LANES] / pv[:, LANES:])
        o_ref[:, vs] = jnp.where(lane < MLA_V, outs[0], outs[1]).astype(BF16)


def _mla_attn(q, k, v, batch, n, ctx_len, tq, hps):
    return pl.pallas_call(
        functools.partial(_mla_body, hps=hps),
        grid=(batch, MLA_HEADS // hps, n // tq),
        in_specs=[pl.BlockSpec((None, tq, hps * LANES), lambda b, h, i: (b, i, h)),
                  pl.BlockSpec((None, n, hps * LANES), lambda b, h, i: (b, 0, h)),
                  pl.BlockSpec((None, ctx_len, hps * LANES), lambda b, h, i: (batch, b, h)),
                  pl.BlockSpec((None, n, hps * MLA_V), lambda b, h, i: (b, 0, h)),
                  pl.BlockSpec((None, ctx_len, hps * MLA_V), lambda b, h, i: (batch, b, h))],
        out_specs=pl.BlockSpec((None, tq, hps * MLA_V), lambda b, h, i: (b, i, h)),
        out_shape=jax.ShapeDtypeStruct((batch, n, MLA_HEADS * MLA_V), BF16),
        compiler_params=_cparams("arbitrary", "arbitrary", "arbitrary"),
        name="mla_attn",
    )(q, k, k, v, v)


def _dup_half(x, kv):
    lane = lax.broadcasted_iota(jnp.int32, x.shape, 1)
    swapped = pltpu.roll(x, HEAD_DIM, axis=1)
    lo = lane < HEAD_DIM
    return jnp.where(lo, x, swapped) if kv == 0 else jnp.where(lo, swapped, x)


def _win_body(sink_ref, q_ref, kp_ref, kc_ref, kn_ref, kx_ref, vp_ref, vc_ref, vn_ref, vx_ref, o_ref, bias_ref,
              *, tq):
    i = pl.program_id(1)
    nb = pl.num_programs(1)
    w = WINDOW
    g = GQA_HEADS // GQA_KV
    n_ctx = kx_ref.shape[0]
    n_key = 2 * w + tq + n_ctx

    @pl.when((pl.program_id(0) == 0) & (i == 0))
    def _():
        r = lax.broadcasted_iota(jnp.int32, (g * tq, n_key), 0) % tq
        c = lax.broadcasted_iota(jnp.int32, (g * tq, n_key), 1)
        rel = c - w - r
        visible = (jnp.abs(rel) <= w) | (c >= 2 * w + tq)
        bias_ref[...] = jnp.where(visible, 0.0, -jnp.inf)

    edge = jnp.full((1, LANES), -jnp.inf, F32)
    zero = jnp.zeros((1, LANES), F32)
    bias_prev = jnp.where(i > 0, zero, edge)
    bias_next = jnp.where(i < nb - 1, zero, edge)
    lane = lax.broadcasted_iota(jnp.int32, (tq, LANES), 1)
    lo = lane < HEAD_DIM
    for kv in range(GQA_KV):
        rows = []
        for hh in range(g):
            hd = kv * g + hh
            blk = q_ref[:, (hd // 2) * LANES:(hd // 2 + 1) * LANES].astype(F32)
            rows.append(jnp.where(lo if hd % 2 == 0 else ~lo, blk, 0.0).astype(BF16))
        q = jnp.concatenate(rows, axis=0)
        k_all = jnp.concatenate([_dup_half(ref[...].astype(F32), kv).astype(BF16)
                                 for ref in (kp_ref, kc_ref, kn_ref, kx_ref)], axis=0)
        v_all = jnp.concatenate([_dup_half(ref[...].astype(F32), kv).astype(BF16)
                                 for ref in (vp_ref, vc_ref, vn_ref, vx_ref)], axis=0)
        v_all = jnp.concatenate([v_all, jnp.ones((n_key, LANES), BF16)], axis=1)
        s = _dot_nt(q, k_all) + bias_ref[...]
        s = jnp.concatenate([s[:, :w] + bias_prev, s[:, w:w + tq], s[:, w + tq:2 * w + tq] + bias_next,
                             s[:, 2 * w + tq:]], axis=1)
        hrow = lax.broadcasted_iota(jnp.int32, (g * tq, 1), 0) // tq
        sk = jnp.zeros((g * tq, 1), F32)
        for hh in range(g):
            sk = jnp.where(hrow == hh, sink_ref[kv * g + hh], sk)
        m = jnp.maximum(_row_max(s), sk)
        acc = _dot(jnp.exp((s - m).astype(BF16)), v_all)
        res = acc[:, :LANES] / (acc[:, LANES:] + jnp.exp(sk - m))
        for pr in range(g // 2):
            blk = (kv * g) // 2 + pr
            o_ref[:, blk * LANES:(blk + 1) * LANES] = jnp.where(
                lo, res[(2 * pr) * tq:(2 * pr + 1) * tq, :], res[(2 * pr + 1) * tq:(2 * pr + 2) * tq, :]).astype(BF16)


def _win_attn(q, k, v, sink, batch, n, ctx_len, tq):
    w = WINDOW
    per = tq // w
    gq, gk = GQA_HEADS * HEAD_DIM, GQA_KV * HEAD_DIM
    prev = lambda b, i: (b, jnp.maximum(i * per - 1, 0), 0)
    cur = lambda b, i: (b, i, 0)
    nxt = lambda b, i: (b, jnp.minimum((i + 1) * per, n // w - 1), 0)
    ctx = lambda b, i: (batch, b, 0)
    kspec = lambda rows, im: pl.BlockSpec((None, rows, gk), im)
    return pl.pallas_call(
        functools.partial(_win_body, tq=tq),
        grid=(batch, n // tq),
        in_specs=[pl.BlockSpec(memory_space=pltpu.SMEM),
                  pl.BlockSpec((None, tq, gq), cur),
                  kspec(w, prev), kspec(tq, cur), kspec(w, nxt), kspec(ctx_len, ctx),
                  kspec(w, prev), kspec(tq, cur), kspec(w, nxt), kspec(ctx_len, ctx)],
        out_specs=pl.BlockSpec((None, tq, gq), cur),
        out_shape=jax.ShapeDtypeStruct((batch, n, gq), BF16),
        scratch_shapes=[pltpu.VMEM((GQA_HEADS // GQA_KV * tq, 2 * w + tq + ctx_len), F32)],
        compiler_params=_cparams("arbitrary", "arbitrary"),
        name="win_attn",
    )(sink, q, k, k, k, k, v, v, v, v)


def _route_body(h_ref, mod_ref, g_ref, r_ref, a_ref, info_ref, cnt_ref, rows_ref, carry_ref):
    first = (pl.program_id(0) == 0) & (pl.program_id(1) == 0)

    @pl.when(first)
    def _():
        carry_ref[...] = jnp.zeros_like(carry_ref)

    a = _norm_mod(h_ref[...], g_ref[...], mod_ref[3:4, :], mod_ref[4:5, :])
    a_ref[...] = a
    tm = a.shape[0]
    logits = jnp.dot(a, r_ref[...], precision=HIGHEST, preferred_element_type=F32)
    lane = lax.broadcasted_iota(jnp.int32, logits.shape, 1)
    neg = -jnp.inf
    logits = jnp.where(lane < N_EXPERTS, logits, neg)
    m1 = jnp.max(logits, axis=-1, keepdims=True)
    i1 = jnp.min(jnp.where(logits == m1, lane, LANES), axis=-1, keepdims=True)
    rest = jnp.where(lane == i1, neg, logits)
    m2 = jnp.max(rest, axis=-1, keepdims=True)
    i2 = jnp.min(jnp.where(rest == m2, lane, LANES), axis=-1, keepdims=True)
    e2 = jnp.exp(m2 - m1)
    w1 = 1.0 / (1.0 + e2)
    w2 = e2 / (1.0 + e2)
    chosen = ((lane == i1) | (lane == i2)).astype(F32)
    ri = lax.broadcasted_iota(jnp.int32, (tm, tm), 0)
    ci = lax.broadcasted_iota(jnp.int32, (tm, tm), 1)
    tri = (ci < ri).astype(BF16)
    before = _dot(tri, chosen.astype(BF16)) + carry_ref[...]
    p1 = jnp.sum(jnp.where(lane == i1, before, 0.0), axis=-1, keepdims=True)
    p2 = jnp.sum(jnp.where(lane == i2, before, 0.0), axis=-1, keepdims=True)
    carry_ref[...] += jnp.sum(chosen, axis=0, keepdims=True)
    cnt_ref[...] = carry_ref[...]
    vals = (i1.astype(F32), i2.astype(F32), w1, w2, p1, p2)
    info = jnp.zeros(logits.shape, F32)
    for j, v in enumerate(vals):
        info = jnp.where(lane == j, v, info)
    info_ref[...] = info
    src = (0, 1, 4, 5)
    sr = lax.broadcasted_iota(jnp.int32, (SUBLANES, LANES), 0)
    sl = lax.broadcasted_iota(jnp.int32, (SUBLANES, LANES), 1)
    sel = jnp.zeros((SUBLANES, LANES), F32)
    for r, c in enumerate(src):
        sel = jnp.where((sr == r) & (sl == c), 1.0, sel)
    rows_ref[...] = lax.dot_general(sel, info, (((1,), (1,)), ((), ())), precision=HIGHEST,
                                    preferred_element_type=F32)


def _route(h_lat, mod, g_pre, router, tm):
    batch, n, d = h_lat.shape
    rp = jnp.pad(router, ((0, 0), (0, LANES - N_EXPERTS)))
    nt = n // tm
    return pl.pallas_call(
        _route_body,
        grid=(batch, nt),
        in_specs=[pl.BlockSpec((None, tm, d), lambda g, t: (g, t, 0)),
                  pl.BlockSpec((None, 6, d), lambda g, t: (g, 0, 0)),
                  pl.BlockSpec((1, d), lambda g, t: (0, 0)),
                  pl.BlockSpec((d, LANES), lambda g, t: (0, 0))],
        out_specs=[pl.BlockSpec((tm, d), lambda g, t: (g * nt + t, 0)),
                   pl.BlockSpec((tm, LANES), lambda g, t: (g * nt + t, 0)),
                   pl.BlockSpec((1, LANES), lambda g, t: (0, 0)),
                   pl.BlockSpec((SUBLANES, tm), lambda g, t: (0, g * nt + t))],
        out_shape=[jax.ShapeDtypeStruct((batch * n, d), F32),
                   jax.ShapeDtypeStruct((batch * n, LANES), F32),
                   jax.ShapeDtypeStruct((1, LANES), F32),
                   jax.ShapeDtypeStruct((SUBLANES, batch * n), F32)],
        scratch_shapes=[pltpu.VMEM((1, LANES), F32)],
        compiler_params=_cparams("arbitrary", "arbitrary"),
        name="moe_route",
    )(h_lat, mod, g_pre, rp)


SUBLANES = 8


def _scatter_body(slot_ref, fill_ref, a_ref, xs_ref, zero_ref, sem, zsem, *, ts, n_tok, n_fill):
    base = pl.program_id(0) * ts

    @pl.when(pl.program_id(0) == 0)
    def _():
        zero_ref[...] = jnp.zeros_like(zero_ref)

        def fill(t):
            row = pl.multiple_of(fill_ref[t] * MOE_TILE, MOE_TILE)
            return pltpu.make_async_copy(zero_ref, xs_ref.at[pl.ds(row, MOE_TILE), :], zsem)

        for t in range(n_fill):
            @pl.when(fill_ref[t] >= 0)
            def _(t=t):
                fill(t).start()
        for t in range(n_fill):
            @pl.when(fill_ref[t] >= 0)
            def _(t=t):
                fill(t).wait()

    def issue(i, carry):
        for u in range(SUBLANES):
            for k in range(2):
                slot = slot_ref[k * n_tok + base + i * SUBLANES + u]
                pltpu.make_async_copy(a_ref.at[i, pl.ds(u, 1), :], xs_ref.at[pl.ds(slot, 1), :],
                                      sem).start(priority=k)
        return carry

    lax.fori_loop(0, ts // SUBLANES, issue, 0)
    for k in range(2):
        pltpu.make_async_copy(xs_ref.at[pl.ds(0, ts), :], xs_ref.at[pl.ds(0, ts), :], sem).wait()


def _scatter(slots, fill_tiles, a, n_slots, ts):
    n_tok, d = a.shape
    return pl.pallas_call(
        functools.partial(_scatter_body, ts=ts, n_tok=n_tok, n_fill=fill_tiles.shape[0]),
        grid_spec=pltpu.PrefetchScalarGridSpec(
            num_scalar_prefetch=2,
            grid=(n_tok // ts,),
            in_specs=[pl.BlockSpec((ts // SUBLANES, SUBLANES, d), lambda i, *_: (i, 0, 0))],
            out_specs=pl.BlockSpec(memory_space=pl.ANY),
            scratch_shapes=[pltpu.VMEM((MOE_TILE, d), F32), pltpu.SemaphoreType.DMA(()),
                            pltpu.SemaphoreType.DMA(())]),
        out_shape=jax.ShapeDtypeStruct((n_slots, d), F32),
        compiler_params=_cparams("arbitrary"),
        name="moe_scatter",
    )(slots, fill_tiles, a.reshape(n_tok // SUBLANES, SUBLANES, d))


def _experts_body(te_ref, tv_ref, x_ref, wg_ref, wu_ref, wd_ref, y_ref):
    del te_ref
    j = pl.program_id(0)
    f = pl.program_id(1)
    valid = tv_ref[j]

    @pl.when(f == 0)
    def _():
        y_ref[...] = jnp.zeros_like(y_ref)

    n_sub = (valid + MOE_SUB - 1) // MOE_SUB
    for ns in range(1, MOE_TILE // MOE_SUB + 1):
        rows = slice(0, ns * MOE_SUB)

        @pl.when(n_sub == ns)
        def _(rows=rows):
            x = x_ref[rows, :].astype(BF16)
            gate = _dot(x, wg_ref[...].astype(BF16))
            up = _dot(x, wu_ref[...].astype(BF16))
            hid = (_silu(gate) * up).astype(BF16)
            y_ref[rows, :] += _dot(hid, wd_ref[...].astype(BF16))


def _experts(tile_expert, tile_valid, xs, wg, wu, wd):
    n_slots, d = xs.shape
    ff = wg.shape[-1]
    n_tiles = n_slots // MOE_TILE
    tf = MOE_FCHUNK
    chunk = lambda j, f, tv: jnp.where(tv[j] > 0, f, ff // tf - 1)
    return pl.pallas_call(
        _experts_body,
        grid_spec=pltpu.PrefetchScalarGridSpec(
            num_scalar_prefetch=2,
            grid=(n_tiles, ff // tf),
            in_specs=[pl.BlockSpec((MOE_TILE, d), lambda j, f, te, tv: (j, 0)),
                      pl.BlockSpec((None, d, tf), lambda j, f, te, tv: (te[j], 0, chunk(j, f, tv))),
                      pl.BlockSpec((None, d, tf), lambda j, f, te, tv: (te[j], 0, chunk(j, f, tv))),
                      pl.BlockSpec((None, tf, d), lambda j, f, te, tv: (te[j], chunk(j, f, tv), 0))],
            out_specs=pl.BlockSpec((MOE_TILE, d), lambda j, f, te, tv: (j, 0))),
        out_shape=jax.ShapeDtypeStruct((n_slots, d), F32),
        compiler_params=_cparams("arbitrary", "arbitrary"),
        name="moe_experts",
    )(tile_expert, tile_valid, xs, wg, wu, wd)


def _combine_body(slot_ref, ys_ref, info_ref, h_ref, mod_ref, pg_ref, o_ref, buf_ref, sem, *, tc, n_tok):
    base = (pl.program_id(0) * pl.num_programs(1) + pl.program_id(1)) * tc

    def issue(i, carry):
        for u in range(SUBLANES):
            for k in range(2):
                slot = slot_ref[k * n_tok + base + i * SUBLANES + u]
                pltpu.make_async_copy(ys_ref.at[pl.ds(slot, 1), :], buf_ref.at[k, i, pl.ds(u, 1), :],
                                      sem).start(priority=k)
        return carry

    lax.fori_loop(0, tc // SUBLANES, issue, 0)
    for k in range(2):
        pltpu.make_async_copy(ys_ref.at[pl.ds(0, tc), :], ys_ref.at[pl.ds(0, tc), :], sem).wait()
    d = o_ref.shape[-1]
    f = info_ref[:, 2:3] * buf_ref[0].reshape(tc, d) + info_ref[:, 3:4] * buf_ref[1].reshape(tc, d)
    o_ref[...] = h_ref[...] + mod_ref[5:6, :] * _rms(f, pg_ref[...])


def _combine(slots, ys, info, h_lat, mod, post_g, tc):
    batch, n, d = h_lat.shape
    nt = n // tc
    return pl.pallas_call(
        functools.partial(_combine_body, tc=tc, n_tok=batch * n),
        grid_spec=pltpu.PrefetchScalarGridSpec(
            num_scalar_prefetch=1,
            grid=(batch, nt),
            in_specs=[pl.BlockSpec(memory_space=pl.ANY),
                      pl.BlockSpec((tc, LANES), lambda g, t, *_: (g * nt + t, 0)),
                      pl.BlockSpec((None, tc, d), lambda g, t, *_: (g, t, 0)),
                      pl.BlockSpec((None, 6, d), lambda g, t, *_: (g, 0, 0)),
                      pl.BlockSpec((1, d), lambda g, t, *_: (0, 0))],
            out_specs=pl.BlockSpec((None, tc, d), lambda g, t, *_: (g, t, 0)),
            scratch_shapes=[pltpu.VMEM((2, tc // SUBLANES, SUBLANES, d), F32), pltpu.SemaphoreType.DMA(())]),
        out_shape=jax.ShapeDtypeStruct((batch, n, d), F32),
        compiler_params=_cparams("arbitrary", "arbitrary"),
        name="moe_combine",
    )(slots, ys, info, h_lat, mod, post_g)


def _moe(h_lat, mod, g_pre, router, wg, wu, wd, post_g, tm):
    batch, n, d = h_lat.shape
    n_tok = batch * n
    a, info, counts, rt = _route(h_lat, mod, g_pre, router, tm)
    cnt = counts[0, :N_EXPERTS].astype(jnp.int32)
    padded = ((cnt + MOE_TILE - 1) // MOE_TILE) * MOE_TILE
    ends = jnp.cumsum(padded)
    starts = ends - padded
    seg = jnp.zeros((2, n_tok), jnp.int32)
    for e in range(N_EXPERTS):
        seg = jnp.where(rt[0:2] == e, starts[e], seg)
    slots = (seg + rt[2:4].astype(jnp.int32)).reshape(2 * n_tok)
    n_tiles = (2 * n_tok + N_EXPERTS * (MOE_TILE - 1)) // MOE_TILE
    tile_row = jnp.arange(n_tiles, dtype=jnp.int32) * MOE_TILE
    tile_expert = jnp.minimum(jnp.sum(tile_row[:, None] >= ends[None, :], axis=1), N_EXPERTS - 1).astype(jnp.int32)
    tile_valid = jnp.clip(cnt[tile_expert] - (tile_row - starts[tile_expert]), 0, MOE_TILE)
    tile_valid = jnp.where(tile_row < ends[-1], tile_valid, 0).astype(jnp.int32)
    last = jnp.where(padded > 0, ends // MOE_TILE - 1, -1)
    extra = ends[-1] // MOE_TILE + jnp.arange(n_tiles - 2 * n_tok // MOE_TILE, dtype=jnp.int32)
    fill_tiles = jnp.concatenate([last, jnp.where(extra < n_tiles, extra, -1)]).astype(jnp.int32)
    xs = _scatter(slots, fill_tiles, a, n_tiles * MOE_TILE, tm)
    ys = _experts(tile_expert, tile_valid, xs, wg, wu, wd)
    return _combine(slots, ys, info, h_lat, mod, post_g, tm)


def kernel(x, c, ctx, c_ctx, ada_w, ada_b, mix_pre_g, mix_post_g, ffn_pre_g, ffn_post_g, ev_w_in, ev_hy_conv_w, ev_hy_conv_b, ev_hy_filt_w1, ev_hy_filt_b1, ev_hy_filt_w2, ev_hy_filt_b2, ev_hy_filt_w3, ev_hy_freq, ev_hy_bias, ev_ssm_conv_w, ev_ssm_conv_b, ev_ssm_dt_bias, ev_ssm_a_log, ev_ssm_d, ev_ssm_norm_g, ev_w_out, ev_ffn_w_gate, ev_ffn_w_up, ev_ffn_w_down, od_w_in, od_mla_q_norm_g, od_mla_w_q_up, od_mla_kv_norm_g, od_mla_w_kv_up, od_gqa_sink, od_w_out, od_router, od_moe_w_gate, od_moe_w_up, od_moe_w_down):
    batch, n, d = x.shape
    ctx_len = ctx.shape[1]
    assert batch * ctx_len == n and ada_w.shape[0] == 2
    G = batch + 1
    tm = min(n, 512)

    cond = jnp.concatenate([c, c_ctx[None, :], jnp.zeros((16 - G, d), F32)], axis=0)
    mods = _ada(cond, ada_w, ada_b).reshape(2, 16, 6, d)
    ctx3 = ctx.reshape(1, n, d)

    n_main = 3 * HY_CH + SSM_INNER + SSM_CONV_CH
    w_in = ev_w_in[0]
    w_main = w_in[:, :n_main].astype(BF16)
    w_dt = jnp.pad(w_in[:, n_main:], ((0, 0), (0, LANES - 2 * SSM_HEADS))).astype(BF16)
    proj, dt = _even_in(x, ctx3, mods[0], mix_pre_g[0][None, :], w_main, w_dt, min(n, 1024))
    y_hy_l, y_hy_c = _hyena(proj, batch, n, ctx_len,
                            (ev_hy_conv_w[0], ev_hy_conv_b[0], ev_hy_filt_w1[0], ev_hy_filt_b1[0], ev_hy_filt_w2[0],
                             ev_hy_filt_b2[0], ev_hy_filt_w3[0], ev_hy_freq[0], ev_hy_bias[0]))
    y_ss_l, y_ss_c = _ssd(proj, dt, batch, n, ctx_len,
                          (ev_ssm_conv_w[0], ev_ssm_conv_b[0], ev_ssm_dt_bias[0], ev_ssm_a_log[0], ev_ssm_d[0]))
    h_all = _mix_out_even(y_hy_l, y_hy_c, y_ss_l, y_ss_c, proj, ev_ssm_norm_g[0][None, :],
                          ev_w_out[0].astype(BF16), x, ctx3, mods[0], mix_post_g[0][None, :], tm)
    h_all = _ffn(h_all, mods[0], ffn_pre_g[0][None, :], ev_ffn_w_gate[0].astype(BF16), ev_ffn_w_up[0].astype(BF16),
                 ev_ffn_w_down[0].astype(BF16), ffn_post_g[0][None, :], min(n, 1024))

    q_m, k_m, v_m, q_w, k_w, v_w = _odd_in(h_all, mods[1], mix_pre_g[1][None, :], od_w_in[0], od_mla_q_norm_g[0],
                                           od_mla_w_q_up[0], od_mla_kv_norm_g[0], od_mla_w_kv_up[0], tm)
    o_mla = _mla_attn(q_m, k_m, v_m, batch, n, ctx_len, min(n, 512), 4)
    o_win = _win_attn(q_w, k_w, v_w, od_gqa_sink[0], batch, n, ctx_len, min(n, 2 * WINDOW))
    h_lat = _mix_out_odd(o_mla, o_win, od_w_out[0].astype(BF16), h_all, mods[1], mix_post_g[1][None, :], tm)
    return _moe(h_lat, mods[1], ffn_pre_g[1][None, :], od_router[0], od_moe_w_gate[0], od_moe_w_up[0],
                od_moe_w_down[0], ffn_post_g[1][None, :], tm)
```

```python
import functools
import math

import jax
import jax.numpy as jnp
from jax import lax
from jax.experimental import pallas as pl
from jax.experimental.pallas import tpu as pltpu

F32 = jnp.float32
BF16 = jnp.bfloat16
HIGHEST = lax.Precision.HIGHEST

D_MODEL = 1024
GRID_W = 64
EPS = 1e-6
HEAD_DIM = 64
GROUP_WIDTH = D_MODEL // 2

HY_CH = GROUP_WIDTH
HY_BANDS = 16
HY_EMB = 2 * HY_BANDS + 1
HY_FILT_HID = 64
HY_FAST_DECAY = 0.3
HY_SLOW_DECAY = 1.5
HY_DECAY_TARGET = 1e-2

SSM_INNER = GROUP_WIDTH
SSM_HEADDIM = 64
SSM_HEADS = SSM_INNER // SSM_HEADDIM
SSM_GROUPS = 2
SSM_STATE = 128
SSM_CHUNK = 128
SSM_CONV_CH = SSM_INNER + 2 * SSM_GROUPS * SSM_STATE

MLA_HEADS = GROUP_WIDTH // HEAD_DIM
MLA_NOPE = 64
MLA_ROPE = 32
MLA_V = 64
MLA_Q_LORA = D_MODEL // 4
MLA_KV_LORA = D_MODEL // 8

GQA_HEADS = GROUP_WIDTH // HEAD_DIM
GQA_KV = GQA_HEADS // 4
WINDOW = 128
ROPE_BASE = 10000.0

D_FF = ((8 * D_MODEL // 3 + 127) // 128) * 128
N_EXPERTS = 8
D_FF_EXPERT = 7 * D_MODEL // 2

LANES = 128
SUBLANES = 8
DFT_SPLIT = 64
VMEM_LIMIT = 56 * 1024 * 1024
MOE_TILE = 1024
MOE_SUB = 256
MOE_FCHUNK = 512


def _cparams(*sem):
    return pltpu.CompilerParams(dimension_semantics=sem, vmem_limit_bytes=VMEM_LIMIT)


def _silu(x):
    return x / (1.0 + jnp.exp(-x))


def _rms(x, g):
    return x * lax.rsqrt(jnp.mean(x * x, axis=-1, keepdims=True) + EPS) * g


def _norm_mod(h, g, shift, scale):
    return _rms(h, g) * (1.0 + scale) + shift


def _dot(a, b):
    return jnp.dot(a, b, preferred_element_type=F32)


def _row_max(*blocks):
    tiles = [b[:, j:j + LANES] for b in blocks for j in range(0, b.shape[1], LANES)]
    m = tiles[0]
    for t in tiles[1:]:
        m = jnp.maximum(m, t)
    return jnp.max(m, axis=-1, keepdims=True)


def _dot_nt(a, b):
    return lax.dot_general(a, b, (((1,), (1,)), ((), ())), preferred_element_type=F32)


def _ada_body(c_ref, w_ref, b_ref, o_ref):
    s = _silu(c_ref[...]).astype(BF16)
    o_ref[0] = _dot(s, w_ref[0].astype(BF16)) + b_ref[0]


def _ada(cc, ada_w, ada_b):
    depth, d, n6 = ada_w.shape
    rows = cc.shape[0]
    tn = 1536
    return pl.pallas_call(
        _ada_body,
        grid=(depth, n6 // tn),
        in_specs=[pl.BlockSpec((rows, d), lambda l, j: (0, 0)),
                  pl.BlockSpec((1, d, tn), lambda l, j: (l, 0, j)),
                  pl.BlockSpec((1, 1, tn), lambda l, j: (l, 0, j))],
        out_specs=pl.BlockSpec((1, rows, tn), lambda l, j: (l, 0, j)),
        out_shape=jax.ShapeDtypeStruct((depth, rows, n6), F32),
        compiler_params=_cparams("arbitrary", "arbitrary"),
        name="ada",
    )(cc, ada_w, ada_b.reshape(depth, 1, n6))


def _lat_ctx_specs(batch, nt, tm, d):
    lat = pl.BlockSpec((None, tm, d), lambda g, t: (jnp.minimum(g, batch - 1), jnp.where(g < batch, t, nt - 1), 0))
    ctx = pl.BlockSpec((None, tm, d), lambda g, t: (0, jnp.where(g < batch, 0, t), 0))
    return lat, ctx


def _even_in_body(x_ref, c_ref, mod_ref, g_ref, w_ref, wdt_ref, o_ref, dt_ref, *, batch):
    h = jnp.where(pl.program_id(0) == batch, c_ref[...], x_ref[...])
    a = _norm_mod(h, g_ref[...], mod_ref[0:1, :], mod_ref[1:2, :]).astype(BF16)
    n_out = o_ref.shape[-1]
    for j in range(0, n_out, 512):
        o_ref[:, j:j + 512] = _dot(a, w_ref[:, j:j + 512]).astype(BF16)
    dt_ref[...] = _dot(a, wdt_ref[...])


def _even_in(x, ctx3, mod, g_pre, w_main, w_dt, tm):
    batch, n, d = x.shape
    G = batch + 1
    n_main = w_main.shape[1]
    lat_spec, ctx_spec = _lat_ctx_specs(batch, n // tm, tm, d)
    return pl.pallas_call(
        functools.partial(_even_in_body, batch=batch),
        grid=(G, n // tm),
        in_specs=[lat_spec, ctx_spec,
                  pl.BlockSpec((None, 6, d), lambda g, t: (g, 0, 0)),
                  pl.BlockSpec((1, d), lambda g, t: (0, 0)),
                  pl.BlockSpec((d, n_main), lambda g, t: (0, 0), pipeline_mode=pl.Buffered(1)),
                  pl.BlockSpec((d, LANES), lambda g, t: (0, 0))],
        out_specs=[pl.BlockSpec((None, tm, n_main), lambda g, t: (g, t, 0)),
                   pl.BlockSpec((None, tm, LANES), lambda g, t: (g, t, 0))],
        out_shape=[jax.ShapeDtypeStruct((G, n, n_main), BF16),
                   jax.ShapeDtypeStruct((G, n, LANES), F32)],
        compiler_params=_cparams("arbitrary", "arbitrary"),
        name="even_in",
    )(x, ctx3, mod, g_pre, w_main, w_dt)


def _dwconv3(x, w, b):
    L = x.shape[0]
    y = pltpu.roll(x, 1, axis=0) * w[0:1, :] + x * w[1:2, :] + pltpu.roll(x, L - 1, axis=0) * w[2:3, :] + b
    row = lax.broadcasted_iota(jnp.int32, (SUBLANES, x.shape[1]), 0)
    head = y[0:SUBLANES, :] - jnp.where(row == 0, x[L - 1:L, :] * w[0:1, :], 0.0)
    tail = y[L - SUBLANES:, :] - jnp.where(row == SUBLANES - 1, x[0:1, :] * w[2:3, :], 0.0)
    return jnp.concatenate([head, y[SUBLANES:L - SUBLANES, :], tail], axis=0)


def _hy_pre_body(p_ref, cw_ref, cb_ref, u_ref, x0_ref):
    c = HY_CH
    parts = []
    for k in range(3):
        x = p_ref[:, k * c:(k + 1) * c].astype(F32)
        parts.append(_dwconv3(x, cw_ref[:, k * c:(k + 1) * c], cb_ref[:, k * c:(k + 1) * c]))
    x0_ref[...] = parts[0].astype(BF16)
    u_ref[...] = (parts[1] * parts[2]).astype(BF16)


def _hy_pre(proj, nseq, L, index_map, cw, cb):
    c3 = 3 * HY_CH
    return pl.pallas_call(
        _hy_pre_body,
        grid=(nseq,),
        in_specs=[pl.BlockSpec((None, L, c3), index_map),
                  pl.BlockSpec((3, c3), lambda s: (0, 0)),
                  pl.BlockSpec((1, c3), lambda s: (0, 0))],
        out_specs=[pl.BlockSpec((None, L, HY_CH), lambda s: (s, 0, 0)),
                   pl.BlockSpec((None, L, HY_CH), lambda s: (s, 0, 0))],
        out_shape=[jax.ShapeDtypeStruct((nseq, L, HY_CH), BF16),
                   jax.ShapeDtypeStruct((nseq, L, HY_CH), BF16)],
        compiler_params=_cparams("arbitrary"),
        name="hy_pre",
    )(proj, cw, cb)


def _hy_filter_body(z_ref, t_ref, w1_ref, b1_ref, w2_ref, b2_ref, w3_ref, f_ref, dl_ref, o_ref):
    f = f_ref[...]
    h = jnp.sin(f * (jnp.dot(z_ref[...], w1_ref[...], precision=HIGHEST, preferred_element_type=F32) + b1_ref[...]))
    h = jnp.sin(f * (jnp.dot(h, w2_ref[...], precision=HIGHEST, preferred_element_type=F32) + b2_ref[...]))
    h = jnp.dot(h, w3_ref[...], precision=HIGHEST, preferred_element_type=F32)
    decay = jnp.exp(-t_ref[...] * dl_ref[...])
    hf = h[:, :HY_CH] * decay
    hb = h[:, HY_CH:] * decay
    row = lax.broadcasted_iota(jnp.int32, hb.shape, 0)
    hb = jnp.where(row == 0, 0.0, hb)
    o_ref[:, :HY_CH] = hf.astype(BF16)
    o_ref[:, HY_CH:] = hb.astype(BF16)


def _hy_filter(L, w1, b1, w2, b2, w3, freq):
    pos = jnp.arange(L, dtype=F32)
    t = pos / (L - 1)
    omega = 2.0 * math.pi * pos / L
    bands = jnp.linspace(1e-4, HY_BANDS - 1, HY_BANDS, dtype=F32)
    z = jnp.concatenate([t[:, None], jnp.cos(omega[:, None] * bands), -jnp.sin(omega[:, None] * bands)], axis=-1)
    z = jnp.pad(z, ((0, 0), (0, LANES - HY_EMB)))
    w1p = jnp.pad(w1, ((0, LANES - HY_EMB), (0, 0)))
    deltas = jnp.abs(jnp.linspace(math.log(HY_DECAY_TARGET) / HY_SLOW_DECAY,
                                  math.log(HY_DECAY_TARGET) / HY_FAST_DECAY, HY_CH, dtype=F32))
    return pl.pallas_call(
        _hy_filter_body,
        out_shape=jax.ShapeDtypeStruct((L, 2 * HY_CH), BF16),
        compiler_params=pltpu.CompilerParams(vmem_limit_bytes=VMEM_LIMIT),
        name="hy_filter",
    )(z, t[:, None], w1p, b1[None, :], w2, b2[None, :], w3, freq[None, :], deltas[None, :])


def _dft_tables(L):
    split = DFT_SPLIT
    t = jnp.arange(L, dtype=jnp.int32)[None, :]
    k1 = jnp.arange(L // split, dtype=jnp.int32)[:, None] * split
    k0 = jnp.arange(split, dtype=jnp.int32)[:, None]
    ang1 = ((k1 * t) % (2 * L)).astype(F32) * (math.pi / L)
    ang0 = ((k0 * t) % (2 * L)).astype(F32) * (math.pi / L)
    factors = jnp.stack([jnp.cos(ang0), jnp.sin(ang0)])
    coarse = jnp.stack([jnp.cos(ang1), jnp.sin(ang1)])
    alt = jnp.where(jnp.arange(L, dtype=jnp.int32) % 2 == 0, 1.0, -1.0).astype(F32)
    groups = min(L // split, 8)
    tr = groups * split
    return pl.pallas_call(
        functools.partial(_dft_tables_body, groups=groups),
        grid=(L // tr,),
        in_specs=[pl.BlockSpec((2, split, L), lambda i: (0, 0, 0)),
                  pl.BlockSpec((2, groups, L), lambda i: (0, i, 0)),
                  pl.BlockSpec((1, L), lambda i: (0, 0)),
                  pl.BlockSpec((tr, 1), lambda i: (i, 0))],
        out_specs=[pl.BlockSpec((2, tr, L), lambda i: (0, i, 0)),
                   pl.BlockSpec((tr, L), lambda i: (i, 0))],
        out_shape=[jax.ShapeDtypeStruct((2, L, L), BF16), jax.ShapeDtypeStruct((L, L), BF16)],
        compiler_params=_cparams("arbitrary"),
        name="dft_tables",
    )(factors, coarse, alt[None, :], alt[:, None])


def _dft_tables_body(f_ref, c_ref, altt_ref, altk_ref, f2_ref, ft_ref, *, groups):
    split = f_ref.shape[1]
    c0, s0 = f_ref[0], f_ref[1]
    first = pl.program_id(0) == 0
    row = lax.broadcasted_iota(jnp.int32, (split, LANES), 0)
    lane = lax.broadcasted_iota(jnp.int32, (split, LANES), 1)
    for j in range(groups):
        rows = slice(j * split, (j + 1) * split)
        c1, s1 = c_ref[0, j:j + 1, :], c_ref[1, j:j + 1, :]
        f2_ref[0, rows, :] = (c1 * c0 - s1 * s0).astype(BF16)
        nsin = -(s1 * c0 + c1 * s0)
        ft_ref[rows, 0:LANES] = jnp.where(lane == 0, altk_ref[rows, :], nsin[:, 0:LANES]).astype(BF16)
        ft_ref[rows, LANES:] = nsin[:, LANES:].astype(BF16)
        if j == 0:
            is_dc = first & (row[:, 0:1] == 0)
            nsin = jnp.where(is_dc, altt_ref[...], nsin)
        f2_ref[1, rows, :] = nsin.astype(BF16)


def _filt_dft_body(f_ref, x_ref, o_ref, *, L, tr):
    c = HY_CH
    x = x_ref[...]
    sre = _dot(f_ref[0], x)
    sim = _dot(f_ref[1], x)
    is0 = (lax.broadcasted_iota(jnp.int32, (tr, c), 0) + pl.program_id(0) * tr) == 0
    wgt = jnp.where(is0, 1.0 / (2 * L), 1.0 / L)
    hre = sre[:, :c] + sre[:, c:]
    o_ref[0] = (hre * wgt).astype(BF16)
    o_ref[1] = (jnp.where(is0, 0.0, sim[:, :c] - sim[:, c:]) * wgt).astype(BF16)
    o_ref[2] = (jnp.where(is0, sim[:, :c] + sim[:, c:], hre) * wgt).astype(BF16)


def _filt_dft(f2, hfb):
    _, L, _ = f2.shape
    cols = hfb.shape[1]
    tr = min(L, 512)
    return pl.pallas_call(
        functools.partial(_filt_dft_body, L=L, tr=tr),
        grid=(L // tr,),
        in_specs=[pl.BlockSpec((2, tr, L), lambda i: (0, i, 0)),
                  pl.BlockSpec((L, cols), lambda i: (0, 0))],
        out_specs=pl.BlockSpec((3, tr, cols // 2), lambda i: (0, i, 0)),
        out_shape=jax.ShapeDtypeStruct((3, L, cols // 2), BF16),
        compiler_params=_cparams("arbitrary"),
        name="filt_dft",
    )(f2, hfb)


def _hy_conv_body(u_ref, x0_ref, f2_ref, fimt_ref, s_ref, bias_ref, o_ref, *, L, fc):
    u = u_ref[...]
    acc = None
    for k in range(L // fc):
        ks = slice(k * fc, (k + 1) * fc)
        ure = _dot(f2_ref[0, ks, :], u)
        uim = _dot(f2_ref[1, ks, :], u)
        a_m = s_ref[0, ks, :].astype(F32)
        b_m = s_ref[1, ks, :].astype(F32)
        d_m = s_ref[2, ks, :].astype(F32)
        yre = (ure * a_m - uim * b_m).astype(BF16)
        yim = (ure * b_m + uim * d_m).astype(BF16)
        part = _dot(f2_ref[0, :, ks], yre) + _dot(fimt_ref[:, ks], yim)
        acc = part if acc is None else acc + part
    y = acc + u.astype(F32) * bias_ref[...]
    o_ref[...] = (x0_ref[...].astype(F32) * y).astype(BF16)


def _hy_conv(u, x0, f2, fimt, s, bias):
    nseq, L, c = u.shape
    fc = min(L, 512)
    resident = lambda shape: pl.BlockSpec(shape, lambda s_: (0,) * len(shape), pipeline_mode=pl.Buffered(1))
    return pl.pallas_call(
        functools.partial(_hy_conv_body, L=L, fc=fc),
        grid=(nseq,),
        in_specs=[pl.BlockSpec((None, L, c), lambda s_: (s_, 0, 0)),
                  pl.BlockSpec((None, L, c), lambda s_: (s_, 0, 0)),
                  resident((2, L, L)), resident((L, L)), resident((3, L, c)),
                  pl.BlockSpec((1, c), lambda s_: (0, 0))],
        out_specs=pl.BlockSpec((None, L, c), lambda s_: (s_, 0, 0)),
        out_shape=jax.ShapeDtypeStruct((nseq, L, c), BF16),
        compiler_params=_cparams("arbitrary"),
        name="hy_conv",
    )(u, x0, f2, fimt, s, bias)


def _hyena(proj, batch, n, ctx_len, p):
    cw, cb, w1, b1, w2, b2, w3, freq, bias = p
    outs = []
    for L, imap in ((n, lambda s: (s, 0, 0)), (ctx_len, lambda s: (batch, s, 0))):
        u, x0 = _hy_pre(proj, batch, L, imap, cw, cb[None, :])
        hfb = _hy_filter(L, w1, b1, w2, b2, w3, freq)
        f2, fimt = _dft_tables(L)
        s = _filt_dft(f2, hfb)
        outs.append(_hy_conv(u, x0, f2, fimt, s, bias[None, :]))
    return outs


def _softplus(x):
    return jnp.maximum(x, 0.0) + jnp.log(1.0 + jnp.exp(-jnp.abs(x)))


def _chunk_cumsum(a, axis, reverse):
    q = SSM_CHUNK
    size = a.shape[axis]
    pos = lax.broadcasted_iota(jnp.int32, a.shape, axis) % q
    s = 1
    while s < q:
        if reverse:
            a = a + jnp.where(pos < q - s, pltpu.roll(a, size - s, axis=axis), 0.0)
        else:
            a = a + jnp.where(pos >= s, pltpu.roll(a, s, axis=axis), 0.0)
        s *= 2
    return a


def _ssd_prepare(L, xbc_ref, dtc_ref, dtr_ref, cw_ref, cb_ref, dtbc_ref, alc_ref, dtbr_ref, alr_ref,
                 x_s, c_s, bt_s, csc_s, csr_s, dtr_s):
    nc = L // SSM_CHUNK
    nh = SSM_HEADS
    inner = SSM_INNER
    gs = SSM_GROUPS * SSM_STATE
    xs = _silu(_dwconv3(xbc_ref[:, :inner].astype(F32), cw_ref[:, :inner], cb_ref[:, :inner]))
    x_s[0:L, :] = xs.astype(BF16)
    bm = _silu(_dwconv3(xbc_ref[:, inner:inner + gs].astype(F32), cw_ref[:, inner:inner + gs],
                        cb_ref[:, inner:inner + gs]))
    for c in range(nc):
        bt_s[c] = bm[c * SSM_CHUNK:(c + 1) * SSM_CHUNK, :].T.astype(BF16)
    cm = _silu(_dwconv3(xbc_ref[:, inner + gs:].astype(F32), cw_ref[:, inner + gs:], cb_ref[:, inner + gs:]))
    c_s[0:L, :] = cm.astype(BF16)
    a_col = _softplus(dtc_ref[...] + dtbc_ref[...]) * (-jnp.exp(alc_ref[...]))
    lane = lax.broadcasted_iota(jnp.int32, a_col.shape, 1)
    csc_s[0:L, :] = jnp.where(lane < nh, _chunk_cumsum(a_col, 0, False), _chunk_cumsum(a_col, 0, True))
    dt_row = _softplus(dtr_ref[...] + dtbr_ref[...])
    a_row = (dt_row * (-jnp.exp(alr_ref[...]))).reshape(2 * nh * nc, SSM_CHUNK)
    rown = lax.broadcasted_iota(jnp.int32, a_row.shape, 0)
    csr_s[0:2 * nh * nc, :] = jnp.where(rown < nh * nc, _chunk_cumsum(a_row, 1, False),
                                        _chunk_cumsum(a_row, 1, True))
    dtr_s[0:2 * nh * nc, :] = dt_row.reshape(2 * nh * nc, SSM_CHUNK)


def _ssd_chunk(c, d, nc, x_s, c_s, bt_s, csc_s, csr_s, dtr_s, st_s, y_s):
    q = SSM_CHUNK
    ns = SSM_STATE
    hp = SSM_HEADDIM
    hpg = SSM_HEADS // SSM_GROUPS
    gw = hpg * hp
    r0 = pl.multiple_of(c * q, q)
    ri = lax.broadcasted_iota(jnp.int32, (q, q), 0)
    ci = lax.broadcasted_iota(jnp.int32, (q, q), 1)
    keep = (ci <= ri) if d == 0 else (ci >= ri)
    lane_blk = lax.broadcasted_iota(jnp.int32, (q, gw), 1) // hp
    cc = c_s[pl.ds(r0, q), :]
    xc = x_s[pl.ds(r0, q), :]
    btc = bt_s[c]
    ys = []
    for g in range(SSM_GROUPS):
        cg = cc[:, g * ns:(g + 1) * ns]
        btg = btc[g * ns:(g + 1) * ns, :]
        xg = xc[:, g * gw:(g + 1) * gw]
        s_cb = _dot(cg, btg)
        btg32 = btg.astype(F32)
        m_rows, w_rows, e_cols, tots = [], [], [], []
        for hh in range(hpg):
            idx = d * SSM_HEADS + g * hpg + hh
            csc = csc_s[pl.ds(r0, q), idx:idx + 1]
            csr = csr_s[pl.ds(idx * nc + c, 1), :]
            dtr = dtr_s[pl.ds(idx * nc + c, 1), :]
            lmat = jnp.where(keep, jnp.exp(csc - csr), 0.0)
            m_rows.append((s_cb * lmat * dtr).astype(BF16))
            tot = csr[:, q - 1:q] if d == 0 else csr[:, 0:1]
            w_rows.append((btg32 * (jnp.exp(tot - csr) * dtr)).astype(BF16))
            e_cols.append(jnp.exp(csc))
            tots.append(jnp.exp(tot))
        yd = _dot(jnp.concatenate(m_rows, axis=0), xg)
        ds = _dot(jnp.concatenate(w_rows, axis=0), xg)
        st = st_s[d, g]
        yo = _dot(cg, st.astype(BF16))
        y_g = jnp.zeros((q, gw), F32)
        st_new = jnp.zeros((ns, gw), F32)
        for hh in range(hpg):
            sel = lane_blk == hh
            y_g = jnp.where(sel, yd[hh * q:(hh + 1) * q, :] + yo * e_cols[hh], y_g)
            st_new = jnp.where(sel, st * tots[hh] + ds[hh * ns:(hh + 1) * ns, :], st_new)
        st_s[d, g] = st_new
        ys.append(y_g)
    y_s[pl.ds(r0, q), :] += jnp.concatenate(ys, axis=1)


def _ssd_body(xl_ref, xc_ref, dcl_ref, dcc_ref, drl_ref, drc_ref, cw_ref, cb_ref, dtbc_ref, alc_ref,
              dtbr_ref, alr_ref, dsk_ref, yl_ref, yc_ref,
              x_s, c_s, bt_s, csc_s, csr_s, dtr_s, st_s, y_s, *, n, ctx_len):
    st_s[...] = jnp.zeros_like(st_s)
    for L, xbc_ref, dtc_ref, dtr_ref, o_ref in ((ctx_len, xc_ref, dcc_ref, drc_ref, yc_ref),
                                                 (n, xl_ref, dcl_ref, drl_ref, yl_ref)):
        nc = L // SSM_CHUNK
        _ssd_prepare(L, xbc_ref, dtc_ref, dtr_ref, cw_ref, cb_ref, dtbc_ref, alc_ref, dtbr_ref, alr_ref,
                     x_s, c_s, bt_s, csc_s, csr_s, dtr_s)
        args = (x_s, c_s, bt_s, csc_s, csr_s, dtr_s, st_s, y_s)

        y_s[0:L, :] = jnp.zeros((L, SSM_INNER), F32)

        def both(i, carry, nc=nc, args=args):
            _ssd_chunk(i, 0, nc, *args)
            _ssd_chunk(nc - 1 - i, 1, nc, *args)
            return carry

        lax.fori_loop(0, nc, both, 0)
        o_ref[...] = (y_s[0:L, :] + x_s[0:L, :].astype(F32) * dsk_ref[...]).astype(BF16)


def _ssd(proj, dt, batch, n, ctx_len, p):
    conv_w, conv_b, dt_bias, a_log, d_skip = p
    nh = SSM_HEADS
    ncl, ncc = n // SSM_CHUNK, ctx_len // SSM_CHUNK
    cch = SSM_CONV_CH
    xbc_blk = (SSM_INNER * 3 + SSM_INNER) // cch
    dt16 = dt[:, :, :2 * nh]
    dtr_l = dt16[:batch].transpose(0, 2, 1).reshape(batch, 2 * nh, ncl, SSM_CHUNK)
    dtr_c = dt16[batch].reshape(batch, ctx_len, 2 * nh).transpose(0, 2, 1).reshape(batch, 2 * nh, ncc, SSM_CHUNK)
    pad = LANES - 2 * nh
    dtb_col = jnp.pad(dt_bias.reshape(1, 2 * nh), ((0, 0), (0, pad)))
    al_col = jnp.pad(a_log.reshape(1, 2 * nh), ((0, 0), (0, pad)))
    dtb_row = jnp.broadcast_to(dt_bias.reshape(2 * nh, 1, 1), (2 * nh, 1, SSM_CHUNK))
    al_row = jnp.broadcast_to(a_log.reshape(2 * nh, 1, 1), (2 * nh, 1, SSM_CHUNK))
    dsk = jnp.repeat(d_skip, SSM_HEADDIM)[None, :]
    full = lambda shape: pl.BlockSpec(shape, lambda b: (0,) * len(shape))
    return pl.pallas_call(
        functools.partial(_ssd_body, n=n, ctx_len=ctx_len),
        grid=(batch,),
        in_specs=[pl.BlockSpec((None, n, cch), lambda b: (b, 0, xbc_blk)),
                  pl.BlockSpec((None, ctx_len, cch), lambda b: (batch, b, xbc_blk)),
                  pl.BlockSpec((None, n, LANES), lambda b: (b, 0, 0)),
                  pl.BlockSpec((None, ctx_len, LANES), lambda b: (batch, b, 0)),
                  pl.BlockSpec((None, 2 * nh, ncl, SSM_CHUNK), lambda b: (b, 0, 0, 0)),
                  pl.BlockSpec((None, 2 * nh, ncc, SSM_CHUNK), lambda b: (b, 0, 0, 0)),
                  full((3, cch)), full((1, cch)), full((1, LANES)), full((1, LANES)),
                  full((2 * nh, 1, SSM_CHUNK)), full((2 * nh, 1, SSM_CHUNK)), full((1, SSM_INNER))],
        out_specs=[pl.BlockSpec((None, n, SSM_INNER), lambda b: (b, 0, 0)),
                   pl.BlockSpec((None, ctx_len, SSM_INNER), lambda b: (b, 0, 0))],
        out_shape=[jax.ShapeDtypeStruct((batch, n, SSM_INNER), BF16),
                   jax.ShapeDtypeStruct((batch, ctx_len, SSM_INNER), BF16)],
        scratch_shapes=[pltpu.VMEM((n, SSM_INNER), BF16),
                        pltpu.VMEM((n, SSM_GROUPS * SSM_STATE), BF16),
                        pltpu.VMEM((ncl, SSM_GROUPS * SSM_STATE, SSM_CHUNK), BF16),
                        pltpu.VMEM((n, LANES), F32),
                        pltpu.VMEM((2 * nh * ncl, SSM_CHUNK), F32),
                        pltpu.VMEM((2 * nh * ncl, SSM_CHUNK), F32),
                        pltpu.VMEM((2, SSM_GROUPS, SSM_STATE, SSM_INNER // SSM_GROUPS), F32),
                        pltpu.VMEM((n, SSM_INNER), F32)],
        compiler_params=_cparams("arbitrary"),
        name="ssd",
    )(proj, proj, dt, dt, dtr_l, dtr_c, conv_w, conv_b[None, :], dtb_col, al_col, dtb_row, al_row, dsk)


def _mix_out_even_body(yhl_ref, yhc_ref, ysl_ref, ysc_ref, z_ref, ng_ref, w_ref, x_ref, c_ref, mod_ref, pg_ref, o_ref,
                       *, batch):
    is_ctx = pl.program_id(0) == batch
    h = jnp.where(is_ctx, c_ref[...], x_ref[...])
    y_hy = jnp.where(is_ctx, yhc_ref[...], yhl_ref[...])
    ys = jnp.where(is_ctx, ysc_ref[...], ysl_ref[...]).astype(F32) * _silu(z_ref[...].astype(F32))
    gw = SSM_INNER // SSM_GROUPS
    parts = []
    for g in range(SSM_GROUPS):
        yg = ys[:, g * gw:(g + 1) * gw]
        parts.append(yg * lax.rsqrt(jnp.mean(yg * yg, axis=-1, keepdims=True) + EPS))
    y_ssm = (jnp.concatenate(parts, axis=1) * ng_ref[...]).astype(BF16)
    m = _dot(y_hy, w_ref[0:HY_CH, :]) + _dot(y_ssm, w_ref[HY_CH:, :])
    o_ref[...] = h + mod_ref[2:3, :] * _rms(m, pg_ref[...])


def _mix_out_even(y_hy_l, y_hy_c, y_ss_l, y_ss_c, proj, norm_g, w_out, x, ctx3, mod, post_g, tm):
    batch, n, d = x.shape
    G = batch + 1
    c = HY_CH
    z_blk = (3 * HY_CH) // SSM_INNER
    lat_c, ctx_c = _lat_ctx_specs(batch, n // tm, tm, c)
    lat_d, ctx_d = _lat_ctx_specs(batch, n // tm, tm, d)
    return pl.pallas_call(
        functools.partial(_mix_out_even_body, batch=batch),
        grid=(G, n // tm),
        in_specs=[lat_c, ctx_c, lat_c, ctx_c,
                  pl.BlockSpec((None, tm, SSM_INNER), lambda g, t: (g, t, z_blk)),
                  pl.BlockSpec((1, SSM_INNER), lambda g, t: (0, 0)),
                  pl.BlockSpec((2 * c, d), lambda g, t: (0, 0)),
                  lat_d, ctx_d,
                  pl.BlockSpec((None, 6, d), lambda g, t: (g, 0, 0)),
                  pl.BlockSpec((1, d), lambda g, t: (0, 0))],
        out_specs=pl.BlockSpec((None, tm, d), lambda g, t: (g, t, 0)),
        out_shape=jax.ShapeDtypeStruct((G, n, d), F32),
        compiler_params=_cparams("arbitrary", "arbitrary"),
        name="mix_out_even",
    )(y_hy_l, y_hy_c.reshape(1, n, c), y_ss_l, y_ss_c.reshape(1, n, c), proj, norm_g, w_out, x, ctx3, mod, post_g)


def _mix_out_odd_body(a_ref, b_ref, w_ref, h_ref, mod_ref, pg_ref, o_ref):
    half = a_ref.shape[-1]
    m = _dot(a_ref[...], w_ref[0:half, :]) + _dot(b_ref[...], w_ref[half:, :])
    o_ref[...] = h_ref[...] + mod_ref[2:3, :] * _rms(m, pg_ref[...])


def _mix_out_odd(o_a, o_b, w_out, h_all, mod, post_g, tm):
    batch, n, c = o_a.shape
    d = h_all.shape[-1]
    return pl.pallas_call(
        _mix_out_odd_body,
        grid=(batch, n // tm),
        in_specs=[pl.BlockSpec((None, tm, c), lambda g, t: (g, t, 0)),
                  pl.BlockSpec((None, tm, c), lambda g, t: (g, t, 0)),
                  pl.BlockSpec((2 * c, d), lambda g, t: (0, 0)),
                  pl.BlockSpec((None, tm, d), lambda g, t: (g, t, 0)),
                  pl.BlockSpec((None, 6, d), lambda g, t: (g, 0, 0)),
                  pl.BlockSpec((1, d), lambda g, t: (0, 0))],
        out_specs=pl.BlockSpec((None, tm, d), lambda g, t: (g, t, 0)),
        out_shape=jax.ShapeDtypeStruct((batch, n, d), F32),
        compiler_params=_cparams("arbitrary", "arbitrary"),
        name="mix_out_odd",
    )(o_a, o_b, w_out, h_all, mod, post_g)


def _ffn_body(h_ref, mod_ref, g_ref, wg_ref, wu_ref, wd_ref, pg_ref, o_ref, *, fchunk):
    h = h_ref[...]
    a = _norm_mod(h, g_ref[...], mod_ref[3:4, :], mod_ref[4:5, :]).astype(BF16)
    ff = wg_ref.shape[1]
    f = jnp.zeros(h.shape, F32)
    for j in range(0, ff, fchunk):
        gate = _dot(a, wg_ref[:, j:j + fchunk])
        up = _dot(a, wu_ref[:, j:j + fchunk])
        f = f + _dot((_silu(gate) * up).astype(BF16), wd_ref[j:j + fchunk, :])
    o_ref[...] = h + mod_ref[5:6, :] * _rms(f, pg_ref[...])


def _ffn(h_all, mod, g_pre, wg, wu, wd, post_g, tm):
    G, n, d = h_all.shape
    ff = wg.shape[1]
    return pl.pallas_call(
        functools.partial(_ffn_body, fchunk=256),
        grid=(G, n // tm),
        in_specs=[pl.BlockSpec((None, tm, d), lambda g, t: (g, t, 0)),
                  pl.BlockSpec((None, 6, d), lambda g, t: (g, 0, 0)),
                  pl.BlockSpec((1, d), lambda g, t: (0, 0)),
                  pl.BlockSpec((d, ff), lambda g, t: (0, 0), pipeline_mode=pl.Buffered(1)),
                  pl.BlockSpec((d, ff), lambda g, t: (0, 0), pipeline_mode=pl.Buffered(1)),
                  pl.BlockSpec((ff, d), lambda g, t: (0, 0), pipeline_mode=pl.Buffered(1)),
                  pl.BlockSpec((1, d), lambda g, t: (0, 0))],
        out_specs=pl.BlockSpec((None, tm, d), lambda g, t: (g, t, 0)),
        out_shape=jax.ShapeDtypeStruct((G, n, d), F32),
        compiler_params=_cparams("arbitrary", "arbitrary"),
        name="ffn",
    )(h_all, mod, g_pre, wg, wu, wd, post_g)


def _rope(x, cos, sin_a, sin_b, half):
    return x * cos + pltpu.roll(x, LANES - half, axis=1) * sin_a + pltpu.roll(x, half, axis=1) * sin_b


def _odd_in_body(h_ref, mod_ref, g_ref, w_ref, qg_ref, wq_ref, kg_ref, wk_ref, wv_ref, rp_ref,
                 qm_ref, km_ref, vm_ref, qw_ref, kw_ref, vw_ref):
    a = _norm_mod(h_ref[...], g_ref[...], mod_ref[0:1, :], mod_ref[1:2, :]).astype(BF16)
    ql, kl = MLA_Q_LORA, MLA_KV_LORA
    o_qg = ql + kl
    o_kg = o_qg + GQA_HEADS * HEAD_DIM
    o_vg = o_kg + GQA_KV * HEAD_DIM
    o_kr = o_vg + GQA_KV * HEAD_DIM
    cm, s1m, s2m = rp_ref[:, 0:LANES], rp_ref[:, LANES:2 * LANES], rp_ref[:, 2 * LANES:3 * LANES]
    cg, s1g, s2g = rp_ref[:, 3 * LANES:4 * LANES], rp_ref[:, 4 * LANES:5 * LANES], rp_ref[:, 5 * LANES:6 * LANES]
    mla_scale = (MLA_NOPE + MLA_ROPE) ** -0.5
    gqa_scale = HEAD_DIM ** -0.5
    q_lat = _rms(_dot(a, w_ref[:, 0:ql]), qg_ref[...]).astype(BF16)
    q = _dot(q_lat, wq_ref[...])
    kv_lat = _rms(_dot(a, w_ref[:, ql:o_qg]), kg_ref[...]).astype(BF16)
    k = _dot(kv_lat, wk_ref[...])
    vm_ref[...] = _dot(kv_lat, wv_ref[...]).astype(BF16)
    k_rope = _rope(_dot(a, w_ref[:, o_kr:o_kr + LANES]), cm, s1m, s2m, MLA_ROPE // 2)
    for hd in range(MLA_HEADS):
        sl = slice(hd * LANES, (hd + 1) * LANES)
        qm_ref[:, sl] = (_rope(q[:, sl], cm, s1m, s2m, MLA_ROPE // 2) * mla_scale).astype(BF16)
        km_ref[:, sl] = (k[:, sl] + k_rope).astype(BF16)
    qg = _dot(a, w_ref[:, o_qg:o_kg])
    for blk in range(GQA_HEADS * HEAD_DIM // LANES):
        sl = slice(blk * LANES, (blk + 1) * LANES)
        qw_ref[:, sl] = (_rope(qg[:, sl], cg, s1g, s2g, HEAD_DIM // 2) * gqa_scale).astype(BF16)
    kw_ref[...] = _rope(_dot(a, w_ref[:, o_kg:o_vg]), cg, s1g, s2g, HEAD_DIM // 2).astype(BF16)
    vw_ref[...] = _dot(a, w_ref[:, o_vg:o_kr]).astype(BF16)


def _rope_tables(n):
    rows = (jnp.arange(n) // GRID_W).astype(F32)
    cols = (jnp.arange(n) % GRID_W).astype(F32)

    def cs(rot_dim):
        nf = rot_dim // 4
        inv = ROPE_BASE ** (-jnp.arange(nf, dtype=F32) / nf)
        ang = jnp.concatenate([rows[:, None] * inv, cols[:, None] * inv], axis=-1)
        return jnp.cos(ang), jnp.sin(ang)

    one = jnp.ones((n, 1), F32)
    zero = jnp.zeros((n, 1), F32)
    c, s = cs(MLA_ROPE)
    hm = MLA_ROPE // 2
    pad_n, pad_t = MLA_NOPE, LANES - MLA_NOPE - MLA_ROPE
    cm = jnp.concatenate([jnp.tile(one, (1, pad_n)), c, c, jnp.tile(one, (1, pad_t))], axis=1)
    s1m = jnp.concatenate([jnp.tile(zero, (1, pad_n)), -s, jnp.tile(zero, (1, hm + pad_t))], axis=1)
    s2m = jnp.concatenate([jnp.tile(zero, (1, pad_n + hm)), s, jnp.tile(zero, (1, pad_t))], axis=1)
    c, s = cs(HEAD_DIM)
    z32 = jnp.zeros_like(s)
    cg = jnp.concatenate([c, c, c, c], axis=1)
    s1g = jnp.concatenate([-s, z32, -s, z32], axis=1)
    s2g = jnp.concatenate([z32, s, z32, s], axis=1)
    lat = jnp.concatenate([cm, s1m, s2m, cg, s1g, s2g], axis=1)
    ident = jnp.concatenate([jnp.ones((n, LANES), F32), jnp.zeros((n, 2 * LANES), F32)] * 2, axis=1)
    return jnp.stack([lat, ident])


def _odd_in(h_all, mod, g_pre, w_in, q_norm_g, w_q_up, kv_norm_g, w_kv_up, tm):
    G, n, d = h_all.shape
    batch = G - 1
    ql, kl, rp = MLA_Q_LORA, MLA_KV_LORA, MLA_ROPE
    gq, gk = GQA_HEADS * HEAD_DIM, GQA_KV * HEAD_DIM
    o = [0, ql, ql + kl, ql + kl + rp, ql + kl + rp + gq, ql + kl + rp + gq + gk]
    kr_cols = jnp.pad(w_in[:, o[2]:o[3]], ((0, 0), (MLA_NOPE, LANES - MLA_NOPE - rp)))
    w = jnp.concatenate([w_in[:, o[0]:o[2]], w_in[:, o[3]:], kr_cols], axis=1).astype(BF16)
    hq = MLA_NOPE + MLA_ROPE
    wq = jnp.pad(w_q_up.reshape(ql, MLA_HEADS, hq), ((0, 0), (0, 0), (0, LANES - hq)))
    wq = wq.reshape(ql, MLA_HEADS * LANES).astype(BF16)
    wkv = w_kv_up.reshape(kl, MLA_HEADS, MLA_NOPE + MLA_V)
    wk = jnp.pad(wkv[:, :, :MLA_NOPE], ((0, 0), (0, 0), (0, LANES - MLA_NOPE)))
    wk = wk.reshape(kl, MLA_HEADS * LANES).astype(BF16)
    wv = wkv[:, :, MLA_NOPE:].reshape(kl, MLA_HEADS * MLA_V).astype(BF16)
    ropes = _rope_tables(n)
    nw = w.shape[1]
    full = lambda shape: pl.BlockSpec(shape, lambda g, t: (0,) * len(shape))
    blk = lambda width: pl.BlockSpec((None, tm, width), lambda g, t: (g, t, 0))
    widths = [MLA_HEADS * LANES, MLA_HEADS * LANES, MLA_HEADS * MLA_V, gq, gk, gk]
    return pl.pallas_call(
        _odd_in_body,
        grid=(G, n // tm),
        in_specs=[blk(d), pl.BlockSpec((None, 6, d), lambda g, t: (g, 0, 0)), full((1, d)), full((d, nw)),
                  full((1, ql)), full((ql, MLA_HEADS * LANES)), full((1, kl)), full((kl, MLA_HEADS * LANES)),
                  full((kl, MLA_HEADS * MLA_V)),
                  pl.BlockSpec((None, tm, 6 * LANES), lambda g, t: (g // batch, t, 0))],
        out_specs=[blk(wd) for wd in widths],
        out_shape=[jax.ShapeDtypeStruct((G, n, wd), BF16) for wd in widths],
        compiler_params=_cparams("arbitrary", "arbitrary"),
        name="odd_in",
    )(h_all, mod, g_pre, w, q_norm_g[None, :], wq, kv_norm_g[None, :], wk, wv, ropes)


def _mla_body(q_ref, kl_ref, kc_ref, vl_ref, vc_ref, o_ref, *, hps):
    lane = lax.broadcasted_iota(jnp.int32, (o_ref.shape[0], LANES), 1)
    for pr in range(hps // 2):
        vs = slice(pr * LANES, (pr + 1) * LANES)
        v_l = jnp.concatenate([vl_ref[:, vs], jnp.ones((vl_ref.shape[0], LANES), BF16)], axis=1)
        v_c = jnp.concatenate([vc_ref[:, vs], jnp.ones((vc_ref.shape[0], LANES), BF16)], axis=1)
        outs = []
        for hd in (2 * pr, 2 * pr + 1):
            sl = slice(hd * LANES, (hd + 1) * LANES)
            q = q_ref[:, sl]
            s_l = _dot_nt(q, kl_ref[:, sl])
            s_c = _dot_nt(q, kc_ref[:, sl])
            m = jnp.maximum(jnp.max(s_l, axis=-1, keepdims=True), jnp.max(s_c, axis=-1, keepdims=True))
            p_l = jnp.exp((s_l - m).astype(BF16))
            p_c = jnp.exp((s_c - m).astype(BF16))
            pv = _dot(p_l, v_l) + _dot(p_c, v_c)
            outs.append(pv[:, :LANES] / pv[:, LANES:])
        o_ref[:, vs] = jnp.where(lane < MLA_V, outs[0], outs[1]).astype(BF16)


def _mla_attn(q, k, v, batch, n, ctx_len, tq, hps):
    return pl.pallas_call(
        functools.partial(_mla_body, hps=hps),
        grid=(batch, MLA_HEADS // hps, n // tq),
        in_specs=[pl.BlockSpec((None, tq, hps * LANES), lambda b, h, i: (b, i, h)),
                  pl.BlockSpec((None, n, hps * LANES), lambda b, h, i: (b, 0, h)),
                  pl.BlockSpec((None, ctx_len, hps * LANES), lambda b, h, i: (batch, b, h)),
                  pl.BlockSpec((None, n, hps * MLA_V), lambda b, h, i: (b, 0, h)),
                  pl.BlockSpec((None, ctx_len, hps * MLA_V), lambda b, h, i: (batch, b, h))],
        out_specs=pl.BlockSpec((None, tq, hps * MLA_V), lambda b, h, i: (b, i, h)),
        out_shape=jax.ShapeDtypeStruct((batch, n, MLA_HEADS * MLA_V), BF16),
        compiler_params=_cparams("arbitrary", "arbitrary", "arbitrary"),
        name="mla_attn",
    )(q, k, k, v, v)


def _dup_half(x, kv):
    lane = lax.broadcasted_iota(jnp.int32, x.shape, 1)
    swapped = pltpu.roll(x, HEAD_DIM, axis=1)
    lo = lane < HEAD_DIM
    return jnp.where(lo, x, swapped) if kv == 0 else jnp.where(lo, swapped, x)


def _win_body(sink_ref, q_ref, kp_ref, kc_ref, kn_ref, kx_ref, vp_ref, vc_ref, vn_ref, vx_ref, o_ref, bias_ref,
              *, tq):
    i = pl.program_id(1)
    nb = pl.num_programs(1)
    w = WINDOW
    g = GQA_HEADS // GQA_KV
    n_ctx = kx_ref.shape[0]
    n_key = 2 * w + tq + n_ctx

    @pl.when((pl.program_id(0) == 0) & (i == 0))
    def _():
        r = lax.broadcasted_iota(jnp.int32, (g * tq, n_key), 0) % tq
        c = lax.broadcasted_iota(jnp.int32, (g * tq, n_key), 1)
        rel = c - w - r
        visible = (jnp.abs(rel) <= w) | (c >= 2 * w + tq)
        bias_ref[...] = jnp.where(visible, 0.0, -jnp.inf)

    edge = jnp.full((1, LANES), -jnp.inf, F32)
    zero = jnp.zeros((1, LANES), F32)
    bias_prev = jnp.where(i > 0, zero, edge)
    bias_next = jnp.where(i < nb - 1, zero, edge)
    lane = lax.broadcasted_iota(jnp.int32, (tq, LANES), 1)
    lo = lane < HEAD_DIM
    for kv in range(GQA_KV):
        rows = []
        for hh in range(g):
            hd = kv * g + hh
            blk = q_ref[:, (hd // 2) * LANES:(hd // 2 + 1) * LANES].astype(F32)
            rows.append(jnp.where(lo if hd % 2 == 0 else ~lo, blk, 0.0).astype(BF16))
        q = jnp.concatenate(rows, axis=0)
        k_all = jnp.concatenate([_dup_half(ref[...].astype(F32), kv).astype(BF16)
                                 for ref in (kp_ref, kc_ref, kn_ref, kx_ref)], axis=0)
        v_all = jnp.concatenate([_dup_half(ref[...].astype(F32), kv).astype(BF16)
                                 for ref in (vp_ref, vc_ref, vn_ref, vx_ref)], axis=0)
        v_all = jnp.concatenate([v_all, jnp.ones((n_key, LANES), BF16)], axis=1)
        s = _dot_nt(q, k_all) + bias_ref[...]
        s = jnp.concatenate([s[:, :w] + bias_prev, s[:, w:w + tq], s[:, w + tq:2 * w + tq] + bias_next,
                             s[:, 2 * w + tq:]], axis=1)
        hrow = lax.broadcasted_iota(jnp.int32, (g * tq, 1), 0) // tq
        sk = jnp.zeros((g * tq, 1), F32)
        for hh in range(g):
            sk = jnp.where(hrow == hh, sink_ref[kv * g + hh], sk)
        m = jnp.maximum(_row_max(s), sk)
        acc = _dot(jnp.exp((s - m).astype(BF16)), v_all)
        res = acc[:, :LANES] / (acc[:, LANES:] + jnp.exp(sk - m))
        for pr in range(g // 2):
            blk = (kv * g) // 2 + pr
            o_ref[:, blk * LANES:(blk + 1) * LANES] = jnp.where(
                lo, res[(2 * pr) * tq:(2 * pr + 1) * tq, :], res[(2 * pr + 1) * tq:(2 * pr + 2) * tq, :]).astype(BF16)


def _win_attn(q, k, v, sink, batch, n, ctx_len, tq):
    w = WINDOW
    per = tq // w
    gq, gk = GQA_HEADS * HEAD_DIM, GQA_KV * HEAD_DIM
    prev = lambda b, i: (b, jnp.maximum(i * per - 1, 0), 0)
    cur = lambda b, i: (b, i, 0)
    nxt = lambda b, i: (b, jnp.minimum((i + 1) * per, n // w - 1), 0)
    ctx = lambda b, i: (batch, b, 0)
    kspec = lambda rows, im: pl.BlockSpec((None, rows, gk), im)
    return pl.pallas_call(
        functools.partial(_win_body, tq=tq),
        grid=(batch, n // tq),
        in_specs=[pl.BlockSpec(memory_space=pltpu.SMEM),
                  pl.BlockSpec((None, tq, gq), cur),
                  kspec(w, prev), kspec(tq, cur), kspec(w, nxt), kspec(ctx_len, ctx),
                  kspec(w, prev), kspec(tq, cur), kspec(w, nxt), kspec(ctx_len, ctx)],
        out_specs=pl.BlockSpec((None, tq, gq), cur),
        out_shape=jax.ShapeDtypeStruct((batch, n, gq), BF16),
        scratch_shapes=[pltpu.VMEM((GQA_HEADS // GQA_KV * tq, 2 * w + tq + ctx_len), F32)],
        compiler_params=_cparams("arbitrary", "arbitrary"),
        name="win_attn",
    )(sink, q, k, k, k, k, v, v, v, v)


def _route_body(h_ref, mod_ref, g_ref, r_ref, a_ref, info_ref, cnt_ref, rows_ref, carry_ref, tri_ref):
    first = (pl.program_id(0) == 0) & (pl.program_id(1) == 0)

    @pl.when(first)
    def _():
        carry_ref[...] = jnp.zeros_like(carry_ref)
        ri = lax.broadcasted_iota(jnp.int32, tri_ref.shape, 0)
        ci = lax.broadcasted_iota(jnp.int32, tri_ref.shape, 1)
        tri_ref[...] = (ci < ri).astype(BF16)

    a = _norm_mod(h_ref[...], g_ref[...], mod_ref[3:4, :], mod_ref[4:5, :])
    a_ref[...] = a
    tm = a.shape[0]
    logits = jnp.dot(a, r_ref[...], precision=HIGHEST, preferred_element_type=F32)
    lane = lax.broadcasted_iota(jnp.int32, logits.shape, 1)
    neg = -jnp.inf
    logits = jnp.where(lane < N_EXPERTS, logits, neg)
    m1 = jnp.max(logits, axis=-1, keepdims=True)
    i1 = jnp.min(jnp.where(logits == m1, lane, LANES), axis=-1, keepdims=True)
    rest = jnp.where(lane == i1, neg, logits)
    m2 = jnp.max(rest, axis=-1, keepdims=True)
    i2 = jnp.min(jnp.where(rest == m2, lane, LANES), axis=-1, keepdims=True)
    e2 = jnp.exp(m2 - m1)
    w1 = 1.0 / (1.0 + e2)
    w2 = e2 / (1.0 + e2)
    chosen = ((lane == i1) | (lane == i2)).astype(F32)
    before = _dot(tri_ref[...], chosen.astype(BF16)) + carry_ref[...]
    p1 = jnp.sum(jnp.where(lane == i1, before, 0.0), axis=-1, keepdims=True)
    p2 = jnp.sum(jnp.where(lane == i2, before, 0.0), axis=-1, keepdims=True)
    carry_ref[...] += jnp.sum(chosen, axis=0, keepdims=True)
    cnt_ref[...] = carry_ref[...]
    vals = (i1.astype(F32), i2.astype(F32), w1, w2, p1, p2)
    info = jnp.zeros(logits.shape, F32)
    for j, v in enumerate(vals):
        info = jnp.where(lane == j, v, info)
    info_ref[...] = info
    src = (0, 1, 4, 5)
    sr = lax.broadcasted_iota(jnp.int32, (SUBLANES, LANES), 0)
    sl = lax.broadcasted_iota(jnp.int32, (SUBLANES, LANES), 1)
    sel = jnp.zeros((SUBLANES, LANES), F32)
    for r, c in enumerate(src):
        sel = jnp.where((sr == r) & (sl == c), 1.0, sel)
    rows_ref[...] = lax.dot_general(sel, info, (((1,), (1,)), ((), ())), precision=HIGHEST,
                                    preferred_element_type=F32)


def _route(h_lat, mod, g_pre, router, tm):
    batch, n, d = h_lat.shape
    rp = jnp.pad(router, ((0, 0), (0, LANES - N_EXPERTS)))
    nt = n // tm
    return pl.pallas_call(
        _route_body,
        grid=(batch, nt),
        in_specs=[pl.BlockSpec((None, tm, d), lambda g, t: (g, t, 0)),
                  pl.BlockSpec((None, 6, d), lambda g, t: (g, 0, 0)),
                  pl.BlockSpec((1, d), lambda g, t: (0, 0)),
                  pl.BlockSpec((d, LANES), lambda g, t: (0, 0))],
        out_specs=[pl.BlockSpec((tm, d), lambda g, t: (g * nt + t, 0)),
                   pl.BlockSpec((tm, LANES), lambda g, t: (g * nt + t, 0)),
                   pl.BlockSpec((1, LANES), lambda g, t: (0, 0)),
                   pl.BlockSpec((SUBLANES, tm), lambda g, t: (0, g * nt + t))],
        out_shape=[jax.ShapeDtypeStruct((batch * n, d), F32),
                   jax.ShapeDtypeStruct((batch * n, LANES), F32),
                   jax.ShapeDtypeStruct((1, LANES), F32),
                   jax.ShapeDtypeStruct((SUBLANES, batch * n), F32)],
        scratch_shapes=[pltpu.VMEM((1, LANES), F32), pltpu.VMEM((tm, tm), BF16)],
        compiler_params=_cparams("arbitrary", "arbitrary"),
        name="moe_route",
    )(h_lat, mod, g_pre, rp)


def _scatter_body(slot_ref, fill_ref, a_ref, xs_ref, zero_ref, sem, zsem, *, ts, n_tok, n_fill):
    base = pl.program_id(0) * ts

    @pl.when(pl.program_id(0) == 0)
    def _():
        zero_ref[...] = jnp.zeros_like(zero_ref)

        def fill(t):
            row = pl.multiple_of(fill_ref[t] * MOE_TILE, MOE_TILE)
            return pltpu.make_async_copy(zero_ref, xs_ref.at[pl.ds(row, MOE_TILE), :], zsem)

        for t in range(n_fill):
            @pl.when(fill_ref[t] >= 0)
            def _(t=t):
                fill(t).start()
        for t in range(n_fill):
            @pl.when(fill_ref[t] >= 0)
            def _(t=t):
                fill(t).wait()

    def issue(i, carry):
        for u in range(SUBLANES):
            for k in range(2):
                slot = slot_ref[k * n_tok + base + i * SUBLANES + u]
                pltpu.make_async_copy(a_ref.at[i, pl.ds(u, 1), :], xs_ref.at[pl.ds(slot, 1), :],
                                      sem).start(priority=k)
        return carry

    lax.fori_loop(0, ts // SUBLANES, issue, 0)
    for k in range(2):
        pltpu.make_async_copy(xs_ref.at[pl.ds(0, ts), :], xs_ref.at[pl.ds(0, ts), :], sem).wait()


def _scatter(slots, fill_tiles, a, n_slots, ts):
    n_tok, d = a.shape
    return pl.pallas_call(
        functools.partial(_scatter_body, ts=ts, n_tok=n_tok, n_fill=fill_tiles.shape[0]),
        grid_spec=pltpu.PrefetchScalarGridSpec(
            num_scalar_prefetch=2,
            grid=(n_tok // ts,),
            in_specs=[pl.BlockSpec((ts // SUBLANES, SUBLANES, d), lambda i, *_: (i, 0, 0))],
            out_specs=pl.BlockSpec(memory_space=pl.ANY),
            scratch_shapes=[pltpu.VMEM((MOE_TILE, d), F32), pltpu.SemaphoreType.DMA(()),
                            pltpu.SemaphoreType.DMA(())]),
        out_shape=jax.ShapeDtypeStruct((n_slots, d), F32),
        compiler_params=_cparams("arbitrary"),
        name="moe_scatter",
    )(slots, fill_tiles, a.reshape(n_tok // SUBLANES, SUBLANES, d))


def _experts_body(te_ref, tv_ref, x_ref, wg_ref, wu_ref, wd_ref, y_ref):
    del te_ref
    j = pl.program_id(0)
    f = pl.program_id(1)
    valid = tv_ref[j]

    @pl.when(f == 0)
    def _():
        y_ref[...] = jnp.zeros_like(y_ref)

    n_sub = (valid + MOE_SUB - 1) // MOE_SUB
    for ns in range(1, MOE_TILE // MOE_SUB + 1):
        rows = slice(0, ns * MOE_SUB)

        @pl.when(n_sub == ns)
        def _(rows=rows):
            x = x_ref[rows, :].astype(BF16)
            gate = _dot(x, wg_ref[...].astype(BF16))
            up = _dot(x, wu_ref[...].astype(BF16))
            hid = (_silu(gate) * up).astype(BF16)
            y_ref[rows, :] += _dot(hid, wd_ref[...].astype(BF16))


def _experts(tile_expert, tile_valid, xs, wg, wu, wd):
    n_slots, d = xs.shape
    ff = wg.shape[-1]
    n_tiles = n_slots // MOE_TILE
    tf = MOE_FCHUNK
    chunk = lambda j, f, tv: jnp.where(tv[j] > 0, f, ff // tf - 1)
    return pl.pallas_call(
        _experts_body,
        grid_spec=pltpu.PrefetchScalarGridSpec(
            num_scalar_prefetch=2,
            grid=(n_tiles, ff // tf),
            in_specs=[pl.BlockSpec((MOE_TILE, d), lambda j, f, te, tv: (j, 0)),
                      pl.BlockSpec((None, d, tf), lambda j, f, te, tv: (te[j], 0, chunk(j, f, tv))),
                      pl.BlockSpec((None, d, tf), lambda j, f, te, tv: (te[j], 0, chunk(j, f, tv))),
                      pl.BlockSpec((None, tf, d), lambda j, f, te, tv: (te[j], chunk(j, f, tv), 0))],
            out_specs=pl.BlockSpec((MOE_TILE, d), lambda j, f, te, tv: (j, 0))),
        out_shape=jax.ShapeDtypeStruct((n_slots, d), F32),
        compiler_params=_cparams("arbitrary", "arbitrary"),
        name="moe_experts",
    )(tile_expert, tile_valid, xs, wg, wu, wd)


def _combine_body(slot_ref, ys_ref, info_ref, h_ref, mod_ref, pg_ref, o_ref, buf_ref, sem, *, tc, n_tok):
    base = (pl.program_id(0) * pl.num_programs(1) + pl.program_id(1)) * tc

    def issue(i, carry):
        for u in range(SUBLANES):
            for k in range(2):
                slot = slot_ref[k * n_tok + base + i * SUBLANES + u]
                pltpu.make_async_copy(ys_ref.at[pl.ds(slot, 1), :], buf_ref.at[k, i, pl.ds(u, 1), :],
                                      sem).start(priority=k)
        return carry

    lax.fori_loop(0, tc // SUBLANES, issue, 0)
    for k in range(2):
        pltpu.make_async_copy(ys_ref.at[pl.ds(0, tc), :], ys_ref.at[pl.ds(0, tc), :], sem).wait()
    d = o_ref.shape[-1]
    f = info_ref[:, 2:3] * buf_ref[0].reshape(tc, d) + info_ref[:, 3:4] * buf_ref[1].reshape(tc, d)
    o_ref[...] = h_ref[...] + mod_ref[5:6, :] * _rms(f, pg_ref[...])


def _combine(slots, ys, info, h_lat, mod, post_g, tc):
    batch, n, d = h_lat.shape
    nt = n // tc
    return pl.pallas_call(
        functools.partial(_combine_body, tc=tc, n_tok=batch * n),
        grid_spec=pltpu.PrefetchScalarGridSpec(
            num_scalar_prefetch=1,
            grid=(batch, nt),
            in_specs=[pl.BlockSpec(memory_space=pl.ANY),
                      pl.BlockSpec((tc, LANES), lambda g, t, *_: (g * nt + t, 0)),
                      pl.BlockSpec((None, tc, d), lambda g, t, *_: (g, t, 0)),
                      pl.BlockSpec((None, 6, d), lambda g, t, *_: (g, 0, 0)),
                      pl.BlockSpec((1, d), lambda g, t, *_: (0, 0))],
            out_specs=pl.BlockSpec((None, tc, d), lambda g, t, *_: (g, t, 0)),
            scratch_shapes=[pltpu.VMEM((2, tc // SUBLANES, SUBLANES, d), F32), pltpu.SemaphoreType.DMA(())]),
        out_shape=jax.ShapeDtypeStruct((batch, n, d), F32),
        compiler_params=_cparams("arbitrary", "arbitrary"),
        name="moe_combine",
    )(slots, ys, info, h_lat, mod, post_g)


def _moe(h_lat, mod, g_pre, router, wg, wu, wd, post_g, tm):
    batch, n, d = h_lat.shape
    n_tok = batch * n
    a, info, counts, rt = _route(h_lat, mod, g_pre, router, tm)
    cnt = counts[0, :N_EXPERTS].astype(jnp.int32)
    padded = ((cnt + MOE_TILE - 1) // MOE_TILE) * MOE_TILE
    ends = jnp.cumsum(padded)
    starts = ends - padded
    seg = jnp.zeros((2, n_tok), jnp.int32)
    for e in range(N_EXPERTS):
        seg = jnp.where(rt[0:2] == e, starts[e], seg)
    slots = (seg + rt[2:4].astype(jnp.int32)).reshape(2 * n_tok)
    n_tiles = (2 * n_tok + N_EXPERTS * (MOE_TILE - 1)) // MOE_TILE
    tile_row = jnp.arange(n_tiles, dtype=jnp.int32) * MOE_TILE
    tile_expert = jnp.minimum(jnp.sum(tile_row[:, None] >= ends[None, :], axis=1), N_EXPERTS - 1).astype(jnp.int32)
    tile_valid = jnp.clip(cnt[tile_expert] - (tile_row - starts[tile_expert]), 0, MOE_TILE)
    tile_valid = jnp.where(tile_row < ends[-1], tile_valid, 0).astype(jnp.int32)
    last = jnp.where(padded > 0, ends // MOE_TILE - 1, -1)
    extra = ends[-1] // MOE_TILE + jnp.arange(n_tiles - 2 * n_tok // MOE_TILE, dtype=jnp.int32)
    fill_tiles = jnp.concatenate([last, jnp.where(extra < n_tiles, extra, -1)]).astype(jnp.int32)
    xs = _scatter(slots, fill_tiles, a, n_tiles * MOE_TILE, tm)
    ys = _experts(tile_expert, tile_valid, xs, wg, wu, wd)
    return _combine(slots, ys, info, h_lat, mod, post_g, tm)


def kernel(x, c, ctx, c_ctx, ada_w, ada_b, mix_pre_g, mix_post_g, ffn_pre_g, ffn_post_g, ev_w_in, ev_hy_conv_w, ev_hy_conv_b, ev_hy_filt_w1, ev_hy_filt_b1, ev_hy_filt_w2, ev_hy_filt_b2, ev_hy_filt_w3, ev_hy_freq, ev_hy_bias, ev_ssm_conv_w, ev_ssm_conv_b, ev_ssm_dt_bias, ev_ssm_a_log, ev_ssm_d, ev_ssm_norm_g, ev_w_out, ev_ffn_w_gate, ev_ffn_w_up, ev_ffn_w_down, od_w_in, od_mla_q_norm_g, od_mla_w_q_up, od_mla_kv_norm_g, od_mla_w_kv_up, od_gqa_sink, od_w_out, od_router, od_moe_w_gate, od_moe_w_up, od_moe_w_down):
    batch, n, d = x.shape
    ctx_len = ctx.shape[1]
    assert batch * ctx_len == n and ada_w.shape[0] == 2
    G = batch + 1
    tm = min(n, 512)
    tl = min(n, 1024)

    cond = jnp.concatenate([c, c_ctx[None, :], jnp.zeros((16 - G, d), F32)], axis=0)
    mods = _ada(cond, ada_w, ada_b).reshape(2, 16, 6, d)
    ctx3 = ctx.reshape(1, n, d)

    n_main = 3 * HY_CH + SSM_INNER + SSM_CONV_CH
    w_in = ev_w_in[0]
    w_main = w_in[:, :n_main].astype(BF16)
    w_dt = jnp.pad(w_in[:, n_main:], ((0, 0), (0, LANES - 2 * SSM_HEADS))).astype(BF16)
    proj, dt = _even_in(x, ctx3, mods[0], mix_pre_g[0][None, :], w_main, w_dt, tl)
    y_hy_l, y_hy_c = _hyena(proj, batch, n, ctx_len,
                            (ev_hy_conv_w[0], ev_hy_conv_b[0], ev_hy_filt_w1[0], ev_hy_filt_b1[0], ev_hy_filt_w2[0],
                             ev_hy_filt_b2[0], ev_hy_filt_w3[0], ev_hy_freq[0], ev_hy_bias[0]))
    y_ss_l, y_ss_c = _ssd(proj, dt, batch, n, ctx_len,
                          (ev_ssm_conv_w[0], ev_ssm_conv_b[0], ev_ssm_dt_bias[0], ev_ssm_a_log[0], ev_ssm_d[0]))
    h_all = _mix_out_even(y_hy_l, y_hy_c, y_ss_l, y_ss_c, proj, ev_ssm_norm_g[0][None, :],
                          ev_w_out[0].astype(BF16), x, ctx3, mods[0], mix_post_g[0][None, :], tl)
    h_all = _ffn(h_all, mods[0], ffn_pre_g[0][None, :], ev_ffn_w_gate[0].astype(BF16), ev_ffn_w_up[0].astype(BF16),
                 ev_ffn_w_down[0].astype(BF16), ffn_post_g[0][None, :], tl)

    q_m, k_m, v_m, q_w, k_w, v_w = _odd_in(h_all, mods[1], mix_pre_g[1][None, :], od_w_in[0], od_mla_q_norm_g[0],
                                           od_mla_w_q_up[0], od_mla_kv_norm_g[0], od_mla_w_kv_up[0], tl)
    o_mla = _mla_attn(q_m, k_m, v_m, batch, n, ctx_len, min(n, 512), 4)
    o_win = _win_attn(q_w, k_w, v_w, od_gqa_sink[0], batch, n, ctx_len, min(n, 2 * WINDOW))
    h_lat = _mix_out_odd(o_mla, o_win, od_w_out[0].astype(BF16), h_all, mods[1], mix_post_g[1][None, :], tl)
    return _moe(h_lat, mods[1], ffn_pre_g[1][None, :], od_router[0], od_moe_w_gate[0], od_moe_w_up[0],
                od_moe_w_down[0], ffn_post_g[1][None, :], tm)
```

```python
import functools
import math

import jax
import jax.numpy as jnp
import numpy as np
from jax import lax
from jax.experimental import pallas as pl
from jax.experimental.pallas import tpu as pltpu

F32 = jnp.float32
BF16 = jnp.bfloat16
HIGHEST = lax.Precision.HIGHEST

D_MODEL = 1024
GRID_W = 64
EPS = 1e-6
HEAD_DIM = 64
GROUP_WIDTH = D_MODEL // 2

HY_CH = GROUP_WIDTH
HY_BANDS = 16
HY_EMB = 2 * HY_BANDS + 1
HY_FILT_HID = 64
HY_FAST_DECAY = 0.3
HY_SLOW_DECAY = 1.5
HY_DECAY_TARGET = 1e-2

SSM_INNER = GROUP_WIDTH
SSM_HEADDIM = 64
SSM_HEADS = SSM_INNER // SSM_HEADDIM
SSM_GROUPS = 2
SSM_STATE = 128
SSM_CHUNK = 128
SSM_CONV_CH = SSM_INNER + 2 * SSM_GROUPS * SSM_STATE

MLA_HEADS = GROUP_WIDTH // HEAD_DIM
MLA_NOPE = 64
MLA_ROPE = 32
MLA_V = 64
MLA_Q_LORA = D_MODEL // 4
MLA_KV_LORA = D_MODEL // 8

GQA_HEADS = GROUP_WIDTH // HEAD_DIM
GQA_KV = GQA_HEADS // 4
WINDOW = 128
ROPE_BASE = 10000.0

D_FF = ((8 * D_MODEL // 3 + 127) // 128) * 128
N_EXPERTS = 8
D_FF_EXPERT = 7 * D_MODEL // 2

LANES = 128
SUBLANES = 8
DFT_SPLIT = 64
VMEM_LIMIT = 56 * 1024 * 1024
MOE_TILE = 1024
MOE_SUB = 256
MOE_FCHUNK = 512


def _cparams(*sem):
    return pltpu.CompilerParams(dimension_semantics=sem, vmem_limit_bytes=VMEM_LIMIT)


def _silu(x):
    return x / (1.0 + jnp.exp(-x))


def _rms(x, g):
    return x * lax.rsqrt(jnp.mean(x * x, axis=-1, keepdims=True) + EPS) * g


def _norm_mod(h, g, shift, scale):
    return _rms(h, g) * (1.0 + scale) + shift


def _dot(a, b):
    return jnp.dot(a, b, preferred_element_type=F32)


def _row_max(*blocks):
    tiles = [b[:, j:j + LANES] for b in blocks for j in range(0, b.shape[1], LANES)]
    m = tiles[0]
    for t in tiles[1:]:
        m = jnp.maximum(m, t)
    return jnp.max(m, axis=-1, keepdims=True)


def _dot_nt(a, b):
    return lax.dot_general(a, b, (((1,), (1,)), ((), ())), preferred_element_type=F32)


def _ada_body(c_ref, w_ref, b_ref, o_ref):
    s = _silu(c_ref[...]).astype(BF16)
    o_ref[0] = _dot(s, w_ref[0].astype(BF16)) + b_ref[0]


def _ada(cc, ada_w, ada_b):
    depth, d, n6 = ada_w.shape
    rows = cc.shape[0]
    tn = 1536
    return pl.pallas_call(
        _ada_body,
        grid=(depth, n6 // tn),
        in_specs=[pl.BlockSpec((rows, d), lambda l, j: (0, 0)),
                  pl.BlockSpec((1, d, tn), lambda l, j: (l, 0, j)),
                  pl.BlockSpec((1, 1, tn), lambda l, j: (l, 0, j))],
        out_specs=pl.BlockSpec((1, rows, tn), lambda l, j: (l, 0, j)),
        out_shape=jax.ShapeDtypeStruct((depth, rows, n6), F32),
        compiler_params=_cparams("arbitrary", "arbitrary"),
        name="ada",
    )(cc, ada_w, ada_b.reshape(depth, 1, n6))


def _lat_ctx_specs(batch, nt, tm, d):
    lat = pl.BlockSpec((None, tm, d), lambda g, t: (jnp.minimum(g, batch - 1), jnp.where(g < batch, t, nt - 1), 0))
    ctx = pl.BlockSpec((None, tm, d), lambda g, t: (0, jnp.where(g < batch, 0, t), 0))
    return lat, ctx


def _even_in_body(x_ref, c_ref, mod_ref, g_ref, w_ref, wdt_ref, o_ref, dt_ref, *, batch):
    h = jnp.where(pl.program_id(0) == batch, c_ref[...], x_ref[...])
    a = _norm_mod(h, g_ref[...], mod_ref[0:1, :], mod_ref[1:2, :]).astype(BF16)
    n_out = o_ref.shape[-1]
    for j in range(0, n_out, 512):
        o_ref[:, j:j + 512] = _dot(a, w_ref[:, j:j + 512]).astype(BF16)
    dt_ref[...] = _dot(a, wdt_ref[...])


def _even_in(x, ctx3, mod, g_pre, w_main, w_dt, tm):
    batch, n, d = x.shape
    G = batch + 1
    n_main = w_main.shape[1]
    lat_spec, ctx_spec = _lat_ctx_specs(batch, n // tm, tm, d)
    return pl.pallas_call(
        functools.partial(_even_in_body, batch=batch),
        grid=(G, n // tm),
        in_specs=[lat_spec, ctx_spec,
                  pl.BlockSpec((None, 6, d), lambda g, t: (g, 0, 0)),
                  pl.BlockSpec((1, d), lambda g, t: (0, 0)),
                  pl.BlockSpec((d, n_main), lambda g, t: (0, 0), pipeline_mode=pl.Buffered(1)),
                  pl.BlockSpec((d, LANES), lambda g, t: (0, 0))],
        out_specs=[pl.BlockSpec((None, tm, n_main), lambda g, t: (g, t, 0)),
                   pl.BlockSpec((None, tm, LANES), lambda g, t: (g, t, 0))],
        out_shape=[jax.ShapeDtypeStruct((G, n, n_main), BF16),
                   jax.ShapeDtypeStruct((G, n, LANES), F32)],
        compiler_params=_cparams("arbitrary", "arbitrary"),
        name="even_in",
    )(x, ctx3, mod, g_pre, w_main, w_dt)


def _dwconv3(x, w, b):
    L = x.shape[0]
    y = pltpu.roll(x, 1, axis=0) * w[0:1, :] + x * w[1:2, :] + pltpu.roll(x, L - 1, axis=0) * w[2:3, :] + b
    row = lax.broadcasted_iota(jnp.int32, (SUBLANES, x.shape[1]), 0)
    head = y[0:SUBLANES, :] - jnp.where(row == 0, x[L - 1:L, :] * w[0:1, :], 0.0)
    tail = y[L - SUBLANES:, :] - jnp.where(row == SUBLANES - 1, x[0:1, :] * w[2:3, :], 0.0)
    return jnp.concatenate([head, y[SUBLANES:L - SUBLANES, :], tail], axis=0)


def _hy_pre_body(p_ref, cw_ref, cb_ref, u_ref, x0_ref):
    c = HY_CH
    parts = []
    for k in range(3):
        x = p_ref[:, k * c:(k + 1) * c].astype(F32)
        parts.append(_dwconv3(x, cw_ref[:, k * c:(k + 1) * c], cb_ref[:, k * c:(k + 1) * c]))
    x0_ref[...] = parts[0].astype(BF16)
    u_ref[...] = (parts[1] * parts[2]).astype(BF16)


def _hy_pre(proj, nseq, L, index_map, cw, cb):
    c3 = 3 * HY_CH
    return pl.pallas_call(
        _hy_pre_body,
        grid=(nseq,),
        in_specs=[pl.BlockSpec((None, L, c3), index_map),
                  pl.BlockSpec((3, c3), lambda s: (0, 0)),
                  pl.BlockSpec((1, c3), lambda s: (0, 0))],
        out_specs=[pl.BlockSpec((None, L, HY_CH), lambda s: (s, 0, 0)),
                   pl.BlockSpec((None, L, HY_CH), lambda s: (s, 0, 0))],
        out_shape=[jax.ShapeDtypeStruct((nseq, L, HY_CH), BF16),
                   jax.ShapeDtypeStruct((nseq, L, HY_CH), BF16)],
        compiler_params=_cparams("arbitrary"),
        name="hy_pre",
    )(proj, cw, cb)


def _hy_filter_body(z_ref, t_ref, w1_ref, b1_ref, w2_ref, b2_ref, w3_ref, f_ref, dl_ref, o_ref):
    f = f_ref[...]
    h = jnp.sin(f * (jnp.dot(z_ref[...], w1_ref[...], precision=HIGHEST, preferred_element_type=F32) + b1_ref[...]))
    h = jnp.sin(f * (jnp.dot(h, w2_ref[...], precision=HIGHEST, preferred_element_type=F32) + b2_ref[...]))
    h = jnp.dot(h, w3_ref[...], precision=HIGHEST, preferred_element_type=F32)
    decay = jnp.exp(-t_ref[...] * dl_ref[...])
    hf = h[:, :HY_CH] * decay
    hb = h[:, HY_CH:] * decay
    row = lax.broadcasted_iota(jnp.int32, hb.shape, 0)
    hb = jnp.where(row == 0, 0.0, hb)
    o_ref[:, :HY_CH] = hf.astype(BF16)
    o_ref[:, HY_CH:] = hb.astype(BF16)


def _hy_filter(L, w1, b1, w2, b2, w3, freq):
    pos = jnp.arange(L, dtype=F32)
    t = pos / (L - 1)
    omega = 2.0 * math.pi * pos / L
    bands = jnp.linspace(1e-4, HY_BANDS - 1, HY_BANDS, dtype=F32)
    z = jnp.concatenate([t[:, None], jnp.cos(omega[:, None] * bands), -jnp.sin(omega[:, None] * bands)], axis=-1)
    z = jnp.pad(z, ((0, 0), (0, LANES - HY_EMB)))
    w1p = jnp.pad(w1, ((0, LANES - HY_EMB), (0, 0)))
    deltas = jnp.abs(jnp.linspace(math.log(HY_DECAY_TARGET) / HY_SLOW_DECAY,
                                  math.log(HY_DECAY_TARGET) / HY_FAST_DECAY, HY_CH, dtype=F32))
    return pl.pallas_call(
        _hy_filter_body,
        out_shape=jax.ShapeDtypeStruct((L, 2 * HY_CH), BF16),
        compiler_params=pltpu.CompilerParams(vmem_limit_bytes=VMEM_LIMIT),
        name="hy_filter",
    )(z, t[:, None], w1p, b1[None, :], w2, b2[None, :], w3, freq[None, :], deltas[None, :])


def _dft_tables(L):
    split = DFT_SPLIT
    t = jnp.arange(L, dtype=jnp.int32)[None, :]
    k1 = jnp.arange(L // split, dtype=jnp.int32)[:, None] * split
    k0 = jnp.arange(split, dtype=jnp.int32)[:, None]
    ang1 = ((k1 * t) % (2 * L)).astype(F32) * (math.pi / L)
    ang0 = ((k0 * t) % (2 * L)).astype(F32) * (math.pi / L)
    factors = jnp.stack([jnp.cos(ang0), jnp.sin(ang0)])
    coarse = jnp.stack([jnp.cos(ang1), jnp.sin(ang1)])
    alt = jnp.where(jnp.arange(L, dtype=jnp.int32) % 2 == 0, 1.0, -1.0).astype(F32)
    groups = min(L // split, 8)
    tr = groups * split
    return pl.pallas_call(
        functools.partial(_dft_tables_body, groups=groups),
        grid=(L // tr,),
        in_specs=[pl.BlockSpec((2, split, L), lambda i: (0, 0, 0)),
                  pl.BlockSpec((2, groups, L), lambda i: (0, i, 0)),
                  pl.BlockSpec((1, L), lambda i: (0, 0)),
                  pl.BlockSpec((tr, 1), lambda i: (i, 0))],
        out_specs=[pl.BlockSpec((2, tr, L), lambda i: (0, i, 0)),
                   pl.BlockSpec((tr, L), lambda i: (i, 0))],
        out_shape=[jax.ShapeDtypeStruct((2, L, L), BF16), jax.ShapeDtypeStruct((L, L), BF16)],
        compiler_params=_cparams("arbitrary"),
        name="dft_tables",
    )(factors, coarse, alt[None, :], alt[:, None])


def _dft_tables_body(f_ref, c_ref, altt_ref, altk_ref, f2_ref, ft_ref, *, groups):
    split = f_ref.shape[1]
    c0, s0 = f_ref[0], f_ref[1]
    first = pl.program_id(0) == 0
    row = lax.broadcasted_iota(jnp.int32, (split, LANES), 0)
    lane = lax.broadcasted_iota(jnp.int32, (split, LANES), 1)
    for j in range(groups):
        rows = slice(j * split, (j + 1) * split)
        c1, s1 = c_ref[0, j:j + 1, :], c_ref[1, j:j + 1, :]
        f2_ref[0, rows, :] = (c1 * c0 - s1 * s0).astype(BF16)
        nsin = -(s1 * c0 + c1 * s0)
        ft_ref[rows, 0:LANES] = jnp.where(lane == 0, altk_ref[rows, :], nsin[:, 0:LANES]).astype(BF16)
        ft_ref[rows, LANES:] = nsin[:, LANES:].astype(BF16)
        if j == 0:
            is_dc = first & (row[:, 0:1] == 0)
            nsin = jnp.where(is_dc, altt_ref[...], nsin)
        f2_ref[1, rows, :] = nsin.astype(BF16)


def _filt_dft_body(f_ref, x_ref, o_ref, *, L, tr):
    c = HY_CH
    x = x_ref[...]
    sre = _dot(f_ref[0], x)
    sim = _dot(f_ref[1], x)
    is0 = (lax.broadcasted_iota(jnp.int32, (tr, c), 0) + pl.program_id(0) * tr) == 0
    wgt = jnp.where(is0, 1.0 / (2 * L), 1.0 / L)
    hre = sre[:, :c] + sre[:, c:]
    o_ref[0] = (hre * wgt).astype(BF16)
    o_ref[1] = (jnp.where(is0, 0.0, sim[:, :c] - sim[:, c:]) * wgt).astype(BF16)
    o_ref[2] = (jnp.where(is0, sim[:, :c] + sim[:, c:], hre) * wgt).astype(BF16)


def _filt_dft(f2, hfb):
    _, L, _ = f2.shape
    cols = hfb.shape[1]
    tr = min(L, 512)
    return pl.pallas_call(
        functools.partial(_filt_dft_body, L=L, tr=tr),
        grid=(L // tr,),
        in_specs=[pl.BlockSpec((2, tr, L), lambda i: (0, i, 0)),
                  pl.BlockSpec((L, cols), lambda i: (0, 0))],
        out_specs=pl.BlockSpec((3, tr, cols // 2), lambda i: (0, i, 0)),
        out_shape=jax.ShapeDtypeStruct((3, L, cols // 2), BF16),
        compiler_params=_cparams("arbitrary"),
        name="filt_dft",
    )(f2, hfb)


def _hy_conv_body(u_ref, x0_ref, f2_ref, fimt_ref, s_ref, bias_ref, o_ref, *, L, fc):
    u = u_ref[...]
    acc = None
    for k in range(L // fc):
        ks = slice(k * fc, (k + 1) * fc)
        ure = _dot(f2_ref[0, ks, :], u)
        uim = _dot(f2_ref[1, ks, :], u)
        a_m = s_ref[0, ks, :].astype(F32)
        b_m = s_ref[1, ks, :].astype(F32)
        d_m = s_ref[2, ks, :].astype(F32)
        yre = (ure * a_m - uim * b_m).astype(BF16)
        yim = (ure * b_m + uim * d_m).astype(BF16)
        part = _dot(f2_ref[0, :, ks], yre) + _dot(fimt_ref[:, ks], yim)
        acc = part if acc is None else acc + part
    y = acc + u.astype(F32) * bias_ref[...]
    o_ref[...] = (x0_ref[...].astype(F32) * y).astype(BF16)


def _hy_conv(u, x0, f2, fimt, s, bias):
    nseq, L, c = u.shape
    fc = min(L, 512)
    resident = lambda shape: pl.BlockSpec(shape, lambda s_: (0,) * len(shape), pipeline_mode=pl.Buffered(1))
    return pl.pallas_call(
        functools.partial(_hy_conv_body, L=L, fc=fc),
        grid=(nseq,),
        in_specs=[pl.BlockSpec((None, L, c), lambda s_: (s_, 0, 0)),
                  pl.BlockSpec((None, L, c), lambda s_: (s_, 0, 0)),
                  resident((2, L, L)), resident((L, L)), resident((3, L, c)),
                  pl.BlockSpec((1, c), lambda s_: (0, 0))],
        out_specs=pl.BlockSpec((None, L, c), lambda s_: (s_, 0, 0)),
        out_shape=jax.ShapeDtypeStruct((nseq, L, c), BF16),
        compiler_params=_cparams("arbitrary"),
        name="hy_conv",
    )(u, x0, f2, fimt, s, bias)


def _hyena(proj, batch, n, ctx_len, p):
    cw, cb, w1, b1, w2, b2, w3, freq, bias = p
    outs = []
    for L, imap in ((n, lambda s: (s, 0, 0)), (ctx_len, lambda s: (batch, s, 0))):
        u, x0 = _hy_pre(proj, batch, L, imap, cw, cb[None, :])
        hfb = _hy_filter(L, w1, b1, w2, b2, w3, freq)
        f2, fimt = _dft_tables(L)
        s = _filt_dft(f2, hfb)
        outs.append(_hy_conv(u, x0, f2, fimt, s, bias[None, :]))
    return outs


def _softplus(x):
    return jnp.maximum(x, 0.0) + jnp.log(1.0 + jnp.exp(-jnp.abs(x)))


def _chunk_cumsum(a, axis, reverse):
    q = SSM_CHUNK
    size = a.shape[axis]
    pos = lax.broadcasted_iota(jnp.int32, a.shape, axis) % q
    s = 1
    while s < q:
        if reverse:
            a = a + jnp.where(pos < q - s, pltpu.roll(a, size - s, axis=axis), 0.0)
        else:
            a = a + jnp.where(pos >= s, pltpu.roll(a, s, axis=axis), 0.0)
        s *= 2
    return a


def _ssd_prepare(L, xbc_ref, dtc_ref, dtr_ref, cw_ref, cb_ref, dtbc_ref, alc_ref, dtbr_ref, alr_ref,
                 x_s, c_s, bt_s, csc_s, csr_s, dtr_s):
    nc = L // SSM_CHUNK
    nh = SSM_HEADS
    inner = SSM_INNER
    gs = SSM_GROUPS * SSM_STATE
    xs = _silu(_dwconv3(xbc_ref[:, :inner].astype(F32), cw_ref[:, :inner], cb_ref[:, :inner]))
    x_s[0:L, :] = xs.astype(BF16)
    bm = _silu(_dwconv3(xbc_ref[:, inner:inner + gs].astype(F32), cw_ref[:, inner:inner + gs],
                        cb_ref[:, inner:inner + gs]))
    for c in range(nc):
        bt_s[c] = bm[c * SSM_CHUNK:(c + 1) * SSM_CHUNK, :].T.astype(BF16)
    cm = _silu(_dwconv3(xbc_ref[:, inner + gs:].astype(F32), cw_ref[:, inner + gs:], cb_ref[:, inner + gs:]))
    c_s[0:L, :] = cm.astype(BF16)
    a_col = _softplus(dtc_ref[...] + dtbc_ref[...]) * (-jnp.exp(alc_ref[...]))
    lane = lax.broadcasted_iota(jnp.int32, a_col.shape, 1)
    csc_s[0:L, :] = jnp.where(lane < nh, _chunk_cumsum(a_col, 0, False), _chunk_cumsum(a_col, 0, True))
    dt_row = _softplus(dtr_ref[...] + dtbr_ref[...])
    a_row = (dt_row * (-jnp.exp(alr_ref[...]))).reshape(2 * nh * nc, SSM_CHUNK)
    rown = lax.broadcasted_iota(jnp.int32, a_row.shape, 0)
    csr_s[0:2 * nh * nc, :] = jnp.where(rown < nh * nc, _chunk_cumsum(a_row, 1, False),
                                        _chunk_cumsum(a_row, 1, True))
    dtr_s[0:2 * nh * nc, :] = dt_row.reshape(2 * nh * nc, SSM_CHUNK)


def _ssd_chunk(c, d, nc, x_s, c_s, bt_s, csc_s, csr_s, dtr_s, st_s, y_s):
    q = SSM_CHUNK
    ns = SSM_STATE
    hp = SSM_HEADDIM
    hpg = SSM_HEADS // SSM_GROUPS
    gw = hpg * hp
    r0 = pl.multiple_of(c * q, q)
    ri = lax.broadcasted_iota(jnp.int32, (q, q), 0)
    ci = lax.broadcasted_iota(jnp.int32, (q, q), 1)
    keep = (ci <= ri) if d == 0 else (ci >= ri)
    lane_blk = lax.broadcasted_iota(jnp.int32, (q, gw), 1) // hp
    cc = c_s[pl.ds(r0, q), :]
    xc = x_s[pl.ds(r0, q), :]
    btc = bt_s[c]
    ys = []
    for g in range(SSM_GROUPS):
        cg = cc[:, g * ns:(g + 1) * ns]
        btg = btc[g * ns:(g + 1) * ns, :]
        xg = xc[:, g * gw:(g + 1) * gw]
        s_cb = _dot(cg, btg)
        btg32 = btg.astype(F32)
        m_rows, w_rows, e_cols, tots = [], [], [], []
        for hh in range(hpg):
            idx = d * SSM_HEADS + g * hpg + hh
            csc = csc_s[pl.ds(r0, q), idx:idx + 1]
            csr = csr_s[pl.ds(idx * nc + c, 1), :]
            dtr = dtr_s[pl.ds(idx * nc + c, 1), :]
            lmat = jnp.where(keep, jnp.exp(csc - csr), 0.0)
            m_rows.append((s_cb * lmat * dtr).astype(BF16))
            tot = csr[:, q - 1:q] if d == 0 else csr[:, 0:1]
            w_rows.append((btg32 * (jnp.exp(tot - csr) * dtr)).astype(BF16))
            e_cols.append(jnp.exp(csc))
            tots.append(jnp.exp(tot))
        yd = _dot(jnp.concatenate(m_rows, axis=0), xg)
        ds = _dot(jnp.concatenate(w_rows, axis=0), xg)
        st = st_s[d, g]
        yo = _dot(cg, st.astype(BF16))
        y_g = jnp.zeros((q, gw), F32)
        st_new = jnp.zeros((ns, gw), F32)
        for hh in range(hpg):
            sel = lane_blk == hh
            y_g = jnp.where(sel, yd[hh * q:(hh + 1) * q, :] + yo * e_cols[hh], y_g)
            st_new = jnp.where(sel, st * tots[hh] + ds[hh * ns:(hh + 1) * ns, :], st_new)
        st_s[d, g] = st_new
        ys.append(y_g)
    y_s[pl.ds(r0, q), :] += jnp.concatenate(ys, axis=1)


def _ssd_body(xl_ref, xc_ref, dcl_ref, dcc_ref, drl_ref, drc_ref, cw_ref, cb_ref, dtbc_ref, alc_ref,
              dtbr_ref, alr_ref, dsk_ref, yl_ref, yc_ref,
              x_s, c_s, bt_s, csc_s, csr_s, dtr_s, st_s, y_s, *, n, ctx_len):
    st_s[...] = jnp.zeros_like(st_s)
    for L, xbc_ref, dtc_ref, dtr_ref, o_ref in ((ctx_len, xc_ref, dcc_ref, drc_ref, yc_ref),
                                                 (n, xl_ref, dcl_ref, drl_ref, yl_ref)):
        nc = L // SSM_CHUNK
        _ssd_prepare(L, xbc_ref, dtc_ref, dtr_ref, cw_ref, cb_ref, dtbc_ref, alc_ref, dtbr_ref, alr_ref,
                     x_s, c_s, bt_s, csc_s, csr_s, dtr_s)
        args = (x_s, c_s, bt_s, csc_s, csr_s, dtr_s, st_s, y_s)

        y_s[0:L, :] = jnp.zeros((L, SSM_INNER), F32)

        def both(i, carry, nc=nc, args=args):
            _ssd_chunk(i, 0, nc, *args)
            _ssd_chunk(nc - 1 - i, 1, nc, *args)
            return carry

        lax.fori_loop(0, nc, both, 0)
        o_ref[...] = (y_s[0:L, :] + x_s[0:L, :].astype(F32) * dsk_ref[...]).astype(BF16)


def _ssd(proj, dt, batch, n, ctx_len, p):
    conv_w, conv_b, dt_bias, a_log, d_skip = p
    nh = SSM_HEADS
    ncl, ncc = n // SSM_CHUNK, ctx_len // SSM_CHUNK
    cch = SSM_CONV_CH
    xbc_blk = (SSM_INNER * 3 + SSM_INNER) // cch
    dt16 = dt[:, :, :2 * nh]
    dtr_l = dt16[:batch].transpose(0, 2, 1).reshape(batch, 2 * nh, ncl, SSM_CHUNK)
    dtr_c = dt16[batch].reshape(batch, ctx_len, 2 * nh).transpose(0, 2, 1).reshape(batch, 2 * nh, ncc, SSM_CHUNK)
    pad = LANES - 2 * nh
    dtb_col = jnp.pad(dt_bias.reshape(1, 2 * nh), ((0, 0), (0, pad)))
    al_col = jnp.pad(a_log.reshape(1, 2 * nh), ((0, 0), (0, pad)))
    dtb_row = jnp.broadcast_to(dt_bias.reshape(2 * nh, 1, 1), (2 * nh, 1, SSM_CHUNK))
    al_row = jnp.broadcast_to(a_log.reshape(2 * nh, 1, 1), (2 * nh, 1, SSM_CHUNK))
    dsk = jnp.repeat(d_skip, SSM_HEADDIM)[None, :]
    full = lambda shape: pl.BlockSpec(shape, lambda b: (0,) * len(shape))
    return pl.pallas_call(
        functools.partial(_ssd_body, n=n, ctx_len=ctx_len),
        grid=(batch,),
        in_specs=[pl.BlockSpec((None, n, cch), lambda b: (b, 0, xbc_blk)),
                  pl.BlockSpec((None, ctx_len, cch), lambda b: (batch, b, xbc_blk)),
                  pl.BlockSpec((None, n, LANES), lambda b: (b, 0, 0)),
                  pl.BlockSpec((None, ctx_len, LANES), lambda b: (batch, b, 0)),
                  pl.BlockSpec((None, 2 * nh, ncl, SSM_CHUNK), lambda b: (b, 0, 0, 0)),
                  pl.BlockSpec((None, 2 * nh, ncc, SSM_CHUNK), lambda b: (b, 0, 0, 0)),
                  full((3, cch)), full((1, cch)), full((1, LANES)), full((1, LANES)),
                  full((2 * nh, 1, SSM_CHUNK)), full((2 * nh, 1, SSM_CHUNK)), full((1, SSM_INNER))],
        out_specs=[pl.BlockSpec((None, n, SSM_INNER), lambda b: (b, 0, 0)),
                   pl.BlockSpec((None, ctx_len, SSM_INNER), lambda b: (b, 0, 0))],
        out_shape=[jax.ShapeDtypeStruct((batch, n, SSM_INNER), BF16),
                   jax.ShapeDtypeStruct((batch, ctx_len, SSM_INNER), BF16)],
        scratch_shapes=[pltpu.VMEM((n, SSM_INNER), BF16),
                        pltpu.VMEM((n, SSM_GROUPS * SSM_STATE), BF16),
                        pltpu.VMEM((ncl, SSM_GROUPS * SSM_STATE, SSM_CHUNK), BF16),
                        pltpu.VMEM((n, LANES), F32),
                        pltpu.VMEM((2 * nh * ncl, SSM_CHUNK), F32),
                        pltpu.VMEM((2 * nh * ncl, SSM_CHUNK), F32),
                        pltpu.VMEM((2, SSM_GROUPS, SSM_STATE, SSM_INNER // SSM_GROUPS), F32),
                        pltpu.VMEM((n, SSM_INNER), F32)],
        compiler_params=_cparams("arbitrary"),
        name="ssd",
    )(proj, proj, dt, dt, dtr_l, dtr_c, conv_w, conv_b[None, :], dtb_col, al_col, dtb_row, al_row, dsk)


def _mix_out_even_body(yhl_ref, yhc_ref, ysl_ref, ysc_ref, z_ref, ng_ref, w_ref, x_ref, c_ref, mod_ref, pg_ref, o_ref,
                       *, batch):
    is_ctx = pl.program_id(0) == batch
    h = jnp.where(is_ctx, c_ref[...], x_ref[...])
    y_hy = jnp.where(is_ctx, yhc_ref[...], yhl_ref[...])
    ys = jnp.where(is_ctx, ysc_ref[...], ysl_ref[...]).astype(F32) * _silu(z_ref[...].astype(F32))
    gw = SSM_INNER // SSM_GROUPS
    parts = []
    for g in range(SSM_GROUPS):
        yg = ys[:, g * gw:(g + 1) * gw]
        parts.append(yg * lax.rsqrt(jnp.mean(yg * yg, axis=-1, keepdims=True) + EPS))
    y_ssm = (jnp.concatenate(parts, axis=1) * ng_ref[...]).astype(BF16)
    m = _dot(y_hy, w_ref[0:HY_CH, :]) + _dot(y_ssm, w_ref[HY_CH:, :])
    o_ref[...] = h + mod_ref[2:3, :] * _rms(m, pg_ref[...])


def _mix_out_even(y_hy_l, y_hy_c, y_ss_l, y_ss_c, proj, norm_g, w_out, x, ctx3, mod, post_g, tm):
    batch, n, d = x.shape
    G = batch + 1
    c = HY_CH
    z_blk = (3 * HY_CH) // SSM_INNER
    lat_c, ctx_c = _lat_ctx_specs(batch, n // tm, tm, c)
    lat_d, ctx_d = _lat_ctx_specs(batch, n // tm, tm, d)
    return pl.pallas_call(
        functools.partial(_mix_out_even_body, batch=batch),
        grid=(G, n // tm),
        in_specs=[lat_c, ctx_c, lat_c, ctx_c,
                  pl.BlockSpec((None, tm, SSM_INNER), lambda g, t: (g, t, z_blk)),
                  pl.BlockSpec((1, SSM_INNER), lambda g, t: (0, 0)),
                  pl.BlockSpec((2 * c, d), lambda g, t: (0, 0)),
                  lat_d, ctx_d,
                  pl.BlockSpec((None, 6, d), lambda g, t: (g, 0, 0)),
                  pl.BlockSpec((1, d), lambda g, t: (0, 0))],
        out_specs=pl.BlockSpec((None, tm, d), lambda g, t: (g, t, 0)),
        out_shape=jax.ShapeDtypeStruct((G, n, d), F32),
        compiler_params=_cparams("arbitrary", "arbitrary"),
        name="mix_out_even",
    )(y_hy_l, y_hy_c.reshape(1, n, c), y_ss_l, y_ss_c.reshape(1, n, c), proj, norm_g, w_out, x, ctx3, mod, post_g)


def _mix_out_odd_body(a_ref, b_ref, w_ref, h_ref, mod_ref, pg_ref, o_ref):
    half = a_ref.shape[-1]
    m = _dot(a_ref[...], w_ref[0:half, :]) + _dot(b_ref[...], w_ref[half:, :])
    o_ref[...] = h_ref[...] + mod_ref[2:3, :] * _rms(m, pg_ref[...])


def _mix_out_odd(o_a, o_b, w_out, h_all, mod, post_g, tm):
    batch, n, c = o_a.shape
    d = h_all.shape[-1]
    return pl.pallas_call(
        _mix_out_odd_body,
        grid=(batch, n // tm),
        in_specs=[pl.BlockSpec((None, tm, c), lambda g, t: (g, t, 0)),
                  pl.BlockSpec((None, tm, c), lambda g, t: (g, t, 0)),
                  pl.BlockSpec((2 * c, d), lambda g, t: (0, 0)),
                  pl.BlockSpec((None, tm, d), lambda g, t: (g, t, 0)),
                  pl.BlockSpec((None, 6, d), lambda g, t: (g, 0, 0)),
                  pl.BlockSpec((1, d), lambda g, t: (0, 0))],
        out_specs=pl.BlockSpec((None, tm, d), lambda g, t: (g, t, 0)),
        out_shape=jax.ShapeDtypeStruct((batch, n, d), F32),
        compiler_params=_cparams("arbitrary", "arbitrary"),
        name="mix_out_odd",
    )(o_a, o_b, w_out, h_all, mod, post_g)


def _ffn_body(h_ref, mod_ref, g_ref, wg_ref, wu_ref, wd_ref, pg_ref, o_ref, *, fchunk):
    h = h_ref[...]
    a = _norm_mod(h, g_ref[...], mod_ref[3:4, :], mod_ref[4:5, :]).astype(BF16)
    ff = wg_ref.shape[1]
    f = jnp.zeros(h.shape, F32)
    for j in range(0, ff, fchunk):
        gate = _dot(a, wg_ref[:, j:j + fchunk])
        up = _dot(a, wu_ref[:, j:j + fchunk])
        f = f + _dot((_silu(gate) * up).astype(BF16), wd_ref[j:j + fchunk, :])
    o_ref[...] = h + mod_ref[5:6, :] * _rms(f, pg_ref[...])


def _ffn(h_all, mod, g_pre, wg, wu, wd, post_g, tm):
    G, n, d = h_all.shape
    ff = wg.shape[1]
    return pl.pallas_call(
        functools.partial(_ffn_body, fchunk=256),
        grid=(G, n // tm),
        in_specs=[pl.BlockSpec((None, tm, d), lambda g, t: (g, t, 0)),
                  pl.BlockSpec((None, 6, d), lambda g, t: (g, 0, 0)),
                  pl.BlockSpec((1, d), lambda g, t: (0, 0)),
                  pl.BlockSpec((d, ff), lambda g, t: (0, 0), pipeline_mode=pl.Buffered(1)),
                  pl.BlockSpec((d, ff), lambda g, t: (0, 0), pipeline_mode=pl.Buffered(1)),
                  pl.BlockSpec((ff, d), lambda g, t: (0, 0), pipeline_mode=pl.Buffered(1)),
                  pl.BlockSpec((1, d), lambda g, t: (0, 0))],
        out_specs=pl.BlockSpec((None, tm, d), lambda g, t: (g, t, 0)),
        out_shape=jax.ShapeDtypeStruct((G, n, d), F32),
        compiler_params=_cparams("arbitrary", "arbitrary"),
        name="ffn",
    )(h_all, mod, g_pre, wg, wu, wd, post_g)


def _rope(x, cos, sin_a, sin_b, half):
    return x * cos + pltpu.roll(x, LANES - half, axis=1) * sin_a + pltpu.roll(x, half, axis=1) * sin_b


def _odd_in_body(h_ref, mod_ref, g_ref, w_ref, qg_ref, wq_ref, kg_ref, wk_ref, wv_ref, rp_ref,
                 qm_ref, km_ref, vm_ref, qw_ref, kw_ref, vw_ref, *, batch):
    a = _norm_mod(h_ref[...], g_ref[...], mod_ref[0:1, :], mod_ref[1:2, :]).astype(BF16)
    ql, kl = MLA_Q_LORA, MLA_KV_LORA
    o_qg = ql + kl
    o_kg = o_qg + GQA_HEADS * HEAD_DIM
    o_vg = o_kg + GQA_KV * HEAD_DIM
    o_kr = o_vg + GQA_KV * HEAD_DIM
    is_ctx = pl.program_id(0) == batch
    tab = lambda j, ident: jnp.where(is_ctx, ident, rp_ref[:, j * LANES:(j + 1) * LANES])
    cm, s1m, s2m = tab(0, 1.0), tab(1, 0.0), tab(2, 0.0)
    cg, s1g, s2g = tab(3, 1.0), tab(4, 0.0), tab(5, 0.0)
    mla_scale = (MLA_NOPE + MLA_ROPE) ** -0.5
    gqa_scale = HEAD_DIM ** -0.5
    q_lat = _rms(_dot(a, w_ref[:, 0:ql]), qg_ref[...]).astype(BF16)
    q = _dot(q_lat, wq_ref[...])
    kv_lat = _rms(_dot(a, w_ref[:, ql:o_qg]), kg_ref[...]).astype(BF16)
    k = _dot(kv_lat, wk_ref[...])
    vm_ref[...] = _dot(kv_lat, wv_ref[...]).astype(BF16)
    k_rope = _rope(_dot(a, w_ref[:, o_kr:o_kr + LANES]), cm, s1m, s2m, MLA_ROPE // 2)
    for hd in range(MLA_HEADS):
        sl = slice(hd * LANES, (hd + 1) * LANES)
        qm_ref[:, sl] = (_rope(q[:, sl], cm, s1m, s2m, MLA_ROPE // 2) * mla_scale).astype(BF16)
        km_ref[:, sl] = (k[:, sl] + k_rope).astype(BF16)
    qg = _dot(a, w_ref[:, o_qg:o_kg])
    for blk in range(GQA_HEADS * HEAD_DIM // LANES):
        sl = slice(blk * LANES, (blk + 1) * LANES)
        qw_ref[:, sl] = (_rope(qg[:, sl], cg, s1g, s2g, HEAD_DIM // 2) * gqa_scale).astype(BF16)
    kw_ref[...] = _rope(_dot(a, w_ref[:, o_kg:o_vg]), cg, s1g, s2g, HEAD_DIM // 2).astype(BF16)
    vw_ref[...] = _dot(a, w_ref[:, o_vg:o_kr]).astype(BF16)


def _rope_tables(n):
    rows = (jnp.arange(n) // GRID_W).astype(F32)
    cols = (jnp.arange(n) % GRID_W).astype(F32)

    def cs(rot_dim):
        nf = rot_dim // 4
        inv = ROPE_BASE ** (-jnp.arange(nf, dtype=F32) / nf)
        ang = jnp.concatenate([rows[:, None] * inv, cols[:, None] * inv], axis=-1)
        return jnp.cos(ang), jnp.sin(ang)

    hm, hg = MLA_ROPE // 2, HEAD_DIM // 2
    c_m, s_m = cs(MLA_ROPE)
    c_g, s_g = cs(HEAD_DIM)
    o_sm, o_cg, o_sg, o_one = hm, 2 * hm, 2 * hm + hg, 2 * hm + 2 * hg
    compact = jnp.concatenate([c_m, s_m, c_g, s_g, jnp.ones((n, LANES - o_one), F32)], axis=1)
    spread = np.zeros((LANES, 6 * LANES), np.float32)
    for j in range(LANES):
        in_rope = MLA_NOPE <= j < MLA_NOPE + MLA_ROPE
        r = (j - MLA_NOPE) % hm
        spread[r if in_rope else o_one, j] = 1.0
        if MLA_NOPE <= j < MLA_NOPE + hm:
            spread[o_sm + r, LANES + j] = -1.0
        if MLA_NOPE + hm <= j < MLA_NOPE + MLA_ROPE:
            spread[o_sm + r, 2 * LANES + j] = 1.0
        spread[o_cg + j % hg, 3 * LANES + j] = 1.0
        if (j // hg) % 2 == 0:
            spread[o_sg + j % hg, 4 * LANES + j] = -1.0
        else:
            spread[o_sg + j % hg, 5 * LANES + j] = 1.0
    return jnp.dot(compact, jnp.asarray(spread), precision=HIGHEST)


def _odd_in(h_all, mod, g_pre, w_in, q_norm_g, w_q_up, kv_norm_g, w_kv_up, tm):
    G, n, d = h_all.shape
    batch = G - 1
    ql, kl, rp = MLA_Q_LORA, MLA_KV_LORA, MLA_ROPE
    gq, gk = GQA_HEADS * HEAD_DIM, GQA_KV * HEAD_DIM
    o = [0, ql, ql + kl, ql + kl + rp, ql + kl + rp + gq, ql + kl + rp + gq + gk]
    kr_cols = jnp.pad(w_in[:, o[2]:o[3]], ((0, 0), (MLA_NOPE, LANES - MLA_NOPE - rp)))
    w = jnp.concatenate([w_in[:, o[0]:o[2]], w_in[:, o[3]:], kr_cols], axis=1).astype(BF16)
    hq = MLA_NOPE + MLA_ROPE
    wq = jnp.pad(w_q_up.reshape(ql, MLA_HEADS, hq), ((0, 0), (0, 0), (0, LANES - hq)))
    wq = wq.reshape(ql, MLA_HEADS * LANES).astype(BF16)
    wkv = w_kv_up.reshape(kl, MLA_HEADS, MLA_NOPE + MLA_V)
    wk = jnp.pad(wkv[:, :, :MLA_NOPE], ((0, 0), (0, 0), (0, LANES - MLA_NOPE)))
    wk = wk.reshape(kl, MLA_HEADS * LANES).astype(BF16)
    wv = wkv[:, :, MLA_NOPE:].reshape(kl, MLA_HEADS * MLA_V).astype(BF16)
    ropes = _rope_tables(n)
    nw = w.shape[1]
    full = lambda shape: pl.BlockSpec(shape, lambda g, t: (0,) * len(shape))
    blk = lambda width: pl.BlockSpec((None, tm, width), lambda g, t: (g, t, 0))
    widths = [MLA_HEADS * LANES, MLA_HEADS * LANES, MLA_HEADS * MLA_V, gq, gk, gk]
    return pl.pallas_call(
        functools.partial(_odd_in_body, batch=batch),
        grid=(G, n // tm),
        in_specs=[blk(d), pl.BlockSpec((None, 6, d), lambda g, t: (g, 0, 0)), full((1, d)), full((d, nw)),
                  full((1, ql)), full((ql, MLA_HEADS * LANES)), full((1, kl)), full((kl, MLA_HEADS * LANES)),
                  full((kl, MLA_HEADS * MLA_V)),
                  pl.BlockSpec((tm, 6 * LANES), lambda g, t: (t, 0))],
        out_specs=[blk(wd) for wd in widths],
        out_shape=[jax.ShapeDtypeStruct((G, n, wd), BF16) for wd in widths],
        compiler_params=_cparams("arbitrary", "arbitrary"),
        name="odd_in",
    )(h_all, mod, g_pre, w, q_norm_g[None, :], wq, kv_norm_g[None, :], wk, wv, ropes)


def _mla_body(q_ref, kl_ref, kc_ref, vl_ref, vc_ref, o_ref, *, hps):
    lane = lax.broadcasted_iota(jnp.int32, (o_ref.shape[0], LANES), 1)
    for pr in range(hps // 2):
        vs = slice(pr * LANES, (pr + 1) * LANES)
        v_l = jnp.concatenate([vl_ref[:, vs], jnp.ones((vl_ref.shape[0], LANES), BF16)], axis=1)
        v_c = jnp.concatenate([vc_ref[:, vs], jnp.ones((vc_ref.shape[0], LANES), BF16)], axis=1)
        outs = []
        for hd in (2 * pr, 2 * pr + 1):
            sl = slice(hd * LANES, (hd + 1) * LANES)
            q = q_ref[:, sl]
            s_l = _dot_nt(q, kl_ref[:, sl])
            s_c = _dot_nt(q, kc_ref[:, sl])
            m = jnp.maximum(jnp.max(s_l, axis=-1, keepdims=True), jnp.max(s_c, axis=-1, keepdims=True))
            p_l = jnp.exp((s_l - m).astype(BF16))
            p_c = jnp.exp((s_c - m).astype(BF16))
            pv = _dot(p_l, v_l) + _dot(p_c, v_c)
            outs.append(pv[:, :LANES] / pv[:, LANES:])
        o_ref[:, vs] = jnp.where(lane < MLA_V, outs[0], outs[1]).astype(BF16)


def _mla_attn(q, k, v, batch, n, ctx_len, tq, hps):
    return pl.pallas_call(
        functools.partial(_mla_body, hps=hps),
        grid=(batch, MLA_HEADS // hps, n // tq),
        in_specs=[pl.BlockSpec((None, tq, hps * LANES), lambda b, h, i: (b, i, h)),
                  pl.BlockSpec((None, n, hps * LANES), lambda b, h, i: (b, 0, h)),
                  pl.BlockSpec((None, ctx_len, hps * LANES), lambda b, h, i: (batch, b, h)),
                  pl.BlockSpec((None, n, hps * MLA_V), lambda b, h, i: (b, 0, h)),
                  pl.BlockSpec((None, ctx_len, hps * MLA_V), lambda b, h, i: (batch, b, h))],
        out_specs=pl.BlockSpec((None, tq, hps * MLA_V), lambda b, h, i: (b, i, h)),
        out_shape=jax.ShapeDtypeStruct((batch, n, MLA_HEADS * MLA_V), BF16),
        compiler_params=_cparams("arbitrary", "arbitrary", "arbitrary"),
        name="mla_attn",
    )(q, k, k, v, v)


def _dup_half(x, kv):
    lane = lax.broadcasted_iota(jnp.int32, x.shape, 1)
    swapped = pltpu.roll(x, HEAD_DIM, axis=1)
    lo = lane < HEAD_DIM
    return jnp.where(lo, x, swapped) if kv == 0 else jnp.where(lo, swapped, x)


def _win_body(sink_ref, q_ref, kp_ref, kc_ref, kn_ref, kx_ref, vp_ref, vc_ref, vn_ref, vx_ref, o_ref, bias_ref,
              *, tq):
    i = pl.program_id(1)
    nb = pl.num_programs(1)
    w = WINDOW
    g = GQA_HEADS // GQA_KV
    n_ctx = kx_ref.shape[0]
    n_key = 2 * w + tq + n_ctx

    @pl.when((pl.program_id(0) == 0) & (i == 0))
    def _():
        r = lax.broadcasted_iota(jnp.int32, (g * tq, n_key), 0) % tq
        c = lax.broadcasted_iota(jnp.int32, (g * tq, n_key), 1)
        rel = c - w - r
        visible = (jnp.abs(rel) <= w) | (c >= 2 * w + tq)
        bias_ref[...] = jnp.where(visible, 0.0, -jnp.inf)

    edge = jnp.full((1, LANES), -jnp.inf, F32)
    zero = jnp.zeros((1, LANES), F32)
    bias_prev = jnp.where(i > 0, zero, edge)
    bias_next = jnp.where(i < nb - 1, zero, edge)
    lane = lax.broadcasted_iota(jnp.int32, (tq, LANES), 1)
    lo = lane < HEAD_DIM
    for kv in range(GQA_KV):
        rows = []
        for hh in range(g):
            hd = kv * g + hh
            blk = q_ref[:, (hd // 2) * LANES:(hd // 2 + 1) * LANES].astype(F32)
            rows.append(jnp.where(lo if hd % 2 == 0 else ~lo, blk, 0.0).astype(BF16))
        q = jnp.concatenate(rows, axis=0)
        k_all = jnp.concatenate([_dup_half(ref[...].astype(F32), kv).astype(BF16)
                                 for ref in (kp_ref, kc_ref, kn_ref, kx_ref)], axis=0)
        v_all = jnp.concatenate([_dup_half(ref[...].astype(F32), kv).astype(BF16)
                                 for ref in (vp_ref, vc_ref, vn_ref, vx_ref)], axis=0)
        v_all = jnp.concatenate([v_all, jnp.ones((n_key, LANES), BF16)], axis=1)
        s = _dot_nt(q, k_all) + bias_ref[...]
        s = jnp.concatenate([s[:, :w] + bias_prev, s[:, w:w + tq], s[:, w + tq:2 * w + tq] + bias_next,
                             s[:, 2 * w + tq:]], axis=1)
        hrow = lax.broadcasted_iota(jnp.int32, (g * tq, 1), 0) // tq
        sk = jnp.zeros((g * tq, 1), F32)
        for hh in range(g):
            sk = jnp.where(hrow == hh, sink_ref[kv * g + hh], sk)
        m = jnp.maximum(_row_max(s), sk)
        acc = _dot(jnp.exp((s - m).astype(BF16)), v_all)
        res = acc[:, :LANES] / (acc[:, LANES:] + jnp.exp(sk - m))
        for pr in range(g // 2):
            blk = (kv * g) // 2 + pr
            o_ref[:, blk * LANES:(blk + 1) * LANES] = jnp.where(
                lo, res[(2 * pr) * tq:(2 * pr + 1) * tq, :], res[(2 * pr + 1) * tq:(2 * pr + 2) * tq, :]).astype(BF16)


def _win_attn(q, k, v, sink, batch, n, ctx_len, tq):
    w = WINDOW
    per = tq // w
    gq, gk = GQA_HEADS * HEAD_DIM, GQA_KV * HEAD_DIM
    prev = lambda b, i: (b, jnp.maximum(i * per - 1, 0), 0)
    cur = lambda b, i: (b, i, 0)
    nxt = lambda b, i: (b, jnp.minimum((i + 1) * per, n // w - 1), 0)
    ctx = lambda b, i: (batch, b, 0)
    kspec = lambda rows, im: pl.BlockSpec((None, rows, gk), im)
    return pl.pallas_call(
        functools.partial(_win_body, tq=tq),
        grid=(batch, n // tq),
        in_specs=[pl.BlockSpec(memory_space=pltpu.SMEM),
                  pl.BlockSpec((None, tq, gq), cur),
                  kspec(w, prev), kspec(tq, cur), kspec(w, nxt), kspec(ctx_len, ctx),
                  kspec(w, prev), kspec(tq, cur), kspec(w, nxt), kspec(ctx_len, ctx)],
        out_specs=pl.BlockSpec((None, tq, gq), cur),
        out_shape=jax.ShapeDtypeStruct((batch, n, gq), BF16),
        scratch_shapes=[pltpu.VMEM((GQA_HEADS // GQA_KV * tq, 2 * w + tq + ctx_len), F32)],
        compiler_params=_cparams("arbitrary", "arbitrary"),
        name="win_attn",
    )(sink, q, k, k, k, k, v, v, v, v)


def _route_body(h_ref, mod_ref, g_ref, r_ref, a_ref, info_ref, cnt_ref, rows_ref, carry_ref, tri_ref):
    first = (pl.program_id(0) == 0) & (pl.program_id(1) == 0)

    @pl.when(first)
    def _():
        carry_ref[...] = jnp.zeros_like(carry_ref)
        ri = lax.broadcasted_iota(jnp.int32, tri_ref.shape, 0)
        ci = lax.broadcasted_iota(jnp.int32, tri_ref.shape, 1)
        tri_ref[...] = (ci < ri).astype(BF16)

    a = _norm_mod(h_ref[...], g_ref[...], mod_ref[3:4, :], mod_ref[4:5, :])
    a_ref[...] = a
    tm = a.shape[0]
    a_hi = a.astype(BF16)
    a_lo = (a - a_hi.astype(F32)).astype(BF16)
    part = _dot(jnp.concatenate([a_hi, a_lo], axis=0), r_ref[...])
    part = part[:tm, :] + part[tm:, :]
    logits = part + pltpu.roll(part, LANES - N_EXPERTS, axis=1)
    lane = lax.broadcasted_iota(jnp.int32, logits.shape, 1)
    neg = -jnp.inf
    logits = jnp.where(lane < N_EXPERTS, logits, neg)
    m1 = jnp.max(logits, axis=-1, keepdims=True)
    i1 = jnp.min(jnp.where(logits == m1, lane, LANES), axis=-1, keepdims=True)
    rest = jnp.where(lane == i1, neg, logits)
    m2 = jnp.max(rest, axis=-1, keepdims=True)
    i2 = jnp.min(jnp.where(rest == m2, lane, LANES), axis=-1, keepdims=True)
    e2 = jnp.exp(m2 - m1)
    w1 = 1.0 / (1.0 + e2)
    w2 = e2 / (1.0 + e2)
    chosen = ((lane == i1) | (lane == i2)).astype(F32)
    before = _dot(tri_ref[...], chosen.astype(BF16)) + carry_ref[...]
    p1 = jnp.sum(jnp.where(lane == i1, before, 0.0), axis=-1, keepdims=True)
    p2 = jnp.sum(jnp.where(lane == i2, before, 0.0), axis=-1, keepdims=True)
    carry_ref[...] += jnp.sum(chosen, axis=0, keepdims=True)
    cnt_ref[...] = carry_ref[...]
    vals = (i1.astype(F32), i2.astype(F32), w1, w2, p1, p2)
    info = jnp.zeros(logits.shape, F32)
    for j, v in enumerate(vals):
        info = jnp.where(lane == j, v, info)
    info_ref[...] = info
    src = (0, 1, 4, 5)
    sr = lax.broadcasted_iota(jnp.int32, (SUBLANES, LANES), 0)
    sl = lax.broadcasted_iota(jnp.int32, (SUBLANES, LANES), 1)
    sel = jnp.zeros((SUBLANES, LANES), F32)
    for r, c in enumerate(src):
        sel = jnp.where((sr == r) & (sl == c), 1.0, sel)
    rows_ref[...] = lax.dot_general(sel, info, (((1,), (1,)), ((), ())), precision=HIGHEST,
                                    preferred_element_type=F32)


def _route(h_lat, mod, g_pre, router, tm):
    batch, n, d = h_lat.shape
    r_hi = router.astype(BF16)
    r_lo = (router - r_hi.astype(F32)).astype(BF16)
    rp = jnp.pad(jnp.concatenate([r_hi, r_lo], axis=1), ((0, 0), (0, LANES - 2 * N_EXPERTS)))
    nt = n // tm
    return pl.pallas_call(
        _route_body,
        grid=(batch, nt),
        in_specs=[pl.BlockSpec((None, tm, d), lambda g, t: (g, t, 0)),
                  pl.BlockSpec((None, 6, d), lambda g, t: (g, 0, 0)),
                  pl.BlockSpec((1, d), lambda g, t: (0, 0)),
                  pl.BlockSpec((d, LANES), lambda g, t: (0, 0))],
        out_specs=[pl.BlockSpec((tm, d), lambda g, t: (g * nt + t, 0)),
                   pl.BlockSpec((tm, LANES), lambda g, t: (g * nt + t, 0)),
                   pl.BlockSpec((1, LANES), lambda g, t: (0, 0)),
                   pl.BlockSpec((SUBLANES, tm), lambda g, t: (0, g * nt + t))],
        out_shape=[jax.ShapeDtypeStruct((batch * n, d), F32),
                   jax.ShapeDtypeStruct((batch * n, LANES), F32),
                   jax.ShapeDtypeStruct((1, LANES), F32),
                   jax.ShapeDtypeStruct((SUBLANES, batch * n), F32)],
        scratch_shapes=[pltpu.VMEM((1, LANES), F32), pltpu.VMEM((tm, tm), BF16)],
        compiler_params=_cparams("arbitrary", "arbitrary"),
        name="moe_route",
    )(h_lat, mod, g_pre, rp)


def _scatter_body(slot_ref, fill_ref, a_ref, xs_ref, zero_ref, sem, zsem, *, ts, n_tok, n_fill):
    base = pl.program_id(0) * ts

    @pl.when(pl.program_id(0) == 0)
    def _():
        zero_ref[...] = jnp.zeros_like(zero_ref)

        def fill(t):
            row = pl.multiple_of(fill_ref[t] * MOE_TILE, MOE_TILE)
            return pltpu.make_async_copy(zero_ref, xs_ref.at[pl.ds(row, MOE_TILE), :], zsem)

        for t in range(n_fill):
            @pl.when(fill_ref[t] >= 0)
            def _(t=t):
                fill(t).start()
        for t in range(n_fill):
            @pl.when(fill_ref[t] >= 0)
            def _(t=t):
                fill(t).wait()

    def issue(i, carry):
        for u in range(SUBLANES):
            for k in range(2):
                slot = slot_ref[k * n_tok + base + i * SUBLANES + u]
                pltpu.make_async_copy(a_ref.at[i, pl.ds(u, 1), :], xs_ref.at[pl.ds(slot, 1), :],
                                      sem).start(priority=k)
        return carry

    lax.fori_loop(0, ts // SUBLANES, issue, 0)
    for k in range(2):
        pltpu.make_async_copy(xs_ref.at[pl.ds(0, ts), :], xs_ref.at[pl.ds(0, ts), :], sem).wait()


def _scatter(slots, fill_tiles, a, n_slots, ts):
    n_tok, d = a.shape
    return pl.pallas_call(
        functools.partial(_scatter_body, ts=ts, n_tok=n_tok, n_fill=fill_tiles.shape[0]),
        grid_spec=pltpu.PrefetchScalarGridSpec(
            num_scalar_prefetch=2,
            grid=(n_tok // ts,),
            in_specs=[pl.BlockSpec((ts // SUBLANES, SUBLANES, d), lambda i, *_: (i, 0, 0))],
            out_specs=pl.BlockSpec(memory_space=pl.ANY),
            scratch_shapes=[pltpu.VMEM((MOE_TILE, d), F32), pltpu.SemaphoreType.DMA(()),
                            pltpu.SemaphoreType.DMA(())]),
        out_shape=jax.ShapeDtypeStruct((n_slots, d), F32),
        compiler_params=_cparams("arbitrary"),
        name="moe_scatter",
    )(slots, fill_tiles, a.reshape(n_tok // SUBLANES, SUBLANES, d))


def _experts_body(te_ref, tv_ref, x_ref, wg_ref, wu_ref, wd_ref, y_ref):
    del te_ref
    j = pl.program_id(0)
    f = pl.program_id(1)
    valid = tv_ref[j]

    @pl.when(f == 0)
    def _():
        y_ref[...] = jnp.zeros_like(y_ref)

    n_sub = (valid + MOE_SUB - 1) // MOE_SUB
    for ns in range(1, MOE_TILE // MOE_SUB + 1):
        rows = slice(0, ns * MOE_SUB)

        @pl.when(n_sub == ns)
        def _(rows=rows):
            x = x_ref[rows, :].astype(BF16)
            gate = _dot(x, wg_ref[...].astype(BF16))
            up = _dot(x, wu_ref[...].astype(BF16))
            hid = (_silu(gate) * up).astype(BF16)
            y_ref[rows, :] += _dot(hid, wd_ref[...].astype(BF16))


def _experts(tile_expert, tile_valid, xs, wg, wu, wd):
    n_slots, d = xs.shape
    ff = wg.shape[-1]
    n_tiles = n_slots // MOE_TILE
    tf = MOE_FCHUNK
    chunk = lambda j, f, tv: jnp.where(tv[j] > 0, f, ff // tf - 1)
    return pl.pallas_call(
        _experts_body,
        grid_spec=pltpu.PrefetchScalarGridSpec(
            num_scalar_prefetch=2,
            grid=(n_tiles, ff // tf),
            in_specs=[pl.BlockSpec((MOE_TILE, d), lambda j, f, te, tv: (j, 0)),
                      pl.BlockSpec((None, d, tf), lambda j, f, te, tv: (te[j], 0, chunk(j, f, tv))),
                      pl.BlockSpec((None, d, tf), lambda j, f, te, tv: (te[j], 0, chunk(j, f, tv))),
                      pl.BlockSpec((None, tf, d), lambda j, f, te, tv: (te[j], chunk(j, f, tv), 0))],
            out_specs=pl.BlockSpec((MOE_TILE, d), lambda j, f, te, tv: (j, 0))),
        out_shape=jax.ShapeDtypeStruct((n_slots, d), F32),
        compiler_params=_cparams("arbitrary", "arbitrary"),
        name="moe_experts",
    )(tile_expert, tile_valid, xs, wg, wu, wd)


def _combine_body(slot_ref, ys_ref, info_ref, h_ref, mod_ref, pg_ref, o_ref, buf_ref, sem, *, tc, n_tok):
    base = (pl.program_id(0) * pl.num_programs(1) + pl.program_id(1)) * tc

    def issue(i, carry):
        for u in range(SUBLANES):
            for k in range(2):
                slot = slot_ref[k * n_tok + base + i * SUBLANES + u]
                pltpu.make_async_copy(ys_ref.at[pl.ds(slot, 1), :], buf_ref.at[k, i, pl.ds(u, 1), :],
                                      sem).start(priority=k)
        return carry

    lax.fori_loop(0, tc // SUBLANES, issue, 0)
    for k in range(2):
        pltpu.make_async_copy(ys_ref.at[pl.ds(0, tc), :], ys_ref.at[pl.ds(0, tc), :], sem).wait()
    d = o_ref.shape[-1]
    f = info_ref[:, 2:3] * buf_ref[0].reshape(tc, d) + info_ref[:, 3:4] * buf_ref[1].reshape(tc, d)
    o_ref[...] = h_ref[...] + mod_ref[5:6, :] * _rms(f, pg_ref[...])


def _combine(slots, ys, info, h_lat, mod, post_g, tc):
    batch, n, d = h_lat.shape
    nt = n // tc
    return pl.pallas_call(
        functools.partial(_combine_body, tc=tc, n_tok=batch * n),
        grid_spec=pltpu.PrefetchScalarGridSpec(
            num_scalar_prefetch=1,
            grid=(batch, nt),
            in_specs=[pl.BlockSpec(memory_space=pl.ANY),
                      pl.BlockSpec((tc, LANES), lambda g, t, *_: (g * nt + t, 0)),
                      pl.BlockSpec((None, tc, d), lambda g, t, *_: (g, t, 0)),
                      pl.BlockSpec((None, 6, d), lambda g, t, *_: (g, 0, 0)),
                      pl.BlockSpec((1, d), lambda g, t, *_: (0, 0))],
            out_specs=pl.BlockSpec((None, tc, d), lambda g, t, *_: (g, t, 0)),
            scratch_shapes=[pltpu.VMEM((2, tc // SUBLANES, SUBLANES, d), F32), pltpu.SemaphoreType.DMA(())]),
        out_shape=jax.ShapeDtypeStruct((batch, n, d), F32),
        compiler_params=_cparams("arbitrary", "arbitrary"),
        name="moe_combine",
    )(slots, ys, info, h_lat, mod, post_g)


def _moe(h_lat, mod, g_pre, router, wg, wu, wd, post_g, tm):
    batch, n, d = h_lat.shape
    n_tok = batch * n
    a, info, counts, rt = _route(h_lat, mod, g_pre, router, tm)
    cnt = counts[0, :N_EXPERTS].astype(jnp.int32)
    padded = ((cnt + MOE_TILE - 1) // MOE_TILE) * MOE_TILE
    ends = jnp.cumsum(padded)
    starts = ends - padded
    seg = jnp.zeros((2, n_tok), jnp.int32)
    for e in range(N_EXPERTS):
        seg = jnp.where(rt[0:2] == e, starts[e], seg)
    slots = (seg + rt[2:4].astype(jnp.int32)).reshape(2 * n_tok)
    n_tiles = (2 * n_tok + N_EXPERTS * (MOE_TILE - 1)) // MOE_TILE
    tile_row = jnp.arange(n_tiles, dtype=jnp.int32) * MOE_TILE
    tile_expert = jnp.minimum(jnp.sum(tile_row[:, None] >= ends[None, :], axis=1), N_EXPERTS - 1).astype(jnp.int32)
    tile_valid = jnp.clip(cnt[tile_expert] - (tile_row - starts[tile_expert]), 0, MOE_TILE)
    tile_valid = jnp.where(tile_row < ends[-1], tile_valid, 0).astype(jnp.int32)
    last = jnp.where(padded > 0, ends // MOE_TILE - 1, -1)
    extra = ends[-1] // MOE_TILE + jnp.arange(n_tiles - 2 * n_tok // MOE_TILE, dtype=jnp.int32)
    fill_tiles = jnp.concatenate([last, jnp.where(extra < n_tiles, extra, -1)]).astype(jnp.int32)
    xs = _scatter(slots, fill_tiles, a, n_tiles * MOE_TILE, tm)
    ys = _experts(tile_expert, tile_valid, xs, wg, wu, wd)
    return _combine(slots, ys, info, h_lat, mod, post_g, tm)


def kernel(x, c, ctx, c_ctx, ada_w, ada_b, mix_pre_g, mix_post_g, ffn_pre_g, ffn_post_g, ev_w_in, ev_hy_conv_w, ev_hy_conv_b, ev_hy_filt_w1, ev_hy_filt_b1, ev_hy_filt_w2, ev_hy_filt_b2, ev_hy_filt_w3, ev_hy_freq, ev_hy_bias, ev_ssm_conv_w, ev_ssm_conv_b, ev_ssm_dt_bias, ev_ssm_a_log, ev_ssm_d, ev_ssm_norm_g, ev_w_out, ev_ffn_w_gate, ev_ffn_w_up, ev_ffn_w_down, od_w_in, od_mla_q_norm_g, od_mla_w_q_up, od_mla_kv_norm_g, od_mla_w_kv_up, od_gqa_sink, od_w_out, od_router, od_moe_w_gate, od_moe_w_up, od_moe_w_down):
    batch, n, d = x.shape
    ctx_len = ctx.shape[1]
    assert batch * ctx_len == n and ada_w.shape[0] == 2
    G = batch + 1
    tm = min(n, 512)
    tl = min(n, 1024)

    cond = jnp.concatenate([c, c_ctx[None, :], jnp.zeros((16 - G, d), F32)], axis=0)
    mods = _ada(cond, ada_w, ada_b).reshape(2, 16, 6, d)
    ctx3 = ctx.reshape(1, n, d)

    n_main = 3 * HY_CH + SSM_INNER + SSM_CONV_CH
    w_in = ev_w_in[0]
    w_main = w_in[:, :n_main].astype(BF16)
    w_dt = jnp.pad(w_in[:, n_main:], ((0, 0), (0, LANES - 2 * SSM_HEADS))).astype(BF16)
    proj, dt = _even_in(x, ctx3, mods[0], mix_pre_g[0][None, :], w_main, w_dt, tl)
    y_hy_l, y_hy_c = _hyena(proj, batch, n, ctx_len,
                            (ev_hy_conv_w[0], ev_hy_conv_b[0], ev_hy_filt_w1[0], ev_hy_filt_b1[0], ev_hy_filt_w2[0],
                             ev_hy_filt_b2[0], ev_hy_filt_w3[0], ev_hy_freq[0], ev_hy_bias[0]))
    y_ss_l, y_ss_c = _ssd(proj, dt, batch, n, ctx_len,
                          (ev_ssm_conv_w[0], ev_ssm_conv_b[0], ev_ssm_dt_bias[0], ev_ssm_a_log[0], ev_ssm_d[0]))
    h_all = _mix_out_even(y_hy_l, y_hy_c, y_ss_l, y_ss_c, proj, ev_ssm_norm_g[0][None, :],
                          ev_w_out[0].astype(BF16), x, ctx3, mods[0], mix_post_g[0][None, :], tl)
    h_all = _ffn(h_all, mods[0], ffn_pre_g[0][None, :], ev_ffn_w_gate[0].astype(BF16), ev_ffn_w_up[0].astype(BF16),
                 ev_ffn_w_down[0].astype(BF16), ffn_post_g[0][None, :], tl)

    q_m, k_m, v_m, q_w, k_w, v_w = _odd_in(h_all, mods[1], mix_pre_g[1][None, :], od_w_in[0], od_mla_q_norm_g[0],
                                           od_mla_w_q_up[0], od_mla_kv_norm_g[0], od_mla_w_kv_up[0], tl)
    o_mla = _mla_attn(q_m, k_m, v_m, batch, n, ctx_len, min(n, 512), 4)
    o_win = _win_attn(q_w, k_w, v_w, od_gqa_sink[0], batch, n, ctx_len, min(n, 2 * WINDOW))
    h_lat = _mix_out_odd(o_mla, o_win, od_w_out[0].astype(BF16), h_all, mods[1], mix_post_g[1][None, :], tl)
    return _moe(h_lat, mods[1], ffn_pre_g[1][None, :], od_router[0], od_moe_w_gate[0], od_moe_w_up[0],
                od_moe_w_down[0], ffn_post_g[1][None, :], tm)
```

```python
import functools
import math

import jax
import jax.numpy as jnp
import numpy as np
from jax import lax
from jax.experimental import pallas as pl
from jax.experimental.pallas import tpu as pltpu

F32 = jnp.float32
BF16 = jnp.bfloat16
HIGHEST = lax.Precision.HIGHEST

D_MODEL = 1024
GRID_W = 64
EPS = 1e-6
HEAD_DIM = 64
GROUP_WIDTH = D_MODEL // 2

HY_CH = GROUP_WIDTH
HY_BANDS = 16
HY_EMB = 2 * HY_BANDS + 1
HY_FILT_HID = 64
HY_FAST_DECAY = 0.3
HY_SLOW_DECAY = 1.5
HY_DECAY_TARGET = 1e-2

SSM_INNER = GROUP_WIDTH
SSM_HEADDIM = 64
SSM_HEADS = SSM_INNER // SSM_HEADDIM
SSM_GROUPS = 2
SSM_STATE = 128
SSM_CHUNK = 128
SSM_CONV_CH = SSM_INNER + 2 * SSM_GROUPS * SSM_STATE

MLA_HEADS = GROUP_WIDTH // HEAD_DIM
MLA_NOPE = 64
MLA_ROPE = 32
MLA_V = 64
MLA_Q_LORA = D_MODEL // 4
MLA_KV_LORA = D_MODEL // 8

GQA_HEADS = GROUP_WIDTH // HEAD_DIM
GQA_KV = GQA_HEADS // 4
WINDOW = 128
ROPE_BASE = 10000.0

D_FF = ((8 * D_MODEL // 3 + 127) // 128) * 128
N_EXPERTS = 8
D_FF_EXPERT = 7 * D_MODEL // 2

LANES = 128
SUBLANES = 8
DFT_SPLIT = 64
VMEM_LIMIT = 56 * 1024 * 1024
MOE_TILE = 1024
MOE_SUB = 256
MOE_FCHUNK = 512


def _cparams(*sem):
    return pltpu.CompilerParams(dimension_semantics=sem, vmem_limit_bytes=VMEM_LIMIT)


def _silu(x):
    return x / (1.0 + jnp.exp(-x))


def _rms(x, g):
    return x * lax.rsqrt(jnp.mean(x * x, axis=-1, keepdims=True) + EPS) * g


def _norm_mod(h, g, shift, scale):
    return _rms(h, g) * (1.0 + scale) + shift


def _dot(a, b):
    return jnp.dot(a, b, preferred_element_type=F32)


def _row_max(*blocks):
    tiles = [b[:, j:j + LANES] for b in blocks for j in range(0, b.shape[1], LANES)]
    m = tiles[0]
    for t in tiles[1:]:
        m = jnp.maximum(m, t)
    return jnp.max(m, axis=-1, keepdims=True)


def _dot_nt(a, b):
    return lax.dot_general(a, b, (((1,), (1,)), ((), ())), preferred_element_type=F32)


def _ada_body(c_ref, w_ref, b_ref, o_ref):
    s = _silu(c_ref[...]).astype(BF16)
    o_ref[0] = _dot(s, w_ref[0].astype(BF16)) + b_ref[0]


def _ada(cc, ada_w, ada_b):
    depth, d, n6 = ada_w.shape
    rows = cc.shape[0]
    tn = 1536
    return pl.pallas_call(
        _ada_body,
        grid=(depth, n6 // tn),
        in_specs=[pl.BlockSpec((rows, d), lambda l, j: (0, 0)),
                  pl.BlockSpec((1, d, tn), lambda l, j: (l, 0, j)),
                  pl.BlockSpec((1, 1, tn), lambda l, j: (l, 0, j))],
        out_specs=pl.BlockSpec((1, rows, tn), lambda l, j: (l, 0, j)),
        out_shape=jax.ShapeDtypeStruct((depth, rows, n6), F32),
        compiler_params=_cparams("arbitrary", "arbitrary"),
        name="ada",
    )(cc, ada_w, ada_b.reshape(depth, 1, n6))


def _lat_ctx_specs(batch, nt, tm, d):
    lat = pl.BlockSpec((None, tm, d), lambda g, t: (jnp.minimum(g, batch - 1), jnp.where(g < batch, t, nt - 1), 0))
    ctx = pl.BlockSpec((None, tm, d), lambda g, t: (0, jnp.where(g < batch, 0, t), 0))
    return lat, ctx


def _even_in_body(x_ref, c_ref, mod_ref, g_ref, w_ref, wdt_ref, o_ref, dt_ref, *, batch):
    h = jnp.where(pl.program_id(0) == batch, c_ref[...], x_ref[...])
    a = _norm_mod(h, g_ref[...], mod_ref[0:1, :], mod_ref[1:2, :]).astype(BF16)
    n_out = o_ref.shape[-1]
    for j in range(0, n_out, 512):
        o_ref[:, j:j + 512] = _dot(a, w_ref[:, j:j + 512]).astype(BF16)
    dt_ref[...] = _dot(a, wdt_ref[...])


def _even_in(x, ctx3, mod, g_pre, w_main, w_dt, tm):
    batch, n, d = x.shape
    G = batch + 1
    n_main = w_main.shape[1]
    lat_spec, ctx_spec = _lat_ctx_specs(batch, n // tm, tm, d)
    return pl.pallas_call(
        functools.partial(_even_in_body, batch=batch),
        grid=(G, n // tm),
        in_specs=[lat_spec, ctx_spec,
                  pl.BlockSpec((None, 6, d), lambda g, t: (g, 0, 0)),
                  pl.BlockSpec((1, d), lambda g, t: (0, 0)),
                  pl.BlockSpec((d, n_main), lambda g, t: (0, 0), pipeline_mode=pl.Buffered(1)),
                  pl.BlockSpec((d, LANES), lambda g, t: (0, 0))],
        out_specs=[pl.BlockSpec((None, tm, n_main), lambda g, t: (g, t, 0)),
                   pl.BlockSpec((None, tm, LANES), lambda g, t: (g, t, 0))],
        out_shape=[jax.ShapeDtypeStruct((G, n, n_main), BF16),
                   jax.ShapeDtypeStruct((G, n, LANES), F32)],
        compiler_params=_cparams("arbitrary", "arbitrary"),
        name="even_in",
    )(x, ctx3, mod, g_pre, w_main, w_dt)


def _dwconv3(x, w, b):
    L = x.shape[0]
    y = pltpu.roll(x, 1, axis=0) * w[0:1, :] + x * w[1:2, :] + pltpu.roll(x, L - 1, axis=0) * w[2:3, :] + b
    row = lax.broadcasted_iota(jnp.int32, (SUBLANES, x.shape[1]), 0)
    head = y[0:SUBLANES, :] - jnp.where(row == 0, x[L - 1:L, :] * w[0:1, :], 0.0)
    tail = y[L - SUBLANES:, :] - jnp.where(row == SUBLANES - 1, x[0:1, :] * w[2:3, :], 0.0)
    return jnp.concatenate([head, y[SUBLANES:L - SUBLANES, :], tail], axis=0)


def _hy_pre_body(p_ref, cw_ref, cb_ref, u_ref, x0_ref):
    c = HY_CH
    parts = []
    for k in range(3):
        x = p_ref[:, k * c:(k + 1) * c].astype(F32)
        parts.append(_dwconv3(x, cw_ref[:, k * c:(k + 1) * c], cb_ref[:, k * c:(k + 1) * c]))
    x0_ref[...] = parts[0].astype(BF16)
    u_ref[...] = (parts[1] * parts[2]).astype(BF16)


def _hy_pre(proj, nseq, L, index_map, cw, cb):
    c3 = 3 * HY_CH
    return pl.pallas_call(
        _hy_pre_body,
        grid=(nseq,),
        in_specs=[pl.BlockSpec((None, L, c3), index_map),
                  pl.BlockSpec((3, c3), lambda s: (0, 0)),
                  pl.BlockSpec((1, c3), lambda s: (0, 0))],
        out_specs=[pl.BlockSpec((None, L, HY_CH), lambda s: (s, 0, 0)),
                   pl.BlockSpec((None, L, HY_CH), lambda s: (s, 0, 0))],
        out_shape=[jax.ShapeDtypeStruct((nseq, L, HY_CH), BF16),
                   jax.ShapeDtypeStruct((nseq, L, HY_CH), BF16)],
        compiler_params=_cparams("arbitrary"),
        name="hy_pre",
    )(proj, cw, cb)


def _hy_filter_body(z_ref, t_ref, w1_ref, b1_ref, w2_ref, b2_ref, w3_ref, f_ref, dl_ref, o_ref):
    f = f_ref[...]
    h = jnp.sin(f * (jnp.dot(z_ref[...], w1_ref[...], precision=HIGHEST, preferred_element_type=F32) + b1_ref[...]))
    h = jnp.sin(f * (jnp.dot(h, w2_ref[...], precision=HIGHEST, preferred_element_type=F32) + b2_ref[...]))
    h = jnp.dot(h, w3_ref[...], precision=HIGHEST, preferred_element_type=F32)
    decay = jnp.exp(-t_ref[...] * dl_ref[...])
    hf = h[:, :HY_CH] * decay
    hb = h[:, HY_CH:] * decay
    row = lax.broadcasted_iota(jnp.int32, hb.shape, 0)
    hb = jnp.where(row == 0, 0.0, hb)
    o_ref[:, :HY_CH] = hf.astype(BF16)
    o_ref[:, HY_CH:] = hb.astype(BF16)


def _hy_filter(L, w1, b1, w2, b2, w3, freq):
    pos = jnp.arange(L, dtype=F32)
    t = pos / (L - 1)
    omega = 2.0 * math.pi * pos / L
    bands = jnp.linspace(1e-4, HY_BANDS - 1, HY_BANDS, dtype=F32)
    z = jnp.concatenate([t[:, None], jnp.cos(omega[:, None] * bands), -jnp.sin(omega[:, None] * bands)], axis=-1)
    z = jnp.pad(z, ((0, 0), (0, LANES - HY_EMB)))
    w1p = jnp.pad(w1, ((0, LANES - HY_EMB), (0, 0)))
    deltas = jnp.abs(jnp.linspace(math.log(HY_DECAY_TARGET) / HY_SLOW_DECAY,
                                  math.log(HY_DECAY_TARGET) / HY_FAST_DECAY, HY_CH, dtype=F32))
    return pl.pallas_call(
        _hy_filter_body,
        out_shape=jax.ShapeDtypeStruct((L, 2 * HY_CH), BF16),
        compiler_params=pltpu.CompilerParams(vmem_limit_bytes=VMEM_LIMIT),
        name="hy_filter",
    )(z, t[:, None], w1p, b1[None, :], w2, b2[None, :], w3, freq[None, :], deltas[None, :])


def _dft_tables(L):
    split = DFT_SPLIT
    t = jnp.arange(L, dtype=jnp.int32)[None, :]
    k1 = jnp.arange(L // split, dtype=jnp.int32)[:, None] * split
    k0 = jnp.arange(split, dtype=jnp.int32)[:, None]
    ang1 = ((k1 * t) % (2 * L)).astype(F32) * (math.pi / L)
    ang0 = ((k0 * t) % (2 * L)).astype(F32) * (math.pi / L)
    factors = jnp.stack([jnp.cos(ang0), jnp.sin(ang0)])
    coarse = jnp.stack([jnp.cos(ang1), jnp.sin(ang1)])
    alt = jnp.where(jnp.arange(L, dtype=jnp.int32) % 2 == 0, 1.0, -1.0).astype(F32)
    groups = min(L // split, 8)
    tr = groups * split
    return pl.pallas_call(
        functools.partial(_dft_tables_body, groups=groups),
        grid=(L // tr,),
        in_specs=[pl.BlockSpec((2, split, L), lambda i: (0, 0, 0)),
                  pl.BlockSpec((2, groups, L), lambda i: (0, i, 0)),
                  pl.BlockSpec((1, L), lambda i: (0, 0)),
                  pl.BlockSpec((tr, 1), lambda i: (i, 0))],
        out_specs=[pl.BlockSpec((2, tr, L), lambda i: (0, i, 0)),
                   pl.BlockSpec((tr, L), lambda i: (i, 0))],
        out_shape=[jax.ShapeDtypeStruct((2, L, L), BF16), jax.ShapeDtypeStruct((L, L), BF16)],
        compiler_params=_cparams("arbitrary"),
        name="dft_tables",
    )(factors, coarse, alt[None, :], alt[:, None])


def _dft_tables_body(f_ref, c_ref, altt_ref, altk_ref, f2_ref, ft_ref, *, groups):
    split = f_ref.shape[1]
    c0, s0 = f_ref[0], f_ref[1]
    first = pl.program_id(0) == 0
    row = lax.broadcasted_iota(jnp.int32, (split, LANES), 0)
    lane = lax.broadcasted_iota(jnp.int32, (split, LANES), 1)
    for j in range(groups):
        rows = slice(j * split, (j + 1) * split)
        c1, s1 = c_ref[0, j:j + 1, :], c_ref[1, j:j + 1, :]
        f2_ref[0, rows, :] = (c1 * c0 - s1 * s0).astype(BF16)
        nsin = -(s1 * c0 + c1 * s0)
        ft_ref[rows, 0:LANES] = jnp.where(lane == 0, altk_ref[rows, :], nsin[:, 0:LANES]).astype(BF16)
        ft_ref[rows, LANES:] = nsin[:, LANES:].astype(BF16)
        if j == 0:
            is_dc = first & (row[:, 0:1] == 0)
            nsin = jnp.where(is_dc, altt_ref[...], nsin)
        f2_ref[1, rows, :] = nsin.astype(BF16)


def _filt_dft_body(f_ref, x_ref, o_ref, *, L, tr):
    c = HY_CH
    x = x_ref[...]
    sre = _dot(f_ref[0], x)
    sim = _dot(f_ref[1], x)
    is0 = (lax.broadcasted_iota(jnp.int32, (tr, c), 0) + pl.program_id(0) * tr) == 0
    wgt = jnp.where(is0, 1.0 / (2 * L), 1.0 / L)
    hre = sre[:, :c] + sre[:, c:]
    o_ref[0] = (hre * wgt).astype(BF16)
    o_ref[1] = (jnp.where(is0, 0.0, sim[:, :c] - sim[:, c:]) * wgt).astype(BF16)
    o_ref[2] = (jnp.where(is0, sim[:, :c] + sim[:, c:], hre) * wgt).astype(BF16)


def _filt_dft(f2, hfb):
    _, L, _ = f2.shape
    cols = hfb.shape[1]
    tr = min(L, 512)
    return pl.pallas_call(
        functools.partial(_filt_dft_body, L=L, tr=tr),
        grid=(L // tr,),
        in_specs=[pl.BlockSpec((2, tr, L), lambda i: (0, i, 0)),
                  pl.BlockSpec((L, cols), lambda i: (0, 0))],
        out_specs=pl.BlockSpec((3, tr, cols // 2), lambda i: (0, i, 0)),
        out_shape=jax.ShapeDtypeStruct((3, L, cols // 2), BF16),
        compiler_params=_cparams("arbitrary"),
        name="filt_dft",
    )(f2, hfb)


def _hy_conv_body(u_ref, x0_ref, f2_ref, fimt_ref, s_ref, bias_ref, o_ref, *, L, fc):
    u = u_ref[...]
    acc = None
    for k in range(L // fc):
        ks = slice(k * fc, (k + 1) * fc)
        ure = _dot(f2_ref[0, ks, :], u)
        uim = _dot(f2_ref[1, ks, :], u)
        a_m = s_ref[0, ks, :].astype(F32)
        b_m = s_ref[1, ks, :].astype(F32)
        d_m = s_ref[2, ks, :].astype(F32)
        yre = (ure * a_m - uim * b_m).astype(BF16)
        yim = (ure * b_m + uim * d_m).astype(BF16)
        part = _dot(f2_ref[0, :, ks], yre) + _dot(fimt_ref[:, ks], yim)
        acc = part if acc is None else acc + part
    y = acc + u.astype(F32) * bias_ref[...]
    o_ref[...] = (x0_ref[...].astype(F32) * y).astype(BF16)


def _hy_conv(u, x0, f2, fimt, s, bias):
    nseq, L, c = u.shape
    fc = min(L, 512)
    resident = lambda shape: pl.BlockSpec(shape, lambda s_: (0,) * len(shape), pipeline_mode=pl.Buffered(1))
    return pl.pallas_call(
        functools.partial(_hy_conv_body, L=L, fc=fc),
        grid=(nseq,),
        in_specs=[pl.BlockSpec((None, L, c), lambda s_: (s_, 0, 0)),
                  pl.BlockSpec((None, L, c), lambda s_: (s_, 0, 0)),
                  resident((2, L, L)), resident((L, L)), resident((3, L, c)),
                  pl.BlockSpec((1, c), lambda s_: (0, 0))],
        out_specs=pl.BlockSpec((None, L, c), lambda s_: (s_, 0, 0)),
        out_shape=jax.ShapeDtypeStruct((nseq, L, c), BF16),
        compiler_params=_cparams("arbitrary"),
        name="hy_conv",
    )(u, x0, f2, fimt, s, bias)


def _hyena(proj, batch, n, ctx_len, p):
    cw, cb, w1, b1, w2, b2, w3, freq, bias = p
    outs = []
    for L, imap in ((n, lambda s: (s, 0, 0)), (ctx_len, lambda s: (batch, s, 0))):
        u, x0 = _hy_pre(proj, batch, L, imap, cw, cb[None, :])
        hfb = _hy_filter(L, w1, b1, w2, b2, w3, freq)
        f2, fimt = _dft_tables(L)
        s = _filt_dft(f2, hfb)
        outs.append(_hy_conv(u, x0, f2, fimt, s, bias[None, :]))
    return outs


def _softplus(x):
    return jnp.maximum(x, 0.0) + jnp.log(1.0 + jnp.exp(-jnp.abs(x)))


def _chunk_cumsum(a, axis, reverse):
    q = SSM_CHUNK
    size = a.shape[axis]
    pos = lax.broadcasted_iota(jnp.int32, a.shape, axis) % q
    s = 1
    while s < q:
        if reverse:
            a = a + jnp.where(pos < q - s, pltpu.roll(a, size - s, axis=axis), 0.0)
        else:
            a = a + jnp.where(pos >= s, pltpu.roll(a, s, axis=axis), 0.0)
        s *= 2
    return a


def _ssd_prepare(L, xbc_ref, dtc_ref, dtr_ref, cw_ref, cb_ref, dtbc_ref, alc_ref, dtbr_ref, alr_ref,
                 x_s, c_s, bt_s, csc_s, csr_s, dtr_s):
    nc = L // SSM_CHUNK
    nh = SSM_HEADS
    inner = SSM_INNER
    gs = SSM_GROUPS * SSM_STATE
    xs = _silu(_dwconv3(xbc_ref[:, :inner].astype(F32), cw_ref[:, :inner], cb_ref[:, :inner]))
    x_s[0:L, :] = xs.astype(BF16)
    bm = _silu(_dwconv3(xbc_ref[:, inner:inner + gs].astype(F32), cw_ref[:, inner:inner + gs],
                        cb_ref[:, inner:inner + gs]))
    for c in range(nc):
        bt_s[c] = bm[c * SSM_CHUNK:(c + 1) * SSM_CHUNK, :].T.astype(BF16)
    cm = _silu(_dwconv3(xbc_ref[:, inner + gs:].astype(F32), cw_ref[:, inner + gs:], cb_ref[:, inner + gs:]))
    c_s[0:L, :] = cm.astype(BF16)
    a_col = _softplus(dtc_ref[...] + dtbc_ref[...]) * (-jnp.exp(alc_ref[...]))
    lane = lax.broadcasted_iota(jnp.int32, a_col.shape, 1)
    csc_s[0:L, :] = jnp.where(lane < nh, _chunk_cumsum(a_col, 0, False), _chunk_cumsum(a_col, 0, True))
    dt_row = _softplus(dtr_ref[...] + dtbr_ref[...])
    a_row = (dt_row * (-jnp.exp(alr_ref[...]))).reshape(2 * nh * nc, SSM_CHUNK)
    rown = lax.broadcasted_iota(jnp.int32, a_row.shape, 0)
    csr_s[0:2 * nh * nc, :] = jnp.where(rown < nh * nc, _chunk_cumsum(a_row, 1, False),
                                        _chunk_cumsum(a_row, 1, True))
    dtr_s[0:2 * nh * nc, :] = dt_row.reshape(2 * nh * nc, SSM_CHUNK)


def _ssd_chunk(c, d, nc, x_s, c_s, bt_s, csc_s, csr_s, dtr_s, st_s, y_s):
    q = SSM_CHUNK
    ns = SSM_STATE
    hp = SSM_HEADDIM
    hpg = SSM_HEADS // SSM_GROUPS
    gw = hpg * hp
    r0 = pl.multiple_of(c * q, q)
    ri = lax.broadcasted_iota(jnp.int32, (q, q), 0)
    ci = lax.broadcasted_iota(jnp.int32, (q, q), 1)
    keep = (ci <= ri) if d == 0 else (ci >= ri)
    lane_blk = lax.broadcasted_iota(jnp.int32, (q, gw), 1) // hp
    cc = c_s[pl.ds(r0, q), :]
    xc = x_s[pl.ds(r0, q), :]
    btc = bt_s[c]
    ys = []
    for g in range(SSM_GROUPS):
        cg = cc[:, g * ns:(g + 1) * ns]
        btg = btc[g * ns:(g + 1) * ns, :]
        xg = xc[:, g * gw:(g + 1) * gw]
        s_cb = _dot(cg, btg)
        btg32 = btg.astype(F32)
        m_rows, w_rows, e_cols, tots = [], [], [], []
        for hh in range(hpg):
            idx = d * SSM_HEADS + g * hpg + hh
            csc = csc_s[pl.ds(r0, q), idx:idx + 1]
            csr = csr_s[pl.ds(idx * nc + c, 1), :]
            dtr = dtr_s[pl.ds(idx * nc + c, 1), :]
            lmat = jnp.where(keep, jnp.exp(csc - csr), 0.0)
            m_rows.append((s_cb * lmat * dtr).astype(BF16))
            tot = csr[:, q - 1:q] if d == 0 else csr[:, 0:1]
            w_rows.append((btg32 * (jnp.exp(tot - csr) * dtr)).astype(BF16))
            e_cols.append(jnp.exp(csc))
            tots.append(jnp.exp(tot))
        yd = _dot(jnp.concatenate(m_rows, axis=0), xg)
        ds = _dot(jnp.concatenate(w_rows, axis=0), xg)
        st = st_s[d, g]
        yo = _dot(cg, st.astype(BF16))
        y_g = jnp.zeros((q, gw), F32)
        st_new = jnp.zeros((ns, gw), F32)
        for hh in range(hpg):
            sel = lane_blk == hh
            y_g = jnp.where(sel, yd[hh * q:(hh + 1) * q, :] + yo * e_cols[hh], y_g)
            st_new = jnp.where(sel, st * tots[hh] + ds[hh * ns:(hh + 1) * ns, :], st_new)
        st_s[d, g] = st_new
        ys.append(y_g)
    y_s[pl.ds(r0, q), :] += jnp.concatenate(ys, axis=1)


def _ssd_body(xl_ref, xc_ref, dcl_ref, dcc_ref, drl_ref, drc_ref, cw_ref, cb_ref, dtbc_ref, alc_ref,
              dtbr_ref, alr_ref, dsk_ref, yl_ref, yc_ref,
              x_s, c_s, bt_s, csc_s, csr_s, dtr_s, st_s, y_s, *, n, ctx_len):
    st_s[...] = jnp.zeros_like(st_s)
    for L, xbc_ref, dtc_ref, dtr_ref, o_ref in ((ctx_len, xc_ref, dcc_ref, drc_ref, yc_ref),
                                                 (n, xl_ref, dcl_ref, drl_ref, yl_ref)):
        nc = L // SSM_CHUNK
        _ssd_prepare(L, xbc_ref, dtc_ref, dtr_ref, cw_ref, cb_ref, dtbc_ref, alc_ref, dtbr_ref, alr_ref,
                     x_s, c_s, bt_s, csc_s, csr_s, dtr_s)
        args = (x_s, c_s, bt_s, csc_s, csr_s, dtr_s, st_s, y_s)

        y_s[0:L, :] = jnp.zeros((L, SSM_INNER), F32)

        def both(i, carry, nc=nc, args=args):
            _ssd_chunk(i, 0, nc, *args)
            _ssd_chunk(nc - 1 - i, 1, nc, *args)
            return carry

        lax.fori_loop(0, nc, both, 0)
        o_ref[...] = (y_s[0:L, :] + x_s[0:L, :].astype(F32) * dsk_ref[...]).astype(BF16)


def _ssd(proj, dt, batch, n, ctx_len, p):
    conv_w, conv_b, dt_bias, a_log, d_skip = p
    nh = SSM_HEADS
    ncl, ncc = n // SSM_CHUNK, ctx_len // SSM_CHUNK
    cch = SSM_CONV_CH
    xbc_blk = (SSM_INNER * 3 + SSM_INNER) // cch
    dt16 = dt[:, :, :2 * nh]
    dtr_l = dt16[:batch].transpose(0, 2, 1).reshape(batch, 2 * nh, ncl, SSM_CHUNK)
    dtr_c = dt16[batch].reshape(batch, ctx_len, 2 * nh).transpose(0, 2, 1).reshape(batch, 2 * nh, ncc, SSM_CHUNK)
    pad = LANES - 2 * nh
    dtb_col = jnp.pad(dt_bias.reshape(1, 2 * nh), ((0, 0), (0, pad)))
    al_col = jnp.pad(a_log.reshape(1, 2 * nh), ((0, 0), (0, pad)))
    dtb_row = jnp.broadcast_to(dt_bias.reshape(2 * nh, 1, 1), (2 * nh, 1, SSM_CHUNK))
    al_row = jnp.broadcast_to(a_log.reshape(2 * nh, 1, 1), (2 * nh, 1, SSM_CHUNK))
    dsk = jnp.repeat(d_skip, SSM_HEADDIM)[None, :]
    full = lambda shape: pl.BlockSpec(shape, lambda b: (0,) * len(shape))
    return pl.pallas_call(
        functools.partial(_ssd_body, n=n, ctx_len=ctx_len),
        grid=(batch,),
        in_specs=[pl.BlockSpec((None, n, cch), lambda b: (b, 0, xbc_blk)),
                  pl.BlockSpec((None, ctx_len, cch), lambda b: (batch, b, xbc_blk)),
                  pl.BlockSpec((None, n, LANES), lambda b: (b, 0, 0)),
                  pl.BlockSpec((None, ctx_len, LANES), lambda b: (batch, b, 0)),
                  pl.BlockSpec((None, 2 * nh, ncl, SSM_CHUNK), lambda b: (b, 0, 0, 0)),
                  pl.BlockSpec((None, 2 * nh, ncc, SSM_CHUNK), lambda b: (b, 0, 0, 0)),
                  full((3, cch)), full((1, cch)), full((1, LANES)), full((1, LANES)),
                  full((2 * nh, 1, SSM_CHUNK)), full((2 * nh, 1, SSM_CHUNK)), full((1, SSM_INNER))],
        out_specs=[pl.BlockSpec((None, n, SSM_INNER), lambda b: (b, 0, 0)),
                   pl.BlockSpec((None, ctx_len, SSM_INNER), lambda b: (b, 0, 0))],
        out_shape=[jax.ShapeDtypeStruct((batch, n, SSM_INNER), BF16),
                   jax.ShapeDtypeStruct((batch, ctx_len, SSM_INNER), BF16)],
        scratch_shapes=[pltpu.VMEM((n, SSM_INNER), BF16),
                        pltpu.VMEM((n, SSM_GROUPS * SSM_STATE), BF16),
                        pltpu.VMEM((ncl, SSM_GROUPS * SSM_STATE, SSM_CHUNK), BF16),
                        pltpu.VMEM((n, LANES), F32),
                        pltpu.VMEM((2 * nh * ncl, SSM_CHUNK), F32),
                        pltpu.VMEM((2 * nh * ncl, SSM_CHUNK), F32),
                        pltpu.VMEM((2, SSM_GROUPS, SSM_STATE, SSM_INNER // SSM_GROUPS), F32),
                        pltpu.VMEM((n, SSM_INNER), F32)],
        compiler_params=_cparams("arbitrary"),
        name="ssd",
    )(proj, proj, dt, dt, dtr_l, dtr_c, conv_w, conv_b[None, :], dtb_col, al_col, dtb_row, al_row, dsk)


def _mix_out_even_body(yhl_ref, yhc_ref, ysl_ref, ysc_ref, z_ref, ng_ref, w_ref, x_ref, c_ref, mod_ref, pg_ref, o_ref,
                       *, batch):
    is_ctx = pl.program_id(0) == batch
    h = jnp.where(is_ctx, c_ref[...], x_ref[...])
    y_hy = jnp.where(is_ctx, yhc_ref[...], yhl_ref[...])
    ys = jnp.where(is_ctx, ysc_ref[...], ysl_ref[...]).astype(F32) * _silu(z_ref[...].astype(F32))
    gw = SSM_INNER // SSM_GROUPS
    parts = []
    for g in range(SSM_GROUPS):
        yg = ys[:, g * gw:(g + 1) * gw]
        parts.append(yg * lax.rsqrt(jnp.mean(yg * yg, axis=-1, keepdims=True) + EPS))
    y_ssm = (jnp.concatenate(parts, axis=1) * ng_ref[...]).astype(BF16)
    m = _dot(y_hy, w_ref[0:HY_CH, :]) + _dot(y_ssm, w_ref[HY_CH:, :])
    o_ref[...] = h + mod_ref[2:3, :] * _rms(m, pg_ref[...])


def _mix_out_even(y_hy_l, y_hy_c, y_ss_l, y_ss_c, proj, norm_g, w_out, x, ctx3, mod, post_g, tm):
    batch, n, d = x.shape
    G = batch + 1
    c = HY_CH
    z_blk = (3 * HY_CH) // SSM_INNER
    lat_c, ctx_c = _lat_ctx_specs(batch, n // tm, tm, c)
    lat_d, ctx_d = _lat_ctx_specs(batch, n // tm, tm, d)
    return pl.pallas_call(
        functools.partial(_mix_out_even_body, batch=batch),
        grid=(G, n // tm),
        in_specs=[lat_c, ctx_c, lat_c, ctx_c,
                  pl.BlockSpec((None, tm, SSM_INNER), lambda g, t: (g, t, z_blk)),
                  pl.BlockSpec((1, SSM_INNER), lambda g, t: (0, 0)),
                  pl.BlockSpec((2 * c, d), lambda g, t: (0, 0)),
                  lat_d, ctx_d,
                  pl.BlockSpec((None, 6, d), lambda g, t: (g, 0, 0)),
                  pl.BlockSpec((1, d), lambda g, t: (0, 0))],
        out_specs=pl.BlockSpec((None, tm, d), lambda g, t: (g, t, 0)),
        out_shape=jax.ShapeDtypeStruct((G, n, d), F32),
        compiler_params=_cparams("arbitrary", "arbitrary"),
        name="mix_out_even",
    )(y_hy_l, y_hy_c.reshape(1, n, c), y_ss_l, y_ss_c.reshape(1, n, c), proj, norm_g, w_out, x, ctx3, mod, post_g)


def _mix_out_odd_body(a_ref, b_ref, w_ref, h_ref, mod_ref, pg_ref, o_ref):
    half = a_ref.shape[-1]
    m = _dot(a_ref[...], w_ref[0:half, :]) + _dot(b_ref[...], w_ref[half:, :])
    o_ref[...] = h_ref[...] + mod_ref[2:3, :] * _rms(m, pg_ref[...])


def _mix_out_odd(o_a, o_b, w_out, h_all, mod, post_g, tm):
    batch, n, c = o_a.shape
    d = h_all.shape[-1]
    return pl.pallas_call(
        _mix_out_odd_body,
        grid=(batch, n // tm),
        in_specs=[pl.BlockSpec((None, tm, c), lambda g, t: (g, t, 0)),
                  pl.BlockSpec((None, tm, c), lambda g, t: (g, t, 0)),
                  pl.BlockSpec((2 * c, d), lambda g, t: (0, 0)),
                  pl.BlockSpec((None, tm, d), lambda g, t: (g, t, 0)),
                  pl.BlockSpec((None, 6, d), lambda g, t: (g, 0, 0)),
                  pl.BlockSpec((1, d), lambda g, t: (0, 0))],
        out_specs=pl.BlockSpec((None, tm, d), lambda g, t: (g, t, 0)),
        out_shape=jax.ShapeDtypeStruct((batch, n, d), F32),
        compiler_params=_cparams("arbitrary", "arbitrary"),
        name="mix_out_odd",
    )(o_a, o_b, w_out, h_all, mod, post_g)


def _ffn_body(h_ref, mod_ref, g_ref, wg_ref, wu_ref, wd_ref, pg_ref, o_ref, *, fchunk):
    h = h_ref[...]
    a = _norm_mod(h, g_ref[...], mod_ref[3:4, :], mod_ref[4:5, :]).astype(BF16)
    ff = wg_ref.shape[1]
    f = jnp.zeros(h.shape, F32)
    for j in range(0, ff, fchunk):
        gate = _dot(a, wg_ref[:, j:j + fchunk])
        up = _dot(a, wu_ref[:, j:j + fchunk])
        f = f + _dot((_silu(gate) * up).astype(BF16), wd_ref[j:j + fchunk, :])
    o_ref[...] = h + mod_ref[5:6, :] * _rms(f, pg_ref[...])


def _ffn(h_all, mod, g_pre, wg, wu, wd, post_g, tm):
    G, n, d = h_all.shape
    ff = wg.shape[1]
    return pl.pallas_call(
        functools.partial(_ffn_body, fchunk=256),
        grid=(G, n // tm),
        in_specs=[pl.BlockSpec((None, tm, d), lambda g, t: (g, t, 0)),
                  pl.BlockSpec((None, 6, d), lambda g, t: (g, 0, 0)),
                  pl.BlockSpec((1, d), lambda g, t: (0, 0)),
                  pl.BlockSpec((d, ff), lambda g, t: (0, 0), pipeline_mode=pl.Buffered(1)),
                  pl.BlockSpec((d, ff), lambda g, t: (0, 0), pipeline_mode=pl.Buffered(1)),
                  pl.BlockSpec((ff, d), lambda g, t: (0, 0), pipeline_mode=pl.Buffered(1)),
                  pl.BlockSpec((1, d), lambda g, t: (0, 0))],
        out_specs=pl.BlockSpec((None, tm, d), lambda g, t: (g, t, 0)),
        out_shape=jax.ShapeDtypeStruct((G, n, d), F32),
        compiler_params=_cparams("arbitrary", "arbitrary"),
        name="ffn",
    )(h_all, mod, g_pre, wg, wu, wd, post_g)


def _rope(x, cos, sin_a, sin_b, half):
    return x * cos + pltpu.roll(x, LANES - half, axis=1) * sin_a + pltpu.roll(x, half, axis=1) * sin_b


def _odd_in_body(h_ref, mod_ref, g_ref, w_ref, qg_ref, wq_ref, kg_ref, wk_ref, wv_ref, rp_ref,
                 qm_ref, km_ref, vm_ref, qw_ref, kw_ref, vw_ref, *, batch):
    a = _norm_mod(h_ref[...], g_ref[...], mod_ref[0:1, :], mod_ref[1:2, :]).astype(BF16)
    ql, kl = MLA_Q_LORA, MLA_KV_LORA
    o_qg = ql + kl
    o_kg = o_qg + GQA_HEADS * HEAD_DIM
    o_vg = o_kg + GQA_KV * HEAD_DIM
    o_kr = o_vg + GQA_KV * HEAD_DIM
    is_ctx = pl.program_id(0) == batch
    tab = lambda j, ident: jnp.where(is_ctx, ident, rp_ref[:, j * LANES:(j + 1) * LANES])
    cm, s1m, s2m = tab(0, 1.0), tab(1, 0.0), tab(2, 0.0)
    cg, s1g, s2g = tab(3, 1.0), tab(4, 0.0), tab(5, 0.0)
    mla_scale = (MLA_NOPE + MLA_ROPE) ** -0.5
    gqa_scale = HEAD_DIM ** -0.5
    q_lat = _rms(_dot(a, w_ref[:, 0:ql]), qg_ref[...]).astype(BF16)
    q = _dot(q_lat, wq_ref[...])
    kv_lat = _rms(_dot(a, w_ref[:, ql:o_qg]), kg_ref[...]).astype(BF16)
    k = _dot(kv_lat, wk_ref[...])
    vm_ref[...] = _dot(kv_lat, wv_ref[...]).astype(BF16)
    k_rope = _rope(_dot(a, w_ref[:, o_kr:o_kr + LANES]), cm, s1m, s2m, MLA_ROPE // 2)
    for hd in range(MLA_HEADS):
        sl = slice(hd * LANES, (hd + 1) * LANES)
        qm_ref[:, sl] = (_rope(q[:, sl], cm, s1m, s2m, MLA_ROPE // 2) * mla_scale).astype(BF16)
        km_ref[:, sl] = (k[:, sl] + k_rope).astype(BF16)
    qg = _dot(a, w_ref[:, o_qg:o_kg])
    for blk in range(GQA_HEADS * HEAD_DIM // LANES):
        sl = slice(blk * LANES, (blk + 1) * LANES)
        qw_ref[:, sl] = (_rope(qg[:, sl], cg, s1g, s2g, HEAD_DIM // 2) * gqa_scale).astype(BF16)
    kw_ref[...] = _rope(_dot(a, w_ref[:, o_kg:o_vg]), cg, s1g, s2g, HEAD_DIM // 2).astype(BF16)
    vw_ref[...] = _dot(a, w_ref[:, o_vg:o_kr]).astype(BF16)


def _rope_tables(n):
    rows = (jnp.arange(n) // GRID_W).astype(F32)
    cols = (jnp.arange(n) % GRID_W).astype(F32)

    def cs(rot_dim):
        nf = rot_dim // 4
        inv = ROPE_BASE ** (-jnp.arange(nf, dtype=F32) / nf)
        ang = jnp.concatenate([rows[:, None] * inv, cols[:, None] * inv], axis=-1)
        return jnp.cos(ang), jnp.sin(ang)

    hm, hg = MLA_ROPE // 2, HEAD_DIM // 2
    c_m, s_m = cs(MLA_ROPE)
    c_g, s_g = cs(HEAD_DIM)
    o_sm, o_cg, o_sg, o_one = hm, 2 * hm, 2 * hm + hg, 2 * hm + 2 * hg
    compact = jnp.concatenate([c_m, s_m, c_g, s_g, jnp.ones((n, LANES - o_one), F32)], axis=1)
    spread = np.zeros((LANES, 6 * LANES), np.float32)
    for j in range(LANES):
        in_rope = MLA_NOPE <= j < MLA_NOPE + MLA_ROPE
        r = (j - MLA_NOPE) % hm
        spread[r if in_rope else o_one, j] = 1.0
        if MLA_NOPE <= j < MLA_NOPE + hm:
            spread[o_sm + r, LANES + j] = -1.0
        if MLA_NOPE + hm <= j < MLA_NOPE + MLA_ROPE:
            spread[o_sm + r, 2 * LANES + j] = 1.0
        spread[o_cg + j % hg, 3 * LANES + j] = 1.0
        if (j // hg) % 2 == 0:
            spread[o_sg + j % hg, 4 * LANES + j] = -1.0
        else:
            spread[o_sg + j % hg, 5 * LANES + j] = 1.0
    return jnp.dot(compact, jnp.asarray(spread), precision=HIGHEST)


def _odd_in(h_all, mod, g_pre, w_in, q_norm_g, w_q_up, kv_norm_g, w_kv_up, tm):
    G, n, d = h_all.shape
    batch = G - 1
    ql, kl, rp = MLA_Q_LORA, MLA_KV_LORA, MLA_ROPE
    gq, gk = GQA_HEADS * HEAD_DIM, GQA_KV * HEAD_DIM
    o = [0, ql, ql + kl, ql + kl + rp, ql + kl + rp + gq, ql + kl + rp + gq + gk]
    kr_cols = jnp.pad(w_in[:, o[2]:o[3]], ((0, 0), (MLA_NOPE, LANES - MLA_NOPE - rp)))
    w = jnp.concatenate([w_in[:, o[0]:o[2]], w_in[:, o[3]:], kr_cols], axis=1).astype(BF16)
    hq = MLA_NOPE + MLA_ROPE
    wq = jnp.pad(w_q_up.reshape(ql, MLA_HEADS, hq), ((0, 0), (0, 0), (0, LANES - hq)))
    wq = wq.reshape(ql, MLA_HEADS * LANES).astype(BF16)
    wkv = w_kv_up.reshape(kl, MLA_HEADS, MLA_NOPE + MLA_V)
    wk = jnp.pad(wkv[:, :, :MLA_NOPE], ((0, 0), (0, 0), (0, LANES - MLA_NOPE)))
    wk = wk.reshape(kl, MLA_HEADS * LANES).astype(BF16)
    wv = wkv[:, :, MLA_NOPE:].reshape(kl, MLA_HEADS * MLA_V).astype(BF16)
    ropes = _rope_tables(n)
    nw = w.shape[1]
    full = lambda shape: pl.BlockSpec(shape, lambda g, t: (0,) * len(shape))
    blk = lambda width: pl.BlockSpec((None, tm, width), lambda g, t: (g, t, 0))
    widths = [MLA_HEADS * LANES, MLA_HEADS * LANES, MLA_HEADS * MLA_V, gq, gk, gk]
    return pl.pallas_call(
        functools.partial(_odd_in_body, batch=batch),
        grid=(G, n // tm),
        in_specs=[blk(d), pl.BlockSpec((None, 6, d), lambda g, t: (g, 0, 0)), full((1, d)), full((d, nw)),
                  full((1, ql)), full((ql, MLA_HEADS * LANES)), full((1, kl)), full((kl, MLA_HEADS * LANES)),
                  full((kl, MLA_HEADS * MLA_V)),
                  pl.BlockSpec((tm, 6 * LANES), lambda g, t: (t, 0))],
        out_specs=[blk(wd) for wd in widths],
        out_shape=[jax.ShapeDtypeStruct((G, n, wd), BF16) for wd in widths],
        compiler_params=_cparams("arbitrary", "arbitrary"),
        name="odd_in",
    )(h_all, mod, g_pre, w, q_norm_g[None, :], wq, kv_norm_g[None, :], wk, wv, ropes)


def _mla_body(q_ref, kl_ref, kc_ref, vl_ref, vc_ref, o_ref, *, hps):
    lane = lax.broadcasted_iota(jnp.int32, (o_ref.shape[0], LANES), 1)
    for pr in range(hps // 2):
        vs = slice(pr * LANES, (pr + 1) * LANES)
        v_l = jnp.concatenate([vl_ref[:, vs], jnp.ones((vl_ref.shape[0], LANES), BF16)], axis=1)
        v_c = jnp.concatenate([vc_ref[:, vs], jnp.ones((vc_ref.shape[0], LANES), BF16)], axis=1)
        outs = []
        for hd in (2 * pr, 2 * pr + 1):
            sl = slice(hd * LANES, (hd + 1) * LANES)
            q = q_ref[:, sl]
            s_l = _dot_nt(q, kl_ref[:, sl])
            s_c = _dot_nt(q, kc_ref[:, sl])
            m = jnp.maximum(jnp.max(s_l, axis=-1, keepdims=True), jnp.max(s_c, axis=-1, keepdims=True))
            p_l = jnp.exp((s_l - m).astype(BF16))
            p_c = jnp.exp((s_c - m).astype(BF16))
            pv = _dot(p_l, v_l) + _dot(p_c, v_c)
            outs.append(pv[:, :LANES] / pv[:, LANES:])
        o_ref[:, vs] = jnp.where(lane < MLA_V, outs[0], outs[1]).astype(BF16)


def _mla_attn(q, k, v, batch, n, ctx_len, tq, hps):
    return pl.pallas_call(
        functools.partial(_mla_body, hps=hps),
        grid=(batch, MLA_HEADS // hps, n // tq),
        in_specs=[pl.BlockSpec((None, tq, hps * LANES), lambda b, h, i: (b, i, h)),
                  pl.BlockSpec((None, n, hps * LANES), lambda b, h, i: (b, 0, h)),
                  pl.BlockSpec((None, ctx_len, hps * LANES), lambda b, h, i: (batch, b, h)),
                  pl.BlockSpec((None, n, hps * MLA_V), lambda b, h, i: (b, 0, h)),
                  pl.BlockSpec((None, ctx_len, hps * MLA_V), lambda b, h, i: (batch, b, h))],
        out_specs=pl.BlockSpec((None, tq, hps * MLA_V), lambda b, h, i: (b, i, h)),
        out_shape=jax.ShapeDtypeStruct((batch, n, MLA_HEADS * MLA_V), BF16),
        compiler_params=_cparams("arbitrary", "arbitrary", "arbitrary"),
        name="mla_attn",
    )(q, k, k, v, v)


def _dup_half(x, kv):
    lane = lax.broadcasted_iota(jnp.int32, x.shape, 1)
    swapped = pltpu.roll(x, HEAD_DIM, axis=1)
    lo = lane < HEAD_DIM
    return jnp.where(lo, x, swapped) if kv == 0 else jnp.where(lo, swapped, x)


def _win_body(sink_ref, q_ref, kp_ref, kc_ref, kn_ref, kx_ref, vp_ref, vc_ref, vn_ref, vx_ref, o_ref, bias_ref,
              *, tq):
    i = pl.program_id(1)
    nb = pl.num_programs(1)
    w = WINDOW
    g = GQA_HEADS // GQA_KV
    n_ctx = kx_ref.shape[0]
    n_key = 2 * w + tq + n_ctx

    @pl.when((pl.program_id(0) == 0) & (i == 0))
    def _():
        r = lax.broadcasted_iota(jnp.int32, (g * tq, n_key), 0) % tq
        c = lax.broadcasted_iota(jnp.int32, (g * tq, n_key), 1)
        rel = c - w - r
        visible = (jnp.abs(rel) <= w) | (c >= 2 * w + tq)
        bias_ref[...] = jnp.where(visible, 0.0, -jnp.inf)

    edge = jnp.full((1, LANES), -jnp.inf, F32)
    zero = jnp.zeros((1, LANES), F32)
    bias_prev = jnp.where(i > 0, zero, edge)
    bias_next = jnp.where(i < nb - 1, zero, edge)
    lane = lax.broadcasted_iota(jnp.int32, (tq, LANES), 1)
    lo = lane < HEAD_DIM
    for kv in range(GQA_KV):
        rows = []
        for hh in range(g):
            hd = kv * g + hh
            blk = q_ref[:, (hd // 2) * LANES:(hd // 2 + 1) * LANES].astype(F32)
            rows.append(jnp.where(lo if hd % 2 == 0 else ~lo, blk, 0.0).astype(BF16))
        q = jnp.concatenate(rows, axis=0)
        k_all = jnp.concatenate([_dup_half(ref[...].astype(F32), kv).astype(BF16)
                                 for ref in (kp_ref, kc_ref, kn_ref, kx_ref)], axis=0)
        v_all = jnp.concatenate([_dup_half(ref[...].astype(F32), kv).astype(BF16)
                                 for ref in (vp_ref, vc_ref, vn_ref, vx_ref)], axis=0)
        v_all = jnp.concatenate([v_all, jnp.ones((n_key, LANES), BF16)], axis=1)
        s = _dot_nt(q, k_all) + bias_ref[...]
        s = jnp.concatenate([s[:, :w] + bias_prev, s[:, w:w + tq], s[:, w + tq:2 * w + tq] + bias_next,
                             s[:, 2 * w + tq:]], axis=1)
        hrow = lax.broadcasted_iota(jnp.int32, (g * tq, 1), 0) // tq
        sk = jnp.zeros((g * tq, 1), F32)
        for hh in range(g):
            sk = jnp.where(hrow == hh, sink_ref[kv * g + hh], sk)
        m = jnp.maximum(_row_max(s), sk)
        acc = _dot(jnp.exp((s - m).astype(BF16)), v_all)
        res = acc[:, :LANES] / (acc[:, LANES:] + jnp.exp(sk - m))
        for pr in range(g // 2):
            blk = (kv * g) // 2 + pr
            o_ref[:, blk * LANES:(blk + 1) * LANES] = jnp.where(
                lo, res[(2 * pr) * tq:(2 * pr + 1) * tq, :], res[(2 * pr + 1) * tq:(2 * pr + 2) * tq, :]).astype(BF16)


def _win_attn(q, k, v, sink, batch, n, ctx_len, tq):
    w = WINDOW
    per = tq // w
    gq, gk = GQA_HEADS * HEAD_DIM, GQA_KV * HEAD_DIM
    prev = lambda b, i: (b, jnp.maximum(i * per - 1, 0), 0)
    cur = lambda b, i: (b, i, 0)
    nxt = lambda b, i: (b, jnp.minimum((i + 1) * per, n // w - 1), 0)
    ctx = lambda b, i: (batch, b, 0)
    kspec = lambda rows, im: pl.BlockSpec((None, rows, gk), im)
    return pl.pallas_call(
        functools.partial(_win_body, tq=tq),
        grid=(batch, n // tq),
        in_specs=[pl.BlockSpec(memory_space=pltpu.SMEM),
                  pl.BlockSpec((None, tq, gq), cur),
                  kspec(w, prev), kspec(tq, cur), kspec(w, nxt), kspec(ctx_len, ctx),
                  kspec(w, prev), kspec(tq, cur), kspec(w, nxt), kspec(ctx_len, ctx)],
        out_specs=pl.BlockSpec((None, tq, gq), cur),
        out_shape=jax.ShapeDtypeStruct((batch, n, gq), BF16),
        scratch_shapes=[pltpu.VMEM((GQA_HEADS // GQA_KV * tq, 2 * w + tq + ctx_len), F32)],
        compiler_params=_cparams("arbitrary", "arbitrary"),
        name="win_attn",
    )(sink, q, k, k, k, k, v, v, v, v)


def _route_body(h_ref, mod_ref, g_ref, r_ref, a_ref, info_ref, cnt_ref, rows_ref, carry_ref, tri_ref):
    first = (pl.program_id(0) == 0) & (pl.program_id(1) == 0)

    @pl.when(first)
    def _():
        carry_ref[...] = jnp.zeros_like(carry_ref)
        ri = lax.broadcasted_iota(jnp.int32, tri_ref.shape, 0)
        ci = lax.broadcasted_iota(jnp.int32, tri_ref.shape, 1)
        tri_ref[...] = (ci < ri).astype(BF16)

    a = _norm_mod(h_ref[...], g_ref[...], mod_ref[3:4, :], mod_ref[4:5, :])
    a_ref[...] = a
    tm = a.shape[0]
    a_hi = a.astype(BF16)
    a_lo = (a - a_hi.astype(F32)).astype(BF16)
    part = _dot(jnp.concatenate([a_hi, a_lo], axis=0), r_ref[...])
    part = part[:tm, :] + part[tm:, :]
    logits = part + pltpu.roll(part, LANES - N_EXPERTS, axis=1)
    lane = lax.broadcasted_iota(jnp.int32, logits.shape, 1)
    neg = -jnp.inf
    logits = jnp.where(lane < N_EXPERTS, logits, neg)
    m1 = jnp.max(logits, axis=-1, keepdims=True)
    i1 = jnp.min(jnp.where(logits == m1, lane, LANES), axis=-1, keepdims=True)
    rest = jnp.where(lane == i1, neg, logits)
    m2 = jnp.max(rest, axis=-1, keepdims=True)
    i2 = jnp.min(jnp.where(rest == m2, lane, LANES), axis=-1, keepdims=True)
    e2 = jnp.exp(m2 - m1)
    w1 = 1.0 / (1.0 + e2)
    w2 = e2 / (1.0 + e2)
    chosen = ((lane == i1) | (lane == i2)).astype(F32)
    before = _dot(tri_ref[...], chosen.astype(BF16)) + carry_ref[...]
    p1 = jnp.sum(jnp.where(lane == i1, before, 0.0), axis=-1, keepdims=True)
    p2 = jnp.sum(jnp.where(lane == i2, before, 0.0), axis=-1, keepdims=True)
    carry_ref[...] += jnp.sum(chosen, axis=0, keepdims=True)
    cnt_ref[...] = carry_ref[...]
    vals = (i1.astype(F32), i2.astype(F32), w1, w2, p1, p2)
    info = jnp.zeros(logits.shape, F32)
    for j, v in enumerate(vals):
        info = jnp.where(lane == j, v, info)
    info_ref[...] = info
    src = (0, 1, 4, 5)
    sr = lax.broadcasted_iota(jnp.int32, (SUBLANES, LANES), 0)
    sl = lax.broadcasted_iota(jnp.int32, (SUBLANES, LANES), 1)
    sel = jnp.zeros((SUBLANES, LANES), F32)
    for r, c in enumerate(src):
        sel = jnp.where((sr == r) & (sl == c), 1.0, sel)
    rows_ref[...] = lax.dot_general(sel, info, (((1,), (1,)), ((), ())), precision=HIGHEST,
                                    preferred_element_type=F32)


def _route(h_lat, mod, g_pre, router, tm):
    batch, n, d = h_lat.shape
    r_hi = router.astype(BF16)
    r_lo = (router - r_hi.astype(F32)).astype(BF16)
    rp = jnp.pad(jnp.concatenate([r_hi, r_lo], axis=1), ((0, 0), (0, LANES - 2 * N_EXPERTS)))
    nt = n // tm
    return pl.pallas_call(
        _route_body,
        grid=(batch, nt),
        in_specs=[pl.BlockSpec((None, tm, d), lambda g, t: (g, t, 0)),
                  pl.BlockSpec((None, 6, d), lambda g, t: (g, 0, 0)),
                  pl.BlockSpec((1, d), lambda g, t: (0, 0)),
                  pl.BlockSpec((d, LANES), lambda g, t: (0, 0))],
        out_specs=[pl.BlockSpec((tm, d), lambda g, t: (g * nt + t, 0)),
                   pl.BlockSpec((tm, LANES), lambda g, t: (g * nt + t, 0)),
                   pl.BlockSpec((1, LANES), lambda g, t: (0, 0)),
                   pl.BlockSpec((SUBLANES, tm), lambda g, t: (0, g * nt + t))],
        out_shape=[jax.ShapeDtypeStruct((batch * n, d), F32),
                   jax.ShapeDtypeStruct((batch * n, LANES), F32),
                   jax.ShapeDtypeStruct((1, LANES), F32),
                   jax.ShapeDtypeStruct((SUBLANES, batch * n), F32)],
        scratch_shapes=[pltpu.VMEM((1, LANES), F32), pltpu.VMEM((tm, tm), BF16)],
        compiler_params=_cparams("arbitrary", "arbitrary"),
        name="moe_route",
    )(h_lat, mod, g_pre, rp)


def _scatter_body(slot_ref, fill_ref, a_ref, xs_ref, zero_ref, sem, zsem, *, ts, n_tok, n_fill):
    base = pl.program_id(0) * ts

    @pl.when(pl.program_id(0) == 0)
    def _():
        zero_ref[...] = jnp.zeros_like(zero_ref)

        def fill(t):
            row = pl.multiple_of(fill_ref[t] * MOE_TILE, MOE_TILE)
            return pltpu.make_async_copy(zero_ref, xs_ref.at[pl.ds(row, MOE_TILE), :], zsem)

        for t in range(n_fill):
            @pl.when(fill_ref[t] >= 0)
            def _(t=t):
                fill(t).start()
        for t in range(n_fill):
            @pl.when(fill_ref[t] >= 0)
            def _(t=t):
                fill(t).wait()

    def issue(i, carry):
        for u in range(SUBLANES):
            for k in range(2):
                slot = slot_ref[k * n_tok + base + i * SUBLANES + u]
                pltpu.make_async_copy(a_ref.at[i, pl.ds(u, 1), :], xs_ref.at[pl.ds(slot, 1), :],
                                      sem).start(priority=k)
        return carry

    lax.fori_loop(0, ts // SUBLANES, issue, 0)
    for k in range(2):
        pltpu.make_async_copy(xs_ref.at[pl.ds(0, ts), :], xs_ref.at[pl.ds(0, ts), :], sem).wait()


def _scatter(slots, fill_tiles, a, n_slots, ts):
    n_tok, d = a.shape
    return pl.pallas_call(
        functools.partial(_scatter_body, ts=ts, n_tok=n_tok, n_fill=fill_tiles.shape[0]),
        grid_spec=pltpu.PrefetchScalarGridSpec(
            num_scalar_prefetch=2,
            grid=(n_tok // ts,),
            in_specs=[pl.BlockSpec((ts // SUBLANES, SUBLANES, d), lambda i, *_: (i, 0, 0))],
            out_specs=pl.BlockSpec(memory_space=pl.ANY),
            scratch_shapes=[pltpu.VMEM((MOE_TILE, d), F32), pltpu.SemaphoreType.DMA(()),
                            pltpu.SemaphoreType.DMA(())]),
        out_shape=jax.ShapeDtypeStruct((n_slots, d), F32),
        compiler_params=_cparams("arbitrary"),
        name="moe_scatter",
    )(slots, fill_tiles, a.reshape(n_tok // SUBLANES, SUBLANES, d))


def _experts_body(te_ref, tv_ref, x_ref, wg_ref, wu_ref, wd_ref, y_ref):
    del te_ref
    j = pl.program_id(0)
    f = pl.program_id(1)
    valid = tv_ref[j]

    @pl.when(f == 0)
    def _():
        y_ref[...] = jnp.zeros_like(y_ref)

    n_sub = (valid + MOE_SUB - 1) // MOE_SUB
    for ns in range(1, MOE_TILE // MOE_SUB + 1):
        rows = slice(0, ns * MOE_SUB)

        @pl.when(n_sub == ns)
        def _(rows=rows):
            x = x_ref[rows, :].astype(BF16)
            gate = _dot(x, wg_ref[...].astype(BF16))
            up = _dot(x, wu_ref[...].astype(BF16))
            hid = (_silu(gate) * up).astype(BF16)
            y_ref[rows, :] += _dot(hid, wd_ref[...].astype(BF16))


def _experts(tile_expert, tile_valid, xs, wg, wu, wd):
    n_slots, d = xs.shape
    ff = wg.shape[-1]
    n_tiles = n_slots // MOE_TILE
    tf = MOE_FCHUNK
    chunk = lambda j, f, tv: jnp.where(tv[j] > 0, f, ff // tf - 1)
    return pl.pallas_call(
        _experts_body,
        grid_spec=pltpu.PrefetchScalarGridSpec(
            num_scalar_prefetch=2,
            grid=(n_tiles, ff // tf),
            in_specs=[pl.BlockSpec((MOE_TILE, d), lambda j, f, te, tv: (j, 0)),
                      pl.BlockSpec((None, d, tf), lambda j, f, te, tv: (te[j], 0, chunk(j, f, tv))),
                      pl.BlockSpec((None, d, tf), lambda j, f, te, tv: (te[j], 0, chunk(j, f, tv))),
                      pl.BlockSpec((None, tf, d), lambda j, f, te, tv: (te[j], chunk(j, f, tv), 0))],
            out_specs=pl.BlockSpec((MOE_TILE, d), lambda j, f, te, tv: (j, 0))),
        out_shape=jax.ShapeDtypeStruct((n_slots, d), F32),
        compiler_params=_cparams("arbitrary", "arbitrary"),
        name="moe_experts",
    )(tile_expert, tile_valid, xs, wg, wu, wd)


def _combine_body(slot_ref, ys_ref, info_ref, h_ref, mod_ref, pg_ref, o_ref, buf_ref, sem, *, tc, n_tok):
    base = (pl.program_id(0) * pl.num_programs(1) + pl.program_id(1)) * tc

    def issue(i, carry):
        for u in range(SUBLANES):
            for k in range(2):
                slot = slot_ref[k * n_tok + base + i * SUBLANES + u]
                pltpu.make_async_copy(ys_ref.at[pl.ds(slot, 1), :], buf_ref.at[k, i, pl.ds(u, 1), :],
                                      sem).start(priority=k)
        return carry

    lax.fori_loop(0, tc // SUBLANES, issue, 0)
    for k in range(2):
        pltpu.make_async_copy(ys_ref.at[pl.ds(0, tc), :], ys_ref.at[pl.ds(0, tc), :], sem).wait()
    d = o_ref.shape[-1]
    f = info_ref[:, 2:3] * buf_ref[0].reshape(tc, d) + info_ref[:, 3:4] * buf_ref[1].reshape(tc, d)
    o_ref[...] = h_ref[...] + mod_ref[5:6, :] * _rms(f, pg_ref[...])


def _combine(slots, ys, info, h_lat, mod, post_g, tc):
    batch, n, d = h_lat.shape
    nt = n // tc
    return pl.pallas_call(
        functools.partial(_combine_body, tc=tc, n_tok=batch * n),
        grid_spec=pltpu.PrefetchScalarGridSpec(
            num_scalar_prefetch=1,
            grid=(batch, nt),
            in_specs=[pl.BlockSpec(memory_space=pl.ANY),
                      pl.BlockSpec((tc, LANES), lambda g, t, *_: (g * nt + t, 0)),
                      pl.BlockSpec((None, tc, d), lambda g, t, *_: (g, t, 0)),
                      pl.BlockSpec((None, 6, d), lambda g, t, *_: (g, 0, 0)),
                      pl.BlockSpec((1, d), lambda g, t, *_: (0, 0))],
            out_specs=pl.BlockSpec((None, tc, d), lambda g, t, *_: (g, t, 0)),
            scratch_shapes=[pltpu.VMEM((2, tc // SUBLANES, SUBLANES, d), F32), pltpu.SemaphoreType.DMA(())]),
        out_shape=jax.ShapeDtypeStruct((batch, n, d), F32),
        compiler_params=_cparams("arbitrary", "arbitrary"),
        name="moe_combine",
    )(slots, ys, info, h_lat, mod, post_g)


def _moe(h_lat, mod, g_pre, router, wg, wu, wd, post_g, tm):
    batch, n, d = h_lat.shape
    n_tok = batch * n
    a, info, counts, rt = _route(h_lat, mod, g_pre, router, tm)
    cnt = counts[0, :N_EXPERTS].astype(jnp.int32)
    padded = ((cnt + MOE_TILE - 1) // MOE_TILE) * MOE_TILE
    ends = jnp.cumsum(padded)
    starts = ends - padded
    seg = jnp.zeros((2, n_tok), jnp.int32)
    for e in range(N_EXPERTS):
        seg = jnp.where(rt[0:2] == e, starts[e], seg)
    slots = (seg + rt[2:4].astype(jnp.int32)).reshape(2 * n_tok)
    n_tiles = (2 * n_tok + N_EXPERTS * (MOE_TILE - 1)) // MOE_TILE
    tile_row = jnp.arange(n_tiles, dtype=jnp.int32) * MOE_TILE
    tile_expert = jnp.minimum(jnp.sum(tile_row[:, None] >= ends[None, :], axis=1), N_EXPERTS - 1).astype(jnp.int32)
    tile_valid = jnp.clip(cnt[tile_expert] - (tile_row - starts[tile_expert]), 0, MOE_TILE)
    tile_valid = jnp.where(tile_row < ends[-1], tile_valid, 0).astype(jnp.int32)
    last = jnp.where(padded > 0, ends // MOE_TILE - 1, -1)
    extra = ends[-1] // MOE_TILE + jnp.arange(n_tiles - 2 * n_tok // MOE_TILE, dtype=jnp.int32)
    fill_tiles = jnp.concatenate([last, jnp.where(extra < n_tiles, extra, -1)]).astype(jnp.int32)
    xs = _scatter(slots, fill_tiles, a, n_tiles * MOE_TILE, tm)
    ys = _experts(tile_expert, tile_valid, xs, wg, wu, wd)
    return _combine(slots, ys, info, h_lat, mod, post_g, tm)


def kernel(x, c, ctx, c_ctx, ada_w, ada_b, mix_pre_g, mix_post_g, ffn_pre_g, ffn_post_g, ev_w_in, ev_hy_conv_w, ev_hy_conv_b, ev_hy_filt_w1, ev_hy_filt_b1, ev_hy_filt_w2, ev_hy_filt_b2, ev_hy_filt_w3, ev_hy_freq, ev_hy_bias, ev_ssm_conv_w, ev_ssm_conv_b, ev_ssm_dt_bias, ev_ssm_a_log, ev_ssm_d, ev_ssm_norm_g, ev_w_out, ev_ffn_w_gate, ev_ffn_w_up, ev_ffn_w_down, od_w_in, od_mla_q_norm_g, od_mla_w_q_up, od_mla_kv_norm_g, od_mla_w_kv_up, od_gqa_sink, od_w_out, od_router, od_moe_w_gate, od_moe_w_up, od_moe_w_down):
    batch, n, d = x.shape
    ctx_len = ctx.shape[1]
    assert batch * ctx_len == n and ada_w.shape[0] == 2
    G = batch + 1
    tl = min(n, 1024)
    tm = tl

    cond = jnp.concatenate([c, c_ctx[None, :], jnp.zeros((16 - G, d), F32)], axis=0)
    mods = _ada(cond, ada_w, ada_b).reshape(2, 16, 6, d)
    ctx3 = ctx.reshape(1, n, d)

    n_main = 3 * HY_CH + SSM_INNER + SSM_CONV_CH
    w_in = ev_w_in[0]
    w_main = w_in[:, :n_main].astype(BF16)
    w_dt = jnp.pad(w_in[:, n_main:], ((0, 0), (0, LANES - 2 * SSM_HEADS))).astype(BF16)
    proj, dt = _even_in(x, ctx3, mods[0], mix_pre_g[0][None, :], w_main, w_dt, tl)
    y_hy_l, y_hy_c = _hyena(proj, batch, n, ctx_len,
                            (ev_hy_conv_w[0], ev_hy_conv_b[0], ev_hy_filt_w1[0], ev_hy_filt_b1[0], ev_hy_filt_w2[0],
                             ev_hy_filt_b2[0], ev_hy_filt_w3[0], ev_hy_freq[0], ev_hy_bias[0]))
    y_ss_l, y_ss_c = _ssd(proj, dt, batch, n, ctx_len,
                          (ev_ssm_conv_w[0], ev_ssm_conv_b[0], ev_ssm_dt_bias[0], ev_ssm_a_log[0], ev_ssm_d[0]))
    h_all = _mix_out_even(y_hy_l, y_hy_c, y_ss_l, y_ss_c, proj, ev_ssm_norm_g[0][None, :],
                          ev_w_out[0].astype(BF16), x, ctx3, mods[0], mix_post_g[0][None, :], tl)
    h_all = _ffn(h_all, mods[0], ffn_pre_g[0][None, :], ev_ffn_w_gate[0].astype(BF16), ev_ffn_w_up[0].astype(BF16),
                 ev_ffn_w_down[0].astype(BF16), ffn_post_g[0][None, :], tl)

    q_m, k_m, v_m, q_w, k_w, v_w = _odd_in(h_all, mods[1], mix_pre_g[1][None, :], od_w_in[0], od_mla_q_norm_g[0],
                                           od_mla_w_q_up[0], od_mla_kv_norm_g[0], od_mla_w_kv_up[0], tl)
    o_mla = _mla_attn(q_m, k_m, v_m, batch, n, ctx_len, min(n, 512), 4)
    o_win = _win_attn(q_w, k_w, v_w, od_gqa_sink[0], batch, n, ctx_len, min(n, 2 * WINDOW))
    h_lat = _mix_out_odd(o_mla, o_win, od_w_out[0].astype(BF16), h_all, mods[1], mix_post_g[1][None, :], tl)
    return _moe(h_lat, mods[1], ffn_pre_g[1][None, :], od_router[0], od_moe_w_gate[0], od_moe_w_up[0],
                od_moe_w_down[0], ffn_post_g[1][None, :], tm)
```
